```python
import jax, jax.numpy as jnp
from jax import lax
import numpy as np

D_MODEL = 1024
BATCH = 8
SEQ = 2048
DEPTH = 2
DEC_BATCH = 32
DEC_SEQ = 64
PAST_LEN = 4096

CHUNK = 64
Q_BLOCK = 128
LEFT_CHUNKS = 8
BAND_CHUNKS = LEFT_CHUNKS + 1
LEFT_CTX = LEFT_CHUNKS * CHUNK
REL_CLIP = 128
N_REL = 2 * REL_CLIP + 1
EPS = 1e-6
NEG = -1e30
ROPE_THETA = 10000.0
FORGET_BIAS_INIT = 3.0

A_HEADS = 8
A_Q_LORA = 256
A_KV_LORA = 128
A_NOPE = 64
A_ROPE = 32
A_V = 64
A_WIDTH = A_HEADS * A_V
A_SCALE = (A_NOPE + A_ROPE) ** -0.5
B_HEADS = 8
B_DIM = 64
B_WIDTH = B_HEADS * B_DIM
C_HEADS = 8
C_DIM = 64
C_WIDTH = C_HEADS * C_DIM
D_HEADS = 8
D_DIM = 64
D_WIDTH = D_HEADS * D_DIM

N_EVEN = (DEPTH + 1) // 2
N_ODD = DEPTH // 2
EVEN_SIZES = (A_Q_LORA, A_KV_LORA, A_ROPE, A_WIDTH, B_WIDTH, B_WIDTH, B_WIDTH, B_WIDTH)
EVEN_IN = A_Q_LORA + A_KV_LORA + A_ROPE + A_WIDTH + 4 * B_WIDTH
ODD_SIZES = (C_WIDTH, C_WIDTH, C_WIDTH, C_WIDTH, D_WIDTH, D_WIDTH, D_WIDTH, D_HEADS, D_WIDTH)
ODD_IN = 4 * C_WIDTH + 4 * D_WIDTH + D_HEADS

kernel_name = 'hybrid_streaming_encoder_step'


def rmsnorm(x, g):
    xf = x.astype(jnp.float32)
    y = xf * lax.rsqrt(jnp.mean(xf * xf, axis=-1, keepdims=True) + EPS)
    return (y * g.astype(jnp.float32)).astype(x.dtype)


def split_cols(t, sizes):
    out, off = [], 0
    for n in sizes:
        out.append(t[..., off:off + n])
        off += n
    return out


def cat(a, b):
    return jnp.concatenate([a, b.astype(a.dtype)], axis=1)


def apply_rope(x, pos):
    half = x.shape[-1] // 2
    inv_freq = ROPE_THETA ** (-jnp.arange(half, dtype=jnp.float32) / half)
    ang = pos.astype(jnp.float32)[:, None] * inv_freq[None, :]
    shape = (1, pos.shape[0]) + (1,) * (x.ndim - 3) + (half,)
    cos, sin = jnp.cos(ang).reshape(shape), jnp.sin(ang).reshape(shape)
    xf = x.astype(jnp.float32)
    x1, x2 = xf[..., :half], xf[..., half:]
    return jnp.concatenate([x1 * cos - x2 * sin, x2 * cos + x1 * sin], axis=-1).astype(x.dtype)


def sweep_query_blocks(fn, q_args, qpos):
    n_blk = qpos.shape[0] // Q_BLOCK
    def to_blocks(a):
        return jnp.moveaxis(a.reshape((a.shape[0], n_blk, Q_BLOCK) + a.shape[2:]), 1, 0)
    blocks = tuple(to_blocks(a) for a in q_args) + (qpos.reshape(n_blk, Q_BLOCK),)
    out = jnp.moveaxis(lax.map(lambda xs: fn(*xs), blocks), 0, 1)
    return out.reshape((out.shape[0], n_blk * Q_BLOCK) + out.shape[3:])


def mla_attend(q_lat, q_rope, qpos, c_kv, k_rope, kpos):
    s = (jnp.einsum('bqhl,bkl->bhqk', q_lat, c_kv, preferred_element_type=jnp.float32)
         + jnp.einsum('bqhr,bkr->bhqk', q_rope, k_rope, preferred_element_type=jnp.float32)) * A_SCALE
    mask = (kpos[None, :] // CHUNK) <= (qpos[:, None] // CHUNK)
    p = jax.nn.softmax(jnp.where(mask, s, NEG), axis=-1).astype(c_kv.dtype)
    return jnp.einsum('bhqk,bkl->bqhl', p, c_kv)


def sb_attend(q, k, v, qpos, kpos):
    z = jnp.einsum('bqhd,bkhd->bhqk', q, k, preferred_element_type=jnp.float32) * (B_DIM ** -0.5)
    causal = kpos[None, :] < qpos[:, None]
    log_beta = jax.nn.log_sigmoid(z)
    log_1mb = jnp.where(causal, log_beta - z, 0.0)
    suffix = lax.cumsum(log_1mb, axis=3, reverse=True) - log_1mb
    w = jnp.where(causal, jnp.exp(log_beta + suffix), 0.0)
    return jnp.einsum('bhqk,bkhd->bqhd', w.astype(v.dtype), v)


def band_attend(q, k, v, qpos, kpos, rel_bias):
    s = jnp.einsum('bqhd,bkhd->bhqk', q, k, preferred_element_type=jnp.float32) * (C_DIM ** -0.5)
    rel = jnp.clip(qpos[:, None] - kpos[None, :], -REL_CLIP, REL_CLIP) + REL_CLIP
    s = s + rel_bias.astype(jnp.float32)[:, rel][None]
    qc, kc = qpos[:, None] // CHUNK, kpos[None, :] // CHUNK
    mask = (kpos[None, :] >= 0) & (kc <= qc) & (kc >= qc - LEFT_CHUNKS)
    p = jax.nn.softmax(jnp.where(mask, s, NEG), axis=-1).astype(v.dtype)
    return jnp.einsum('bhqk,bkhd->bqhd', p, v)


def fox_attend(q, k, v, cum_q, cum_k, qpos, kpos):
    s = jnp.einsum('bqhd,bkhd->bhqk', q, k, preferred_element_type=jnp.float32) * (D_DIM ** -0.5)
    s = s + (jnp.swapaxes(cum_q, 1, 2)[:, :, :, None] - jnp.swapaxes(cum_k, 1, 2)[:, :, None, :])
    mask = kpos[None, :] <= qpos[:, None]
    p = jax.nn.softmax(jnp.where(mask, s, NEG), axis=-1).astype(v.dtype)
    return jnp.einsum('bhqk,bkhd->bqhd', p, v)


def even_mixer(h, pos, past, w_in, q_norm, w_uq, kv_norm, w_uk, w_uv, w_out):
    n_b, n_s, _ = h.shape
    q_a, kv_a, k_r, g_a, q_b, k_b, v_b, g_b = split_cols(h @ w_in, EVEN_SIZES)
    c_q = rmsnorm(q_a, q_norm)
    q = jnp.einsum('bsc,chr->bshr', c_q, w_uq)
    q_lat = jnp.einsum('bshn,chn->bshc', q[..., :A_NOPE], w_uk)
    q_rope = apply_rope(q[..., A_NOPE:], pos)
    c_kv = rmsnorm(kv_a, kv_norm)
    k_rope = apply_rope(k_r, pos)
    heads = lambda t: t.reshape(n_b, n_s, B_HEADS, B_DIM)
    q_b, k_b, v_b = heads(q_b), heads(k_b), heads(v_b)
    if past is None:
        lat = sweep_query_blocks(lambda ql, qr, qp: mla_attend(ql, qr, qp, c_kv, k_rope, pos), (q_lat, q_rope), pos)
        sb = sweep_query_blocks(lambda qq, qp: sb_attend(qq, k_b, v_b, qp, pos), (q_b,), pos)
    else:
        p_ckv, p_kr, p_k, p_v = past
        kpos = jnp.arange(p_ckv.shape[1] + n_s)
        lat = mla_attend(q_lat, q_rope, pos, cat(p_ckv, c_kv), cat(p_kr, k_rope), kpos)
        sb = sb_attend(q_b, cat(p_k, k_b), cat(p_v, v_b), pos, kpos)
    out_a = jnp.einsum('bshc,chv->bshv', lat, w_uv).reshape(n_b, n_s, A_WIDTH)
    mixed = jnp.concatenate([jax.nn.silu(g_a) * out_a,
                             jax.nn.silu(g_b) * sb.reshape(n_b, n_s, B_WIDTH)], axis=-1)
    return mixed @ w_out, (c_kv, k_rope, k_b, v_b)


def odd_mixer(h, pos, past, w_in, rel_bias, forget_bias, w_out):
    n_b, n_s, _ = h.shape
    q_c, k_c, v_c, g_c, q_d, k_d, v_d, f_d, g_d = split_cols(h @ w_in, ODD_SIZES)
    hc = lambda t: t.reshape(n_b, n_s, C_HEADS, C_DIM)
    hd = lambda t: t.reshape(n_b, n_s, D_HEADS, D_DIM)
    q_c, k_c, v_c = hc(q_c), hc(k_c), hc(v_c)
    q_d, k_d, v_d = hd(q_d), hd(k_d), hd(v_d)
    log_f = jax.nn.log_sigmoid(f_d.astype(jnp.float32) + forget_bias.astype(jnp.float32))
    if past is None:
        n_c = n_s // CHUNK
        band_idx = (jnp.arange(n_c) * CHUNK)[:, None] + jnp.arange(BAND_CHUNKS * CHUNK)[None, :]
        pad = ((0, 0), (LEFT_CTX, 0), (0, 0), (0, 0))
        k_band = jnp.pad(k_c, pad)[:, band_idx]
        v_band = jnp.pad(v_c, pad)[:, band_idx]
        per_chunk = jax.vmap(band_attend, in_axes=(1, 1, 1, 0, 0, None), out_axes=1)
        band = per_chunk(q_c.reshape(n_b, n_c, CHUNK, C_HEADS, C_DIM), k_band, v_band,
                         pos.reshape(n_c, CHUNK), band_idx - LEFT_CTX, rel_bias)
        band = band.reshape(n_b, n_s, C_HEADS, C_DIM)
        cum = jnp.cumsum(log_f, axis=1)
        fox = sweep_query_blocks(lambda qq, cq, qp: fox_attend(qq, k_d, v_d, cq, cum, qp, pos), (q_d, cum), pos)
        keep = min(LEFT_CTX, n_s)
        band_state = (k_c[:, n_s - keep:], v_c[:, n_s - keep:])
    else:
        p_bk, p_bv, p_fk, p_fv, p_lf = past
        n_keep, n_past = p_bk.shape[1], p_fk.shape[1]
        bk, bv = cat(p_bk, k_c), cat(p_bv, v_c)
        kpos_band = jnp.concatenate([n_past - n_keep + jnp.arange(n_keep), pos])
        band = band_attend(q_c, bk, bv, pos, kpos_band, rel_bias)
        cum = jnp.cumsum(jnp.concatenate([p_lf.astype(jnp.float32), log_f], axis=1), axis=1)
        fox = fox_attend(q_d, cat(p_fk, k_d), cat(p_fv, v_d), cum[:, n_past:], cum, pos,
                         jnp.arange(n_past + n_s))
        band_state = (bk[:, n_s:], bv[:, n_s:])
    mixed = jnp.concatenate([jax.nn.silu(g_c) * band.reshape(n_b, n_s, C_WIDTH),
                             jax.nn.silu(g_d) * fox.reshape(n_b, n_s, D_WIDTH)], axis=-1)
    return mixed @ w_out, band_state + (k_d, v_d, log_f)


def setup_inputs(seed: int = 0) -> dict:
    key = jax.random.key(seed)
    ks = jax.random.split(key, 26)
    nrm = lambda k, shape, scale=1.0: scale * jax.random.normal(k, shape, jnp.float32)
    cl = min(LEFT_CTX, PAST_LEN)
    return {
        'x_prompt': nrm(ks[0], (BATCH, SEQ, D_MODEL)),
        'x_sample': nrm(ks[1], (DEC_BATCH, DEC_SEQ, D_MODEL)),
        'cache_mla_ckv': nrm(ks[2], (N_EVEN, DEC_BATCH, PAST_LEN, A_KV_LORA)),
        'cache_mla_krope': nrm(ks[3], (N_EVEN, DEC_BATCH, PAST_LEN, A_ROPE)),
        'cache_sb_k': nrm(ks[4], (N_EVEN, DEC_BATCH, PAST_LEN, B_HEADS, B_DIM)),
        'cache_sb_v': nrm(ks[5], (N_EVEN, DEC_BATCH, PAST_LEN, B_HEADS, B_DIM)),
        'cache_band_k': nrm(ks[6], (N_ODD, DEC_BATCH, cl, C_HEADS, C_DIM)),
        'cache_band_v': nrm(ks[7], (N_ODD, DEC_BATCH, cl, C_HEADS, C_DIM)),
        'cache_fox_k': nrm(ks[8], (N_ODD, DEC_BATCH, PAST_LEN, D_HEADS, D_DIM)),
        'cache_fox_v': nrm(ks[9], (N_ODD, DEC_BATCH, PAST_LEN, D_HEADS, D_DIM)),
        'cache_fox_logf': jax.nn.log_sigmoid(nrm(ks[10], (N_ODD, DEC_BATCH, PAST_LEN, D_HEADS)) + FORGET_BIAS_INIT),
        'norm_pre': 1.0 + nrm(ks[11], (DEPTH, D_MODEL), 0.05),
        'norm_post': 1.0 + nrm(ks[12], (DEPTH, D_MODEL), 0.05),
        'w_in_even': nrm(ks[13], (N_EVEN, D_MODEL, EVEN_IN), D_MODEL ** -0.5),
        'a_q_norm': 1.0 + nrm(ks[14], (N_EVEN, A_Q_LORA), 0.05),
        'a_w_uq': nrm(ks[15], (N_EVEN, A_Q_LORA, A_HEADS, A_NOPE + A_ROPE), A_Q_LORA ** -0.5),
        'a_kv_norm': 1.0 + nrm(ks[16], (N_EVEN, A_KV_LORA), 0.05),
        'a_w_uk': nrm(ks[17], (N_EVEN, A_KV_LORA, A_HEADS, A_NOPE), A_KV_LORA ** -0.5),
        'a_w_uv': nrm(ks[18], (N_EVEN, A_KV_LORA, A_HEADS, A_V), A_KV_LORA ** -0.5),
        'w_out_even': nrm(ks[19], (N_EVEN, A_WIDTH + B_WIDTH, D_MODEL), (A_WIDTH + B_WIDTH) ** -0.5),
        'w_in_odd': nrm(ks[20], (N_ODD, D_MODEL, ODD_IN), D_MODEL ** -0.5),
        'c_rel_bias': nrm(ks[21], (N_ODD, C_HEADS, N_REL), 0.5),
        'd_forget_bias': FORGET_BIAS_INIT + nrm(ks[22], (N_ODD, D_HEADS), 0.1),
        'w_out_odd': nrm(ks[23], (N_ODD, C_WIDTH + D_WIDTH, D_MODEL), (C_WIDTH + D_WIDTH) ** -0.5),
    }


def reference(x_prompt, x_sample, cache_mla_ckv, cache_mla_krope, cache_sb_k, cache_sb_v,
              cache_band_k, cache_band_v, cache_fox_k, cache_fox_v, cache_fox_logf,
              norm_pre, norm_post, w_in_even, a_q_norm, a_w_uq, a_kv_norm, a_w_uk, a_w_uv,
              w_out_even, w_in_odd, c_rel_bias, d_forget_bias, w_out_odd):
    pos_p = jnp.arange(x_prompt.shape[1])
    pos_s = PAST_LEN + jnp.arange(x_sample.shape[1])
    xp, xs = x_prompt, x_sample
    even_p, even_s, odd_p, odd_s = [], [], [], []
    for l in range(DEPTH):
        i = l // 2
        hp, hs = rmsnorm(xp, norm_pre[l]), rmsnorm(xs, norm_pre[l])
        if l % 2 == 0:
            w = (w_in_even[i], a_q_norm[i], a_w_uq[i], a_kv_norm[i], a_w_uk[i], a_w_uv[i], w_out_even[i])
            mp, stp = even_mixer(hp, pos_p, None, *w)
            ms, sts = even_mixer(hs, pos_s, (cache_mla_ckv[i], cache_mla_krope[i], cache_sb_k[i], cache_sb_v[i]), *w)
            even_p.append(stp)
            even_s.append(sts)
        else:
            w = (w_in_odd[i], c_rel_bias[i], d_forget_bias[i], w_out_odd[i])
            mp, stp = odd_mixer(hp, pos_p, None, *w)
            ms, sts = odd_mixer(hs, pos_s, (cache_band_k[i], cache_band_v[i], cache_fox_k[i], cache_fox_v[i], cache_fox_logf[i]), *w)
            odd_p.append(stp)
            odd_s.append(sts)
        xp = xp + rmsnorm(mp, norm_post[l])
        xs = xs + rmsnorm(ms, norm_post[l])
    st = lambda lst, j: jnp.stack([s[j] for s in lst])
    return (xp, xs,
            st(even_p, 0), st(even_p, 1), st(even_p, 2), st(even_p, 3),
            st(odd_p, 0), st(odd_p, 1), st(odd_p, 2), st(odd_p, 3), st(odd_p, 4),
            st(even_s, 0), st(even_s, 1), st(even_s, 2), st(even_s, 3),
            st(odd_s, 0), st(odd_s, 1), st(odd_s, 2), st(odd_s, 3), st(odd_s, 4))
```

```python
import functools

import numpy as np
import jax
import jax.numpy as jnp
from jax import lax
from jax.experimental import pallas as pl
from jax.experimental.pallas import tpu as pltpu

F32 = jnp.float32
BF16 = jnp.bfloat16

D_MODEL = 1024
PAST_LEN = 4096
CHUNK = 64
LEFT_CTX = 512
REL_CLIP = 128
EPS = 1e-6
NEG = -1e30
ROPE_THETA = 10000.0
A_HEADS = 8
A_Q_LORA = 256
A_KV_LORA = 128
A_NOPE = 64
A_ROPE = 32
A_V = 64
A_SCALE = (A_NOPE + A_ROPE) ** -0.5
HEADS = 8
HEAD_DIM = 64
WIDTH = HEADS * HEAD_DIM
QK_SCALE = HEAD_DIM ** -0.5
N_PAIRS = HEADS // 2

LANES = 128
VMEM_LIMIT = 52 * 1024 * 1024
TM = 256
TQ = 256
TK = 256
TKC = 1024
BAND_TQ = 128


def _dot(a, b):
    return jnp.dot(a, b, preferred_element_type=F32)


def _dot_nt(a, b):
    return lax.dot_general(a, b, (((1,), (1,)), ((), ())), preferred_element_type=F32)


def _rms(x, g):
    y = x * lax.rsqrt(jnp.mean(x * x, axis=-1, keepdims=True) + EPS)
    return y * g


def _log_sigmoid(z):
    return jnp.minimum(z, 0.0) - jnp.log1p(jnp.exp(-jnp.abs(z)))


def _silu(g):
    return g / (1.0 + jnp.exp(-g))


def _stack_pair(q2):
    qf = q2.astype(F32)
    lane = lax.broadcasted_iota(jnp.int32, qf.shape, 1)
    even = jnp.where(lane < HEAD_DIM, qf, 0.0)
    odd = jnp.where(lane >= HEAD_DIM, qf, 0.0)
    return jnp.concatenate([even, odd], axis=0).astype(BF16)


def _merge_pair(o, tq):
    top, bot = o[:tq], o[tq:]
    lane = lax.broadcasted_iota(jnp.int32, top.shape, 1)
    return jnp.where(lane < HEAD_DIM, top, bot)


def _params(*sem):
    return pltpu.CompilerParams(dimension_semantics=sem, vmem_limit_bytes=VMEM_LIMIT)


def _const_spec(shape):
    nd = len(shape)
    return pl.BlockSpec(shape, lambda *_: (0,) * nd)


def _in_even_body(x_ref, gpre_ref, cos_ref, sin_ref, wqa_ref, wkv_ref, wkr_ref, wkrs_ref, wga_ref,
                  wqb_ref, wkb_ref, wvb_ref, wgb_ref, qn_ref, kvn_ref, wuqn_ref, wuqr_ref,
                  wuqrs_ref, wuk_ref,
                  ckv_ref, krope_ref, k2_ref, q2_ref, ga_ref, gb_ref, qb_ref, kb_ref, kb16_ref,
                  vb_ref, vb16_ref):
    h = _rms(x_ref[...], gpre_ref[...]).astype(BF16)
    cos = cos_ref[...]
    sin = sin_ref[...]
    ckv = _rms(_dot(h, wkv_ref[...]), kvn_ref[...])
    ckv_ref[...] = ckv
    kr = _dot(h, wkr_ref[...]) * cos + _dot(h, wkrs_ref[...]) * sin
    krope_ref[...] = kr[:, :A_ROPE]
    k2_ref[:, :LANES] = ckv.astype(BF16)
    k2_ref[:, LANES:] = kr.astype(BF16)
    cq = _rms(_dot(h, wqa_ref[...]), qn_ref[...]).astype(BF16)
    qn = _dot(cq, wuqn_ref[...]).astype(BF16)
    for p in range(N_PAIRS):
        ql = _dot(qn[:, p * LANES:(p + 1) * LANES], wuk_ref[p])
        q2_ref[:, (2 * p) * 256:(2 * p) * 256 + LANES] = ql[:, :LANES].astype(BF16)
        q2_ref[:, (2 * p + 1) * 256:(2 * p + 1) * 256 + LANES] = ql[:, LANES:].astype(BF16)
    qr = _dot(cq, wuqr_ref[...])
    qrs = _dot(cq, wuqrs_ref[...])
    for hd in range(A_HEADS):
        rot = qr[:, hd * LANES:(hd + 1) * LANES] * cos + qrs[:, hd * LANES:(hd + 1) * LANES] * sin
        q2_ref[:, hd * 256 + LANES:(hd + 1) * 256] = rot.astype(BF16)
    ga_ref[...] = _dot(h, wga_ref[...])
    gb_ref[...] = _dot(h, wgb_ref[...])
    qb = _dot(h, wqb_ref[...]) * QK_SCALE
    kb = _dot(h, wkb_ref[...])
    vb = _dot(h, wvb_ref[...])
    kb_ref[...] = kb
    vb_ref[...] = vb
    for p in range(N_PAIRS):
        sl = slice(p * LANES, (p + 1) * LANES)
        qb_ref[p] = qb[:, sl].astype(BF16)
        kb16_ref[p] = kb[:, sl].astype(BF16)
        vb16_ref[p] = vb[:, sl].astype(BF16)


def _in_even(x, gpre, cos, sin, w):
    rows = x.shape[0]
    n_tab = cos.shape[0] // TM
    row_spec = lambda n: pl.BlockSpec((TM, n), lambda i: (i, 0))
    pm_spec = pl.BlockSpec((N_PAIRS, TM, LANES), lambda i: (0, i, 0))
    tab_spec = pl.BlockSpec((TM, LANES), lambda i: (i % n_tab, 0))
    weights = [w['wqa'], w['wkv'], w['wkr'], w['wkrs'], w['wga'], w['wqb'], w['wkb'], w['wvb'], w['wgb'],
               w['qn'], w['kvn'], w['wuqn'], w['wuqr'], w['wuqrs'], w['wuk']]
    sds = jax.ShapeDtypeStruct
    pm = sds((N_PAIRS, rows, LANES), BF16)
    return pl.pallas_call(
        _in_even_body,
        grid=(rows // TM,),
        in_specs=[row_spec(D_MODEL), _const_spec((1, D_MODEL)), tab_spec, tab_spec]
                 + [_const_spec(a.shape) for a in weights],
        out_specs=[row_spec(A_KV_LORA), row_spec(A_ROPE), row_spec(256), row_spec(A_HEADS * 256),
                   row_spec(WIDTH), row_spec(WIDTH), pm_spec, row_spec(WIDTH), pm_spec,
                   row_spec(WIDTH), pm_spec],
        out_shape=[sds((rows, A_KV_LORA), F32), sds((rows, A_ROPE), F32), sds((rows, 256), BF16),
                   sds((rows, A_HEADS * 256), BF16), sds((rows, WIDTH), F32), sds((rows, WIDTH), F32),
                   pm, sds((rows, WIDTH), F32), pm, sds((rows, WIDTH), F32), pm],
        compiler_params=_params("parallel"),
        name="in_proj_even",
    )(x, gpre, cos, sin, *weights)


def _in_odd_body(x_ref, gpre_ref, fb_ref, wqc_ref, wkc_ref, wvc_ref, wgc_ref, wqd_ref, wkd_ref,
                 wvd_ref, wf_ref, wgd_ref,
                 qc_ref, kc_ref, kc16_ref, vc_ref, vc16_ref, gc_ref, qd_ref, kd_ref, kd16_ref,
                 vd_ref, vd16_ref, logf_ref, gd_ref):
    h = _rms(x_ref[...], gpre_ref[...]).astype(BF16)
    gc_ref[...] = _dot(h, wgc_ref[...])
    gd_ref[...] = _dot(h, wgd_ref[...])
    f = _dot(h, wf_ref[...])[:, :HEADS] + fb_ref[...]
    logf_ref[...] = _log_sigmoid(f)
    for q_w, k_w, v_w, q_o, k_o, k16_o, v_o, v16_o in (
            (wqc_ref, wkc_ref, wvc_ref, qc_ref, kc_ref, kc16_ref, vc_ref, vc16_ref),
            (wqd_ref, wkd_ref, wvd_ref, qd_ref, kd_ref, kd16_ref, vd_ref, vd16_ref)):
        q = _dot(h, q_w[...]) * QK_SCALE
        k = _dot(h, k_w[...])
        v = _dot(h, v_w[...])
        k_o[...] = k
        v_o[...] = v
        for p in range(N_PAIRS):
            sl = slice(p * LANES, (p + 1) * LANES)
            q_o[p] = q[:, sl].astype(BF16)
            k16_o[p] = k[:, sl].astype(BF16)
            v16_o[p] = v[:, sl].astype(BF16)


def _in_odd(x, gpre, w):
    rows = x.shape[0]
    row_spec = lambda n: pl.BlockSpec((TM, n), lambda i: (i, 0))
    pm_spec = pl.BlockSpec((N_PAIRS, TM, LANES), lambda i: (0, i, 0))
    weights = [w['wqc'], w['wkc'], w['wvc'], w['wgc'], w['wqd'], w['wkd'], w['wvd'], w['wf'], w['wgd']]
    sds = jax.ShapeDtypeStruct
    pm = sds((N_PAIRS, rows, LANES), BF16)
    full = sds((rows, WIDTH), F32)
    return pl.pallas_call(
        _in_odd_body,
        grid=(rows // TM,),
        in_specs=[row_spec(D_MODEL), _const_spec((1, D_MODEL)), _const_spec((1, HEADS))]
                 + [_const_spec(a.shape) for a in weights],
        out_specs=[pm_spec, row_spec(WIDTH), pm_spec, row_spec(WIDTH), pm_spec, row_spec(WIDTH),
                   pm_spec, row_spec(WIDTH), pm_spec, row_spec(WIDTH), pm_spec, row_spec(HEADS),
                   row_spec(WIDTH)],
        out_shape=[pm, full, pm, full, pm, full, pm, full, pm, full, pm, sds((rows, HEADS), F32), full],
        compiler_params=_params("parallel"),
        name="in_proj_odd",
    )(x, gpre, w['fb'], *weights)


def _out_body(*refs, mla):
    if mla:
        x_ref, gpost_ref, g1_ref, g2_ref, a_ref, b_ref, wuv_ref, wout_ref, o_ref, mix_ref = refs
    else:
        x_ref, gpost_ref, g1_ref, g2_ref, a_ref, b_ref, wout_ref, o_ref, mix_ref = refs
    s1 = _silu(g1_ref[...])
    s2 = _silu(g2_ref[...])
    for p in range(N_PAIRS):
        sl = slice(p * LANES, (p + 1) * LANES)
        if mla:
            a = _dot(a_ref[:, p * 256:(p + 1) * 256], wuv_ref[p])
        else:
            a = a_ref[p]
        mix_ref[:, sl] = (s1[:, sl] * a).astype(BF16)
        mix_ref[:, WIDTH + p * LANES:WIDTH + (p + 1) * LANES] = (s2[:, sl] * b_ref[p]).astype(BF16)
    y = _dot(mix_ref[...], wout_ref[...])
    o_ref[...] = x_ref[...] + _rms(y, gpost_ref[...])


def _out_proj(x, gpost, g1, g2, a, b, wout, wuv=None):
    rows = x.shape[0]
    mla = wuv is not None
    row_spec = lambda n: pl.BlockSpec((TM, n), lambda i: (i, 0))
    pm_spec = pl.BlockSpec((N_PAIRS, TM, LANES), lambda i: (0, i, 0))
    in_specs = [row_spec(D_MODEL), _const_spec((1, D_MODEL)), row_spec(WIDTH), row_spec(WIDTH),
                row_spec(A_HEADS * A_KV_LORA) if mla else pm_spec, pm_spec]
    args = [x, gpost, g1, g2, a, b]
    if mla:
        in_specs.append(_const_spec(wuv.shape))
        args.append(wuv)
    in_specs.append(_const_spec(wout.shape))
    args.append(wout)
    return pl.pallas_call(
        functools.partial(_out_body, mla=mla),
        grid=(rows // TM,),
        in_specs=in_specs,
        out_specs=row_spec(D_MODEL),
        out_shape=jax.ShapeDtypeStruct((rows, D_MODEL), F32),
        scratch_shapes=[pltpu.VMEM((TM, 2 * WIDTH), BF16)],
        compiler_params=_params("parallel"),
        name="out_proj_even" if mla else "out_proj_odd",
    )(*args)


def _softmax_init(m_ref, l_ref, acc_ref):
    m_ref[...] = jnp.full(m_ref.shape, NEG, F32)
    l_ref[...] = jnp.zeros(l_ref.shape, F32)
    acc_ref[...] = jnp.zeros(acc_ref.shape, F32)


def _softmax_update(s, v, m_ref, l_ref, acc_ref):
    m_prev = m_ref[...]
    m_new = jnp.maximum(m_prev, jnp.max(s, axis=1, keepdims=True))
    alpha = jnp.exp(m_prev - m_new)
    p = jnp.exp(s - m_new)
    l_ref[...] = alpha * l_ref[...] + jnp.sum(p, axis=1, keepdims=True)
    acc_ref[...] = alpha * acc_ref[...] + _dot(p.astype(BF16), v)
    m_ref[...] = m_new


def _fill_suffix_ones(tri_ref):
    n = tri_ref.shape[0]
    r = lax.broadcasted_iota(jnp.int32, (n, n), 0)
    c = lax.broadcasted_iota(jnp.int32, (n, n), 1)
    tri_ref[...] = jnp.where(r > c, 1.0, 0.0).astype(BF16)


def _stick_tile(qs, k2, v2, tri, acc_ref, car_ref, mask):
    z = _dot_nt(qs, k2)
    lb = _log_sigmoid(z)
    l1 = lb - z
    if mask is not None:
        l1 = jnp.where(mask, l1, 0.0)
    hi = l1.astype(BF16)
    lo = (l1 - hi.astype(F32)).astype(BF16)
    suf = _dot(hi, tri) + _dot(lo, tri) + car_ref[...]
    w = jnp.exp(lb + suf)
    if mask is not None:
        w = jnp.where(mask, w, 0.0)
    acc_ref[...] += _dot(w.astype(BF16), v2)
    car_ref[...] += jnp.sum(l1, axis=1, keepdims=True)


def _local_causal(rows, keys, tq, strict):
    r = lax.broadcasted_iota(jnp.int32, (rows, keys), 0) & (tq - 1)
    c = lax.broadcasted_iota(jnp.int32, (rows, keys), 1)
    return (c < r) if strict else (c <= r)


def _decay_bias(cq, ck, tq, keys):
    ckb = jnp.concatenate([jnp.broadcast_to(ck[0:1], (tq, keys)),
                           jnp.broadcast_to(ck[1:2], (tq, keys))], axis=0)
    return cq - ckb


def _mla_prompt_body(q_ref, k_ref, o_ref, m_ref, l_ref, acc_ref):
    g = pl.program_id(1)
    rows = CHUNK * A_HEADS

    def update(q, k):
        s = _dot_nt(q, k) * A_SCALE
        _softmax_update(s, k[:, :A_KV_LORA], m_ref, l_ref, acc_ref)

    def group(gl, carry):
        gq = g * 2 + gl
        for ci in range(4):
            r0 = pl.multiple_of((gl * 4 + ci) * rows, rows)
            q = q_ref[pl.ds(r0, rows), :]
            _softmax_init(m_ref, l_ref, acc_ref)

            def kv(j, c2):
                update(q, k_ref[pl.ds(pl.multiple_of(j * TK, TK), TK), :])
                return c2

            lax.fori_loop(0, gq, kv, 0)
            update(q, k_ref[pl.ds(pl.multiple_of(gq * TK, TK), CHUNK * (ci + 1)), :])
            o_ref[pl.ds(r0, rows), :] = (acc_ref[...] / l_ref[...]).astype(BF16)
        return carry

    lax.fori_loop(0, 2, group, 0)


def _mla_prompt(q2, k2, n_b, n_s):
    rows_q = n_s * A_HEADS
    qg = rows_q // 4
    assert qg == 8 * CHUNK * A_HEADS and TK == 4 * CHUNK
    rows = CHUNK * A_HEADS
    return pl.pallas_call(
        _mla_prompt_body,
        grid=(n_b, 4),
        in_specs=[pl.BlockSpec((None, qg, 256), lambda b, g: (b, g, 0)),
                  pl.BlockSpec((None, n_s, 256), lambda b, g: (b, 0, 0))],
        out_specs=pl.BlockSpec((None, qg, A_KV_LORA), lambda b, g: (b, g, 0)),
        out_shape=jax.ShapeDtypeStruct((n_b, rows_q, A_KV_LORA), BF16),
        scratch_shapes=[pltpu.VMEM((rows, 1), F32), pltpu.VMEM((rows, 1), F32),
                        pltpu.VMEM((rows, A_KV_LORA), F32)],
        compiler_params=_params("parallel", "parallel"),
        name="mla_prompt",
    )(q2.reshape(n_b, rows_q, 256), k2.reshape(n_b, n_s, 256))


def _mla_sample_body(q_ref, kn_ref, ckv_ref, kr_ref, o_ref, m_ref, l_ref, acc_ref):
    j = pl.program_id(1)

    @pl.when(j == 0)
    def _():
        _softmax_init(m_ref, l_ref, acc_ref)

    q = q_ref[...]
    ql = q[:, :A_KV_LORA]
    qr = q[:, A_KV_LORA:A_KV_LORA + A_ROPE]
    for sub in range(TKC // TK):
        ck = ckv_ref[sub * TK:(sub + 1) * TK, :].astype(BF16)
        kr = kr_ref[sub * TK:(sub + 1) * TK, :].astype(BF16)
        s = (_dot_nt(ql, ck) + _dot_nt(qr, kr)) * A_SCALE
        _softmax_update(s, ck, m_ref, l_ref, acc_ref)

    @pl.when(j == pl.num_programs(1) - 1)
    def _():
        kn = kn_ref[...]
        s = _dot_nt(q, kn) * A_SCALE
        _softmax_update(s, kn[:, :A_KV_LORA], m_ref, l_ref, acc_ref)
        o_ref[...] = (acc_ref[...] / l_ref[...]).astype(BF16)


def _mla_sample(q2, k2, cache_ckv, cache_kr, n_b, n_s):
    rows = n_s * A_HEADS
    n_past = cache_ckv.shape[1]
    return pl.pallas_call(
        _mla_sample_body,
        grid=(n_b, n_past // TKC),
        in_specs=[pl.BlockSpec((None, rows, 256), lambda b, j: (b, 0, 0)),
                  pl.BlockSpec((None, n_s, 256), lambda b, j: (b, 0, 0)),
                  pl.BlockSpec((None, TKC, A_KV_LORA), lambda b, j: (b, j, 0)),
                  pl.BlockSpec((None, TKC, A_ROPE), lambda b, j: (b, j, 0))],
        out_specs=pl.BlockSpec((None, rows, A_KV_LORA), lambda b, j: (b, 0, 0)),
        out_shape=jax.ShapeDtypeStruct((n_b, rows, A_KV_LORA), BF16),
        scratch_shapes=[pltpu.VMEM((rows, 1), F32), pltpu.VMEM((rows, 1), F32),
                        pltpu.VMEM((rows, A_KV_LORA), F32)],
        compiler_params=_params("parallel", "arbitrary"),
        name="mla_sample",
    )(q2.reshape(n_b, rows, 256), k2.reshape(n_b, n_s, 256), cache_ckv, cache_kr)


def _sb_prompt_body(q_ref, k_ref, v_ref, o_ref, tri_ref, acc_ref, car_ref):
    n_s = q_ref.shape[0]
    _fill_suffix_ones(tri_ref)

    def qblock(i, carry):
        q0 = pl.multiple_of(i * TQ, TQ)
        qs = _stack_pair(q_ref[pl.ds(q0, TQ), :])
        acc_ref[...] = jnp.zeros(acc_ref.shape, F32)
        car_ref[...] = jnp.zeros(car_ref.shape, F32)
        _stick_tile(qs, k_ref[pl.ds(q0, TK), :], v_ref[pl.ds(q0, TK), :], tri_ref[...], acc_ref, car_ref,
                    _local_causal(2 * TQ, TK, TQ, strict=True))

        def kv(jj, c2):
            k0 = pl.multiple_of((i - 1 - jj) * TK, TK)
            _stick_tile(qs, k_ref[pl.ds(k0, TK), :], v_ref[pl.ds(k0, TK), :], tri_ref[...], acc_ref,
                        car_ref, None)
            return c2

        lax.fori_loop(0, i, kv, 0)
        o_ref[pl.ds(q0, TQ), :] = _merge_pair(acc_ref[...], TQ)
        return carry

    lax.fori_loop(0, n_s // TQ, qblock, 0)


def _pair_seq_spec(n_s):
    return pl.BlockSpec((None, n_s, LANES), lambda b, p: (p, b, 0))


def _sb_prompt(q, k, v, n_b, n_s):
    assert TQ == TK
    spec = _pair_seq_spec(n_s)
    return pl.pallas_call(
        _sb_prompt_body,
        grid=(n_b, N_PAIRS),
        in_specs=[spec, spec, spec],
        out_specs=spec,
        out_shape=jax.ShapeDtypeStruct((N_PAIRS, n_b * n_s, LANES), F32),
        scratch_shapes=[pltpu.VMEM((TK, TK), BF16), pltpu.VMEM((2 * TQ, LANES), F32),
                        pltpu.VMEM((2 * TQ, 1), F32)],
        compiler_params=_params("parallel", "parallel"),
        name="sb_prompt",
    )(q, k, v)


def _sb_sample_body(q_ref, kn_ref, vn_ref, kc_ref, vc_ref, o_ref, tri_ref, acc_ref, car_ref):
    j = pl.program_id(2)
    n_q = q_ref.shape[0]
    qs = _stack_pair(q_ref[...])

    @pl.when(j == 0)
    def _():
        _fill_suffix_ones(tri_ref)
        acc_ref[...] = jnp.zeros(acc_ref.shape, F32)
        car_ref[...] = jnp.zeros(car_ref.shape, F32)
        _stick_tile(qs, kn_ref[...], vn_ref[...], tri_ref[:n_q, :n_q], acc_ref, car_ref,
                    _local_causal(2 * n_q, n_q, n_q, strict=True))

    @pl.when(j > 0)
    def _():
        for sub in reversed(range(TKC // TK)):
            k2 = kc_ref[sub * TK:(sub + 1) * TK, :].astype(BF16)
            v2 = vc_ref[sub * TK:(sub + 1) * TK, :].astype(BF16)
            _stick_tile(qs, k2, v2, tri_ref[...], acc_ref, car_ref, None)

    @pl.when(j == pl.num_programs(2) - 1)
    def _():
        o_ref[...] = _merge_pair(acc_ref[...], n_q)


def _sb_sample(q, kn, vn, cache_k, cache_v, n_b, n_s):
    n_past = cache_k.shape[1]
    n_blk = n_past // TKC
    new_spec = pl.BlockSpec((None, n_s, LANES), lambda b, p, j: (p, b, 0))
    cache_spec = pl.BlockSpec((None, TKC, LANES),
                              lambda b, p, j: (b, jnp.clip(n_blk - j, 0, n_blk - 1), p))
    return pl.pallas_call(
        _sb_sample_body,
        grid=(n_b, N_PAIRS, n_blk + 1),
        in_specs=[new_spec, new_spec, new_spec, cache_spec, cache_spec],
        out_specs=new_spec,
        out_shape=jax.ShapeDtypeStruct((N_PAIRS, n_b * n_s, LANES), F32),
        scratch_shapes=[pltpu.VMEM((TK, TK), BF16), pltpu.VMEM((2 * n_s, LANES), F32),
                        pltpu.VMEM((2 * n_s, 1), F32)],
        compiler_params=_params("parallel", "parallel", "arbitrary"),
        name="sb_sample",
    )(q, kn, vn, cache_k, cache_v)


def _band_block(qs, kwin, vwin, bias, valid_from, tq):
    s = _dot_nt(qs, kwin) + bias
    if valid_from is not None:
        col = lax.broadcasted_iota(jnp.int32, s.shape, 1)
        s = jnp.where(col >= valid_from, s, NEG)
    p = jnp.exp(s - jnp.max(s, axis=1, keepdims=True))
    o = _dot(p.astype(BF16), vwin) / jnp.sum(p, axis=1, keepdims=True)
    return _merge_pair(o, tq)


def _band_prompt_body(q_ref, k_ref, v_ref, bias_ref, o_ref, kpad_ref, vpad_ref):
    n_s = q_ref.shape[0]
    win = LEFT_CTX + BAND_TQ
    zeros = jnp.zeros((LEFT_CTX, LANES), BF16)
    kpad_ref[:LEFT_CTX, :] = zeros
    vpad_ref[:LEFT_CTX, :] = zeros
    kpad_ref[LEFT_CTX:, :] = k_ref[...]
    vpad_ref[LEFT_CTX:, :] = v_ref[...]

    def qblock(i, carry):
        q0 = pl.multiple_of(i * BAND_TQ, BAND_TQ)
        qs = _stack_pair(q_ref[pl.ds(q0, BAND_TQ), :])
        o_ref[pl.ds(q0, BAND_TQ), :] = _band_block(
            qs, kpad_ref[pl.ds(q0, win), :], vpad_ref[pl.ds(q0, win), :], bias_ref[...],
            LEFT_CTX - q0, BAND_TQ)
        return carry

    lax.fori_loop(0, n_s // BAND_TQ, qblock, 0)


def _band_prompt(q, k, v, bias, n_b, n_s):
    spec = _pair_seq_spec(n_s)
    win = LEFT_CTX + BAND_TQ
    return pl.pallas_call(
        _band_prompt_body,
        grid=(n_b, N_PAIRS),
        in_specs=[spec, spec, spec, pl.BlockSpec((None, 2 * BAND_TQ, win), lambda b, p: (p, 0, 0))],
        out_specs=spec,
        out_shape=jax.ShapeDtypeStruct((N_PAIRS, n_b * n_s, LANES), F32),
        scratch_shapes=[pltpu.VMEM((LEFT_CTX + n_s, LANES), BF16),
                        pltpu.VMEM((LEFT_CTX + n_s, LANES), BF16)],
        compiler_params=_params("parallel", "parallel"),
        name="band_prompt",
    )(q, k, v, bias)


def _band_sample_body(q_ref, kn_ref, vn_ref, kc_ref, vc_ref, bias_ref, o_ref, kwin_ref, vwin_ref):
    n_q = q_ref.shape[0]
    n_keep = kc_ref.shape[0]
    kwin_ref[:n_keep, :] = kc_ref[...].astype(BF16)
    vwin_ref[:n_keep, :] = vc_ref[...].astype(BF16)
    kwin_ref[n_keep:, :] = kn_ref[...]
    vwin_ref[n_keep:, :] = vn_ref[...]
    o_ref[...] = _band_block(_stack_pair(q_ref[...]), kwin_ref[...], vwin_ref[...], bias_ref[...],
                             None, n_q)


def _band_sample(q, kn, vn, cache_k, cache_v, bias, n_b, n_s):
    n_keep = cache_k.shape[1]
    new_spec = pl.BlockSpec((None, n_s, LANES), lambda b, p: (p, b, 0))
    cache_spec = pl.BlockSpec((None, n_keep, LANES), lambda b, p: (b, 0, p))
    return pl.pallas_call(
        _band_sample_body,
        grid=(n_b, N_PAIRS),
        in_specs=[new_spec, new_spec, new_spec, cache_spec, cache_spec,
                  pl.BlockSpec((None, 2 * n_s, n_keep + n_s), lambda b, p: (p, 0, 0))],
        out_specs=new_spec,
        out_shape=jax.ShapeDtypeStruct((N_PAIRS, n_b * n_s, LANES), F32),
        scratch_shapes=[pltpu.VMEM((n_keep + n_s, LANES), BF16), pltpu.VMEM((n_keep + n_s, LANES), BF16)],
        compiler_params=_params("parallel", "parallel"),
        name="band_sample",
    )(q, kn, vn, cache_k, cache_v, bias)


def _band_bias(rel_bias, tq):
    win = LEFT_CTX + tq
    i = np.arange(tq)[:, None]
    w = np.arange(win)[None, :]
    rel = np.clip(LEFT_CTX + i - w, -REL_CLIP, REL_CLIP) + REL_CLIP
    qc, kc = i // CHUNK, w // CHUNK - LEFT_CTX // CHUNK
    ok = (kc <= qc) & (kc >= qc - LEFT_CTX // CHUNK)
    tab = jnp.where(jnp.asarray(ok)[None], rel_bias.astype(F32)[:, jnp.asarray(rel)], NEG)
    return tab.reshape(N_PAIRS, 2 * tq, win)


def _fox_prompt_body(q_ref, k_ref, v_ref, cq_ref, ck_ref, o_ref, m_ref, l_ref, acc_ref):
    n_s = q_ref.shape[0]

    def qblock(i, carry):
        q0 = pl.multiple_of(i * TQ, TQ)
        qs = _stack_pair(q_ref[pl.ds(q0, TQ), :])
        cq2 = cq_ref[pl.ds(q0, TQ), :]
        cq = jnp.concatenate([cq2[:, 0:1], cq2[:, 1:2]], axis=0)
        _softmax_init(m_ref, l_ref, acc_ref)

        def tile(jb, mask):
            k0 = pl.multiple_of(jb * TK, TK)
            s = _dot_nt(qs, k_ref[pl.ds(k0, TK), :]) + _decay_bias(cq, ck_ref[jb], TQ, TK)
            if mask is not None:
                s = jnp.where(mask, s, NEG)
            _softmax_update(s, v_ref[pl.ds(k0, TK), :], m_ref, l_ref, acc_ref)

        tile(i, _local_causal(2 * TQ, TK, TQ, strict=False))

        def kv(jb, c2):
            tile(jb, None)
            return c2

        lax.fori_loop(0, i, kv, 0)
        o_ref[pl.ds(q0, TQ), :] = _merge_pair(acc_ref[...] / l_ref[...], TQ)
        return carry

    lax.fori_loop(0, n_s // TQ, qblock, 0)


def _fox_prompt(q, k, v, cum, n_b, n_s):
    assert TQ == TK
    spec = _pair_seq_spec(n_s)
    cq = cum.reshape(n_b, N_PAIRS, 2, n_s).transpose(0, 1, 3, 2)
    ck = cum.reshape(n_b, N_PAIRS, 2, n_s // TK, TK).transpose(0, 1, 3, 2, 4)
    return pl.pallas_call(
        _fox_prompt_body,
        grid=(n_b, N_PAIRS),
        in_specs=[spec, spec, spec,
                  pl.BlockSpec((None, None, n_s, 2), lambda b, p: (b, p, 0, 0)),
                  pl.BlockSpec((None, None, n_s // TK, 2, TK), lambda b, p: (b, p, 0, 0, 0))],
        out_specs=spec,
        out_shape=jax.ShapeDtypeStruct((N_PAIRS, n_b * n_s, LANES), F32),
        scratch_shapes=[pltpu.VMEM((2 * TQ, 1), F32), pltpu.VMEM((2 * TQ, 1), F32),
                        pltpu.VMEM((2 * TQ, LANES), F32)],
        compiler_params=_params("parallel", "parallel"),
        name="fox_prompt",
    )(q, k, v, cq, ck)


def _fox_sample_body(q_ref, kn_ref, vn_ref, kc_ref, vc_ref, cq_ref, ckn_ref, ckc_ref, o_ref,
                     m_ref, l_ref, acc_ref):
    j = pl.program_id(2)
    n_q = q_ref.shape[0]
    qs = _stack_pair(q_ref[...])
    cq2 = cq_ref[...]
    cq = jnp.concatenate([cq2[:, 0:1], cq2[:, 1:2]], axis=0)

    @pl.when(j == 0)
    def _():
        _softmax_init(m_ref, l_ref, acc_ref)
        s = _dot_nt(qs, kn_ref[...]) + _decay_bias(cq, ckn_ref[...], n_q, n_q)
        s = jnp.where(_local_causal(2 * n_q, n_q, n_q, strict=False), s, NEG)
        _softmax_update(s, vn_ref[...], m_ref, l_ref, acc_ref)

    @pl.when(j > 0)
    def _():
        for sub in range(TKC // TK):
            k2 = kc_ref[sub * TK:(sub + 1) * TK, :].astype(BF16)
            v2 = vc_ref[sub * TK:(sub + 1) * TK, :].astype(BF16)
            s = _dot_nt(qs, k2) + _decay_bias(cq, ckc_ref[sub], n_q, TK)
            _softmax_update(s, v2, m_ref, l_ref, acc_ref)

    @pl.when(j == pl.num_programs(2) - 1)
    def _():
        o_ref[...] = _merge_pair(acc_ref[...] / l_ref[...], n_q)


def _fox_sample(q, kn, vn, cache_k, cache_v, cum, n_b, n_s):
    n_past = cache_k.shape[1]
    n_blk = n_past // TKC
    n_sub = TKC // TK
    cum = cum.reshape(n_b, N_PAIRS, 2, cum.shape[-1])
    cq = cum[..., n_past:n_past + n_s].transpose(0, 1, 3, 2)
    ckn = cum[..., n_past:n_past + n_s]
    ckc = cum[..., :n_past].reshape(n_b, N_PAIRS, 2, n_past // TK, TK).transpose(0, 1, 3, 2, 4)
    new_spec = pl.BlockSpec((None, n_s, LANES), lambda b, p, j: (p, b, 0))
    cache_blk = lambda j: jnp.maximum(j - 1, 0)
    cache_spec = pl.BlockSpec((None, TKC, LANES), lambda b, p, j: (b, cache_blk(j), p))
    return pl.pallas_call(
        _fox_sample_body,
        grid=(n_b, N_PAIRS, n_blk + 1),
        in_specs=[new_spec, new_spec, new_spec, cache_spec, cache_spec,
                  pl.BlockSpec((None, None, n_s, 2), lambda b, p, j: (b, p, 0, 0)),
                  pl.BlockSpec((None, None, 2, n_s), lambda b, p, j: (b, p, 0, 0)),
                  pl.BlockSpec((None, None, n_sub, 2, TK), lambda b, p, j: (b, p, cache_blk(j), 0, 0))],
        out_specs=new_spec,
        out_shape=jax.ShapeDtypeStruct((N_PAIRS, n_b * n_s, LANES), F32),
        scratch_shapes=[pltpu.VMEM((2 * n_s, 1), F32), pltpu.VMEM((2 * n_s, 1), F32),
                        pltpu.VMEM((2 * n_s, LANES), F32)],
        compiler_params=_params("parallel", "parallel", "arbitrary"),
        name="fox_sample",
    )(q, kn, vn, cache_k, cache_v, cq, ckn, ckc)


def _cumsum_body(x_ref, o_ref):
    n_rows, n_cols = x_ref.shape
    r = lax.broadcasted_iota(jnp.int32, (LANES, LANES), 0)
    c = lax.broadcasted_iota(jnp.int32, (LANES, LANES), 1)
    ones = jnp.where(r <= c, 1.0, 0.0).astype(BF16)
    total = jnp.zeros((n_rows, 1), F32)
    for g in range(n_cols // LANES):
        x = x_ref[:, g * LANES:(g + 1) * LANES]
        h1 = x.astype(BF16)
        r1 = x - h1.astype(F32)
        h2 = r1.astype(BF16)
        h3 = (r1 - h2.astype(F32)).astype(BF16)
        y = _dot(h1, ones) + _dot(h2, ones) + _dot(h3, ones) + total
        o_ref[:, g * LANES:(g + 1) * LANES] = y
        total = y[:, LANES - 1:LANES]


def _cumsum_rows(x):
    rows, n = x.shape
    n_pad = -(-n // LANES) * LANES
    xp = jnp.pad(x, ((0, 0), (0, n_pad - n)))
    out = pl.pallas_call(
        _cumsum_body,
        out_shape=jax.ShapeDtypeStruct((rows, n_pad), F32),
        compiler_params=pltpu.CompilerParams(vmem_limit_bytes=VMEM_LIMIT),
        name="cumsum_rows",
    )(xp)
    return out[:, :n]


def _split_cols(w, sizes):
    out, off = [], 0
    for n in sizes:
        out.append(w[:, off:off + n])
        off += n
    return out


def _rope_tables(pos, n_rows):
    half = A_ROPE // 2
    inv_freq = ROPE_THETA ** (-jnp.arange(half, dtype=F32) / half)
    ang = pos.astype(F32)[:, None] * inv_freq[None, :]
    cos, sin = jnp.cos(ang), jnp.sin(ang)
    zeros = jnp.zeros((pos.shape[0], LANES - A_ROPE), F32)
    cos_t = jnp.concatenate([cos, cos, zeros], axis=1)
    sin_t = jnp.concatenate([-sin, sin, zeros], axis=1)
    reps = max(1, n_rows // pos.shape[0])
    return jnp.tile(cos_t, (reps, 1)), jnp.tile(sin_t, (reps, 1))


def _swap_halves(w):
    half = w.shape[-1] // 2
    return jnp.concatenate([w[..., half:], w[..., :half]], axis=-1)


def _pad_lanes(w):
    return jnp.pad(w, [(0, 0)] * (w.ndim - 1) + [(0, LANES - w.shape[-1])])


def _prep_even(w_in, q_norm, w_uq, kv_norm, w_uk, w_uv):
    wqa, wkv, wkr, wga, wqb, wkb, wvb, wgb = _split_cols(
        w_in, (A_Q_LORA, A_KV_LORA, A_ROPE, WIDTH, WIDTH, WIDTH, WIDTH, WIDTH))
    b = lambda a: a.astype(BF16)
    uq_rope = w_uq[:, :, A_NOPE:]
    uk_t = jnp.transpose(w_uk, (1, 2, 0))
    z = jnp.zeros((A_NOPE, A_KV_LORA), w_uk.dtype)
    wuk = jnp.stack([jnp.block([[uk_t[2 * p], z], [z, uk_t[2 * p + 1]]]) for p in range(N_PAIRS)])
    uv_t = jnp.transpose(w_uv, (1, 0, 2))
    zv = jnp.zeros((A_KV_LORA, A_V), w_uv.dtype)
    wuv = jnp.stack([jnp.block([[uv_t[2 * p], zv], [zv, uv_t[2 * p + 1]]]) for p in range(N_PAIRS)])
    return dict(
        wqa=b(wqa), wkv=b(wkv), wkr=b(_pad_lanes(wkr)), wkrs=b(_pad_lanes(_swap_halves(wkr))),
        wga=b(wga), wqb=b(wqb), wkb=b(wkb), wvb=b(wvb), wgb=b(wgb),
        qn=q_norm.reshape(1, -1), kvn=kv_norm.reshape(1, -1),
        wuqn=b(w_uq[:, :, :A_NOPE].reshape(A_Q_LORA, A_HEADS * A_NOPE)),
        wuqr=b(_pad_lanes(uq_rope).reshape(A_Q_LORA, A_HEADS * LANES)),
        wuqrs=b(_pad_lanes(_swap_halves(uq_rope)).reshape(A_Q_LORA, A_HEADS * LANES)),
        wuk=b(wuk), wuv=b(wuv))


def _prep_odd(w_in, forget_bias):
    wqc, wkc, wvc, wgc, wqd, wkd, wvd, wf, wgd = _split_cols(
        w_in, (WIDTH, WIDTH, WIDTH, WIDTH, WIDTH, WIDTH, WIDTH, HEADS, WIDTH))
    b = lambda a: a.astype(BF16)
    return dict(wqc=b(wqc), wkc=b(wkc), wvc=b(wvc), wgc=b(wgc), wqd=b(wqd), wkd=b(wkd), wvd=b(wvd),
                wf=b(_pad_lanes(wf)), wgd=b(wgd), fb=forget_bias.astype(F32).reshape(1, HEADS))


def kernel(x_prompt, x_sample, cache_mla_ckv, cache_mla_krope, cache_sb_k, cache_sb_v, cache_band_k,
           cache_band_v, cache_fox_k, cache_fox_v, cache_fox_logf, norm_pre, norm_post, w_in_even,
           a_q_norm, a_w_uq, a_kv_norm, a_w_uk, a_w_uv, w_out_even, w_in_odd, c_rel_bias,
           d_forget_bias, w_out_odd):
    n_b, n_s, _ = x_prompt.shape
    d_b, d_s, _ = x_sample.shape
    n_past = cache_sb_k.shape[2]
    n_keep = cache_band_k.shape[2]
    assert n_s % (2 * TQ) == 0 and n_past % TKC == 0 and d_s == CHUNK and n_past % CHUNK == 0
    assert n_keep == LEFT_CTX and (d_b * d_s) % TM == 0 and TM % d_s == 0

    xp = x_prompt.reshape(n_b * n_s, D_MODEL)
    xs = x_sample.reshape(d_b * d_s, D_MODEL)
    row = lambda a: a.reshape(1, -1)
    heads = lambda a, b, s: a.reshape(b, s, HEADS, HEAD_DIM)
    flat = lambda a: a.reshape(a.shape[0], a.shape[1], WIDTH)

    we = _prep_even(w_in_even[0], a_q_norm[0], a_w_uq[0], a_kv_norm[0], a_w_uk[0], a_w_uv[0])
    wout_e = w_out_even[0].astype(BF16)
    cos_p, sin_p = _rope_tables(jnp.arange(n_s), TM)
    cos_s, sin_s = _rope_tables(n_past + jnp.arange(d_s), TM)

    (ckv_p, kr_p, k2_p, q2_p, ga_p, gb_p, qb_p, kb_p, kb16_p, vb_p, vb16_p) = _in_even(
        xp, row(norm_pre[0]), cos_p, sin_p, we)
    (ckv_s, kr_s, k2_s, q2_s, ga_s, gb_s, qb_s, kb_s, kb16_s, vb_s, vb16_s) = _in_even(
        xs, row(norm_pre[0]), cos_s, sin_s, we)

    lat_p = _mla_prompt(q2_p, k2_p, n_b, n_s).reshape(n_b * n_s, A_HEADS * A_KV_LORA)
    lat_s = _mla_sample(q2_s, k2_s, cache_mla_ckv[0], cache_mla_krope[0], d_b, d_s
                        ).reshape(d_b * d_s, A_HEADS * A_KV_LORA)
    sb_p = _sb_prompt(qb_p, kb16_p, vb16_p, n_b, n_s)
    sb_s = _sb_sample(qb_s, kb16_s, vb16_s, flat(cache_sb_k[0]), flat(cache_sb_v[0]), d_b, d_s)

    xp1 = _out_proj(xp, row(norm_post[0]), ga_p, gb_p, lat_p, sb_p, wout_e, we['wuv'])
    xs1 = _out_proj(xs, row(norm_post[0]), ga_s, gb_s, lat_s, sb_s, wout_e, we['wuv'])

    wo = _prep_odd(w_in_odd[0], d_forget_bias[0])
    wout_o = w_out_odd[0].astype(BF16)
    (qc_p, kc_p, kc16_p, vc_p, vc16_p, gc_p, qd_p, kd_p, kd16_p, vd_p, vd16_p, lf_p, gd_p) = _in_odd(
        xp1, row(norm_pre[1]), wo)
    (qc_s, kc_s, kc16_s, vc_s, vc16_s, gc_s, qd_s, kd_s, kd16_s, vd_s, vd16_s, lf_s, gd_s) = _in_odd(
        xs1, row(norm_pre[1]), wo)

    band_p = _band_prompt(qc_p, kc16_p, vc16_p, _band_bias(c_rel_bias[0], BAND_TQ), n_b, n_s)
    band_s = _band_sample(qc_s, kc16_s, vc16_s, flat(cache_band_k[0]), flat(cache_band_v[0]),
                          _band_bias(c_rel_bias[0], d_s), d_b, d_s)

    lf_p3 = lf_p.reshape(n_b, n_s, HEADS)
    lf_s3 = lf_s.reshape(d_b, d_s, HEADS)
    cum_p = _cumsum_rows(lf_p3.transpose(0, 2, 1).reshape(n_b * HEADS, n_s)).reshape(n_b, HEADS, n_s)
    lf_all = jnp.concatenate([cache_fox_logf[0].astype(F32), lf_s3], axis=1)
    cum_s = _cumsum_rows(lf_all.transpose(0, 2, 1).reshape(d_b * HEADS, n_past + d_s)
                         ).reshape(d_b, HEADS, n_past + d_s)
    fox_p = _fox_prompt(qd_p, kd16_p, vd16_p, cum_p, n_b, n_s)
    fox_s = _fox_sample(qd_s, kd16_s, vd16_s, flat(cache_fox_k[0]), flat(cache_fox_v[0]), cum_s, d_b, d_s)

    xp2 = _out_proj(xp1, row(norm_post[1]), gc_p, gd_p, band_p, fox_p, wout_o)
    xs2 = _out_proj(xs1, row(norm_post[1]), gc_s, gd_s, band_s, fox_s, wout_o)

    keep = min(LEFT_CTX, n_s)
    kc_p4, vc_p4 = heads(kc_p, n_b, n_s), heads(vc_p, n_b, n_s)
    band_k_s = jnp.concatenate([cache_band_k[0], heads(kc_s, d_b, d_s)], axis=1)[:, d_s:]
    band_v_s = jnp.concatenate([cache_band_v[0], heads(vc_s, d_b, d_s)], axis=1)[:, d_s:]
    one = lambda a: a[None]
    return (xp2.reshape(n_b, n_s, D_MODEL), xs2.reshape(d_b, d_s, D_MODEL),
            one(ckv_p.reshape(n_b, n_s, A_KV_LORA)), one(kr_p.reshape(n_b, n_s, A_ROPE)),
            one(heads(kb_p, n_b, n_s)), one(heads(vb_p, n_b, n_s)),
            one(kc_p4[:, n_s - keep:]), one(vc_p4[:, n_s - keep:]),
            one(heads(kd_p, n_b, n_s)), one(heads(vd_p, n_b, n_s)), one(lf_p3),
            one(ckv_s.reshape(d_b, d_s, A_KV_LORA)), one(kr_s.reshape(d_b, d_s, A_ROPE)),
            one(heads(kb_s, d_b, d_s)), one(heads(vb_s, d_b, d_s)),
            one(band_k_s), one(band_v_s),
            one(heads(kd_s, d_b, d_s)), one(heads(vd_s, d_b, d_s)), one(lf_s3))
```

```python
import functools

import numpy as np
import jax
import jax.numpy as jnp
from jax import lax
from jax.experimental import pallas as pl
from jax.experimental.pallas import tpu as pltpu

F32 = jnp.float32
BF16 = jnp.bfloat16

D_MODEL = 1024
PAST_LEN = 4096
CHUNK = 64
LEFT_CTX = 512
REL_CLIP = 128
EPS = 1e-6
NEG = -1e30
ROPE_THETA = 10000.0
A_HEADS = 8
A_Q_LORA = 256
A_KV_LORA = 128
A_NOPE = 64
A_ROPE = 32
A_V = 64
A_SCALE = (A_NOPE + A_ROPE) ** -0.5
HEADS = 8
HEAD_DIM = 64
WIDTH = HEADS * HEAD_DIM
QK_SCALE = HEAD_DIM ** -0.5
N_PAIRS = HEADS // 2

LANES = 128
VMEM_LIMIT = 52 * 1024 * 1024
TM = 256
TQ = 256
TK = 256
TKC = 1024
BAND_TQ = 128
MLA_RT = 256
STICK_DEAD = -104.0


def _dot(a, b):
    return jnp.dot(a, b, preferred_element_type=F32)


def _dot_nt(a, b):
    return lax.dot_general(a, b, (((1,), (1,)), ((), ())), preferred_element_type=F32)


def _rms(x, g):
    y = x * lax.rsqrt(jnp.mean(x * x, axis=-1, keepdims=True) + EPS)
    return y * g


def _log_sigmoid(z):
    return jnp.minimum(z, 0.0) - jnp.log1p(jnp.exp(-jnp.abs(z)))


def _silu(g):
    return g / (1.0 + jnp.exp(-g))


def _stack_pair(q2):
    qf = q2.astype(F32)
    lane = lax.broadcasted_iota(jnp.int32, qf.shape, 1)
    even = jnp.where(lane < HEAD_DIM, qf, 0.0)
    odd = jnp.where(lane >= HEAD_DIM, qf, 0.0)
    return jnp.concatenate([even, odd], axis=0).astype(BF16)


def _merge_pair(o, tq):
    top, bot = o[:tq], o[tq:]
    lane = lax.broadcasted_iota(jnp.int32, top.shape, 1)
    return jnp.where(lane < HEAD_DIM, top, bot)


def _params(*sem):
    return pltpu.CompilerParams(dimension_semantics=sem, vmem_limit_bytes=VMEM_LIMIT)


def _const_spec(shape):
    nd = len(shape)
    return pl.BlockSpec(shape, lambda *_: (0,) * nd)


def _in_even_body(x_ref, gpre_ref, cos_ref, sin_ref, wqa_ref, wkv_ref, wkr_ref, wkrs_ref, wga_ref,
                  wqb_ref, wkb_ref, wvb_ref, wgb_ref, qn_ref, kvn_ref, wuqn_ref, wuqr_ref,
                  wuqrs_ref, wuk_ref,
                  ckv_ref, krope_ref, k2_ref, q2_ref, ga_ref, gb_ref, qb_ref, kb_ref, kb16_ref,
                  vb_ref, vb16_ref):
    h = _rms(x_ref[...], gpre_ref[...]).astype(BF16)
    cos = cos_ref[...]
    sin = sin_ref[...]
    ckv = _rms(_dot(h, wkv_ref[...]), kvn_ref[...])
    ckv_ref[...] = ckv
    kr = _dot(h, wkr_ref[...]) * cos + _dot(h, wkrs_ref[...]) * sin
    krope_ref[...] = kr[:, :A_ROPE]
    k2_ref[:, :LANES] = ckv.astype(BF16)
    k2_ref[:, LANES:] = kr.astype(BF16)
    cq = _rms(_dot(h, wqa_ref[...]), qn_ref[...]).astype(BF16)
    qn = _dot(cq, wuqn_ref[...]).astype(BF16)
    for p in range(N_PAIRS):
        ql = _dot(qn[:, p * LANES:(p + 1) * LANES], wuk_ref[p])
        q2_ref[:, (2 * p) * 256:(2 * p) * 256 + LANES] = ql[:, :LANES].astype(BF16)
        q2_ref[:, (2 * p + 1) * 256:(2 * p + 1) * 256 + LANES] = ql[:, LANES:].astype(BF16)
    qr = _dot(cq, wuqr_ref[...])
    qrs = _dot(cq, wuqrs_ref[...])
    for hd in range(A_HEADS):
        rot = qr[:, hd * LANES:(hd + 1) * LANES] * cos + qrs[:, hd * LANES:(hd + 1) * LANES] * sin
        q2_ref[:, hd * 256 + LANES:(hd + 1) * 256] = rot.astype(BF16)
    ga_ref[...] = _dot(h, wga_ref[...])
    gb_ref[...] = _dot(h, wgb_ref[...])
    qb = _dot(h, wqb_ref[...]) * QK_SCALE
    kb = _dot(h, wkb_ref[...])
    vb = _dot(h, wvb_ref[...])
    kb_ref[...] = kb
    vb_ref[...] = vb
    for p in range(N_PAIRS):
        sl = slice(p * LANES, (p + 1) * LANES)
        qb_ref[p] = qb[:, sl].astype(BF16)
        kb16_ref[p] = kb[:, sl].astype(BF16)
        vb16_ref[p] = vb[:, sl].astype(BF16)


def _in_even(x, gpre, cos, sin, w):
    rows = x.shape[0]
    n_tab = cos.shape[0] // TM
    row_spec = lambda n: pl.BlockSpec((TM, n), lambda i: (i, 0))
    pm_spec = pl.BlockSpec((N_PAIRS, TM, LANES), lambda i: (0, i, 0))
    tab_spec = pl.BlockSpec((TM, LANES), lambda i: (i % n_tab, 0))
    weights = [w['wqa'], w['wkv'], w['wkr'], w['wkrs'], w['wga'], w['wqb'], w['wkb'], w['wvb'], w['wgb'],
               w['qn'], w['kvn'], w['wuqn'], w['wuqr'], w['wuqrs'], w['wuk']]
    sds = jax.ShapeDtypeStruct
    pm = sds((N_PAIRS, rows, LANES), BF16)
    return pl.pallas_call(
        _in_even_body,
        grid=(rows // TM,),
        in_specs=[row_spec(D_MODEL), _const_spec((1, D_MODEL)), tab_spec, tab_spec]
                 + [_const_spec(a.shape) for a in weights],
        out_specs=[row_spec(A_KV_LORA), row_spec(A_ROPE), row_spec(256), row_spec(A_HEADS * 256),
                   row_spec(WIDTH), row_spec(WIDTH), pm_spec, row_spec(WIDTH), pm_spec,
                   row_spec(WIDTH), pm_spec],
        out_shape=[sds((rows, A_KV_LORA), F32), sds((rows, A_ROPE), F32), sds((rows, 256), BF16),
                   sds((rows, A_HEADS * 256), BF16), sds((rows, WIDTH), F32), sds((rows, WIDTH), F32),
                   pm, sds((rows, WIDTH), F32), pm, sds((rows, WIDTH), F32), pm],
        compiler_params=_params("parallel"),
        name="in_proj_even",
    )(x, gpre, cos, sin, *weights)


def _in_odd_body(x_ref, gpre_ref, fb_ref, wqc_ref, wkc_ref, wvc_ref, wgc_ref, wqd_ref, wkd_ref,
                 wvd_ref, wf_ref, wgd_ref,
                 qc_ref, kc_ref, kc16_ref, vc_ref, vc16_ref, gc_ref, qd_ref, kd_ref, kd16_ref,
                 vd_ref, vd16_ref, logf_ref, gd_ref):
    h = _rms(x_ref[...], gpre_ref[...]).astype(BF16)
    gc_ref[...] = _dot(h, wgc_ref[...])
    gd_ref[...] = _dot(h, wgd_ref[...])
    f = _dot(h, wf_ref[...])[:, :HEADS] + fb_ref[...]
    logf_ref[...] = _log_sigmoid(f)
    for q_w, k_w, v_w, q_o, k_o, k16_o, v_o, v16_o in (
            (wqc_ref, wkc_ref, wvc_ref, qc_ref, kc_ref, kc16_ref, vc_ref, vc16_ref),
            (wqd_ref, wkd_ref, wvd_ref, qd_ref, kd_ref, kd16_ref, vd_ref, vd16_ref)):
        q = _dot(h, q_w[...]) * QK_SCALE
        k = _dot(h, k_w[...])
        v = _dot(h, v_w[...])
        k_o[...] = k
        v_o[...] = v
        for p in range(N_PAIRS):
            sl = slice(p * LANES, (p + 1) * LANES)
            q_o[p] = q[:, sl].astype(BF16)
            k16_o[p] = k[:, sl].astype(BF16)
            v16_o[p] = v[:, sl].astype(BF16)


def _in_odd(x, gpre, w):
    rows = x.shape[0]
    row_spec = lambda n: pl.BlockSpec((TM, n), lambda i: (i, 0))
    pm_spec = pl.BlockSpec((N_PAIRS, TM, LANES), lambda i: (0, i, 0))
    weights = [w['wqc'], w['wkc'], w['wvc'], w['wgc'], w['wqd'], w['wkd'], w['wvd'], w['wf'], w['wgd']]
    sds = jax.ShapeDtypeStruct
    pm = sds((N_PAIRS, rows, LANES), BF16)
    full = sds((rows, WIDTH), F32)
    return pl.pallas_call(
        _in_odd_body,
        grid=(rows // TM,),
        in_specs=[row_spec(D_MODEL), _const_spec((1, D_MODEL)), _const_spec((1, HEADS))]
                 + [_const_spec(a.shape) for a in weights],
        out_specs=[pm_spec, row_spec(WIDTH), pm_spec, row_spec(WIDTH), pm_spec, row_spec(WIDTH),
                   pm_spec, row_spec(WIDTH), pm_spec, row_spec(WIDTH), pm_spec, row_spec(HEADS),
                   row_spec(WIDTH)],
        out_shape=[pm, full, pm, full, pm, full, pm, full, pm, full, pm, sds((rows, HEADS), F32), full],
        compiler_params=_params("parallel"),
        name="in_proj_odd",
    )(x, gpre, w['fb'], *weights)


def _out_body(*refs, mla):
    if mla:
        x_ref, gpost_ref, g1_ref, g2_ref, a_ref, b_ref, wuv_ref, wout_ref, o_ref, mix_ref = refs
    else:
        x_ref, gpost_ref, g1_ref, g2_ref, a_ref, b_ref, wout_ref, o_ref, mix_ref = refs
    s1 = _silu(g1_ref[...])
    s2 = _silu(g2_ref[...])
    for p in range(N_PAIRS):
        sl = slice(p * LANES, (p + 1) * LANES)
        if mla:
            a = _dot(a_ref[:, p * 256:(p + 1) * 256], wuv_ref[p])
        else:
            a = a_ref[p]
        mix_ref[:, sl] = (s1[:, sl] * a).astype(BF16)
        mix_ref[:, WIDTH + p * LANES:WIDTH + (p + 1) * LANES] = (s2[:, sl] * b_ref[p]).astype(BF16)
    y = _dot(mix_ref[...], wout_ref[...])
    o_ref[...] = x_ref[...] + _rms(y, gpost_ref[...])


def _out_proj(x, gpost, g1, g2, a, b, wout, wuv=None):
    rows = x.shape[0]
    mla = wuv is not None
    row_spec = lambda n: pl.BlockSpec((TM, n), lambda i: (i, 0))
    pm_spec = pl.BlockSpec((N_PAIRS, TM, LANES), lambda i: (0, i, 0))
    in_specs = [row_spec(D_MODEL), _const_spec((1, D_MODEL)), row_spec(WIDTH), row_spec(WIDTH),
                row_spec(A_HEADS * A_KV_LORA) if mla else pm_spec, pm_spec]
    args = [x, gpost, g1, g2, a, b]
    if mla:
        in_specs.append(_const_spec(wuv.shape))
        args.append(wuv)
    in_specs.append(_const_spec(wout.shape))
    args.append(wout)
    return pl.pallas_call(
        functools.partial(_out_body, mla=mla),
        grid=(rows // TM,),
        in_specs=in_specs,
        out_specs=row_spec(D_MODEL),
        out_shape=jax.ShapeDtypeStruct((rows, D_MODEL), F32),
        scratch_shapes=[pltpu.VMEM((TM, 2 * WIDTH), BF16)],
        compiler_params=_params("parallel"),
        name="out_proj_even" if mla else "out_proj_odd",
    )(*args)


def _softmax_init(m_ref, accl_ref):
    m_ref[...] = jnp.full(m_ref.shape, NEG, F32)
    accl_ref[...] = jnp.zeros(accl_ref.shape, F32)


def _lanes(x, n):
    parts = [x] * (n // LANES)
    if n % LANES:
        parts.append(x[:, :n % LANES])
    return parts[0] if len(parts) == 1 else jnp.concatenate(parts, axis=1)


def _softmax_update(s, v, m_ref, accl_ref):
    keys = s.shape[1]
    m_prev = m_ref[...]
    m_new = jnp.maximum(m_prev, jnp.max(s, axis=1, keepdims=True))
    alpha = jnp.exp(m_prev - m_new)
    p = jnp.exp(s - _lanes(m_new, keys))
    v1 = jnp.concatenate([v, jnp.ones((keys, LANES), BF16)], axis=1)
    accl_ref[...] = _lanes(alpha, 2 * LANES) * accl_ref[...] + _dot(p.astype(BF16), v1)
    m_ref[...] = m_new


def _softmax_result(accl_ref):
    accl = accl_ref[...]
    return accl[:, :LANES] / accl[:, LANES:]


def _fill_suffix_ones(tri_ref):
    n = tri_ref.shape[0]
    r = lax.broadcasted_iota(jnp.int32, (n, n), 0)
    c = lax.broadcasted_iota(jnp.int32, (n, n), 1)
    tri_ref[...] = jnp.where(r > c, 1.0, 0.0).astype(BF16)


def _stick_tile(qs, k2, v2, tri, acc_ref, car_ref, mask):
    z = _dot_nt(qs, k2)
    lb = _log_sigmoid(z)
    l1 = lb - z
    if mask is not None:
        l1 = jnp.where(mask, l1, 0.0)
    hi = l1.astype(BF16)
    lo = (l1 - hi.astype(F32)).astype(BF16)
    suf = _dot(hi, tri) + _dot(lo, tri) + car_ref[...]
    w = jnp.exp(lb + suf)
    if mask is not None:
        w = jnp.where(mask, w, 0.0)
    acc_ref[...] += _dot(w.astype(BF16), v2)
    car_ref[...] += jnp.sum(l1, axis=1, keepdims=True)


def _stick_alive(car_ref):
    return (jnp.max(car_ref[...]) >= STICK_DEAD).astype(jnp.int32)


def _local_causal(rows, keys, tq, strict):
    r = lax.broadcasted_iota(jnp.int32, (rows, keys), 0) & (tq - 1)
    c = lax.broadcasted_iota(jnp.int32, (rows, keys), 1)
    return (c < r) if strict else (c <= r)


def _decay_tile(qs, k2, v2, cq, ck, causal, m_ref, accl_ref):
    tq, keys = cq.shape[0], ck.shape[1]
    s = _dot_nt(qs, k2)
    for hh in range(2):
        rows = slice(hh * tq, (hh + 1) * tq)
        sh = s[rows] + (cq[:, hh:hh + 1] - ck[hh:hh + 1])
        if causal:
            sh = jnp.where(_local_causal(tq, keys, tq, strict=False), sh, NEG)
        _softmax_update(sh, v2, m_ref.at[rows], accl_ref.at[rows])


def _mla_prompt_body(q_ref, k_ref, o_ref, m_ref, accl_ref):
    g = pl.program_id(1)
    rows = CHUNK * A_HEADS
    grp = 4 * rows

    def update(r0, n_rows, k):
        sl = pl.ds(r0, n_rows)
        s = _dot_nt(q_ref[sl, :], k) * A_SCALE
        _softmax_update(s, k[:, :A_KV_LORA], m_ref.at[sl], accl_ref.at[sl])

    def group(gl, carry):
        gq = g * 2 + gl
        g0 = pl.multiple_of(gl * grp, grp)
        _softmax_init(m_ref.at[pl.ds(g0, grp)], accl_ref.at[pl.ds(g0, grp)])

        def kv(j, c2):
            k = k_ref[pl.ds(pl.multiple_of(j * TK, TK), TK), :]
            for t in range(grp // MLA_RT):
                update(g0 + t * MLA_RT, MLA_RT, k)
            return c2

        lax.fori_loop(0, gq, kv, 0)
        kd = k_ref[pl.ds(pl.multiple_of(gq * TK, TK), TK), :]
        for ci in range(4):
            for t in range(rows // MLA_RT):
                update(g0 + ci * rows + t * MLA_RT, MLA_RT, kd[:CHUNK * (ci + 1)])
        o_ref[pl.ds(g0, grp), :] = _softmax_result(accl_ref.at[pl.ds(g0, grp)]).astype(BF16)
        return carry

    lax.fori_loop(0, 2, group, 0)


def _mla_prompt(q2, k2, n_b, n_s):
    rows_q = n_s * A_HEADS
    qg = rows_q // 4
    assert qg == 8 * CHUNK * A_HEADS and TK == 4 * CHUNK
    return pl.pallas_call(
        _mla_prompt_body,
        grid=(n_b, 4),
        in_specs=[pl.BlockSpec((None, qg, 256), lambda b, g: (b, g, 0)),
                  pl.BlockSpec((None, n_s, 256), lambda b, g: (b, 0, 0))],
        out_specs=pl.BlockSpec((None, qg, A_KV_LORA), lambda b, g: (b, g, 0)),
        out_shape=jax.ShapeDtypeStruct((n_b, rows_q, A_KV_LORA), BF16),
        scratch_shapes=[pltpu.VMEM((qg, LANES), F32), pltpu.VMEM((qg, 2 * LANES), F32)],
        compiler_params=_params("parallel", "parallel"),
        name="mla_prompt",
    )(q2.reshape(n_b, rows_q, 256), k2.reshape(n_b, n_s, 256))


def _mla_sample_body(q_ref, kn_ref, ckv_ref, kr_ref, o_ref, m_ref, accl_ref):
    j = pl.program_id(1)

    @pl.when(j == 0)
    def _():
        _softmax_init(m_ref, accl_ref)

    n_tiles = q_ref.shape[0] // MLA_RT
    ck = ckv_ref[...].astype(BF16)
    kr = kr_ref[...].astype(BF16)
    for t in range(n_tiles):
        sl = slice(t * MLA_RT, (t + 1) * MLA_RT)
        s = (_dot_nt(q_ref[sl, :A_KV_LORA], ck)
             + _dot_nt(q_ref[sl, A_KV_LORA:A_KV_LORA + A_ROPE], kr)) * A_SCALE
        _softmax_update(s, ck, m_ref.at[sl], accl_ref.at[sl])

    @pl.when(j == pl.num_programs(1) - 1)
    def _():
        kn = kn_ref[...]
        for t in range(n_tiles):
            sl = slice(t * MLA_RT, (t + 1) * MLA_RT)
            s = _dot_nt(q_ref[sl, :], kn) * A_SCALE
            _softmax_update(s, kn[:, :A_KV_LORA], m_ref.at[sl], accl_ref.at[sl])
        o_ref[...] = _softmax_result(accl_ref).astype(BF16)


def _mla_sample(q2, k2, cache_ckv, cache_kr, n_b, n_s):
    rows = n_s * A_HEADS
    n_past = cache_ckv.shape[1]
    return pl.pallas_call(
        _mla_sample_body,
        grid=(n_b, n_past // TKC),
        in_specs=[pl.BlockSpec((None, rows, 256), lambda b, j: (b, 0, 0)),
                  pl.BlockSpec((None, n_s, 256), lambda b, j: (b, 0, 0)),
                  pl.BlockSpec((None, TKC, A_KV_LORA), lambda b, j: (b, j, 0)),
                  pl.BlockSpec((None, TKC, A_ROPE), lambda b, j: (b, j, 0))],
        out_specs=pl.BlockSpec((None, rows, A_KV_LORA), lambda b, j: (b, 0, 0)),
        out_shape=jax.ShapeDtypeStruct((n_b, rows, A_KV_LORA), BF16),
        scratch_shapes=[pltpu.VMEM((rows, LANES), F32), pltpu.VMEM((rows, 2 * LANES), F32)],
        compiler_params=_params("parallel", "arbitrary"),
        name="mla_sample",
    )(q2.reshape(n_b, rows, 256), k2.reshape(n_b, n_s, 256), cache_ckv, cache_kr)


def _sb_prompt_body(q_ref, k_ref, v_ref, o_ref, tri_ref, acc_ref, car_ref):
    n_s = q_ref.shape[0]
    _fill_suffix_ones(tri_ref)

    def qblock(i, carry):
        q0 = pl.multiple_of(i * TQ, TQ)
        qs = _stack_pair(q_ref[pl.ds(q0, TQ), :])
        acc_ref[...] = jnp.zeros(acc_ref.shape, F32)
        car_ref[...] = jnp.zeros(car_ref.shape, F32)
        _stick_tile(qs, k_ref[pl.ds(q0, TK), :], v_ref[pl.ds(q0, TK), :], tri_ref[...], acc_ref, car_ref,
                    _local_causal(2 * TQ, TK, TQ, strict=True))

        def kv(state):
            jj, _ = state
            k0 = pl.multiple_of((i - 1 - jj) * TK, TK)
            _stick_tile(qs, k_ref[pl.ds(k0, TK), :], v_ref[pl.ds(k0, TK), :], tri_ref[...], acc_ref,
                        car_ref, None)
            return jj + 1, _stick_alive(car_ref)

        lax.while_loop(lambda st: (st[0] < i) & (st[1] > 0), kv, (jnp.int32(0), _stick_alive(car_ref)))
        o_ref[pl.ds(q0, TQ), :] = _merge_pair(acc_ref[...], TQ)
        return carry

    lax.fori_loop(0, n_s // TQ, qblock, 0)


def _pair_seq_spec(n_s):
    return pl.BlockSpec((None, n_s, LANES), lambda b, p: (p, b, 0))


def _sb_prompt(q, k, v, n_b, n_s):
    assert TQ == TK
    spec = _pair_seq_spec(n_s)
    return pl.pallas_call(
        _sb_prompt_body,
        grid=(n_b, N_PAIRS),
        in_specs=[spec, spec, spec],
        out_specs=spec,
        out_shape=jax.ShapeDtypeStruct((N_PAIRS, n_b * n_s, LANES), F32),
        scratch_shapes=[pltpu.VMEM((TK, TK), BF16), pltpu.VMEM((2 * TQ, LANES), F32),
                        pltpu.VMEM((2 * TQ, 1), F32)],
        compiler_params=_params("parallel", "parallel"),
        name="sb_prompt",
    )(q, k, v)


def _sb_sample_body(q_ref, kn_ref, vn_ref, kc_ref, vc_ref, o_ref, tri_ref, acc_ref, car_ref):
    j = pl.program_id(2)
    n_q = q_ref.shape[0]
    qs = _stack_pair(q_ref[...])

    @pl.when(j == 0)
    def _():
        _fill_suffix_ones(tri_ref)
        acc_ref[...] = jnp.zeros(acc_ref.shape, F32)
        car_ref[...] = jnp.zeros(car_ref.shape, F32)
        _stick_tile(qs, kn_ref[...], vn_ref[...], tri_ref[:n_q, :n_q], acc_ref, car_ref,
                    _local_causal(2 * n_q, n_q, n_q, strict=True))

    @pl.when(j > 0)
    def _():
        for sub in reversed(range(TKC // TK)):
            @pl.when(_stick_alive(car_ref) > 0)
            def _(sub=sub):
                k2 = kc_ref[sub * TK:(sub + 1) * TK, :].astype(BF16)
                v2 = vc_ref[sub * TK:(sub + 1) * TK, :].astype(BF16)
                _stick_tile(qs, k2, v2, tri_ref[...], acc_ref, car_ref, None)

    @pl.when(j == pl.num_programs(2) - 1)
    def _():
        o_ref[...] = _merge_pair(acc_ref[...], n_q)


def _sb_sample(q, kn, vn, cache_k, cache_v, n_b, n_s):
    n_past = cache_k.shape[1]
    n_blk = n_past // TKC
    new_spec = pl.BlockSpec((None, n_s, LANES), lambda b, p, j: (p, b, 0))
    cache_spec = pl.BlockSpec((None, TKC, LANES),
                              lambda b, p, j: (b, jnp.clip(n_blk - j, 0, n_blk - 1), p))
    return pl.pallas_call(
        _sb_sample_body,
        grid=(n_b, N_PAIRS, n_blk + 1),
        in_specs=[new_spec, new_spec, new_spec, cache_spec, cache_spec],
        out_specs=new_spec,
        out_shape=jax.ShapeDtypeStruct((N_PAIRS, n_b * n_s, LANES), F32),
        scratch_shapes=[pltpu.VMEM((TK, TK), BF16), pltpu.VMEM((2 * n_s, LANES), F32),
                        pltpu.VMEM((2 * n_s, 1), F32)],
        compiler_params=_params("parallel", "parallel", "arbitrary"),
        name="sb_sample",
    )(q, kn, vn, cache_k, cache_v)


def _band_block(qs, kwin, vwin, bias, valid_from, tq):
    s = _dot_nt(qs, kwin) + bias
    if valid_from is not None:
        col = lax.broadcasted_iota(jnp.int32, s.shape, 1)
        s = jnp.where(col >= valid_from, s, NEG)
    p = jnp.exp(s - jnp.max(s, axis=1, keepdims=True))
    o = _dot(p.astype(BF16), vwin) / jnp.sum(p, axis=1, keepdims=True)
    return _merge_pair(o, tq)


def _band_prompt_body(q_ref, k_ref, v_ref, bias_ref, o_ref, kpad_ref, vpad_ref):
    n_s = q_ref.shape[0]
    win = LEFT_CTX + BAND_TQ
    zeros = jnp.zeros((LEFT_CTX, LANES), BF16)
    kpad_ref[:LEFT_CTX, :] = zeros
    vpad_ref[:LEFT_CTX, :] = zeros
    kpad_ref[LEFT_CTX:, :] = k_ref[...]
    vpad_ref[LEFT_CTX:, :] = v_ref[...]

    def qblock(i, carry):
        q0 = pl.multiple_of(i * BAND_TQ, BAND_TQ)
        qs = _stack_pair(q_ref[pl.ds(q0, BAND_TQ), :])
        o_ref[pl.ds(q0, BAND_TQ), :] = _band_block(
            qs, kpad_ref[pl.ds(q0, win), :], vpad_ref[pl.ds(q0, win), :], bias_ref[...],
            LEFT_CTX - q0, BAND_TQ)
        return carry

    lax.fori_loop(0, n_s // BAND_TQ, qblock, 0)


def _band_prompt(q, k, v, bias, n_b, n_s):
    spec = _pair_seq_spec(n_s)
    win = LEFT_CTX + BAND_TQ
    return pl.pallas_call(
        _band_prompt_body,
        grid=(n_b, N_PAIRS),
        in_specs=[spec, spec, spec, pl.BlockSpec((None, 2 * BAND_TQ, win), lambda b, p: (p, 0, 0))],
        out_specs=spec,
        out_shape=jax.ShapeDtypeStruct((N_PAIRS, n_b * n_s, LANES), F32),
        scratch_shapes=[pltpu.VMEM((LEFT_CTX + n_s, LANES), BF16),
                        pltpu.VMEM((LEFT_CTX + n_s, LANES), BF16)],
        compiler_params=_params("parallel", "parallel"),
        name="band_prompt",
    )(q, k, v, bias)


def _band_sample_body(q_ref, kn_ref, vn_ref, kc_ref, vc_ref, bias_ref, o_ref, kwin_ref, vwin_ref):
    n_q = q_ref.shape[0]
    n_keep = kc_ref.shape[0]
    kwin_ref[:n_keep, :] = kc_ref[...].astype(BF16)
    vwin_ref[:n_keep, :] = vc_ref[...].astype(BF16)
    kwin_ref[n_keep:, :] = kn_ref[...]
    vwin_ref[n_keep:, :] = vn_ref[...]
    o_ref[...] = _band_block(_stack_pair(q_ref[...]), kwin_ref[...], vwin_ref[...], bias_ref[...],
                             None, n_q)


def _band_sample(q, kn, vn, cache_k, cache_v, bias, n_b, n_s):
    n_keep = cache_k.shape[1]
    new_spec = pl.BlockSpec((None, n_s, LANES), lambda b, p: (p, b, 0))
    cache_spec = pl.BlockSpec((None, n_keep, LANES), lambda b, p: (b, 0, p))
    return pl.pallas_call(
        _band_sample_body,
        grid=(n_b, N_PAIRS),
        in_specs=[new_spec, new_spec, new_spec, cache_spec, cache_spec,
                  pl.BlockSpec((None, 2 * n_s, n_keep + n_s), lambda b, p: (p, 0, 0))],
        out_specs=new_spec,
        out_shape=jax.ShapeDtypeStruct((N_PAIRS, n_b * n_s, LANES), F32),
        scratch_shapes=[pltpu.VMEM((n_keep + n_s, LANES), BF16), pltpu.VMEM((n_keep + n_s, LANES), BF16)],
        compiler_params=_params("parallel", "parallel"),
        name="band_sample",
    )(q, kn, vn, cache_k, cache_v, bias)


def _band_bias(rel_bias, tq):
    win = LEFT_CTX + tq
    i = np.arange(tq)[:, None]
    w = np.arange(win)[None, :]
    qc, kc = i // CHUNK, w // CHUNK - LEFT_CTX // CHUNK
    ok = (kc <= qc) & (kc >= qc - LEFT_CTX // CHUNK)
    u = np.arange(win + tq - 1)
    rel = np.clip(LEFT_CTX + (tq - 1) - u, -REL_CLIP, REL_CLIP) + REL_CLIP
    diag = rel_bias.astype(F32)[:, rel]
    tab = jnp.stack([diag[:, tq - 1 - r:tq - 1 - r + win] for r in range(tq)], axis=1)
    tab = jnp.where(jnp.asarray(ok)[None], tab, NEG)
    return tab.reshape(N_PAIRS, 2 * tq, win)


def _fox_prompt_body(q_ref, k_ref, v_ref, cq_ref, ck_ref, o_ref, m_ref, accl_ref):
    n_s = q_ref.shape[0]

    def qblock(i, carry):
        q0 = pl.multiple_of(i * TQ, TQ)
        qs = _stack_pair(q_ref[pl.ds(q0, TQ), :])
        cq = cq_ref[pl.ds(q0, TQ), :]
        _softmax_init(m_ref, accl_ref)

        def tile(jb, causal):
            k0 = pl.multiple_of(jb * TK, TK)
            _decay_tile(qs, k_ref[pl.ds(k0, TK), :], v_ref[pl.ds(k0, TK), :], cq, ck_ref[jb], causal,
                        m_ref, accl_ref)

        tile(i, True)

        def kv(jb, c2):
            tile(jb, False)
            return c2

        lax.fori_loop(0, i, kv, 0)
        o_ref[pl.ds(q0, TQ), :] = _merge_pair(_softmax_result(accl_ref), TQ)
        return carry

    lax.fori_loop(0, n_s // TQ, qblock, 0)


def _fox_prompt(q, k, v, cum, n_b, n_s):
    assert TQ == TK
    spec = _pair_seq_spec(n_s)
    cq = cum.reshape(n_b, N_PAIRS, 2, n_s).transpose(0, 1, 3, 2)
    ck = cum.reshape(n_b, N_PAIRS, 2, n_s // TK, TK).transpose(0, 1, 3, 2, 4)
    return pl.pallas_call(
        _fox_prompt_body,
        grid=(n_b, N_PAIRS),
        in_specs=[spec, spec, spec,
                  pl.BlockSpec((None, None, n_s, 2), lambda b, p: (b, p, 0, 0)),
                  pl.BlockSpec((None, None, n_s // TK, 2, TK), lambda b, p: (b, p, 0, 0, 0))],
        out_specs=spec,
        out_shape=jax.ShapeDtypeStruct((N_PAIRS, n_b * n_s, LANES), F32),
        scratch_shapes=[pltpu.VMEM((2 * TQ, LANES), F32), pltpu.VMEM((2 * TQ, 2 * LANES), F32)],
        compiler_params=_params("parallel", "parallel"),
        name="fox_prompt",
    )(q, k, v, cq, ck)


def _fox_sample_body(q_ref, kn_ref, vn_ref, kc_ref, vc_ref, cq_ref, ckn_ref, ckc_ref, o_ref,
                     m_ref, accl_ref):
    j = pl.program_id(2)
    n_q = q_ref.shape[0]
    qs = _stack_pair(q_ref[...])
    cq = cq_ref[...]

    @pl.when(j == 0)
    def _():
        _softmax_init(m_ref, accl_ref)
        _decay_tile(qs, kn_ref[...], vn_ref[...], cq, ckn_ref[...], True, m_ref, accl_ref)

    @pl.when(j > 0)
    def _():
        _decay_tile(qs, kc_ref[...].astype(BF16), vc_ref[...].astype(BF16), cq, ckc_ref[...], False,
                    m_ref, accl_ref)

    @pl.when(j == pl.num_programs(2) - 1)
    def _():
        o_ref[...] = _merge_pair(_softmax_result(accl_ref), n_q)


def _fox_sample(q, kn, vn, cache_k, cache_v, cum, n_b, n_s):
    n_past = cache_k.shape[1]
    n_blk = n_past // TKC
    cum = cum.reshape(n_b, N_PAIRS, 2, cum.shape[-1])
    cq = cum[..., n_past:n_past + n_s].transpose(0, 1, 3, 2)
    ckn = cum[..., n_past:n_past + n_s]
    ckc = cum[..., :n_past]
    new_spec = pl.BlockSpec((None, n_s, LANES), lambda b, p, j: (p, b, 0))
    cache_blk = lambda j: jnp.maximum(j - 1, 0)
    cache_spec = pl.BlockSpec((None, TKC, LANES), lambda b, p, j: (b, cache_blk(j), p))
    return pl.pallas_call(
        _fox_sample_body,
        grid=(n_b, N_PAIRS, n_blk + 1),
        in_specs=[new_spec, new_spec, new_spec, cache_spec, cache_spec,
                  pl.BlockSpec((None, None, n_s, 2), lambda b, p, j: (b, p, 0, 0)),
                  pl.BlockSpec((None, None, 2, n_s), lambda b, p, j: (b, p, 0, 0)),
                  pl.BlockSpec((None, None, 2, TKC), lambda b, p, j: (b, p, 0, cache_blk(j)))],
        out_specs=new_spec,
        out_shape=jax.ShapeDtypeStruct((N_PAIRS, n_b * n_s, LANES), F32),
        scratch_shapes=[pltpu.VMEM((2 * n_s, LANES), F32), pltpu.VMEM((2 * n_s, 2 * LANES), F32)],
        compiler_params=_params("parallel", "parallel", "arbitrary"),
        name="fox_sample",
    )(q, kn, vn, cache_k, cache_v, cq, ckn, ckc)


def _cumsum_body(x_ref, o_ref):
    n_rows, n_cols = x_ref.shape
    r = lax.broadcasted_iota(jnp.int32, (LANES, LANES), 0)
    c = lax.broadcasted_iota(jnp.int32, (LANES, LANES), 1)
    ones = jnp.where(r <= c, 1.0, 0.0).astype(BF16)
    total = jnp.zeros((n_rows, 1), F32)
    for g in range(n_cols // LANES):
        x = x_ref[:, g * LANES:(g + 1) * LANES]
        h1 = x.astype(BF16)
        r1 = x - h1.astype(F32)
        h2 = r1.astype(BF16)
        h3 = (r1 - h2.astype(F32)).astype(BF16)
        y = _dot(h1, ones) + _dot(h2, ones) + _dot(h3, ones) + total
        o_ref[:, g * LANES:(g + 1) * LANES] = y
        total = y[:, LANES - 1:LANES]


def _cumsum_rows(x):
    rows, n = x.shape
    n_pad = -(-n // LANES) * LANES
    xp = jnp.pad(x, ((0, 0), (0, n_pad - n)))
    out = pl.pallas_call(
        _cumsum_body,
        out_shape=jax.ShapeDtypeStruct((rows, n_pad), F32),
        compiler_params=pltpu.CompilerParams(vmem_limit_bytes=VMEM_LIMIT),
        name="cumsum_rows",
    )(xp)
    return out[:, :n]


def _split_cols(w, sizes):
    out, off = [], 0
    for n in sizes:
        out.append(w[:, off:off + n])
        off += n
    return out


def _rope_tables(pos, n_rows):
    half = A_ROPE // 2
    inv_freq = ROPE_THETA ** (-jnp.arange(half, dtype=F32) / half)
    ang = pos.astype(F32)[:, None] * inv_freq[None, :]
    cos, sin = jnp.cos(ang), jnp.sin(ang)
    zeros = jnp.zeros((pos.shape[0], LANES - A_ROPE), F32)
    cos_t = jnp.concatenate([cos, cos, zeros], axis=1)
    sin_t = jnp.concatenate([-sin, sin, zeros], axis=1)
    reps = max(1, n_rows // pos.shape[0])
    return jnp.tile(cos_t, (reps, 1)), jnp.tile(sin_t, (reps, 1))


def _swap_halves(w):
    half = w.shape[-1] // 2
    return jnp.concatenate([w[..., half:], w[..., :half]], axis=-1)


def _pad_lanes(w):
    return jnp.pad(w, [(0, 0)] * (w.ndim - 1) + [(0, LANES - w.shape[-1])])


def _prep_even(w_in, q_norm, w_uq, kv_norm, w_uk, w_uv):
    wqa, wkv, wkr, wga, wqb, wkb, wvb, wgb = _split_cols(
        w_in, (A_Q_LORA, A_KV_LORA, A_ROPE, WIDTH, WIDTH, WIDTH, WIDTH, WIDTH))
    b = lambda a: a.astype(BF16)
    uq_rope = w_uq[:, :, A_NOPE:]
    uk_t = jnp.transpose(w_uk, (1, 2, 0))
    z = jnp.zeros((A_NOPE, A_KV_LORA), w_uk.dtype)
    wuk = jnp.stack([jnp.block([[uk_t[2 * p], z], [z, uk_t[2 * p + 1]]]) for p in range(N_PAIRS)])
    uv_t = jnp.transpose(w_uv, (1, 0, 2))
    zv = jnp.zeros((A_KV_LORA, A_V), w_uv.dtype)
    wuv = jnp.stack([jnp.block([[uv_t[2 * p], zv], [zv, uv_t[2 * p + 1]]]) for p in range(N_PAIRS)])
    return dict(
        wqa=b(wqa), wkv=b(wkv), wkr=b(_pad_lanes(wkr)), wkrs=b(_pad_lanes(_swap_halves(wkr))),
        wga=b(wga), wqb=b(wqb), wkb=b(wkb), wvb=b(wvb), wgb=b(wgb),
        qn=q_norm.reshape(1, -1), kvn=kv_norm.reshape(1, -1),
        wuqn=b(w_uq[:, :, :A_NOPE].reshape(A_Q_LORA, A_HEADS * A_NOPE)),
        wuqr=b(_pad_lanes(uq_rope).reshape(A_Q_LORA, A_HEADS * LANES)),
        wuqrs=b(_pad_lanes(_swap_halves(uq_rope)).reshape(A_Q_LORA, A_HEADS * LANES)),
        wuk=b(wuk), wuv=b(wuv))


def _prep_odd(w_in, forget_bias):
    wqc, wkc, wvc, wgc, wqd, wkd, wvd, wf, wgd = _split_cols(
        w_in, (WIDTH, WIDTH, WIDTH, WIDTH, WIDTH, WIDTH, WIDTH, HEADS, WIDTH))
    b = lambda a: a.astype(BF16)
    return dict(wqc=b(wqc), wkc=b(wkc), wvc=b(wvc), wgc=b(wgc), wqd=b(wqd), wkd=b(wkd), wvd=b(wvd),
                wf=b(_pad_lanes(wf)), wgd=b(wgd), fb=forget_bias.astype(F32).reshape(1, HEADS))


def kernel(x_prompt, x_sample, cache_mla_ckv, cache_mla_krope, cache_sb_k, cache_sb_v, cache_band_k,
           cache_band_v, cache_fox_k, cache_fox_v, cache_fox_logf, norm_pre, norm_post, w_in_even,
           a_q_norm, a_w_uq, a_kv_norm, a_w_uk, a_w_uv, w_out_even, w_in_odd, c_rel_bias,
           d_forget_bias, w_out_odd):
    n_b, n_s, _ = x_prompt.shape
    d_b, d_s, _ = x_sample.shape
    n_past = cache_sb_k.shape[2]
    n_keep = cache_band_k.shape[2]
    assert n_s % (2 * TQ) == 0 and n_past % TKC == 0 and d_s == CHUNK and n_past % CHUNK == 0
    assert n_keep == LEFT_CTX and (d_b * d_s) % TM == 0 and TM % d_s == 0

    xp = x_prompt.reshape(n_b * n_s, D_MODEL)
    xs = x_sample.reshape(d_b * d_s, D_MODEL)
    row = lambda a: a.reshape(1, -1)
    heads = lambda a, b, s: a.reshape(b, s, HEADS, HEAD_DIM)
    flat = lambda a: a.reshape(a.shape[0], a.shape[1], WIDTH)

    we = _prep_even(w_in_even[0], a_q_norm[0], a_w_uq[0], a_kv_norm[0], a_w_uk[0], a_w_uv[0])
    wout_e = w_out_even[0].astype(BF16)
    cos_p, sin_p = _rope_tables(jnp.arange(n_s), TM)
    cos_s, sin_s = _rope_tables(n_past + jnp.arange(d_s), TM)

    (ckv_p, kr_p, k2_p, q2_p, ga_p, gb_p, qb_p, kb_p, kb16_p, vb_p, vb16_p) = _in_even(
        xp, row(norm_pre[0]), cos_p, sin_p, we)
    (ckv_s, kr_s, k2_s, q2_s, ga_s, gb_s, qb_s, kb_s, kb16_s, vb_s, vb16_s) = _in_even(
        xs, row(norm_pre[0]), cos_s, sin_s, we)

    lat_p = _mla_prompt(q2_p, k2_p, n_b, n_s).reshape(n_b * n_s, A_HEADS * A_KV_LORA)
    lat_s = _mla_sample(q2_s, k2_s, cache_mla_ckv[0], cache_mla_krope[0], d_b, d_s
                        ).reshape(d_b * d_s, A_HEADS * A_KV_LORA)
    sb_p = _sb_prompt(qb_p, kb16_p, vb16_p, n_b, n_s)
    sb_s = _sb_sample(qb_s, kb16_s, vb16_s, flat(cache_sb_k[0]), flat(cache_sb_v[0]), d_b, d_s)

    xp1 = _out_proj(xp, row(norm_post[0]), ga_p, gb_p, lat_p, sb_p, wout_e, we['wuv'])
    xs1 = _out_proj(xs, row(norm_post[0]), ga_s, gb_s, lat_s, sb_s, wout_e, we['wuv'])

    wo = _prep_odd(w_in_odd[0], d_forget_bias[0])
    wout_o = w_out_odd[0].astype(BF16)
    (qc_p, kc_p, kc16_p, vc_p, vc16_p, gc_p, qd_p, kd_p, kd16_p, vd_p, vd16_p, lf_p, gd_p) = _in_odd(
        xp1, row(norm_pre[1]), wo)
    (qc_s, kc_s, kc16_s, vc_s, vc16_s, gc_s, qd_s, kd_s, kd16_s, vd_s, vd16_s, lf_s, gd_s) = _in_odd(
        xs1, row(norm_pre[1]), wo)

    band_p = _band_prompt(qc_p, kc16_p, vc16_p, _band_bias(c_rel_bias[0], BAND_TQ), n_b, n_s)
    band_s = _band_sample(qc_s, kc16_s, vc16_s, flat(cache_band_k[0]), flat(cache_band_v[0]),
                          _band_bias(c_rel_bias[0], d_s), d_b, d_s)

    lf_p3 = lf_p.reshape(n_b, n_s, HEADS)
    lf_s3 = lf_s.reshape(d_b, d_s, HEADS)
    cum_p = _cumsum_rows(lf_p3.transpose(0, 2, 1).reshape(n_b * HEADS, n_s)).reshape(n_b, HEADS, n_s)
    lf_all = jnp.concatenate([cache_fox_logf[0].astype(F32), lf_s3], axis=1)
    cum_s = _cumsum_rows(lf_all.transpose(0, 2, 1).reshape(d_b * HEADS, n_past + d_s)
                         ).reshape(d_b, HEADS, n_past + d_s)
    fox_p = _fox_prompt(qd_p, kd16_p, vd16_p, cum_p, n_b, n_s)
    fox_s = _fox_sample(qd_s, kd16_s, vd16_s, flat(cache_fox_k[0]), flat(cache_fox_v[0]), cum_s, d_b, d_s)

    xp2 = _out_proj(xp1, row(norm_post[1]), gc_p, gd_p, band_p, fox_p, wout_o)
    xs2 = _out_proj(xs1, row(norm_post[1]), gc_s, gd_s, band_s, fox_s, wout_o)

    keep = min(LEFT_CTX, n_s)
    kc_p4, vc_p4 = heads(kc_p, n_b, n_s), heads(vc_p, n_b, n_s)
    band_k_s = jnp.concatenate([cache_band_k[0], heads(kc_s, d_b, d_s)], axis=1)[:, d_s:]
    band_v_s = jnp.concatenate([cache_band_v[0], heads(vc_s, d_b, d_s)], axis=1)[:, d_s:]
    one = lambda a: a[None]
    return (xp2.reshape(n_b, n_s, D_MODEL), xs2.reshape(d_b, d_s, D_MODEL),
            one(ckv_p.reshape(n_b, n_s, A_KV_LORA)), one(kr_p.reshape(n_b, n_s, A_ROPE)),
            one(heads(kb_p, n_b, n_s)), one(heads(vb_p, n_b, n_s)),
            one(kc_p4[:, n_s - keep:]), one(vc_p4[:, n_s - keep:]),
            one(heads(kd_p, n_b, n_s)), one(heads(vd_p, n_b, n_s)), one(lf_p3),
            one(ckv_s.reshape(d_b, d_s, A_KV_LORA)), one(kr_s.reshape(d_b, d_s, A_ROPE)),
            one(heads(kb_s, d_b, d_s)), one(heads(vb_s, d_b, d_s)),
            one(band_k_s), one(band_v_s),
            one(heads(kd_s, d_b, d_s)), one(heads(vd_s, d_b, d_s)), one(lf_s3))
```

```python
import functools

import numpy as np
import jax
import jax.numpy as jnp
from jax import lax
from jax.experimental import pallas as pl
from jax.experimental.pallas import tpu as pltpu

F32 = jnp.float32
BF16 = jnp.bfloat16

D_MODEL = 1024
PAST_LEN = 4096
CHUNK = 64
LEFT_CTX = 512
REL_CLIP = 128
EPS = 1e-6
NEG = -1e30
ROPE_THETA = 10000.0
A_HEADS = 8
A_Q_LORA = 256
A_KV_LORA = 128
A_NOPE = 64
A_ROPE = 32
A_V = 64
A_SCALE = (A_NOPE + A_ROPE) ** -0.5
HEADS = 8
HEAD_DIM = 64
WIDTH = HEADS * HEAD_DIM
QK_SCALE = HEAD_DIM ** -0.5
N_PAIRS = HEADS // 2

LANES = 128
VMEM_LIMIT = 52 * 1024 * 1024
TM = 256
TQ = 256
TK = 256
TKC = 1024
BAND_TQ = 128
MLA_RT = 256
PAIRS_PER_STEP = 4
STICK_DEAD = -104.0


def _dot(a, b):
    return jnp.dot(a, b, preferred_element_type=F32)


def _dot_nt(a, b):
    return lax.dot_general(a, b, (((1,), (1,)), ((), ())), preferred_element_type=F32)


def _rms(x, g):
    y = x * lax.rsqrt(jnp.mean(x * x, axis=-1, keepdims=True) + EPS)
    return y * g


def _log_sigmoid(z):
    return jnp.minimum(z, 0.0) - jnp.log1p(jnp.exp(-jnp.abs(z)))


def _silu(g):
    return g / (1.0 + jnp.exp(-g))


def _stack_pair(q2):
    qf = q2.astype(F32)
    lane = lax.broadcasted_iota(jnp.int32, qf.shape, 1)
    even = jnp.where(lane < HEAD_DIM, qf, 0.0)
    odd = jnp.where(lane >= HEAD_DIM, qf, 0.0)
    return jnp.concatenate([even, odd], axis=0).astype(BF16)


def _merge_pair(o, tq):
    top, bot = o[:tq], o[tq:]
    lane = lax.broadcasted_iota(jnp.int32, top.shape, 1)
    return jnp.where(lane < HEAD_DIM, top, bot)


def _params(*sem):
    return pltpu.CompilerParams(dimension_semantics=sem, vmem_limit_bytes=VMEM_LIMIT)


def _const_spec(shape):
    nd = len(shape)
    return pl.BlockSpec(shape, lambda *_: (0,) * nd)


def _in_even_body(x_ref, gpre_ref, cos_ref, sin_ref, wqa_ref, wkv_ref, wkr_ref, wkrs_ref, wga_ref,
                  wqb_ref, wkb_ref, wvb_ref, wgb_ref, qn_ref, kvn_ref, wuqn_ref, wuqr_ref,
                  wuqrs_ref, wuk_ref,
                  ckv_ref, krope_ref, k2_ref, q2_ref, ga_ref, gb_ref, qb_ref, kb_ref, kb16_ref,
                  vb_ref, vb16_ref):
    h = _rms(x_ref[...], gpre_ref[...]).astype(BF16)
    cos = cos_ref[...]
    sin = sin_ref[...]
    ckv = _rms(_dot(h, wkv_ref[...]), kvn_ref[...])
    ckv_ref[...] = ckv
    kr = _dot(h, wkr_ref[...]) * cos + _dot(h, wkrs_ref[...]) * sin
    krope_ref[...] = kr[:, :A_ROPE]
    k2_ref[:, :LANES] = ckv.astype(BF16)
    k2_ref[:, LANES:] = kr.astype(BF16)
    cq = _rms(_dot(h, wqa_ref[...]), qn_ref[...]).astype(BF16)
    qn = _dot(cq, wuqn_ref[...]).astype(BF16)
    for p in range(N_PAIRS):
        ql = _dot(qn[:, p * LANES:(p + 1) * LANES], wuk_ref[p])
        q2_ref[:, (2 * p) * 256:(2 * p) * 256 + LANES] = ql[:, :LANES].astype(BF16)
        q2_ref[:, (2 * p + 1) * 256:(2 * p + 1) * 256 + LANES] = ql[:, LANES:].astype(BF16)
    qr = _dot(cq, wuqr_ref[...])
    qrs = _dot(cq, wuqrs_ref[...])
    for hd in range(A_HEADS):
        rot = qr[:, hd * LANES:(hd + 1) * LANES] * cos + qrs[:, hd * LANES:(hd + 1) * LANES] * sin
        q2_ref[:, hd * 256 + LANES:(hd + 1) * 256] = rot.astype(BF16)
    ga_ref[...] = _dot(h, wga_ref[...])
    gb_ref[...] = _dot(h, wgb_ref[...])
    qb = _dot(h, wqb_ref[...]) * QK_SCALE
    kb = _dot(h, wkb_ref[...])
    vb = _dot(h, wvb_ref[...])
    kb_ref[...] = kb
    vb_ref[...] = vb
    for p in range(N_PAIRS):
        sl = slice(p * LANES, (p + 1) * LANES)
        qb_ref[p] = qb[:, sl].astype(BF16)
        kb16_ref[p] = kb[:, sl].astype(BF16)
        vb16_ref[p] = vb[:, sl].astype(BF16)


def _in_even(x, gpre, cos, sin, w):
    rows = x.shape[0]
    n_tab = cos.shape[0] // TM
    row_spec = lambda n: pl.BlockSpec((TM, n), lambda i: (i, 0))
    pm_spec = pl.BlockSpec((N_PAIRS, TM, LANES), lambda i: (0, i, 0))
    tab_spec = pl.BlockSpec((TM, LANES), lambda i: (i % n_tab, 0))
    weights = [w['wqa'], w['wkv'], w['wkr'], w['wkrs'], w['wga'], w['wqb'], w['wkb'], w['wvb'], w['wgb'],
               w['qn'], w['kvn'], w['wuqn'], w['wuqr'], w['wuqrs'], w['wuk']]
    sds = jax.ShapeDtypeStruct
    pm = sds((N_PAIRS, rows, LANES), BF16)
    return pl.pallas_call(
        _in_even_body,
        grid=(rows // TM,),
        in_specs=[row_spec(D_MODEL), _const_spec((1, D_MODEL)), tab_spec, tab_spec]
                 + [_const_spec(a.shape) for a in weights],
        out_specs=[row_spec(A_KV_LORA), row_spec(A_ROPE), row_spec(256), row_spec(A_HEADS * 256),
                   row_spec(WIDTH), row_spec(WIDTH), pm_spec, row_spec(WIDTH), pm_spec,
                   row_spec(WIDTH), pm_spec],
        out_shape=[sds((rows, A_KV_LORA), F32), sds((rows, A_ROPE), F32), sds((rows, 256), BF16),
                   sds((rows, A_HEADS * 256), BF16), sds((rows, WIDTH), F32), sds((rows, WIDTH), F32),
                   pm, sds((rows, WIDTH), F32), pm, sds((rows, WIDTH), F32), pm],
        compiler_params=_params("parallel"),
        name="in_proj_even",
    )(x, gpre, cos, sin, *weights)


def _in_odd_body(x_ref, gpre_ref, fb_ref, wqc_ref, wkc_ref, wvc_ref, wgc_ref, wqd_ref, wkd_ref,
                 wvd_ref, wf_ref, wgd_ref,
                 qc_ref, kc_ref, kc16_ref, vc_ref, vc16_ref, gc_ref, qd_ref, kd_ref, kd16_ref,
                 vd_ref, vd16_ref, logf_ref, gd_ref):
    h = _rms(x_ref[...], gpre_ref[...]).astype(BF16)
    gc_ref[...] = _dot(h, wgc_ref[...])
    gd_ref[...] = _dot(h, wgd_ref[...])
    f = _dot(h, wf_ref[...])[:, :HEADS] + fb_ref[...]
    logf_ref[...] = _log_sigmoid(f)
    for q_w, k_w, v_w, q_o, k_o, k16_o, v_o, v16_o in (
            (wqc_ref, wkc_ref, wvc_ref, qc_ref, kc_ref, kc16_ref, vc_ref, vc16_ref),
            (wqd_ref, wkd_ref, wvd_ref, qd_ref, kd_ref, kd16_ref, vd_ref, vd16_ref)):
        q = _dot(h, q_w[...]) * QK_SCALE
        k = _dot(h, k_w[...])
        v = _dot(h, v_w[...])
        k_o[...] = k
        v_o[...] = v
        for p in range(N_PAIRS):
            sl = slice(p * LANES, (p + 1) * LANES)
            q_o[p] = q[:, sl].astype(BF16)
            k16_o[p] = k[:, sl].astype(BF16)
            v16_o[p] = v[:, sl].astype(BF16)


def _in_odd(x, gpre, w):
    rows = x.shape[0]
    row_spec = lambda n: pl.BlockSpec((TM, n), lambda i: (i, 0))
    pm_spec = pl.BlockSpec((N_PAIRS, TM, LANES), lambda i: (0, i, 0))
    weights = [w['wqc'], w['wkc'], w['wvc'], w['wgc'], w['wqd'], w['wkd'], w['wvd'], w['wf'], w['wgd']]
    sds = jax.ShapeDtypeStruct
    pm = sds((N_PAIRS, rows, LANES), BF16)
    full = sds((rows, WIDTH), F32)
    return pl.pallas_call(
        _in_odd_body,
        grid=(rows // TM,),
        in_specs=[row_spec(D_MODEL), _const_spec((1, D_MODEL)), _const_spec((1, HEADS))]
                 + [_const_spec(a.shape) for a in weights],
        out_specs=[pm_spec, row_spec(WIDTH), pm_spec, row_spec(WIDTH), pm_spec, row_spec(WIDTH),
                   pm_spec, row_spec(WIDTH), pm_spec, row_spec(WIDTH), pm_spec, row_spec(HEADS),
                   row_spec(WIDTH)],
        out_shape=[pm, full, pm, full, pm, full, pm, full, pm, full, pm, sds((rows, HEADS), F32), full],
        compiler_params=_params("parallel"),
        name="in_proj_odd",
    )(x, gpre, w['fb'], *weights)


def _out_body(*refs, mla):
    if mla:
        x_ref, gpost_ref, g1_ref, g2_ref, a_ref, b_ref, wuv_ref, wout_ref, o_ref, mix_ref = refs
    else:
        x_ref, gpost_ref, g1_ref, g2_ref, a_ref, b_ref, wout_ref, o_ref, mix_ref = refs
    s1 = _silu(g1_ref[...])
    s2 = _silu(g2_ref[...])
    for p in range(N_PAIRS):
        sl = slice(p * LANES, (p + 1) * LANES)
        if mla:
            a = _dot(a_ref[:, p * 256:(p + 1) * 256], wuv_ref[p])
        else:
            a = a_ref[p]
        mix_ref[:, sl] = (s1[:, sl] * a).astype(BF16)
        mix_ref[:, WIDTH + p * LANES:WIDTH + (p + 1) * LANES] = (s2[:, sl] * b_ref[p]).astype(BF16)
    y = _dot(mix_ref[...], wout_ref[...])
    o_ref[...] = x_ref[...] + _rms(y, gpost_ref[...])


def _out_proj(x, gpost, g1, g2, a, b, wout, wuv=None):
    rows = x.shape[0]
    mla = wuv is not None
    row_spec = lambda n: pl.BlockSpec((TM, n), lambda i: (i, 0))
    pm_spec = pl.BlockSpec((N_PAIRS, TM, LANES), lambda i: (0, i, 0))
    in_specs = [row_spec(D_MODEL), _const_spec((1, D_MODEL)), row_spec(WIDTH), row_spec(WIDTH),
                row_spec(A_HEADS * A_KV_LORA) if mla else pm_spec, pm_spec]
    args = [x, gpost, g1, g2, a, b]
    if mla:
        in_specs.append(_const_spec(wuv.shape))
        args.append(wuv)
    in_specs.append(_const_spec(wout.shape))
    args.append(wout)
    return pl.pallas_call(
        functools.partial(_out_body, mla=mla),
        grid=(rows // TM,),
        in_specs=in_specs,
        out_specs=row_spec(D_MODEL),
        out_shape=jax.ShapeDtypeStruct((rows, D_MODEL), F32),
        scratch_shapes=[pltpu.VMEM((TM, 2 * WIDTH), BF16)],
        compiler_params=_params("parallel"),
        name="out_proj_even" if mla else "out_proj_odd",
    )(*args)


def _softmax_init(m_ref, accl_ref):
    m_ref[...] = jnp.full(m_ref.shape, NEG, F32)
    accl_ref[...] = jnp.zeros(accl_ref.shape, F32)


def _lanes(x, n):
    parts = [x] * (n // LANES)
    if n % LANES:
        parts.append(x[:, :n % LANES])
    return parts[0] if len(parts) == 1 else jnp.concatenate(parts, axis=1)


def _softmax_update(s, v, m_ref, accl_ref):
    keys, n = s.shape[1], v.shape[1]
    m_prev = m_ref[...]
    m_new = jnp.maximum(m_prev, jnp.max(s, axis=1, keepdims=True))
    alpha = jnp.exp(m_prev - m_new)
    p = jnp.exp(s - _lanes(m_new, keys))
    v1 = jnp.concatenate([v, jnp.ones((keys, n), BF16)], axis=1)
    accl_ref[...] = _lanes(alpha, 2 * n) * accl_ref[...] + _dot(p.astype(BF16), v1)
    m_ref[...] = m_new


def _softmax_result(accl_ref):
    accl = accl_ref[...]
    n = accl.shape[1] // 2
    return accl[:, :n] / accl[:, n:]


def _fill_suffix_ones(tri_ref):
    n = tri_ref.shape[0]
    r = lax.broadcasted_iota(jnp.int32, (n, n), 0)
    c = lax.broadcasted_iota(jnp.int32, (n, n), 1)
    tri_ref[...] = jnp.where(r > c, 1.0, 0.0).astype(BF16)


def _stick_tile(qs, k2, v2, tri, acc_ref, car_ref, mask):
    z = _dot_nt(qs, k2)
    lb = _log_sigmoid(z)
    l1 = lb - z
    if mask is not None:
        l1 = jnp.where(mask, l1, 0.0)
    hi = l1.astype(BF16)
    lo = (l1 - hi.astype(F32)).astype(BF16)
    suf = _dot(hi, tri) + _dot(lo, tri) + car_ref[...]
    w = jnp.exp(lb + suf)
    if mask is not None:
        w = jnp.where(mask, w, 0.0)
    acc_ref[...] += _dot(w.astype(BF16), v2)
    car_ref[...] += jnp.sum(l1, axis=1, keepdims=True)


def _stick_alive(car_ref):
    return (jnp.max(car_ref[...]) >= STICK_DEAD).astype(jnp.int32)


def _local_causal(rows, keys, tq, strict):
    r = lax.broadcasted_iota(jnp.int32, (rows, keys), 0) & (tq - 1)
    c = lax.broadcasted_iota(jnp.int32, (rows, keys), 1)
    return (c < r) if strict else (c <= r)


def _decay_tile(qs, k2, v2, cq, ck, causal, m_ref, accl_ref):
    tq, keys = cq.shape[0], ck.shape[1]
    s = _dot_nt(qs, k2)
    for hh in range(2):
        rows = slice(hh * tq, (hh + 1) * tq)
        sh = s[rows] + (cq[:, hh:hh + 1] - ck[hh:hh + 1])
        if causal:
            sh = jnp.where(_local_causal(tq, keys, tq, strict=False), sh, NEG)
        _softmax_update(sh, v2, m_ref.at[rows], accl_ref.at[rows])


def _mla_prompt_body(q_ref, k_ref, o_ref, m_ref, accl_ref):
    g = pl.program_id(1)
    rows = CHUNK * A_HEADS
    grp = 4 * rows

    def update(r0, n_rows, k):
        sl = pl.ds(r0, n_rows)
        s = _dot_nt(q_ref[sl, :], k) * A_SCALE
        _softmax_update(s, k[:, :A_KV_LORA], m_ref.at[sl], accl_ref.at[sl])

    def group(gl, carry):
        gq = g * 2 + gl
        g0 = pl.multiple_of(gl * grp, grp)
        _softmax_init(m_ref.at[pl.ds(g0, grp)], accl_ref.at[pl.ds(g0, grp)])

        def kv(j, c2):
            k = k_ref[pl.ds(pl.multiple_of(j * TK, TK), TK), :]
            for t in range(grp // MLA_RT):
                update(g0 + t * MLA_RT, MLA_RT, k)
            return c2

        lax.fori_loop(0, gq, kv, 0)
        kd = k_ref[pl.ds(pl.multiple_of(gq * TK, TK), TK), :]
        for ci in range(4):
            for t in range(rows // MLA_RT):
                update(g0 + ci * rows + t * MLA_RT, MLA_RT, kd[:CHUNK * (ci + 1)])
        o_ref[pl.ds(g0, grp), :] = _softmax_result(accl_ref.at[pl.ds(g0, grp)]).astype(BF16)
        return carry

    lax.fori_loop(0, 2, group, 0)


def _mla_prompt(q2, k2, n_b, n_s):
    rows_q = n_s * A_HEADS
    qg = rows_q // 4
    assert qg == 8 * CHUNK * A_HEADS and TK == 4 * CHUNK
    return pl.pallas_call(
        _mla_prompt_body,
        grid=(n_b, 4),
        in_specs=[pl.BlockSpec((None, qg, 256), lambda b, g: (b, g, 0)),
                  pl.BlockSpec((None, n_s, 256), lambda b, g: (b, 0, 0))],
        out_specs=pl.BlockSpec((None, qg, A_KV_LORA), lambda b, g: (b, g, 0)),
        out_shape=jax.ShapeDtypeStruct((n_b, rows_q, A_KV_LORA), BF16),
        scratch_shapes=[pltpu.VMEM((qg, LANES), F32), pltpu.VMEM((qg, 2 * LANES), F32)],
        compiler_params=_params("parallel", "parallel"),
        name="mla_prompt",
    )(q2.reshape(n_b, rows_q, 256), k2.reshape(n_b, n_s, 256))


def _mla_sample_body(q_ref, kn_ref, ckv_ref, kr_ref, o_ref, m_ref, accl_ref):
    j = pl.program_id(1)

    @pl.when(j == 0)
    def _():
        _softmax_init(m_ref, accl_ref)

    n_tiles = q_ref.shape[0] // MLA_RT
    ck = ckv_ref[...].astype(BF16)
    kr = kr_ref[...].astype(BF16)
    for t in range(n_tiles):
        sl = slice(t * MLA_RT, (t + 1) * MLA_RT)
        s = (_dot_nt(q_ref[sl, :A_KV_LORA], ck)
             + _dot_nt(q_ref[sl, A_KV_LORA:A_KV_LORA + A_ROPE], kr)) * A_SCALE
        _softmax_update(s, ck, m_ref.at[sl], accl_ref.at[sl])

    @pl.when(j == pl.num_programs(1) - 1)
    def _():
        kn = kn_ref[...]
        for t in range(n_tiles):
            sl = slice(t * MLA_RT, (t + 1) * MLA_RT)
            s = _dot_nt(q_ref[sl, :], kn) * A_SCALE
            _softmax_update(s, kn[:, :A_KV_LORA], m_ref.at[sl], accl_ref.at[sl])
        o_ref[...] = _softmax_result(accl_ref).astype(BF16)


def _mla_sample(q2, k2, cache_ckv, cache_kr, n_b, n_s):
    rows = n_s * A_HEADS
    n_past = cache_ckv.shape[1]
    return pl.pallas_call(
        _mla_sample_body,
        grid=(n_b, n_past // TKC),
        in_specs=[pl.BlockSpec((None, rows, 256), lambda b, j: (b, 0, 0)),
                  pl.BlockSpec((None, n_s, 256), lambda b, j: (b, 0, 0)),
                  pl.BlockSpec((None, TKC, A_KV_LORA), lambda b, j: (b, j, 0)),
                  pl.BlockSpec((None, TKC, A_ROPE), lambda b, j: (b, j, 0))],
        out_specs=pl.BlockSpec((None, rows, A_KV_LORA), lambda b, j: (b, 0, 0)),
        out_shape=jax.ShapeDtypeStruct((n_b, rows, A_KV_LORA), BF16),
        scratch_shapes=[pltpu.VMEM((rows, LANES), F32), pltpu.VMEM((rows, 2 * LANES), F32)],
        compiler_params=_params("parallel", "arbitrary"),
        name="mla_sample",
    )(q2.reshape(n_b, rows, 256), k2.reshape(n_b, n_s, 256), cache_ckv, cache_kr)


def _sb_prompt_body(q_ref, k_ref, v_ref, o_ref, tri_ref, acc_ref, car_ref):
    n_s = q_ref.shape[1]
    _fill_suffix_ones(tri_ref)

    def qblock(i, carry):
        q0 = pl.multiple_of(i * TQ, TQ)
        qs = [_stack_pair(q_ref[g, pl.ds(q0, TQ), :]) for g in range(PAIRS_PER_STEP)]
        acc_ref[...] = jnp.zeros(acc_ref.shape, F32)
        car_ref[...] = jnp.zeros(car_ref.shape, F32)

        def tiles(k0, mask):
            for g in range(PAIRS_PER_STEP):
                _stick_tile(qs[g], k_ref[g, pl.ds(k0, TK), :], v_ref[g, pl.ds(k0, TK), :], tri_ref[...],
                            acc_ref.at[g], car_ref.at[g], mask)

        tiles(q0, _local_causal(2 * TQ, TK, TQ, strict=True))

        def kv(state):
            jj, _ = state
            tiles(pl.multiple_of((i - 1 - jj) * TK, TK), None)
            return jj + 1, _stick_alive(car_ref)

        lax.while_loop(lambda st: (st[0] < i) & (st[1] > 0), kv, (jnp.int32(0), _stick_alive(car_ref)))
        for g in range(PAIRS_PER_STEP):
            o_ref[g, pl.ds(q0, TQ), :] = _merge_pair(acc_ref[g], TQ)
        return carry

    lax.fori_loop(0, n_s // TQ, qblock, 0)


def _pair_seq_spec(n_s):
    return pl.BlockSpec((PAIRS_PER_STEP, n_s, LANES), lambda b, g: (g, b, 0))


def _sb_prompt(q, k, v, n_b, n_s):
    assert TQ == TK
    spec = _pair_seq_spec(n_s)
    return pl.pallas_call(
        _sb_prompt_body,
        grid=(n_b, N_PAIRS // PAIRS_PER_STEP),
        in_specs=[spec, spec, spec],
        out_specs=spec,
        out_shape=jax.ShapeDtypeStruct((N_PAIRS, n_b * n_s, LANES), F32),
        scratch_shapes=[pltpu.VMEM((TK, TK), BF16), pltpu.VMEM((PAIRS_PER_STEP, 2 * TQ, LANES), F32),
                        pltpu.VMEM((PAIRS_PER_STEP, 2 * TQ, 1), F32)],
        compiler_params=_params("parallel", "parallel"),
        name="sb_prompt",
    )(q, k, v)


def _head_tile(pm_ref, h):
    lo = (h % 2) * HEAD_DIM
    return pm_ref[h // 2][:, lo:lo + HEAD_DIM]


def _sb_sample_body(q_ref, kn_ref, vn_ref, ck_hbm, cv_hbm, o_ref, kbuf, vbuf, sem, tri_ref, acc_ref,
                    car_ref):
    b = pl.program_id(0)
    n_q = q_ref.shape[1]
    n_blk = ck_hbm.shape[1] // TK

    def block_copies(blk):
        rows = pl.ds(pl.multiple_of(blk * TK, TK), TK)
        return (pltpu.make_async_copy(ck_hbm.at[b, rows], kbuf, sem.at[0]),
                pltpu.make_async_copy(cv_hbm.at[b, rows], vbuf, sem.at[1]))

    def start(blk):
        for c in block_copies(blk):
            c.start()

    def cache_block(blk):
        for c in block_copies(blk):
            c.wait()
        for h in range(HEADS):
            _stick_tile(qh[h], kbuf[:, h, :].astype(BF16), vbuf[:, h, :].astype(BF16), tri_ref[...],
                        acc_ref.at[h], car_ref.at[h], None)

    start(n_blk - 1)
    _fill_suffix_ones(tri_ref)
    acc_ref[...] = jnp.zeros(acc_ref.shape, F32)
    car_ref[...] = jnp.zeros(car_ref.shape, F32)
    qh = [_head_tile(q_ref, h) for h in range(HEADS)]
    mask = _local_causal(n_q, n_q, n_q, strict=True)
    for h in range(HEADS):
        _stick_tile(qh[h], _head_tile(kn_ref, h), _head_tile(vn_ref, h), tri_ref[:n_q, :n_q],
                    acc_ref.at[h], car_ref.at[h], mask)
    cache_block(n_blk - 1)

    def older(state):
        blk, _ = state
        start(blk)
        cache_block(blk)
        return blk - 1, _stick_alive(car_ref)

    lax.while_loop(lambda st: (st[0] >= 0) & (st[1] > 0), older,
                   (jnp.int32(n_blk - 2), _stick_alive(car_ref)))
    for p in range(N_PAIRS):
        o_ref[p] = jnp.concatenate([acc_ref[2 * p], acc_ref[2 * p + 1]], axis=1)


def _sb_sample(q, kn, vn, cache_k, cache_v, n_b, n_s):
    new_spec = pl.BlockSpec((N_PAIRS, n_s, LANES), lambda b: (0, b, 0))
    hbm = pl.BlockSpec(memory_space=pl.ANY)
    blk_shape = (TK, HEADS, HEAD_DIM)
    return pl.pallas_call(
        _sb_sample_body,
        grid=(n_b,),
        in_specs=[new_spec, new_spec, new_spec, hbm, hbm],
        out_specs=new_spec,
        out_shape=jax.ShapeDtypeStruct((N_PAIRS, n_b * n_s, LANES), F32),
        scratch_shapes=[pltpu.VMEM(blk_shape, F32), pltpu.VMEM(blk_shape, F32),
                        pltpu.SemaphoreType.DMA((2,)), pltpu.VMEM((TK, TK), BF16),
                        pltpu.VMEM((HEADS, n_s, HEAD_DIM), F32), pltpu.VMEM((HEADS, n_s, 1), F32)],
        compiler_params=_params("arbitrary"),
        name="sb_sample",
    )(q, kn, vn, cache_k, cache_v)


def _band_block(qs, kwin, vwin, bias, valid_from, tq):
    s = _dot_nt(qs, kwin) + bias
    if valid_from is not None:
        col = lax.broadcasted_iota(jnp.int32, s.shape, 1)
        s = jnp.where(col >= valid_from, s, NEG)
    p = jnp.exp(s - jnp.max(s, axis=1, keepdims=True))
    o = _dot(p.astype(BF16), vwin) / jnp.sum(p, axis=1, keepdims=True)
    return _merge_pair(o, tq)


def _band_prompt_body(q_ref, k_ref, v_ref, bias_ref, o_ref, kpad_ref, vpad_ref):
    n_s = q_ref.shape[1]
    win = LEFT_CTX + BAND_TQ
    zeros = jnp.zeros((PAIRS_PER_STEP, LEFT_CTX, LANES), BF16)
    kpad_ref[:, :LEFT_CTX, :] = zeros
    vpad_ref[:, :LEFT_CTX, :] = zeros
    kpad_ref[:, LEFT_CTX:, :] = k_ref[...]
    vpad_ref[:, LEFT_CTX:, :] = v_ref[...]

    def qblock(i, carry):
        q0 = pl.multiple_of(i * BAND_TQ, BAND_TQ)
        for g in range(PAIRS_PER_STEP):
            qs = _stack_pair(q_ref[g, pl.ds(q0, BAND_TQ), :])
            o_ref[g, pl.ds(q0, BAND_TQ), :] = _band_block(
                qs, kpad_ref[g, pl.ds(q0, win), :], vpad_ref[g, pl.ds(q0, win), :], bias_ref[g],
                LEFT_CTX - q0, BAND_TQ)
        return carry

    lax.fori_loop(0, n_s // BAND_TQ, qblock, 0)


def _band_prompt(q, k, v, bias, n_b, n_s):
    spec = _pair_seq_spec(n_s)
    win = LEFT_CTX + BAND_TQ
    return pl.pallas_call(
        _band_prompt_body,
        grid=(n_b, N_PAIRS // PAIRS_PER_STEP),
        in_specs=[spec, spec, spec,
                  pl.BlockSpec((PAIRS_PER_STEP, 2 * BAND_TQ, win), lambda b, g: (g, 0, 0))],
        out_specs=spec,
        out_shape=jax.ShapeDtypeStruct((N_PAIRS, n_b * n_s, LANES), F32),
        scratch_shapes=[pltpu.VMEM((PAIRS_PER_STEP, LEFT_CTX + n_s, LANES), BF16),
                        pltpu.VMEM((PAIRS_PER_STEP, LEFT_CTX + n_s, LANES), BF16)],
        compiler_params=_params("parallel", "parallel"),
        name="band_prompt",
    )(q, k, v, bias)


def _band_sample_body(q_ref, kn_ref, vn_ref, kc_ref, vc_ref, bias_ref, o_ref, kwin_ref, vwin_ref):
    n_q = q_ref.shape[0]
    n_keep = kc_ref.shape[0]
    kwin_ref[:n_keep, :] = kc_ref[...].astype(BF16)
    vwin_ref[:n_keep, :] = vc_ref[...].astype(BF16)
    kwin_ref[n_keep:, :] = kn_ref[...]
    vwin_ref[n_keep:, :] = vn_ref[...]
    o_ref[...] = _band_block(_stack_pair(q_ref[...]), kwin_ref[...], vwin_ref[...], bias_ref[...],
                             None, n_q)


def _band_sample(q, kn, vn, cache_k, cache_v, bias, n_b, n_s):
    n_keep = cache_k.shape[1]
    new_spec = pl.BlockSpec((None, n_s, LANES), lambda b, p: (p, b, 0))
    cache_spec = pl.BlockSpec((None, n_keep, LANES), lambda b, p: (b, 0, p))
    return pl.pallas_call(
        _band_sample_body,
        grid=(n_b, N_PAIRS),
        in_specs=[new_spec, new_spec, new_spec, cache_spec, cache_spec,
                  pl.BlockSpec((None, 2 * n_s, n_keep + n_s), lambda b, p: (p, 0, 0))],
        out_specs=new_spec,
        out_shape=jax.ShapeDtypeStruct((N_PAIRS, n_b * n_s, LANES), F32),
        scratch_shapes=[pltpu.VMEM((n_keep + n_s, LANES), BF16), pltpu.VMEM((n_keep + n_s, LANES), BF16)],
        compiler_params=_params("parallel", "parallel"),
        name="band_sample",
    )(q, kn, vn, cache_k, cache_v, bias)


def _band_bias(rel_bias, tq):
    win = LEFT_CTX + tq
    i = np.arange(tq)[:, None]
    w = np.arange(win)[None, :]
    qc, kc = i // CHUNK, w // CHUNK - LEFT_CTX // CHUNK
    ok = (kc <= qc) & (kc >= qc - LEFT_CTX // CHUNK)
    u = np.arange(win + tq - 1)
    rel = np.clip(LEFT_CTX + (tq - 1) - u, -REL_CLIP, REL_CLIP) + REL_CLIP
    diag = rel_bias.astype(F32)[:, rel]
    n = win + tq - 1
    flat = jnp.tile(diag, (1, tq))[:, tq - 1:tq - 1 + tq * (n - 1)]
    tab = flat.reshape(HEADS, tq, n - 1)[:, :, :win]
    tab = jnp.where(jnp.asarray(ok)[None], tab, NEG)
    return tab.reshape(N_PAIRS, 2 * tq, win)


def _fox_prompt_body(q_ref, k_ref, v_ref, cq_ref, ck_ref, o_ref, m_ref, accl_ref):
    n_s = q_ref.shape[1]

    def qblock(i, carry):
        q0 = pl.multiple_of(i * TQ, TQ)
        qs = [_stack_pair(q_ref[g, pl.ds(q0, TQ), :]) for g in range(PAIRS_PER_STEP)]
        cq = [cq_ref[g, pl.ds(q0, TQ), :] for g in range(PAIRS_PER_STEP)]
        _softmax_init(m_ref, accl_ref)

        def tile(jb, causal):
            k0 = pl.multiple_of(jb * TK, TK)
            for g in range(PAIRS_PER_STEP):
                _decay_tile(qs[g], k_ref[g, pl.ds(k0, TK), :], v_ref[g, pl.ds(k0, TK), :], cq[g],
                            ck_ref[g, jb], causal, m_ref.at[g], accl_ref.at[g])

        tile(i, True)

        def kv(jb, c2):
            tile(jb, False)
            return c2

        lax.fori_loop(0, i, kv, 0)
        for g in range(PAIRS_PER_STEP):
            o_ref[g, pl.ds(q0, TQ), :] = _merge_pair(_softmax_result(accl_ref.at[g]), TQ)
        return carry

    lax.fori_loop(0, n_s // TQ, qblock, 0)


def _fox_prompt(q, k, v, cum, n_b, n_s):
    assert TQ == TK
    spec = _pair_seq_spec(n_s)
    cq = cum.reshape(n_b, N_PAIRS, 2, n_s).transpose(0, 1, 3, 2)
    ck = cum.reshape(n_b, N_PAIRS, 2, n_s // TK, TK).transpose(0, 1, 3, 2, 4)
    g_ = PAIRS_PER_STEP
    return pl.pallas_call(
        _fox_prompt_body,
        grid=(n_b, N_PAIRS // g_),
        in_specs=[spec, spec, spec,
                  pl.BlockSpec((None, g_, n_s, 2), lambda b, g: (b, g, 0, 0)),
                  pl.BlockSpec((None, g_, n_s // TK, 2, TK), lambda b, g: (b, g, 0, 0, 0))],
        out_specs=spec,
        out_shape=jax.ShapeDtypeStruct((N_PAIRS, n_b * n_s, LANES), F32),
        scratch_shapes=[pltpu.VMEM((g_, 2 * TQ, LANES), F32), pltpu.VMEM((g_, 2 * TQ, 2 * LANES), F32)],
        compiler_params=_params("parallel", "parallel"),
        name="fox_prompt",
    )(q, k, v, cq, ck)


def _fox_sample_body(q_ref, kn_ref, vn_ref, kc_ref, vc_ref, cq_ref, ckn_ref, ckc_ref, o_ref,
                     m_ref, accl_ref):
    j = pl.program_id(1)
    n_q = q_ref.shape[1]
    cq = cq_ref[...]

    def pair_update(p, k2, v2, ck, causal):
        rows = slice(2 * p * n_q, (2 * p + 2) * n_q)
        _decay_tile(_stack_pair(q_ref[p]), k2, v2, cq[:, 2 * p:2 * p + 2], ck[2 * p:2 * p + 2], causal,
                    m_ref.at[rows], accl_ref.at[rows])

    @pl.when(j == 0)
    def _():
        _softmax_init(m_ref, accl_ref)
        ckn = ckn_ref[...]
        for p in range(N_PAIRS):
            pair_update(p, kn_ref[p], vn_ref[p], ckn, True)

    @pl.when(j > 0)
    def _():
        ckc = ckc_ref[...]
        for p in range(N_PAIRS):
            sl = slice(p * LANES, (p + 1) * LANES)
            pair_update(p, kc_ref[:, sl].astype(BF16), vc_ref[:, sl].astype(BF16), ckc, False)

    @pl.when(j == pl.num_programs(1) - 1)
    def _():
        for p in range(N_PAIRS):
            rows = slice(2 * p * n_q, (2 * p + 2) * n_q)
            o_ref[p] = _merge_pair(_softmax_result(accl_ref.at[rows]), n_q)


def _fox_sample(q, kn, vn, cache_k, cache_v, cum, n_b, n_s):
    n_past = cache_k.shape[1]
    n_blk = n_past // TKC
    cq = cum[..., n_past:].transpose(0, 2, 1)
    ckn = cum[..., n_past:]
    ckc = cum[..., :n_past]
    new_spec = pl.BlockSpec((N_PAIRS, n_s, LANES), lambda b, j: (0, b, 0))
    cache_blk = lambda j: jnp.maximum(j - 1, 0)
    cache_spec = pl.BlockSpec((None, TKC, WIDTH), lambda b, j: (b, cache_blk(j), 0))
    return pl.pallas_call(
        _fox_sample_body,
        grid=(n_b, n_blk + 1),
        in_specs=[new_spec, new_spec, new_spec, cache_spec, cache_spec,
                  pl.BlockSpec((None, n_s, HEADS), lambda b, j: (b, 0, 0)),
                  pl.BlockSpec((None, HEADS, n_s), lambda b, j: (b, 0, 0)),
                  pl.BlockSpec((None, HEADS, TKC), lambda b, j: (b, 0, cache_blk(j)))],
        out_specs=new_spec,
        out_shape=jax.ShapeDtypeStruct((N_PAIRS, n_b * n_s, LANES), F32),
        scratch_shapes=[pltpu.VMEM((HEADS * n_s, LANES), F32), pltpu.VMEM((HEADS * n_s, 2 * LANES), F32)],
        compiler_params=_params("parallel", "arbitrary"),
        name="fox_sample",
    )(q, kn, vn, cache_k, cache_v, cq, ckn, ckc)


def _cumsum_body(x_ref, o_ref):
    n_rows, n_cols = x_ref.shape
    r = lax.broadcasted_iota(jnp.int32, (LANES, LANES), 0)
    c = lax.broadcasted_iota(jnp.int32, (LANES, LANES), 1)
    ones = jnp.where(r <= c, 1.0, 0.0).astype(BF16)
    total = jnp.zeros((n_rows, 1), F32)
    for g in range(n_cols // LANES):
        x = x_ref[:, g * LANES:(g + 1) * LANES]
        h1 = x.astype(BF16)
        r1 = x - h1.astype(F32)
        h2 = r1.astype(BF16)
        h3 = (r1 - h2.astype(F32)).astype(BF16)
        y = _dot(h1, ones) + _dot(h2, ones) + _dot(h3, ones) + total
        o_ref[:, g * LANES:(g + 1) * LANES] = y
        total = y[:, LANES - 1:LANES]


def _cumsum_rows(x):
    rows, n = x.shape
    n_pad = -(-n // LANES) * LANES
    xp = jnp.pad(x, ((0, 0), (0, n_pad - n)))
    out = pl.pallas_call(
        _cumsum_body,
        out_shape=jax.ShapeDtypeStruct((rows, n_pad), F32),
        compiler_params=pltpu.CompilerParams(vmem_limit_bytes=VMEM_LIMIT),
        name="cumsum_rows",
    )(xp)
    return out[:, :n]


def _split_cols(w, sizes):
    out, off = [], 0
    for n in sizes:
        out.append(w[:, off:off + n])
        off += n
    return out


def _rope_tables(pos, n_rows):
    half = A_ROPE // 2
    inv_freq = ROPE_THETA ** (-jnp.arange(half, dtype=F32) / half)
    ang = pos.astype(F32)[:, None] * inv_freq[None, :]
    cos, sin = jnp.cos(ang), jnp.sin(ang)
    zeros = jnp.zeros((pos.shape[0], LANES - A_ROPE), F32)
    cos_t = jnp.concatenate([cos, cos, zeros], axis=1)
    sin_t = jnp.concatenate([-sin, sin, zeros], axis=1)
    reps = max(1, n_rows // pos.shape[0])
    return jnp.tile(cos_t, (reps, 1)), jnp.tile(sin_t, (reps, 1))


def _swap_halves(w):
    half = w.shape[-1] // 2
    return jnp.concatenate([w[..., half:], w[..., :half]], axis=-1)


def _pad_lanes(w):
    return jnp.pad(w, [(0, 0)] * (w.ndim - 1) + [(0, LANES - w.shape[-1])])


def _prep_even(w_in, q_norm, w_uq, kv_norm, w_uk, w_uv):
    wqa, wkv, wkr, wga, wqb, wkb, wvb, wgb = _split_cols(
        w_in, (A_Q_LORA, A_KV_LORA, A_ROPE, WIDTH, WIDTH, WIDTH, WIDTH, WIDTH))
    b = lambda a: a.astype(BF16)
    uq_rope = w_uq[:, :, A_NOPE:]
    uk_t = jnp.transpose(w_uk, (1, 2, 0))
    z = jnp.zeros((A_NOPE, A_KV_LORA), w_uk.dtype)
    wuk = jnp.stack([jnp.block([[uk_t[2 * p], z], [z, uk_t[2 * p + 1]]]) for p in range(N_PAIRS)])
    uv_t = jnp.transpose(w_uv, (1, 0, 2))
    zv = jnp.zeros((A_KV_LORA, A_V), w_uv.dtype)
    wuv = jnp.stack([jnp.block([[uv_t[2 * p], zv], [zv, uv_t[2 * p + 1]]]) for p in range(N_PAIRS)])
    return dict(
        wqa=b(wqa), wkv=b(wkv), wkr=b(_pad_lanes(wkr)), wkrs=b(_pad_lanes(_swap_halves(wkr))),
        wga=b(wga), wqb=b(wqb), wkb=b(wkb), wvb=b(wvb), wgb=b(wgb),
        qn=q_norm.reshape(1, -1), kvn=kv_norm.reshape(1, -1),
        wuqn=b(w_uq[:, :, :A_NOPE].reshape(A_Q_LORA, A_HEADS * A_NOPE)),
        wuqr=b(_pad_lanes(uq_rope).reshape(A_Q_LORA, A_HEADS * LANES)),
        wuqrs=b(_pad_lanes(_swap_halves(uq_rope)).reshape(A_Q_LORA, A_HEADS * LANES)),
        wuk=b(wuk), wuv=b(wuv))


def _prep_odd(w_in, forget_bias):
    wqc, wkc, wvc, wgc, wqd, wkd, wvd, wf, wgd = _split_cols(
        w_in, (WIDTH, WIDTH, WIDTH, WIDTH, WIDTH, WIDTH, WIDTH, HEADS, WIDTH))
    b = lambda a: a.astype(BF16)
    return dict(wqc=b(wqc), wkc=b(wkc), wvc=b(wvc), wgc=b(wgc), wqd=b(wqd), wkd=b(wkd), wvd=b(wvd),
                wf=b(_pad_lanes(wf)), wgd=b(wgd), fb=forget_bias.astype(F32).reshape(1, HEADS))


def kernel(x_prompt, x_sample, cache_mla_ckv, cache_mla_krope, cache_sb_k, cache_sb_v, cache_band_k,
           cache_band_v, cache_fox_k, cache_fox_v, cache_fox_logf, norm_pre, norm_post, w_in_even,
           a_q_norm, a_w_uq, a_kv_norm, a_w_uk, a_w_uv, w_out_even, w_in_odd, c_rel_bias,
           d_forget_bias, w_out_odd):
    n_b, n_s, _ = x_prompt.shape
    d_b, d_s, _ = x_sample.shape
    n_past = cache_sb_k.shape[2]
    n_keep = cache_band_k.shape[2]
    assert n_s % (2 * TQ) == 0 and n_past % TKC == 0 and d_s == CHUNK and n_past % CHUNK == 0
    assert n_keep == LEFT_CTX and (d_b * d_s) % TM == 0 and TM % d_s == 0

    xp = x_prompt.reshape(n_b * n_s, D_MODEL)
    xs = x_sample.reshape(d_b * d_s, D_MODEL)
    row = lambda a: a.reshape(1, -1)
    heads = lambda a, b, s: a.reshape(b, s, HEADS, HEAD_DIM)
    flat = lambda a: a.reshape(a.shape[0], a.shape[1], WIDTH)

    we = _prep_even(w_in_even[0], a_q_norm[0], a_w_uq[0], a_kv_norm[0], a_w_uk[0], a_w_uv[0])
    wout_e = w_out_even[0].astype(BF16)
    cos_p, sin_p = _rope_tables(jnp.arange(n_s), TM)
    cos_s, sin_s = _rope_tables(n_past + jnp.arange(d_s), TM)

    (ckv_p, kr_p, k2_p, q2_p, ga_p, gb_p, qb_p, kb_p, kb16_p, vb_p, vb16_p) = _in_even(
        xp, row(norm_pre[0]), cos_p, sin_p, we)
    (ckv_s, kr_s, k2_s, q2_s, ga_s, gb_s, qb_s, kb_s, kb16_s, vb_s, vb16_s) = _in_even(
        xs, row(norm_pre[0]), cos_s, sin_s, we)

    lat_p = _mla_prompt(q2_p, k2_p, n_b, n_s).reshape(n_b * n_s, A_HEADS * A_KV_LORA)
    lat_s = _mla_sample(q2_s, k2_s, cache_mla_ckv[0], cache_mla_krope[0], d_b, d_s
                        ).reshape(d_b * d_s, A_HEADS * A_KV_LORA)
    sb_p = _sb_prompt(qb_p, kb16_p, vb16_p, n_b, n_s)
    sb_s = _sb_sample(qb_s, kb16_s, vb16_s, cache_sb_k[0], cache_sb_v[0], d_b, d_s)

    xp1 = _out_proj(xp, row(norm_post[0]), ga_p, gb_p, lat_p, sb_p, wout_e, we['wuv'])
    xs1 = _out_proj(xs, row(norm_post[0]), ga_s, gb_s, lat_s, sb_s, wout_e, we['wuv'])

    wo = _prep_odd(w_in_odd[0], d_forget_bias[0])
    wout_o = w_out_odd[0].astype(BF16)
    (qc_p, kc_p, kc16_p, vc_p, vc16_p, gc_p, qd_p, kd_p, kd16_p, vd_p, vd16_p, lf_p, gd_p) = _in_odd(
        xp1, row(norm_pre[1]), wo)
    (qc_s, kc_s, kc16_s, vc_s, vc16_s, gc_s, qd_s, kd_s, kd16_s, vd_s, vd16_s, lf_s, gd_s) = _in_odd(
        xs1, row(norm_pre[1]), wo)

    band_p = _band_prompt(qc_p, kc16_p, vc16_p, _band_bias(c_rel_bias[0], BAND_TQ), n_b, n_s)
    band_s = _band_sample(qc_s, kc16_s, vc16_s, flat(cache_band_k[0]), flat(cache_band_v[0]),
                          _band_bias(c_rel_bias[0], d_s), d_b, d_s)

    lf_p3 = lf_p.reshape(n_b, n_s, HEADS)
    lf_s3 = lf_s.reshape(d_b, d_s, HEADS)
    cum_p = _cumsum_rows(lf_p3.transpose(0, 2, 1).reshape(n_b * HEADS, n_s)).reshape(n_b, HEADS, n_s)
    lf_all = jnp.concatenate([cache_fox_logf[0].astype(F32), lf_s3], axis=1)
    cum_s = _cumsum_rows(lf_all.transpose(0, 2, 1).reshape(d_b * HEADS, n_past + d_s)
                         ).reshape(d_b, HEADS, n_past + d_s)
    fox_p = _fox_prompt(qd_p, kd16_p, vd16_p, cum_p, n_b, n_s)
    fox_s = _fox_sample(qd_s, kd16_s, vd16_s, flat(cache_fox_k[0]), flat(cache_fox_v[0]), cum_s, d_b, d_s)

    xp2 = _out_proj(xp1, row(norm_post[1]), gc_p, gd_p, band_p, fox_p, wout_o)
    xs2 = _out_proj(xs1, row(norm_post[1]), gc_s, gd_s, band_s, fox_s, wout_o)

    keep = min(LEFT_CTX, n_s)
    kc_p4, vc_p4 = heads(kc_p, n_b, n_s), heads(vc_p, n_b, n_s)
    band_k_s = jnp.concatenate([cache_band_k[0], heads(kc_s, d_b, d_s)], axis=1)[:, d_s:]
    band_v_s = jnp.concatenate([cache_band_v[0], heads(vc_s, d_b, d_s)], axis=1)[:, d_s:]
    one = lambda a: a[None]
    return (xp2.reshape(n_b, n_s, D_MODEL), xs2.reshape(d_b, d_s, D_MODEL),
            one(ckv_p.reshape(n_b, n_s, A_KV_LORA)), one(kr_p.reshape(n_b, n_s, A_ROPE)),
            one(heads(kb_p, n_b, n_s)), one(heads(vb_p, n_b, n_s)),
            one(kc_p4[:, n_s - keep:]), one(vc_p4[:, n_s - keep:]),
            one(heads(kd_p, n_b, n_s)), one(heads(vd_p, n_b, n_s)), one(lf_p3),
            one(ckv_s.reshape(d_b, d_s, A_KV_LORA)), one(kr_s.reshape(d_b, d_s, A_ROPE)),
            one(heads(kb_s, d_b, d_s)), one(heads(vb_s, d_b, d_s)),
            one(band_k_s), one(band_v_s),
            one(heads(kd_s, d_b, d_s)), one(heads(vd_s, d_b, d_s)), one(lf_s3))
```

```python
import functools

import numpy as np
import jax
import jax.numpy as jnp
from jax import lax
from jax.experimental import pallas as pl
from jax.experimental.pallas import tpu as pltpu

F32 = jnp.float32
BF16 = jnp.bfloat16

D_MODEL = 1024
PAST_LEN = 4096
CHUNK = 64
LEFT_CTX = 512
REL_CLIP = 128
EPS = 1e-6
NEG = -1e30
ROPE_THETA = 10000.0
A_HEADS = 8
A_Q_LORA = 256
A_KV_LORA = 128
A_NOPE = 64
A_ROPE = 32
A_V = 64
A_SCALE = (A_NOPE + A_ROPE) ** -0.5
HEADS = 8
HEAD_DIM = 64
WIDTH = HEADS * HEAD_DIM
QK_SCALE = HEAD_DIM ** -0.5
N_PAIRS = HEADS // 2

LANES = 128
VMEM_LIMIT = 52 * 1024 * 1024
TM = 256
TQ = 256
TK = 256
TKC = 1024
BAND_TQ = 128
MLA_RT = 256
PAIRS_PER_STEP = 4
STICK_DEAD = -104.0


def _dot(a, b):
    return jnp.dot(a, b, preferred_element_type=F32)


def _dot_nt(a, b):
    return lax.dot_general(a, b, (((1,), (1,)), ((), ())), preferred_element_type=F32)


def _rms(x, g):
    y = x * lax.rsqrt(jnp.mean(x * x, axis=-1, keepdims=True) + EPS)
    return y * g


def _log_sigmoid(z):
    return jnp.minimum(z, 0.0) - jnp.log1p(jnp.exp(-jnp.abs(z)))


def _silu(g):
    return g / (1.0 + jnp.exp(-g))


def _stack_pair(q2):
    qf = q2.astype(F32)
    lane = lax.broadcasted_iota(jnp.int32, qf.shape, 1)
    even = jnp.where(lane < HEAD_DIM, qf, 0.0)
    odd = jnp.where(lane >= HEAD_DIM, qf, 0.0)
    return jnp.concatenate([even, odd], axis=0).astype(BF16)


def _merge_pair(o, tq):
    top, bot = o[:tq], o[tq:]
    lane = lax.broadcasted_iota(jnp.int32, top.shape, 1)
    return jnp.where(lane < HEAD_DIM, top, bot)


def _params(*sem):
    return pltpu.CompilerParams(dimension_semantics=sem, vmem_limit_bytes=VMEM_LIMIT)


def _const_spec(shape):
    nd = len(shape)
    return pl.BlockSpec(shape, lambda *_: (0,) * nd)


def _in_even_body(*refs, rows_minor):
    (x_ref, gpre_ref, cos_ref, sin_ref, wqa_ref, wkv_ref, wkr_ref, wkrs_ref, wga_ref, wqb_ref, wkb_ref,
     wvb_ref, wgb_ref, qn_ref, kvn_ref, wuqn_ref, wuqr_ref, wuqrs_ref, wuk_ref) = refs[:19]
    n_in = 25 if rows_minor else 19
    (ckv_ref, krope_ref, k2_ref, q2_ref, ga_ref, gb_ref, qb_ref, kb_ref, kb16_ref, vb_ref,
     vb16_ref) = refs[n_in:]
    h = _rms(x_ref[...], gpre_ref[...]).astype(BF16)
    cos = cos_ref[...]
    sin = sin_ref[...]
    ckv = _rms(_dot(h, wkv_ref[...]), kvn_ref[...])
    ckv_ref[...] = ckv
    kr = _dot(h, wkr_ref[...]) * cos + _dot(h, wkrs_ref[...]) * sin
    if rows_minor:
        wkr_t, wkrs_t, wkb_t, wvb_t, cos_t, sin_t = refs[19:25]
        krope_ref[...] = _dot_nt(wkr_t[...], h) * cos_t[...] + _dot_nt(wkrs_t[...], h) * sin_t[...]
        kb_ref[...] = _dot_nt(wkb_t[...], h)
        vb_ref[...] = _dot_nt(wvb_t[...], h)
    else:
        krope_ref[...] = kr[:, :A_ROPE]
    k2_ref[:, :LANES] = ckv.astype(BF16)
    k2_ref[:, LANES:] = kr.astype(BF16)
    cq = _rms(_dot(h, wqa_ref[...]), qn_ref[...]).astype(BF16)
    qn = _dot(cq, wuqn_ref[...]).astype(BF16)
    for p in range(N_PAIRS):
        ql = _dot(qn[:, p * LANES:(p + 1) * LANES], wuk_ref[p])
        q2_ref[:, (2 * p) * 256:(2 * p) * 256 + LANES] = ql[:, :LANES].astype(BF16)
        q2_ref[:, (2 * p + 1) * 256:(2 * p + 1) * 256 + LANES] = ql[:, LANES:].astype(BF16)
    qr = _dot(cq, wuqr_ref[...])
    qrs = _dot(cq, wuqrs_ref[...])
    for hd in range(A_HEADS):
        rot = qr[:, hd * LANES:(hd + 1) * LANES] * cos + qrs[:, hd * LANES:(hd + 1) * LANES] * sin
        q2_ref[:, hd * 256 + LANES:(hd + 1) * 256] = rot.astype(BF16)
    ga_ref[...] = _dot(h, wga_ref[...])
    gb_ref[...] = _dot(h, wgb_ref[...])
    qb = _dot(h, wqb_ref[...]) * QK_SCALE
    kb = _dot(h, wkb_ref[...])
    vb = _dot(h, wvb_ref[...])
    if not rows_minor:
        kb_ref[...] = kb
        vb_ref[...] = vb
    for p in range(N_PAIRS):
        sl = slice(p * LANES, (p + 1) * LANES)
        qb_ref[p] = qb[:, sl].astype(BF16)
        kb16_ref[p] = kb[:, sl].astype(BF16)
        vb16_ref[p] = vb[:, sl].astype(BF16)


def _state_specs(rows, seq_len):
    sds = jax.ShapeDtypeStruct
    if seq_len is None:
        return (lambda n: pl.BlockSpec((TM, n), lambda i: (i, 0))), (lambda n: sds((rows, n), F32))
    nt = seq_len // TM
    return ((lambda n: pl.BlockSpec((None, n, TM), lambda i: (i // nt, 0, i % nt))),
            (lambda n: sds((rows // seq_len, n, seq_len), F32)))


def _in_even(x, gpre, cos, sin, w, seq_len=None, tabs_t=None):
    rows = x.shape[0]
    n_tab = cos.shape[0] // TM
    row_spec = lambda n: pl.BlockSpec((TM, n), lambda i: (i, 0))
    pm_spec = pl.BlockSpec((N_PAIRS, TM, LANES), lambda i: (0, i, 0))
    tab_spec = pl.BlockSpec((TM, LANES), lambda i: (i % n_tab, 0))
    weights = [w['wqa'], w['wkv'], w['wkr'], w['wkrs'], w['wga'], w['wqb'], w['wkb'], w['wvb'], w['wgb'],
               w['qn'], w['kvn'], w['wuqn'], w['wuqr'], w['wuqrs'], w['wuk']]
    in_specs = ([row_spec(D_MODEL), _const_spec((1, D_MODEL)), tab_spec, tab_spec]
                + [_const_spec(a.shape) for a in weights])
    args = [x, gpre, cos, sin, *weights]
    rows_minor = seq_len is not None
    if rows_minor:
        extra = [w['wkr_t'], w['wkrs_t'], w['wkb_t'], w['wvb_t']]
        tab_t_spec = pl.BlockSpec((A_ROPE, TM), lambda i: (0, i % n_tab))
        in_specs += [_const_spec(a.shape) for a in extra] + [tab_t_spec, tab_t_spec]
        args += extra + list(tabs_t)
    st_spec, st_shape = _state_specs(rows, seq_len)
    sds = jax.ShapeDtypeStruct
    pm = sds((N_PAIRS, rows, LANES), BF16)
    return pl.pallas_call(
        functools.partial(_in_even_body, rows_minor=rows_minor),
        grid=(rows // TM,),
        in_specs=in_specs,
        out_specs=[row_spec(A_KV_LORA), st_spec(A_ROPE), row_spec(256), row_spec(A_HEADS * 256),
                   row_spec(WIDTH), row_spec(WIDTH), pm_spec, st_spec(WIDTH), pm_spec,
                   st_spec(WIDTH), pm_spec],
        out_shape=[sds((rows, A_KV_LORA), F32), st_shape(A_ROPE), sds((rows, 256), BF16),
                   sds((rows, A_HEADS * 256), BF16), sds((rows, WIDTH), F32), sds((rows, WIDTH), F32),
                   pm, st_shape(WIDTH), pm, st_shape(WIDTH), pm],
        compiler_params=_params("parallel"),
        name="in_proj_even",
    )(*args)


def _in_odd_body(*refs, rows_minor):
    (x_ref, gpre_ref, fb_ref, wqc_ref, wkc_ref, wvc_ref, wgc_ref, wqd_ref, wkd_ref, wvd_ref, wf_ref,
     wgd_ref) = refs[:12]
    n_in = 18 if rows_minor else 12
    (qc_ref, kc_ref, kc16_ref, vc_ref, vc16_ref, gc_ref, qd_ref, kd_ref, kd16_ref, vd_ref, vd16_ref,
     logf_ref, gd_ref) = refs[n_in:]
    h = _rms(x_ref[...], gpre_ref[...]).astype(BF16)
    gc_ref[...] = _dot(h, wgc_ref[...])
    gd_ref[...] = _dot(h, wgd_ref[...])
    if rows_minor:
        wkc_t, wvc_t, wkd_t, wvd_t, wf_t, fb_t = refs[12:18]
        logf_ref[...] = _log_sigmoid(_dot_nt(wf_t[...], h) + fb_t[...])
        for w_t, o_ref in ((wkc_t, kc_ref), (wvc_t, vc_ref), (wkd_t, kd_ref), (wvd_t, vd_ref)):
            o_ref[...] = _dot_nt(w_t[...], h)
    else:
        logf_ref[...] = _log_sigmoid(_dot(h, wf_ref[...])[:, :HEADS] + fb_ref[...])
    for q_w, k_w, v_w, q_o, k_o, k16_o, v_o, v16_o in (
            (wqc_ref, wkc_ref, wvc_ref, qc_ref, kc_ref, kc16_ref, vc_ref, vc16_ref),
            (wqd_ref, wkd_ref, wvd_ref, qd_ref, kd_ref, kd16_ref, vd_ref, vd16_ref)):
        q = _dot(h, q_w[...]) * QK_SCALE
        k = _dot(h, k_w[...])
        v = _dot(h, v_w[...])
        if not rows_minor:
            k_o[...] = k
            v_o[...] = v
        for p in range(N_PAIRS):
            sl = slice(p * LANES, (p + 1) * LANES)
            q_o[p] = q[:, sl].astype(BF16)
            k16_o[p] = k[:, sl].astype(BF16)
            v16_o[p] = v[:, sl].astype(BF16)


def _in_odd(x, gpre, w, seq_len=None):
    rows = x.shape[0]
    row_spec = lambda n: pl.BlockSpec((TM, n), lambda i: (i, 0))
    pm_spec = pl.BlockSpec((N_PAIRS, TM, LANES), lambda i: (0, i, 0))
    weights = [w['wqc'], w['wkc'], w['wvc'], w['wgc'], w['wqd'], w['wkd'], w['wvd'], w['wf'], w['wgd']]
    rows_minor = seq_len is not None
    if rows_minor:
        weights += [w['wkc_t'], w['wvc_t'], w['wkd_t'], w['wvd_t'], w['wf_t'], w['fb_t']]
    st_spec, st_shape = _state_specs(rows, seq_len)
    sds = jax.ShapeDtypeStruct
    pm = sds((N_PAIRS, rows, LANES), BF16)
    full = sds((rows, WIDTH), F32)
    return pl.pallas_call(
        functools.partial(_in_odd_body, rows_minor=rows_minor),
        grid=(rows // TM,),
        in_specs=[row_spec(D_MODEL), _const_spec((1, D_MODEL)), _const_spec((1, HEADS))]
                 + [_const_spec(a.shape) for a in weights],
        out_specs=[pm_spec, st_spec(WIDTH), pm_spec, st_spec(WIDTH), pm_spec, row_spec(WIDTH),
                   pm_spec, st_spec(WIDTH), pm_spec, st_spec(WIDTH), pm_spec, st_spec(HEADS),
                   row_spec(WIDTH)],
        out_shape=[pm, st_shape(WIDTH), pm, st_shape(WIDTH), pm, full, pm, st_shape(WIDTH), pm,
                   st_shape(WIDTH), pm, st_shape(HEADS), full],
        compiler_params=_params("parallel"),
        name="in_proj_odd",
    )(x, gpre, w['fb'], *weights)


def _out_body(*refs, mla):
    if mla:
        x_ref, gpost_ref, g1_ref, g2_ref, a_ref, b_ref, wuv_ref, wout_ref, o_ref, mix_ref = refs
    else:
        x_ref, gpost_ref, g1_ref, g2_ref, a_ref, b_ref, wout_ref, o_ref, mix_ref = refs
    s1 = _silu(g1_ref[...])
    s2 = _silu(g2_ref[...])
    for p in range(N_PAIRS):
        sl = slice(p * LANES, (p + 1) * LANES)
        if mla:
            a = _dot(a_ref[:, p * 256:(p + 1) * 256], wuv_ref[p])
        else:
            a = a_ref[p]
        mix_ref[:, sl] = (s1[:, sl] * a).astype(BF16)
        mix_ref[:, WIDTH + p * LANES:WIDTH + (p + 1) * LANES] = (s2[:, sl] * b_ref[p]).astype(BF16)
    y = _dot(mix_ref[...], wout_ref[...])
    o_ref[...] = x_ref[...] + _rms(y, gpost_ref[...])


def _out_proj(x, gpost, g1, g2, a, b, wout, wuv=None):
    rows = x.shape[0]
    mla = wuv is not None
    row_spec = lambda n: pl.BlockSpec((TM, n), lambda i: (i, 0))
    pm_spec = pl.BlockSpec((N_PAIRS, TM, LANES), lambda i: (0, i, 0))
    in_specs = [row_spec(D_MODEL), _const_spec((1, D_MODEL)), row_spec(WIDTH), row_spec(WIDTH),
                row_spec(A_HEADS * A_KV_LORA) if mla else pm_spec, pm_spec]
    args = [x, gpost, g1, g2, a, b]
    if mla:
        in_specs.append(_const_spec(wuv.shape))
        args.append(wuv)
    in_specs.append(_const_spec(wout.shape))
    args.append(wout)
    return pl.pallas_call(
        functools.partial(_out_body, mla=mla),
        grid=(rows // TM,),
        in_specs=in_specs,
        out_specs=row_spec(D_MODEL),
        out_shape=jax.ShapeDtypeStruct((rows, D_MODEL), F32),
        scratch_shapes=[pltpu.VMEM((TM, 2 * WIDTH), BF16)],
        compiler_params=_params("parallel"),
        name="out_proj_even" if mla else "out_proj_odd",
    )(*args)


def _softmax_init(m_ref, accl_ref):
    m_ref[...] = jnp.full(m_ref.shape, NEG, F32)
    accl_ref[...] = jnp.zeros(accl_ref.shape, F32)


def _lanes(x, n):
    parts = [x] * (n // LANES)
    if n % LANES:
        parts.append(x[:, :n % LANES])
    return parts[0] if len(parts) == 1 else jnp.concatenate(parts, axis=1)


def _scores(q, k, kv_t):
    return _dot(q, k) if kv_t else _dot_nt(q, k)


def _weighted(p, v, kv_t):
    return _dot_nt(p, v) if kv_t else _dot(p, v)


def _softmax_update(s, v, m_ref, accl_ref, kv_t=False):
    keys = s.shape[1]
    n = v.shape[0] if kv_t else v.shape[1]
    m_prev = m_ref[...]
    m_new = jnp.maximum(m_prev, jnp.max(s, axis=1, keepdims=True))
    alpha = jnp.exp(m_prev - m_new)
    p = jnp.exp(s - _lanes(m_new, keys))
    v1 = jnp.concatenate([v, jnp.ones(v.shape, BF16)], axis=0 if kv_t else 1)
    accl_ref[...] = _lanes(alpha, 2 * n) * accl_ref[...] + _weighted(p.astype(BF16), v1, kv_t)
    m_ref[...] = m_new


def _softmax_result(accl_ref):
    accl = accl_ref[...]
    n = accl.shape[1] // 2
    return accl[:, :n] / accl[:, n:]


def _fill_suffix_ones(tri_ref):
    n = tri_ref.shape[0]
    r = lax.broadcasted_iota(jnp.int32, (n, n), 0)
    c = lax.broadcasted_iota(jnp.int32, (n, n), 1)
    tri_ref[...] = jnp.where(r > c, 1.0, 0.0).astype(BF16)


def _stick_tile(qs, k2, v2, tri, acc_ref, car_ref, mask, kv_t=False):
    z = _scores(qs, k2, kv_t)
    lb = _log_sigmoid(z)
    l1 = lb - z
    if mask is not None:
        l1 = jnp.where(mask, l1, 0.0)
    hi = l1.astype(BF16)
    lo = (l1 - hi.astype(F32)).astype(BF16)
    suf = _dot(hi, tri) + _dot(lo, tri) + car_ref[...]
    w = jnp.exp(lb + suf)
    if mask is not None:
        w = jnp.where(mask, w, 0.0)
    acc_ref[...] += _weighted(w.astype(BF16), v2, kv_t)
    car_ref[...] += jnp.sum(l1, axis=1, keepdims=True)


def _stick_alive(car_ref):
    return (jnp.max(car_ref[...]) >= STICK_DEAD).astype(jnp.int32)


def _local_causal(rows, keys, tq, strict):
    r = lax.broadcasted_iota(jnp.int32, (rows, keys), 0) & (tq - 1)
    c = lax.broadcasted_iota(jnp.int32, (rows, keys), 1)
    return (c < r) if strict else (c <= r)


def _decay_tile(qs, k2, v2, cq, ck, causal, m_ref, accl_ref, kv_t=False):
    tq, keys = cq.shape[0], ck.shape[1]
    s = _scores(qs, k2, kv_t)
    for hh in range(2):
        rows = slice(hh * tq, (hh + 1) * tq)
        sh = s[rows] + (cq[:, hh:hh + 1] - ck[hh:hh + 1])
        if causal:
            sh = jnp.where(_local_causal(tq, keys, tq, strict=False), sh, NEG)
        _softmax_update(sh, v2, m_ref.at[rows], accl_ref.at[rows], kv_t)


def _mla_prompt_body(q_ref, k_ref, o_ref, m_ref, accl_ref):
    g = pl.program_id(1)
    rows = CHUNK * A_HEADS
    grp = 4 * rows

    def update(r0, n_rows, k):
        sl = pl.ds(r0, n_rows)
        s = _dot_nt(q_ref[sl, :], k) * A_SCALE
        _softmax_update(s, k[:, :A_KV_LORA], m_ref.at[sl], accl_ref.at[sl])

    def group(gl, carry):
        gq = g * 2 + gl
        g0 = pl.multiple_of(gl * grp, grp)
        _softmax_init(m_ref.at[pl.ds(g0, grp)], accl_ref.at[pl.ds(g0, grp)])

        def kv(j, c2):
            k = k_ref[pl.ds(pl.multiple_of(j * TK, TK), TK), :]
            for t in range(grp // MLA_RT):
                update(g0 + t * MLA_RT, MLA_RT, k)
            return c2

        lax.fori_loop(0, gq, kv, 0)
        kd = k_ref[pl.ds(pl.multiple_of(gq * TK, TK), TK), :]
        for ci in range(4):
            for t in range(rows // MLA_RT):
                update(g0 + ci * rows + t * MLA_RT, MLA_RT, kd[:CHUNK * (ci + 1)])
        o_ref[pl.ds(g0, grp), :] = _softmax_result(accl_ref.at[pl.ds(g0, grp)]).astype(BF16)
        return carry

    lax.fori_loop(0, 2, group, 0)


def _mla_prompt(q2, k2, n_b, n_s):
    rows_q = n_s * A_HEADS
    qg = rows_q // 4
    assert qg == 8 * CHUNK * A_HEADS and TK == 4 * CHUNK
    return pl.pallas_call(
        _mla_prompt_body,
        grid=(n_b, 4),
        in_specs=[pl.BlockSpec((None, qg, 256), lambda b, g: (b, g, 0)),
                  pl.BlockSpec((None, n_s, 256), lambda b, g: (b, 0, 0))],
        out_specs=pl.BlockSpec((None, qg, A_KV_LORA), lambda b, g: (b, g, 0)),
        out_shape=jax.ShapeDtypeStruct((n_b, rows_q, A_KV_LORA), BF16),
        scratch_shapes=[pltpu.VMEM((qg, LANES), F32), pltpu.VMEM((qg, 2 * LANES), F32)],
        compiler_params=_params("parallel", "parallel"),
        name="mla_prompt",
    )(q2.reshape(n_b, rows_q, 256), k2.reshape(n_b, n_s, 256))


def _mla_sample_body(q_ref, kn_ref, ckv_ref, kr_ref, o_ref, m_ref, accl_ref):
    j = pl.program_id(1)

    @pl.when(j == 0)
    def _():
        _softmax_init(m_ref, accl_ref)

    n_tiles = q_ref.shape[0] // MLA_RT
    ck = ckv_ref[...].astype(BF16)
    kr = kr_ref[...].astype(BF16)
    for t in range(n_tiles):
        sl = slice(t * MLA_RT, (t + 1) * MLA_RT)
        s = (_dot_nt(q_ref[sl, :A_KV_LORA], ck)
             + _dot(q_ref[sl, A_KV_LORA:A_KV_LORA + A_ROPE], kr)) * A_SCALE
        _softmax_update(s, ck, m_ref.at[sl], accl_ref.at[sl])

    @pl.when(j == pl.num_programs(1) - 1)
    def _():
        kn = kn_ref[...]
        for t in range(n_tiles):
            sl = slice(t * MLA_RT, (t + 1) * MLA_RT)
            s = _dot_nt(q_ref[sl, :], kn) * A_SCALE
            _softmax_update(s, kn[:, :A_KV_LORA], m_ref.at[sl], accl_ref.at[sl])
        o_ref[...] = _softmax_result(accl_ref).astype(BF16)


def _mla_sample(q2, k2, cache_ckv, cache_kr, n_b, n_s):
    rows = n_s * A_HEADS
    n_past = cache_ckv.shape[1]
    return pl.pallas_call(
        _mla_sample_body,
        grid=(n_b, n_past // TKC),
        in_specs=[pl.BlockSpec((None, rows, 256), lambda b, j: (b, 0, 0)),
                  pl.BlockSpec((None, n_s, 256), lambda b, j: (b, 0, 0)),
                  pl.BlockSpec((None, TKC, A_KV_LORA), lambda b, j: (b, j, 0)),
                  pl.BlockSpec((None, A_ROPE, TKC), lambda b, j: (b, 0, j))],
        out_specs=pl.BlockSpec((None, rows, A_KV_LORA), lambda b, j: (b, 0, 0)),
        out_shape=jax.ShapeDtypeStruct((n_b, rows, A_KV_LORA), BF16),
        scratch_shapes=[pltpu.VMEM((rows, LANES), F32), pltpu.VMEM((rows, 2 * LANES), F32)],
        compiler_params=_params("parallel", "arbitrary"),
        name="mla_sample",
    )(q2.reshape(n_b, rows, 256), k2.reshape(n_b, n_s, 256), cache_ckv, cache_kr)


def _sb_prompt_body(q_ref, k_ref, v_ref, o_ref, tri_ref, acc_ref, car_ref):
    n_s = q_ref.shape[1]
    _fill_suffix_ones(tri_ref)

    def qblock(i, carry):
        q0 = pl.multiple_of(i * TQ, TQ)
        qs = [_stack_pair(q_ref[g, pl.ds(q0, TQ), :]) for g in range(PAIRS_PER_STEP)]
        acc_ref[...] = jnp.zeros(acc_ref.shape, F32)
        car_ref[...] = jnp.zeros(car_ref.shape, F32)

        def tiles(k0, mask):
            for g in range(PAIRS_PER_STEP):
                _stick_tile(qs[g], k_ref[g, pl.ds(k0, TK), :], v_ref[g, pl.ds(k0, TK), :], tri_ref[...],
                            acc_ref.at[g], car_ref.at[g], mask)

        tiles(q0, _local_causal(2 * TQ, TK, TQ, strict=True))

        def kv(state):
            jj, _ = state
            tiles(pl.multiple_of((i - 1 - jj) * TK, TK), None)
            return jj + 1, _stick_alive(car_ref)

        lax.while_loop(lambda st: (st[0] < i) & (st[1] > 0), kv, (jnp.int32(0), _stick_alive(car_ref)))
        for g in range(PAIRS_PER_STEP):
            o_ref[g, pl.ds(q0, TQ), :] = _merge_pair(acc_ref[g], TQ)
        return carry

    lax.fori_loop(0, n_s // TQ, qblock, 0)


def _pair_seq_spec(n_s):
    return pl.BlockSpec((PAIRS_PER_STEP, n_s, LANES), lambda b, g: (g, b, 0))


def _sb_prompt(q, k, v, n_b, n_s):
    assert TQ == TK
    spec = _pair_seq_spec(n_s)
    return pl.pallas_call(
        _sb_prompt_body,
        grid=(n_b, N_PAIRS // PAIRS_PER_STEP),
        in_specs=[spec, spec, spec],
        out_specs=spec,
        out_shape=jax.ShapeDtypeStruct((N_PAIRS, n_b * n_s, LANES), F32),
        scratch_shapes=[pltpu.VMEM((TK, TK), BF16), pltpu.VMEM((PAIRS_PER_STEP, 2 * TQ, LANES), F32),
                        pltpu.VMEM((PAIRS_PER_STEP, 2 * TQ, 1), F32)],
        compiler_params=_params("parallel", "parallel"),
        name="sb_prompt",
    )(q, k, v)


def _sb_sample_body(q_ref, kn_ref, vn_ref, ck_hbm, cv_hbm, o_ref, kbuf, vbuf, sem, tri_ref, acc_ref,
                    car_ref):
    b = pl.program_id(0)
    n_q = q_ref.shape[1]
    n_blk = ck_hbm.shape[2] // TK

    def block_copies(blk):
        rows = pl.ds(pl.multiple_of(blk * TK, TK), TK)
        return (pltpu.make_async_copy(ck_hbm.at[b, :, rows], kbuf, sem.at[0]),
                pltpu.make_async_copy(cv_hbm.at[b, :, rows], vbuf, sem.at[1]))

    def start(blk):
        for c in block_copies(blk):
            c.start()

    def cache_block(blk):
        for c in block_copies(blk):
            c.wait()
        for p in range(N_PAIRS):
            sl = slice(p * LANES, (p + 1) * LANES)
            _stick_tile(qs[p], kbuf[sl, :].astype(BF16), vbuf[sl, :].astype(BF16), tri_ref[...],
                        acc_ref.at[p], car_ref.at[p], None, kv_t=True)

    start(n_blk - 1)
    _fill_suffix_ones(tri_ref)
    acc_ref[...] = jnp.zeros(acc_ref.shape, F32)
    car_ref[...] = jnp.zeros(car_ref.shape, F32)
    qs = [_stack_pair(q_ref[p]) for p in range(N_PAIRS)]
    mask = _local_causal(2 * n_q, n_q, n_q, strict=True)
    for p in range(N_PAIRS):
        _stick_tile(qs[p], kn_ref[p], vn_ref[p], tri_ref[:n_q, :n_q], acc_ref.at[p], car_ref.at[p], mask)
    cache_block(n_blk - 1)

    def older(state):
        blk, _ = state
        start(blk)
        cache_block(blk)
        return blk - 1, _stick_alive(car_ref)

    lax.while_loop(lambda st: (st[0] >= 0) & (st[1] > 0), older,
                   (jnp.int32(n_blk - 2), _stick_alive(car_ref)))
    for p in range(N_PAIRS):
        o_ref[p] = _merge_pair(acc_ref[p], n_q)


def _sb_sample(q, kn, vn, cache_kt, cache_vt, n_b, n_s):
    new_spec = pl.BlockSpec((N_PAIRS, n_s, LANES), lambda b: (0, b, 0))
    hbm = pl.BlockSpec(memory_space=pl.ANY)
    return pl.pallas_call(
        _sb_sample_body,
        grid=(n_b,),
        in_specs=[new_spec, new_spec, new_spec, hbm, hbm],
        out_specs=new_spec,
        out_shape=jax.ShapeDtypeStruct((N_PAIRS, n_b * n_s, LANES), F32),
        scratch_shapes=[pltpu.VMEM((WIDTH, TK), F32), pltpu.VMEM((WIDTH, TK), F32),
                        pltpu.SemaphoreType.DMA((2,)), pltpu.VMEM((TK, TK), BF16),
                        pltpu.VMEM((N_PAIRS, 2 * n_s, LANES), F32), pltpu.VMEM((N_PAIRS, 2 * n_s, 1), F32)],
        compiler_params=_params("arbitrary"),
        name="sb_sample",
    )(q, kn, vn, cache_kt, cache_vt)


def _band_block(qs, kwin, vwin, bias, valid_from, tq):
    s = _dot_nt(qs, kwin) + bias
    if valid_from is not None:
        col = lax.broadcasted_iota(jnp.int32, s.shape, 1)
        s = jnp.where(col >= valid_from, s, NEG)
    p = jnp.exp(s - jnp.max(s, axis=1, keepdims=True))
    o = _dot(p.astype(BF16), vwin) / jnp.sum(p, axis=1, keepdims=True)
    return _merge_pair(o, tq)


def _band_prompt_body(q_ref, k_ref, v_ref, bias_ref, o_ref, kpad_ref, vpad_ref):
    n_s = q_ref.shape[1]
    win = LEFT_CTX + BAND_TQ
    zeros = jnp.zeros((PAIRS_PER_STEP, LEFT_CTX, LANES), BF16)
    kpad_ref[:, :LEFT_CTX, :] = zeros
    vpad_ref[:, :LEFT_CTX, :] = zeros
    kpad_ref[:, LEFT_CTX:, :] = k_ref[...]
    vpad_ref[:, LEFT_CTX:, :] = v_ref[...]

    def qblock(i, carry):
        q0 = pl.multiple_of(i * BAND_TQ, BAND_TQ)
        for g in range(PAIRS_PER_STEP):
            qs = _stack_pair(q_ref[g, pl.ds(q0, BAND_TQ), :])
            o_ref[g, pl.ds(q0, BAND_TQ), :] = _band_block(
                qs, kpad_ref[g, pl.ds(q0, win), :], vpad_ref[g, pl.ds(q0, win), :], bias_ref[g],
                LEFT_CTX - q0, BAND_TQ)
        return carry

    lax.fori_loop(0, n_s // BAND_TQ, qblock, 0)


def _band_prompt(q, k, v, bias, n_b, n_s):
    spec = _pair_seq_spec(n_s)
    win = LEFT_CTX + BAND_TQ
    return pl.pallas_call(
        _band_prompt_body,
        grid=(n_b, N_PAIRS // PAIRS_PER_STEP),
        in_specs=[spec, spec, spec,
                  pl.BlockSpec((PAIRS_PER_STEP, 2 * BAND_TQ, win), lambda b, g: (g, 0, 0))],
        out_specs=spec,
        out_shape=jax.ShapeDtypeStruct((N_PAIRS, n_b * n_s, LANES), F32),
        scratch_shapes=[pltpu.VMEM((PAIRS_PER_STEP, LEFT_CTX + n_s, LANES), BF16),
                        pltpu.VMEM((PAIRS_PER_STEP, LEFT_CTX + n_s, LANES), BF16)],
        compiler_params=_params("parallel", "parallel"),
        name="band_prompt",
    )(q, k, v, bias)


def _band_sample_body(q_ref, kn_ref, vn_ref, kc_ref, vc_ref, bias_ref, o_ref):
    n_q = q_ref.shape[1]
    n_keep = kc_ref.shape[1]
    for p in range(N_PAIRS):
        sl = slice(p * LANES, (p + 1) * LANES)
        qs = _stack_pair(q_ref[p])
        bias = bias_ref[p]
        s_old = _dot(qs, kc_ref[sl, :].astype(BF16)) + bias[:, :n_keep]
        s_new = _dot_nt(qs, kn_ref[p]) + bias[:, n_keep:]
        m = jnp.maximum(jnp.max(s_old, axis=1, keepdims=True), jnp.max(s_new, axis=1, keepdims=True))
        p_old = jnp.exp(s_old - m)
        p_new = jnp.exp(s_new - m)
        o = _dot_nt(p_old.astype(BF16), vc_ref[sl, :].astype(BF16)) + _dot(p_new.astype(BF16), vn_ref[p])
        total = jnp.sum(p_old, axis=1, keepdims=True) + jnp.sum(p_new, axis=1, keepdims=True)
        o_ref[p] = _merge_pair(o / total, n_q)


def _band_sample(q, kn, vn, cache_kt, cache_vt, bias, n_b, n_s):
    n_keep = cache_kt.shape[2]
    new_spec = pl.BlockSpec((N_PAIRS, n_s, LANES), lambda b: (0, b, 0))
    cache_spec = pl.BlockSpec((None, WIDTH, n_keep), lambda b: (b, 0, 0))
    return pl.pallas_call(
        _band_sample_body,
        grid=(n_b,),
        in_specs=[new_spec, new_spec, new_spec, cache_spec, cache_spec,
                  _const_spec((N_PAIRS, 2 * n_s, n_keep + n_s))],
        out_specs=new_spec,
        out_shape=jax.ShapeDtypeStruct((N_PAIRS, n_b * n_s, LANES), F32),
        compiler_params=_params("parallel"),
        name="band_sample",
    )(q, kn, vn, cache_kt, cache_vt, bias)


def _band_bias(rel_bias, tq):
    win = LEFT_CTX + tq
    i = np.arange(tq)[:, None]
    w = np.arange(win)[None, :]
    qc, kc = i // CHUNK, w // CHUNK - LEFT_CTX // CHUNK
    ok = (kc <= qc) & (kc >= qc - LEFT_CTX // CHUNK)
    u = np.arange(win + tq - 1)
    rel = np.clip(LEFT_CTX + (tq - 1) - u, -REL_CLIP, REL_CLIP) + REL_CLIP
    diag = rel_bias.astype(F32)[:, rel]
    n = win + tq - 1
    flat = jnp.tile(diag, (1, tq))[:, tq - 1:tq - 1 + tq * (n - 1)]
    tab = flat.reshape(HEADS, tq, n - 1)[:, :, :win]
    tab = jnp.where(jnp.asarray(ok)[None], tab, NEG)
    return tab.reshape(N_PAIRS, 2 * tq, win)


def _fox_prompt_body(q_ref, k_ref, v_ref, cq_ref, ck_ref, o_ref, m_ref, accl_ref):
    n_s = q_ref.shape[1]

    def qblock(i, carry):
        q0 = pl.multiple_of(i * TQ, TQ)
        qs = [_stack_pair(q_ref[g, pl.ds(q0, TQ), :]) for g in range(PAIRS_PER_STEP)]
        cq = [cq_ref[g, pl.ds(q0, TQ), :] for g in range(PAIRS_PER_STEP)]
        _softmax_init(m_ref, accl_ref)

        def tile(jb, causal):
            k0 = pl.multiple_of(jb * TK, TK)
            for g in range(PAIRS_PER_STEP):
                _decay_tile(qs[g], k_ref[g, pl.ds(k0, TK), :], v_ref[g, pl.ds(k0, TK), :], cq[g],
                            ck_ref[g, jb], causal, m_ref.at[g], accl_ref.at[g])

        tile(i, True)

        def kv(jb, c2):
            tile(jb, False)
            return c2

        lax.fori_loop(0, i, kv, 0)
        for g in range(PAIRS_PER_STEP):
            o_ref[g, pl.ds(q0, TQ), :] = _merge_pair(_softmax_result(accl_ref.at[g]), TQ)
        return carry

    lax.fori_loop(0, n_s // TQ, qblock, 0)


def _fox_prompt(q, k, v, cum, n_b, n_s):
    assert TQ == TK
    spec = _pair_seq_spec(n_s)
    cq = cum.reshape(n_b, N_PAIRS, 2, n_s).transpose(0, 1, 3, 2)
    ck = cum.reshape(n_b, N_PAIRS, 2, n_s // TK, TK).transpose(0, 1, 3, 2, 4)
    g_ = PAIRS_PER_STEP
    return pl.pallas_call(
        _fox_prompt_body,
        grid=(n_b, N_PAIRS // g_),
        in_specs=[spec, spec, spec,
                  pl.BlockSpec((None, g_, n_s, 2), lambda b, g: (b, g, 0, 0)),
                  pl.BlockSpec((None, g_, n_s // TK, 2, TK), lambda b, g: (b, g, 0, 0, 0))],
        out_specs=spec,
        out_shape=jax.ShapeDtypeStruct((N_PAIRS, n_b * n_s, LANES), F32),
        scratch_shapes=[pltpu.VMEM((g_, 2 * TQ, LANES), F32), pltpu.VMEM((g_, 2 * TQ, 2 * LANES), F32)],
        compiler_params=_params("parallel", "parallel"),
        name="fox_prompt",
    )(q, k, v, cq, ck)


def _fox_sample_body(q_ref, kn_ref, vn_ref, kc_ref, vc_ref, cq_ref, ckn_ref, ckc_ref, o_ref,
                     m_ref, accl_ref):
    j = pl.program_id(1)
    n_q = q_ref.shape[1]
    cq = cq_ref[...]

    def pair_update(p, k2, v2, ck, new_rows):
        rows = slice(2 * p * n_q, (2 * p + 2) * n_q)
        _decay_tile(_stack_pair(q_ref[p]), k2, v2, cq[:, 2 * p:2 * p + 2], ck[2 * p:2 * p + 2], new_rows,
                    m_ref.at[rows], accl_ref.at[rows], kv_t=not new_rows)

    @pl.when(j == 0)
    def _():
        _softmax_init(m_ref, accl_ref)
        ckn = ckn_ref[...]
        for p in range(N_PAIRS):
            pair_update(p, kn_ref[p], vn_ref[p], ckn, True)

    @pl.when(j > 0)
    def _():
        ckc = ckc_ref[...]
        for p in range(N_PAIRS):
            sl = slice(p * LANES, (p + 1) * LANES)
            pair_update(p, kc_ref[sl, :].astype(BF16), vc_ref[sl, :].astype(BF16), ckc, False)

    @pl.when(j == pl.num_programs(1) - 1)
    def _():
        for p in range(N_PAIRS):
            rows = slice(2 * p * n_q, (2 * p + 2) * n_q)
            o_ref[p] = _merge_pair(_softmax_result(accl_ref.at[rows]), n_q)


def _fox_sample(q, kn, vn, cache_k, cache_v, cum, n_b, n_s):
    n_past = cache_k.shape[2]
    n_blk = n_past // TKC
    cq = cum[..., n_past:].transpose(0, 2, 1)
    ckn = cum[..., n_past:]
    ckc = cum[..., :n_past]
    new_spec = pl.BlockSpec((N_PAIRS, n_s, LANES), lambda b, j: (0, b, 0))
    cache_blk = lambda j: jnp.maximum(j - 1, 0)
    cache_spec = pl.BlockSpec((None, WIDTH, TKC), lambda b, j: (b, 0, cache_blk(j)))
    return pl.pallas_call(
        _fox_sample_body,
        grid=(n_b, n_blk + 1),
        in_specs=[new_spec, new_spec, new_spec, cache_spec, cache_spec,
                  pl.BlockSpec((None, n_s, HEADS), lambda b, j: (b, 0, 0)),
                  pl.BlockSpec((None, HEADS, n_s), lambda b, j: (b, 0, 0)),
                  pl.BlockSpec((None, HEADS, TKC), lambda b, j: (b, 0, cache_blk(j)))],
        out_specs=new_spec,
        out_shape=jax.ShapeDtypeStruct((N_PAIRS, n_b * n_s, LANES), F32),
        scratch_shapes=[pltpu.VMEM((HEADS * n_s, LANES), F32), pltpu.VMEM((HEADS * n_s, 2 * LANES), F32)],
        compiler_params=_params("parallel", "arbitrary"),
        name="fox_sample",
    )(q, kn, vn, cache_k, cache_v, cq, ckn, ckc)


def _cumsum_body(x_ref, o_ref):
    n_rows, n_cols = x_ref.shape
    r = lax.broadcasted_iota(jnp.int32, (LANES, LANES), 0)
    c = lax.broadcasted_iota(jnp.int32, (LANES, LANES), 1)
    ones = jnp.where(r <= c, 1.0, 0.0).astype(BF16)
    total = jnp.zeros((n_rows, 1), F32)
    for g in range(n_cols // LANES):
        x = x_ref[:, g * LANES:(g + 1) * LANES]
        h1 = x.astype(BF16)
        r1 = x - h1.astype(F32)
        h2 = r1.astype(BF16)
        h3 = (r1 - h2.astype(F32)).astype(BF16)
        y = _dot(h1, ones) + _dot(h2, ones) + _dot(h3, ones) + total
        o_ref[:, g * LANES:(g + 1) * LANES] = y
        total = y[:, LANES - 1:LANES]


def _cumsum_rows(x):
    rows, n = x.shape
    n_pad = -(-n // LANES) * LANES
    xp = jnp.pad(x, ((0, 0), (0, n_pad - n)))
    out = pl.pallas_call(
        _cumsum_body,
        out_shape=jax.ShapeDtypeStruct((rows, n_pad), F32),
        compiler_params=pltpu.CompilerParams(vmem_limit_bytes=VMEM_LIMIT),
        name="cumsum_rows",
    )(xp)
    return out[:, :n]


def _split_cols(w, sizes):
    out, off = [], 0
    for n in sizes:
        out.append(w[:, off:off + n])
        off += n
    return out


def _rope_tables(pos, n_rows):
    half = A_ROPE // 2
    inv_freq = ROPE_THETA ** (-jnp.arange(half, dtype=F32) / half)
    ang = pos.astype(F32)[:, None] * inv_freq[None, :]
    cos, sin = jnp.cos(ang), jnp.sin(ang)
    zeros = jnp.zeros((pos.shape[0], LANES - A_ROPE), F32)
    cos_t = jnp.concatenate([cos, cos, zeros], axis=1)
    sin_t = jnp.concatenate([-sin, sin, zeros], axis=1)
    reps = max(1, n_rows // pos.shape[0])
    return jnp.tile(cos_t, (reps, 1)), jnp.tile(sin_t, (reps, 1))


def _swap_halves(w):
    half = w.shape[-1] // 2
    return jnp.concatenate([w[..., half:], w[..., :half]], axis=-1)


def _pad_lanes(w):
    return jnp.pad(w, [(0, 0)] * (w.ndim - 1) + [(0, LANES - w.shape[-1])])


def _prep_even(w_in, q_norm, w_uq, kv_norm, w_uk, w_uv):
    wqa, wkv, wkr, wga, wqb, wkb, wvb, wgb = _split_cols(
        w_in, (A_Q_LORA, A_KV_LORA, A_ROPE, WIDTH, WIDTH, WIDTH, WIDTH, WIDTH))
    b = lambda a: a.astype(BF16)
    uq_rope = w_uq[:, :, A_NOPE:]
    uk_t = jnp.transpose(w_uk, (1, 2, 0))
    z = jnp.zeros((A_NOPE, A_KV_LORA), w_uk.dtype)
    wuk = jnp.stack([jnp.block([[uk_t[2 * p], z], [z, uk_t[2 * p + 1]]]) for p in range(N_PAIRS)])
    uv_t = jnp.transpose(w_uv, (1, 0, 2))
    zv = jnp.zeros((A_KV_LORA, A_V), w_uv.dtype)
    wuv = jnp.stack([jnp.block([[uv_t[2 * p], zv], [zv, uv_t[2 * p + 1]]]) for p in range(N_PAIRS)])
    return dict(
        wqa=b(wqa), wkv=b(wkv), wkr=b(_pad_lanes(wkr)), wkrs=b(_pad_lanes(_swap_halves(wkr))),
        wga=b(wga), wqb=b(wqb), wkb=b(wkb), wvb=b(wvb), wgb=b(wgb),
        qn=q_norm.reshape(1, -1), kvn=kv_norm.reshape(1, -1),
        wuqn=b(w_uq[:, :, :A_NOPE].reshape(A_Q_LORA, A_HEADS * A_NOPE)),
        wuqr=b(_pad_lanes(uq_rope).reshape(A_Q_LORA, A_HEADS * LANES)),
        wuqrs=b(_pad_lanes(_swap_halves(uq_rope)).reshape(A_Q_LORA, A_HEADS * LANES)),
        wuk=b(wuk), wuv=b(wuv),
        wkr_t=b(wkr.T), wkrs_t=b(_swap_halves(wkr).T), wkb_t=b(wkb.T), wvb_t=b(wvb.T))


def _prep_odd(w_in, forget_bias):
    wqc, wkc, wvc, wgc, wqd, wkd, wvd, wf, wgd = _split_cols(
        w_in, (WIDTH, WIDTH, WIDTH, WIDTH, WIDTH, WIDTH, WIDTH, HEADS, WIDTH))
    b = lambda a: a.astype(BF16)
    return dict(wqc=b(wqc), wkc=b(wkc), wvc=b(wvc), wgc=b(wgc), wqd=b(wqd), wkd=b(wkd), wvd=b(wvd),
                wf=b(_pad_lanes(wf)), wgd=b(wgd), fb=forget_bias.astype(F32).reshape(1, HEADS),
                wkc_t=b(wkc.T), wvc_t=b(wvc.T), wkd_t=b(wkd.T), wvd_t=b(wvd.T), wf_t=b(wf.T),
                fb_t=forget_bias.astype(F32).reshape(HEADS, 1))


def kernel(x_prompt, x_sample, cache_mla_ckv, cache_mla_krope, cache_sb_k, cache_sb_v, cache_band_k,
           cache_band_v, cache_fox_k, cache_fox_v, cache_fox_logf, norm_pre, norm_post, w_in_even,
           a_q_norm, a_w_uq, a_kv_norm, a_w_uk, a_w_uv, w_out_even, w_in_odd, c_rel_bias,
           d_forget_bias, w_out_odd):
    n_b, n_s, _ = x_prompt.shape
    d_b, d_s, _ = x_sample.shape
    n_past = cache_sb_k.shape[2]
    n_keep = cache_band_k.shape[2]
    assert n_s % (2 * TQ) == 0 and n_past % TKC == 0 and d_s == CHUNK and n_past % CHUNK == 0
    assert n_keep == LEFT_CTX and (d_b * d_s) % TM == 0 and TM % d_s == 0

    xp = x_prompt.reshape(n_b * n_s, D_MODEL)
    xs = x_sample.reshape(d_b * d_s, D_MODEL)
    row = lambda a: a.reshape(1, -1)
    heads = lambda a, b, s: a.reshape(b, s, HEADS, HEAD_DIM)
    rows_minor = lambda a: a.transpose(0, 2, 3, 1).reshape(a.shape[0], WIDTH, a.shape[1])

    we = _prep_even(w_in_even[0], a_q_norm[0], a_w_uq[0], a_kv_norm[0], a_w_uk[0], a_w_uv[0])
    wout_e = w_out_even[0].astype(BF16)
    cos_p, sin_p = _rope_tables(jnp.arange(n_s), TM)
    cos_s, sin_s = _rope_tables(n_past + jnp.arange(d_s), TM)

    tabs_t = (cos_p[:n_s, :A_ROPE].T, sin_p[:n_s, :A_ROPE].T)
    (ckv_p, kr_p, k2_p, q2_p, ga_p, gb_p, qb_p, kb_p, kb16_p, vb_p, vb16_p) = _in_even(
        xp, row(norm_pre[0]), cos_p, sin_p, we, seq_len=n_s, tabs_t=tabs_t)
    (ckv_s, kr_s, k2_s, q2_s, ga_s, gb_s, qb_s, kb_s, kb16_s, vb_s, vb16_s) = _in_even(
        xs, row(norm_pre[0]), cos_s, sin_s, we)

    lat_p = _mla_prompt(q2_p, k2_p, n_b, n_s).reshape(n_b * n_s, A_HEADS * A_KV_LORA)
    lat_s = _mla_sample(q2_s, k2_s, cache_mla_ckv[0], cache_mla_krope[0].transpose(0, 2, 1), d_b, d_s
                        ).reshape(d_b * d_s, A_HEADS * A_KV_LORA)
    sb_p = _sb_prompt(qb_p, kb16_p, vb16_p, n_b, n_s)
    sb_s = _sb_sample(qb_s, kb16_s, vb16_s, rows_minor(cache_sb_k[0]), rows_minor(cache_sb_v[0]), d_b, d_s)

    xp1 = _out_proj(xp, row(norm_post[0]), ga_p, gb_p, lat_p, sb_p, wout_e, we['wuv'])
    xs1 = _out_proj(xs, row(norm_post[0]), ga_s, gb_s, lat_s, sb_s, wout_e, we['wuv'])

    wo = _prep_odd(w_in_odd[0], d_forget_bias[0])
    wout_o = w_out_odd[0].astype(BF16)
    (qc_p, kc_p, kc16_p, vc_p, vc16_p, gc_p, qd_p, kd_p, kd16_p, vd_p, vd16_p, lf_p, gd_p) = _in_odd(
        xp1, row(norm_pre[1]), wo, seq_len=n_s)
    (qc_s, kc_s, kc16_s, vc_s, vc16_s, gc_s, qd_s, kd_s, kd16_s, vd_s, vd16_s, lf_s, gd_s) = _in_odd(
        xs1, row(norm_pre[1]), wo)

    band_p = _band_prompt(qc_p, kc16_p, vc16_p, _band_bias(c_rel_bias[0], BAND_TQ), n_b, n_s)
    band_s = _band_sample(qc_s, kc16_s, vc16_s, rows_minor(cache_band_k[0]), rows_minor(cache_band_v[0]),
                          _band_bias(c_rel_bias[0], d_s), d_b, d_s)

    lf_s3 = lf_s.reshape(d_b, d_s, HEADS)
    cum_p = _cumsum_rows(lf_p.reshape(n_b * HEADS, n_s)).reshape(n_b, HEADS, n_s)
    lf_all = jnp.concatenate([cache_fox_logf[0].astype(F32), lf_s3], axis=1)
    cum_s = _cumsum_rows(lf_all.transpose(0, 2, 1).reshape(d_b * HEADS, n_past + d_s)
                         ).reshape(d_b, HEADS, n_past + d_s)
    fox_p = _fox_prompt(qd_p, kd16_p, vd16_p, cum_p, n_b, n_s)
    fox_s = _fox_sample(qd_s, kd16_s, vd16_s, rows_minor(cache_fox_k[0]), rows_minor(cache_fox_v[0]), cum_s,
                        d_b, d_s)

    xp2 = _out_proj(xp1, row(norm_post[1]), gc_p, gd_p, band_p, fox_p, wout_o)
    xs2 = _out_proj(xs1, row(norm_post[1]), gc_s, gd_s, band_s, fox_s, wout_o)

    keep = min(LEFT_CTX, n_s)
    band_k_s = jnp.concatenate([cache_band_k[0], heads(kc_s, d_b, d_s)], axis=1)[:, d_s:]
    band_v_s = jnp.concatenate([cache_band_v[0], heads(vc_s, d_b, d_s)], axis=1)[:, d_s:]
    one = lambda a: a[None]
    heads_t = lambda a: a.reshape(n_b, HEADS, HEAD_DIM, a.shape[-1]).transpose(0, 3, 1, 2)
    return (xp2.reshape(n_b, n_s, D_MODEL), xs2.reshape(d_b, d_s, D_MODEL),
            one(ckv_p.reshape(n_b, n_s, A_KV_LORA)), one(kr_p.transpose(0, 2, 1)),
            one(heads_t(kb_p)), one(heads_t(vb_p)),
            one(heads_t(kc_p[:, :, n_s - keep:])), one(heads_t(vc_p[:, :, n_s - keep:])),
            one(heads_t(kd_p)), one(heads_t(vd_p)), one(lf_p.transpose(0, 2, 1)),
            one(ckv_s.reshape(d_b, d_s, A_KV_LORA)), one(kr_s.reshape(d_b, d_s, A_ROPE)),
            one(heads(kb_s, d_b, d_s)), one(heads(vb_s, d_b, d_s)),
            one(band_k_s), one(band_v_s),
            one(heads(kd_s, d_b, d_s)), one(heads(vd_s, d_b, d_s)), one(lf_s3))
```

```python
import functools

import numpy as np
import jax
import jax.numpy as jnp
from jax import lax
from jax.experimental import pallas as pl
from jax.experimental.pallas import tpu as pltpu

F32 = jnp.float32
BF16 = jnp.bfloat16

D_MODEL = 1024
PAST_LEN = 4096
CHUNK = 64
LEFT_CTX = 512
REL_CLIP = 128
EPS = 1e-6
NEG = -1e30
ROPE_THETA = 10000.0
A_HEADS = 8
A_Q_LORA = 256
A_KV_LORA = 128
A_NOPE = 64
A_ROPE = 32
A_V = 64
A_SCALE = (A_NOPE + A_ROPE) ** -0.5
HEADS = 8
HEAD_DIM = 64
WIDTH = HEADS * HEAD_DIM
QK_SCALE = HEAD_DIM ** -0.5
LOG2E = 1.4426950408889634
MLA_LOGIT_SCALE = A_SCALE * LOG2E
N_PAIRS = HEADS // 2

LANES = 128
VMEM_LIMIT = 52 * 1024 * 1024
TM = 256
TQ = 256
TK = 256
TKC = 1024
FOX_TK = 1024
BAND_TQ = 128
MLA_RT = 256
MLA_QB = 512
PAIRS_PER_STEP = 4
STICK_DEAD = -104.0


def _dot(a, b):
    return jnp.dot(a, b, preferred_element_type=F32)


def _dot_nt(a, b):
    return lax.dot_general(a, b, (((1,), (1,)), ((), ())), preferred_element_type=F32)


def _rms(x, g):
    y = x * lax.rsqrt(jnp.mean(x * x, axis=-1, keepdims=True) + EPS)
    return y * g


def _log_sigmoid(z):
    return jnp.minimum(z, 0.0) - jnp.log1p(jnp.exp(-jnp.abs(z)))


def _silu(g):
    return g / (1.0 + jnp.exp(-g))


def _stack_pair(q2):
    qf = q2.astype(F32)
    lane = lax.broadcasted_iota(jnp.int32, qf.shape, 1)
    even = jnp.where(lane < HEAD_DIM, qf, 0.0)
    odd = jnp.where(lane >= HEAD_DIM, qf, 0.0)
    return jnp.concatenate([even, odd], axis=0).astype(BF16)


def _merge_pair(o, tq):
    top, bot = o[:tq], o[tq:]
    lane = lax.broadcasted_iota(jnp.int32, top.shape, 1)
    return jnp.where(lane < HEAD_DIM, top, bot)


def _params(*sem):
    return pltpu.CompilerParams(dimension_semantics=sem, vmem_limit_bytes=VMEM_LIMIT)


def _const_spec(shape):
    nd = len(shape)
    return pl.BlockSpec(shape, lambda *_: (0,) * nd)


def _in_even_body(*refs, rows_minor):
    (x_ref, gpre_ref, cos_ref, sin_ref, wqa_ref, wkv_ref, wkr_ref, wkrs_ref, wga_ref, wqb_ref, wkb_ref,
     wvb_ref, wgb_ref, qn_ref, kvn_ref, wuqn_ref, wuqr_ref, wuqrs_ref, wuk_ref) = refs[:19]
    n_in = 25 if rows_minor else 19
    (ckv_ref, krope_ref, k2_ref, q2_ref, ga_ref, gb_ref, qb_ref, kb_ref, kb16_ref, vb_ref,
     vb16_ref) = refs[n_in:]
    h = _rms(x_ref[...], gpre_ref[...]).astype(BF16)
    cos = cos_ref[...]
    sin = sin_ref[...]
    ckv = _rms(_dot(h, wkv_ref[...]), kvn_ref[...])
    ckv_ref[...] = ckv
    kr = _dot(h, wkr_ref[...]) * cos + _dot(h, wkrs_ref[...]) * sin
    if rows_minor:
        wkr_t, wkrs_t, wkb_t, wvb_t, cos_t, sin_t = refs[19:25]
        krope_ref[...] = _dot_nt(wkr_t[...], h) * cos_t[...] + _dot_nt(wkrs_t[...], h) * sin_t[...]
        kb_ref[...] = _dot_nt(wkb_t[...], h)
        vb_ref[...] = _dot_nt(wvb_t[...], h)
    else:
        krope_ref[...] = kr[:, :A_ROPE]
    k2_ref[:, :LANES] = ckv.astype(BF16)
    k2_ref[:, LANES:] = kr.astype(BF16)
    cq = _rms(_dot(h, wqa_ref[...]), qn_ref[...]).astype(BF16)
    qn = _dot(cq, wuqn_ref[...]).astype(BF16)
    for p in range(N_PAIRS):
        ql = _dot(qn[:, p * LANES:(p + 1) * LANES], wuk_ref[p])
        q2_ref[:, (2 * p) * 256:(2 * p) * 256 + LANES] = ql[:, :LANES].astype(BF16)
        q2_ref[:, (2 * p + 1) * 256:(2 * p + 1) * 256 + LANES] = ql[:, LANES:].astype(BF16)
    qr = _dot(cq, wuqr_ref[...])
    qrs = _dot(cq, wuqrs_ref[...])
    for hd in range(A_HEADS):
        rot = qr[:, hd * LANES:(hd + 1) * LANES] * cos + qrs[:, hd * LANES:(hd + 1) * LANES] * sin
        q2_ref[:, hd * 256 + LANES:(hd + 1) * 256] = rot.astype(BF16)
    ga_ref[...] = _dot(h, wga_ref[...])
    gb_ref[...] = _dot(h, wgb_ref[...])
    qb = _dot(h, wqb_ref[...]) * QK_SCALE
    kb = _dot(h, wkb_ref[...])
    vb = _dot(h, wvb_ref[...])
    if not rows_minor:
        kb_ref[...] = kb
        vb_ref[...] = vb
    for p in range(N_PAIRS):
        sl = slice(p * LANES, (p + 1) * LANES)
        qb_ref[p] = qb[:, sl].astype(BF16)
        kb16_ref[p] = kb[:, sl].astype(BF16)
        vb16_ref[p] = vb[:, sl].astype(BF16)


def _state_specs(rows, seq_len):
    sds = jax.ShapeDtypeStruct
    if seq_len is None:
        return (lambda n: pl.BlockSpec((TM, n), lambda i: (i, 0))), (lambda n: sds((rows, n), F32))
    nt = seq_len // TM
    return ((lambda n: pl.BlockSpec((None, n, TM), lambda i: (i // nt, 0, i % nt))),
            (lambda n: sds((rows // seq_len, n, seq_len), F32)))


def _in_even(x, gpre, cos, sin, w, seq_len=None, tabs_t=None):
    rows = x.shape[0]
    n_tab = cos.shape[0] // TM
    row_spec = lambda n: pl.BlockSpec((TM, n), lambda i: (i, 0))
    pm_spec = pl.BlockSpec((N_PAIRS, TM, LANES), lambda i: (0, i, 0))
    tab_spec = pl.BlockSpec((TM, LANES), lambda i: (i % n_tab, 0))
    weights = [w['wqa'], w['wkv'], w['wkr'], w['wkrs'], w['wga'], w['wqb'], w['wkb'], w['wvb'], w['wgb'],
               w['qn'], w['kvn'], w['wuqn'], w['wuqr'], w['wuqrs'], w['wuk']]
    in_specs = ([row_spec(D_MODEL), _const_spec((1, D_MODEL)), tab_spec, tab_spec]
                + [_const_spec(a.shape) for a in weights])
    args = [x, gpre, cos, sin, *weights]
    rows_minor = seq_len is not None
    if rows_minor:
        extra = [w['wkr_t'], w['wkrs_t'], w['wkb_t'], w['wvb_t']]
        tab_t_spec = pl.BlockSpec((A_ROPE, TM), lambda i: (0, i % n_tab))
        in_specs += [_const_spec(a.shape) for a in extra] + [tab_t_spec, tab_t_spec]
        args += extra + list(tabs_t)
    st_spec, st_shape = _state_specs(rows, seq_len)
    sds = jax.ShapeDtypeStruct
    pm = sds((N_PAIRS, rows, LANES), BF16)
    return pl.pallas_call(
        functools.partial(_in_even_body, rows_minor=rows_minor),
        grid=(rows // TM,),
        in_specs=in_specs,
        out_specs=[row_spec(A_KV_LORA), st_spec(A_ROPE), row_spec(256), row_spec(A_HEADS * 256),
                   row_spec(WIDTH), row_spec(WIDTH), pm_spec, st_spec(WIDTH), pm_spec,
                   st_spec(WIDTH), pm_spec],
        out_shape=[sds((rows, A_KV_LORA), F32), st_shape(A_ROPE), sds((rows, 256), BF16),
                   sds((rows, A_HEADS * 256), BF16), sds((rows, WIDTH), F32), sds((rows, WIDTH), F32),
                   pm, st_shape(WIDTH), pm, st_shape(WIDTH), pm],
        compiler_params=_params("parallel"),
        name="in_proj_even",
    )(*args)


def _in_odd_body(*refs, rows_minor):
    (x_ref, gpre_ref, fb_ref, wqc_ref, wkc_ref, wvc_ref, wgc_ref, wqd_ref, wkd_ref, wvd_ref, wf_ref,
     wgd_ref) = refs[:12]
    n_in = 18 if rows_minor else 12
    (qc_ref, kc_ref, kc16_ref, vc_ref, vc16_ref, gc_ref, qd_ref, kd_ref, kd16_ref, vd_ref, vd16_ref,
     logf_ref, gd_ref) = refs[n_in:]
    h = _rms(x_ref[...], gpre_ref[...]).astype(BF16)
    gc_ref[...] = _dot(h, wgc_ref[...])
    gd_ref[...] = _dot(h, wgd_ref[...])
    if rows_minor:
        wkc_t, wvc_t, wkd_t, wvd_t, wf_t, fb_t = refs[12:18]
        logf_ref[...] = _log_sigmoid(_dot_nt(wf_t[...], h) + fb_t[...])
        for w_t, o_ref in ((wkc_t, kc_ref), (wvc_t, vc_ref), (wkd_t, kd_ref), (wvd_t, vd_ref)):
            o_ref[...] = _dot_nt(w_t[...], h)
    else:
        logf_ref[...] = _log_sigmoid(_dot(h, wf_ref[...])[:, :HEADS] + fb_ref[...])
    for q_w, k_w, v_w, q_o, k_o, k16_o, v_o, v16_o in (
            (wqc_ref, wkc_ref, wvc_ref, qc_ref, kc_ref, kc16_ref, vc_ref, vc16_ref),
            (wqd_ref, wkd_ref, wvd_ref, qd_ref, kd_ref, kd16_ref, vd_ref, vd16_ref)):
        q = _dot(h, q_w[...]) * (QK_SCALE * LOG2E)
        k = _dot(h, k_w[...])
        v = _dot(h, v_w[...])
        if not rows_minor:
            k_o[...] = k
            v_o[...] = v
        for p in range(N_PAIRS):
            sl = slice(p * LANES, (p + 1) * LANES)
            q_o[p] = q[:, sl].astype(BF16)
            k16_o[p] = k[:, sl].astype(BF16)
            v16_o[p] = v[:, sl].astype(BF16)


def _in_odd(x, gpre, w, seq_len=None):
    rows = x.shape[0]
    row_spec = lambda n: pl.BlockSpec((TM, n), lambda i: (i, 0))
    pm_spec = pl.BlockSpec((N_PAIRS, TM, LANES), lambda i: (0, i, 0))
    weights = [w['wqc'], w['wkc'], w['wvc'], w['wgc'], w['wqd'], w['wkd'], w['wvd'], w['wf'], w['wgd']]
    rows_minor = seq_len is not None
    if rows_minor:
        weights += [w['wkc_t'], w['wvc_t'], w['wkd_t'], w['wvd_t'], w['wf_t'], w['fb_t']]
    st_spec, st_shape = _state_specs(rows, seq_len)
    sds = jax.ShapeDtypeStruct
    pm = sds((N_PAIRS, rows, LANES), BF16)
    full = sds((rows, WIDTH), F32)
    return pl.pallas_call(
        functools.partial(_in_odd_body, rows_minor=rows_minor),
        grid=(rows // TM,),
        in_specs=[row_spec(D_MODEL), _const_spec((1, D_MODEL)), _const_spec((1, HEADS))]
                 + [_const_spec(a.shape) for a in weights],
        out_specs=[pm_spec, st_spec(WIDTH), pm_spec, st_spec(WIDTH), pm_spec, row_spec(WIDTH),
                   pm_spec, st_spec(WIDTH), pm_spec, st_spec(WIDTH), pm_spec, st_spec(HEADS),
                   row_spec(WIDTH)],
        out_shape=[pm, st_shape(WIDTH), pm, st_shape(WIDTH), pm, full, pm, st_shape(WIDTH), pm,
                   st_shape(WIDTH), pm, st_shape(HEADS), full],
        compiler_params=_params("parallel"),
        name="in_proj_odd",
    )(x, gpre, w['fb'], *weights)


def _out_body(*refs, mla):
    if mla:
        x_ref, gpost_ref, g1_ref, g2_ref, a_ref, b_ref, wuv_ref, wout_ref, o_ref, mix_ref = refs
    else:
        x_ref, gpost_ref, g1_ref, g2_ref, a_ref, b_ref, wout_ref, o_ref, mix_ref = refs
    s1 = _silu(g1_ref[...])
    s2 = _silu(g2_ref[...])
    for p in range(N_PAIRS):
        sl = slice(p * LANES, (p + 1) * LANES)
        if mla:
            a = _dot(a_ref[:, p * 256:(p + 1) * 256], wuv_ref[p])
        else:
            a = a_ref[p]
        mix_ref[:, sl] = (s1[:, sl] * a).astype(BF16)
        mix_ref[:, WIDTH + p * LANES:WIDTH + (p + 1) * LANES] = (s2[:, sl] * b_ref[p]).astype(BF16)
    y = _dot(mix_ref[...], wout_ref[...])
    o_ref[...] = x_ref[...] + _rms(y, gpost_ref[...])


def _out_proj(x, gpost, g1, g2, a, b, wout, wuv=None):
    rows = x.shape[0]
    mla = wuv is not None
    row_spec = lambda n: pl.BlockSpec((TM, n), lambda i: (i, 0))
    pm_spec = pl.BlockSpec((N_PAIRS, TM, LANES), lambda i: (0, i, 0))
    in_specs = [row_spec(D_MODEL), _const_spec((1, D_MODEL)), row_spec(WIDTH), row_spec(WIDTH),
                row_spec(A_HEADS * A_KV_LORA) if mla else pm_spec, pm_spec]
    args = [x, gpost, g1, g2, a, b]
    if mla:
        in_specs.append(_const_spec(wuv.shape))
        args.append(wuv)
    in_specs.append(_const_spec(wout.shape))
    args.append(wout)
    return pl.pallas_call(
        functools.partial(_out_body, mla=mla),
        grid=(rows // TM,),
        in_specs=in_specs,
        out_specs=row_spec(D_MODEL),
        out_shape=jax.ShapeDtypeStruct((rows, D_MODEL), F32),
        scratch_shapes=[pltpu.VMEM((TM, 2 * WIDTH), BF16)],
        compiler_params=_params("parallel"),
        name="out_proj_even" if mla else "out_proj_odd",
    )(*args)


def _softmax_init(m_ref, accl_ref):
    m_ref[...] = jnp.full(m_ref.shape, NEG, F32)
    accl_ref[...] = jnp.zeros(accl_ref.shape, F32)


def _lanes(x, n):
    parts = [x] * (n // LANES)
    if n % LANES:
        parts.append(x[:, :n % LANES])
    return parts[0] if len(parts) == 1 else jnp.concatenate(parts, axis=1)


def _scores(q, k, kv_t):
    return _dot(q, k) if kv_t else _dot_nt(q, k)


def _weighted(p, v, kv_t):
    return _dot_nt(p, v) if kv_t else _dot(p, v)


def _softmax_update(s, v, m_ref, accl_ref, kv_t=False):
    keys = s.shape[1]
    n = v.shape[0] if kv_t else v.shape[1]
    m_prev = m_ref[...]
    m_new = jnp.maximum(m_prev, jnp.max(s, axis=1, keepdims=True))
    alpha = jnp.exp2(m_prev - m_new)
    p = jnp.exp2(s - _lanes(m_new, keys))
    v1 = jnp.concatenate([v, jnp.ones(v.shape, BF16)], axis=0 if kv_t else 1)
    accl_ref[...] = _lanes(alpha, 2 * n) * accl_ref[...] + _weighted(p.astype(BF16), v1, kv_t)
    m_ref[...] = m_new


def _softmax_result(accl_ref):
    accl = accl_ref[...]
    n = accl.shape[1] // 2
    return accl[:, :n] / accl[:, n:]


def _fill_suffix_ones(tri_ref):
    n = tri_ref.shape[0]
    r = lax.broadcasted_iota(jnp.int32, (n, n), 0)
    c = lax.broadcasted_iota(jnp.int32, (n, n), 1)
    tri_ref[...] = jnp.where(r > c, 1.0, 0.0).astype(BF16)


def _stick_tile(qs, k2, v2, tri, acc_ref, car_ref, mask, kv_t=False):
    z = _scores(qs, k2, kv_t)
    lb = _log_sigmoid(z)
    l1 = lb - z
    if mask is not None:
        l1 = jnp.where(mask, l1, 0.0)
    hi = l1.astype(BF16)
    lo = (l1 - hi.astype(F32)).astype(BF16)
    suf = _dot(hi, tri) + _dot(lo, tri) + car_ref[...]
    w = jnp.exp(lb + suf)
    if mask is not None:
        w = jnp.where(mask, w, 0.0)
    acc_ref[...] += _weighted(w.astype(BF16), v2, kv_t)
    car_ref[...] += jnp.sum(l1, axis=1, keepdims=True)


def _stick_alive(car_ref):
    return (jnp.max(car_ref[...]) >= STICK_DEAD).astype(jnp.int32)


def _local_causal(rows, keys, tq, strict):
    r = lax.broadcasted_iota(jnp.int32, (rows, keys), 0) & (tq - 1)
    c = lax.broadcasted_iota(jnp.int32, (rows, keys), 1)
    return (c < r) if strict else (c <= r)


def _decay_tile(qs, k2, v2, cq, ck, causal, m_ref, accl_ref, kv_t=False):
    tq, keys = cq.shape[0], ck.shape[1]
    s = _scores(qs, k2, kv_t)
    for hh in range(2):
        rows = slice(hh * tq, (hh + 1) * tq)
        sh = s[rows] + (cq[:, hh:hh + 1] - ck[hh:hh + 1])
        if causal:
            sh = jnp.where(_local_causal(tq, keys, tq, strict=False), sh, NEG)
        _softmax_update(sh, v2, m_ref.at[rows], accl_ref.at[rows], kv_t)


def _mla_prompt_body(q_ref, k_ref, o_ref, m_ref, accl_ref):
    g = pl.program_id(1)
    grp = 4 * CHUNK

    def update(h, r0, k, mask):
        sl = pl.ds(r0, grp)
        s = _dot_nt(q_ref[sl, h * 256:(h + 1) * 256], k) * MLA_LOGIT_SCALE
        if mask is not None:
            s = jnp.where(mask, s, NEG)
        _softmax_update(s, k[:, :A_KV_LORA], m_ref.at[h, sl], accl_ref.at[h, sl])

    def group(gl, carry):
        gq = g * (MLA_QB // grp) + gl
        g0 = pl.multiple_of(gl * grp, grp)
        for h in range(A_HEADS):
            _softmax_init(m_ref.at[h, pl.ds(g0, grp)], accl_ref.at[h, pl.ds(g0, grp)])

        def kv(j, c2):
            k = k_ref[pl.ds(pl.multiple_of(j * TK, TK), TK), :]
            for h in range(A_HEADS):
                update(h, g0, k, None)
            return c2

        lax.fori_loop(0, gq, kv, 0)
        kd = k_ref[pl.ds(pl.multiple_of(gq * TK, TK), TK), :]
        row_chunk = lax.broadcasted_iota(jnp.int32, (grp, TK), 0) // CHUNK
        key_chunk = lax.broadcasted_iota(jnp.int32, (grp, TK), 1) // CHUNK
        own = key_chunk <= row_chunk
        for h in range(A_HEADS):
            update(h, g0, kd, own)
        for h in range(A_HEADS):
            o_ref[pl.ds(g0, grp), h * A_KV_LORA:(h + 1) * A_KV_LORA] = _softmax_result(
                accl_ref.at[h, pl.ds(g0, grp)]).astype(BF16)
        return carry

    lax.fori_loop(0, MLA_QB // grp, group, 0)


def _mla_prompt(q2, k2, n_b, n_s):
    assert n_s % MLA_QB == 0 and TK == 4 * CHUNK
    return pl.pallas_call(
        _mla_prompt_body,
        grid=(n_b, n_s // MLA_QB),
        in_specs=[pl.BlockSpec((None, MLA_QB, A_HEADS * 256), lambda b, g: (b, g, 0)),
                  pl.BlockSpec((None, n_s, 256), lambda b, g: (b, 0, 0))],
        out_specs=pl.BlockSpec((None, MLA_QB, A_HEADS * A_KV_LORA), lambda b, g: (b, g, 0)),
        out_shape=jax.ShapeDtypeStruct((n_b, n_s, A_HEADS * A_KV_LORA), BF16),
        scratch_shapes=[pltpu.VMEM((A_HEADS, MLA_QB, LANES), F32),
                        pltpu.VMEM((A_HEADS, MLA_QB, 2 * LANES), F32)],
        compiler_params=_params("parallel", "parallel"),
        name="mla_prompt",
    )(q2.reshape(n_b, n_s, A_HEADS * 256), k2.reshape(n_b, n_s, 256))


def _mla_sample_body(q_ref, kn_ref, ckv_ref, kr_ref, o_ref, m_ref, accl_ref):
    j = pl.program_id(1)

    @pl.when(j == 0)
    def _():
        _softmax_init(m_ref, accl_ref)

    n_tiles = q_ref.shape[0] // MLA_RT
    ck = ckv_ref[...].astype(BF16)
    kr = kr_ref[...].T.astype(BF16)
    k = jnp.concatenate([ck, kr, jnp.zeros((TKC, LANES - A_ROPE), BF16)], axis=1)
    for t in range(n_tiles):
        sl = slice(t * MLA_RT, (t + 1) * MLA_RT)
        s = _dot_nt(q_ref[sl, :], k) * MLA_LOGIT_SCALE
        _softmax_update(s, ck, m_ref.at[sl], accl_ref.at[sl])

    @pl.when(j == pl.num_programs(1) - 1)
    def _():
        kn = kn_ref[...]
        for t in range(n_tiles):
            sl = slice(t * MLA_RT, (t + 1) * MLA_RT)
            s = _dot_nt(q_ref[sl, :], kn) * MLA_LOGIT_SCALE
            _softmax_update(s, kn[:, :A_KV_LORA], m_ref.at[sl], accl_ref.at[sl])
        o_ref[...] = _softmax_result(accl_ref).astype(BF16)


def _mla_sample(q2, k2, cache_ckv, cache_kr, n_b, n_s):
    rows = n_s * A_HEADS
    n_past = cache_ckv.shape[1]
    return pl.pallas_call(
        _mla_sample_body,
        grid=(n_b, n_past // TKC),
        in_specs=[pl.BlockSpec((None, rows, 256), lambda b, j: (b, 0, 0)),
                  pl.BlockSpec((None, n_s, 256), lambda b, j: (b, 0, 0)),
                  pl.BlockSpec((None, TKC, A_KV_LORA), lambda b, j: (b, j, 0)),
                  pl.BlockSpec((None, A_ROPE, TKC), lambda b, j: (b, 0, j))],
        out_specs=pl.BlockSpec((None, rows, A_KV_LORA), lambda b, j: (b, 0, 0)),
        out_shape=jax.ShapeDtypeStruct((n_b, rows, A_KV_LORA), BF16),
        scratch_shapes=[pltpu.VMEM((rows, LANES), F32), pltpu.VMEM((rows, 2 * LANES), F32)],
        compiler_params=_params("parallel", "arbitrary"),
        name="mla_sample",
    )(q2.reshape(n_b, rows, 256), k2.reshape(n_b, n_s, 256), cache_ckv, cache_kr)


def _sb_prompt_body(q_ref, k_ref, v_ref, o_ref, tri_ref, acc_ref, car_ref):
    n_s = q_ref.shape[1]
    _fill_suffix_ones(tri_ref)

    def qblock(i, carry):
        q0 = pl.multiple_of(i * TQ, TQ)
        qs = [_stack_pair(q_ref[g, pl.ds(q0, TQ), :]) for g in range(PAIRS_PER_STEP)]
        acc_ref[...] = jnp.zeros(acc_ref.shape, F32)
        car_ref[...] = jnp.zeros(car_ref.shape, F32)

        def tiles(k0, mask):
            for g in range(PAIRS_PER_STEP):
                _stick_tile(qs[g], k_ref[g, pl.ds(k0, TK), :], v_ref[g, pl.ds(k0, TK), :], tri_ref[...],
                            acc_ref.at[g], car_ref.at[g], mask)

        tiles(q0, _local_causal(2 * TQ, TK, TQ, strict=True))

        def kv(state):
            jj, _ = state
            tiles(pl.multiple_of((i - 1 - jj) * TK, TK), None)
            return jj + 1, _stick_alive(car_ref)

        lax.while_loop(lambda st: (st[0] < i) & (st[1] > 0), kv, (jnp.int32(0), _stick_alive(car_ref)))
        for g in range(PAIRS_PER_STEP):
            o_ref[g, pl.ds(q0, TQ), :] = _merge_pair(acc_ref[g], TQ)
        return carry

    lax.fori_loop(0, n_s // TQ, qblock, 0)


def _pair_seq_spec(n_s):
    return pl.BlockSpec((PAIRS_PER_STEP, n_s, LANES), lambda b, g: (g, b, 0))


def _sb_prompt(q, k, v, n_b, n_s):
    assert TQ == TK
    spec = _pair_seq_spec(n_s)
    return pl.pallas_call(
        _sb_prompt_body,
        grid=(n_b, N_PAIRS // PAIRS_PER_STEP),
        in_specs=[spec, spec, spec],
        out_specs=spec,
        out_shape=jax.ShapeDtypeStruct((N_PAIRS, n_b * n_s, LANES), F32),
        scratch_shapes=[pltpu.VMEM((TK, TK), BF16), pltpu.VMEM((PAIRS_PER_STEP, 2 * TQ, LANES), F32),
                        pltpu.VMEM((PAIRS_PER_STEP, 2 * TQ, 1), F32)],
        compiler_params=_params("parallel", "parallel"),
        name="sb_prompt",
    )(q, k, v)


def _sb_sample_body(q_ref, kn_ref, vn_ref, ck_hbm, cv_hbm, o_ref, kbuf, vbuf, sem, tri_ref, acc_ref,
                    car_ref):
    b = pl.program_id(0)
    n_q = q_ref.shape[1]
    n_blk = ck_hbm.shape[2] // TK

    def block_copies(blk):
        rows = pl.ds(pl.multiple_of(blk * TK, TK), TK)
        return (pltpu.make_async_copy(ck_hbm.at[b, :, rows], kbuf, sem.at[0]),
                pltpu.make_async_copy(cv_hbm.at[b, :, rows], vbuf, sem.at[1]))

    def start(blk):
        for c in block_copies(blk):
            c.start()

    def cache_block(blk):
        for c in block_copies(blk):
            c.wait()
        for p in range(N_PAIRS):
            sl = slice(p * LANES, (p + 1) * LANES)
            _stick_tile(qs[p], kbuf[sl, :].astype(BF16), vbuf[sl, :].astype(BF16), tri_ref[...],
                        acc_ref.at[p], car_ref.at[p], None, kv_t=True)

    start(n_blk - 1)
    _fill_suffix_ones(tri_ref)
    acc_ref[...] = jnp.zeros(acc_ref.shape, F32)
    car_ref[...] = jnp.zeros(car_ref.shape, F32)
    qs = [_stack_pair(q_ref[p]) for p in range(N_PAIRS)]
    mask = _local_causal(2 * n_q, n_q, n_q, strict=True)
    for p in range(N_PAIRS):
        _stick_tile(qs[p], kn_ref[p], vn_ref[p], tri_ref[:n_q, :n_q], acc_ref.at[p], car_ref.at[p], mask)
    cache_block(n_blk - 1)

    def older(state):
        blk, _ = state
        start(blk)
        cache_block(blk)
        return blk - 1, _stick_alive(car_ref)

    lax.while_loop(lambda st: (st[0] >= 0) & (st[1] > 0), older,
                   (jnp.int32(n_blk - 2), _stick_alive(car_ref)))
    for p in range(N_PAIRS):
        o_ref[p] = _merge_pair(acc_ref[p], n_q)


def _sb_sample(q, kn, vn, cache_kt, cache_vt, n_b, n_s):
    new_spec = pl.BlockSpec((N_PAIRS, n_s, LANES), lambda b: (0, b, 0))
    hbm = pl.BlockSpec(memory_space=pl.ANY)
    return pl.pallas_call(
        _sb_sample_body,
        grid=(n_b,),
        in_specs=[new_spec, new_spec, new_spec, hbm, hbm],
        out_specs=new_spec,
        out_shape=jax.ShapeDtypeStruct((N_PAIRS, n_b * n_s, LANES), F32),
        scratch_shapes=[pltpu.VMEM((WIDTH, TK), F32), pltpu.VMEM((WIDTH, TK), F32),
                        pltpu.SemaphoreType.DMA((2,)), pltpu.VMEM((TK, TK), BF16),
                        pltpu.VMEM((N_PAIRS, 2 * n_s, LANES), F32), pltpu.VMEM((N_PAIRS, 2 * n_s, 1), F32)],
        compiler_params=_params("arbitrary"),
        name="sb_sample",
    )(q, kn, vn, cache_kt, cache_vt)


def _band_block(qs, kwin, vwin, bias, valid_from, tq):
    s = _dot_nt(qs, kwin) + bias
    if valid_from is not None:
        col = lax.broadcasted_iota(jnp.int32, s.shape, 1)
        s = jnp.where(col >= valid_from, s, NEG)
    p = jnp.exp2(s - jnp.max(s, axis=1, keepdims=True))
    o = _dot(p.astype(BF16), vwin) / jnp.sum(p, axis=1, keepdims=True)
    return _merge_pair(o, tq)


def _band_prompt_body(q_ref, k_ref, v_ref, bias_ref, o_ref, kpad_ref, vpad_ref):
    n_s = q_ref.shape[1]
    win = LEFT_CTX + BAND_TQ
    zeros = jnp.zeros((PAIRS_PER_STEP, LEFT_CTX, LANES), BF16)
    kpad_ref[:, :LEFT_CTX, :] = zeros
    vpad_ref[:, :LEFT_CTX, :] = zeros
    kpad_ref[:, LEFT_CTX:, :] = k_ref[...]
    vpad_ref[:, LEFT_CTX:, :] = v_ref[...]

    def qblock(i, carry):
        q0 = pl.multiple_of(i * BAND_TQ, BAND_TQ)
        for g in range(PAIRS_PER_STEP):
            qs = _stack_pair(q_ref[g, pl.ds(q0, BAND_TQ), :])
            o_ref[g, pl.ds(q0, BAND_TQ), :] = _band_block(
                qs, kpad_ref[g, pl.ds(q0, win), :], vpad_ref[g, pl.ds(q0, win), :], bias_ref[g],
                LEFT_CTX - q0, BAND_TQ)
        return carry

    lax.fori_loop(0, n_s // BAND_TQ, qblock, 0)


def _band_prompt(q, k, v, bias, n_b, n_s):
    spec = _pair_seq_spec(n_s)
    win = LEFT_CTX + BAND_TQ
    return pl.pallas_call(
        _band_prompt_body,
        grid=(n_b, N_PAIRS // PAIRS_PER_STEP),
        in_specs=[spec, spec, spec,
                  pl.BlockSpec((PAIRS_PER_STEP, 2 * BAND_TQ, win), lambda b, g: (g, 0, 0))],
        out_specs=spec,
        out_shape=jax.ShapeDtypeStruct((N_PAIRS, n_b * n_s, LANES), F32),
        scratch_shapes=[pltpu.VMEM((PAIRS_PER_STEP, LEFT_CTX + n_s, LANES), BF16),
                        pltpu.VMEM((PAIRS_PER_STEP, LEFT_CTX + n_s, LANES), BF16)],
        compiler_params=_params("parallel", "parallel"),
        name="band_prompt",
    )(q, k, v, bias)


def _band_sample_body(q_ref, kn_ref, vn_ref, kc_ref, vc_ref, bias_ref, o_ref):
    n_q = q_ref.shape[1]
    n_keep = kc_ref.shape[1]
    for p in range(N_PAIRS):
        sl = slice(p * LANES, (p + 1) * LANES)
        qs = _stack_pair(q_ref[p])
        bias = bias_ref[p]
        s_old = _dot(qs, kc_ref[sl, :].astype(BF16)) + bias[:, :n_keep]
        s_new = _dot_nt(qs, kn_ref[p]) + bias[:, n_keep:]
        m = jnp.maximum(jnp.max(s_old, axis=1, keepdims=True), jnp.max(s_new, axis=1, keepdims=True))
        p_old = jnp.exp2(s_old - m)
        p_new = jnp.exp2(s_new - m)
        o = _dot_nt(p_old.astype(BF16), vc_ref[sl, :].astype(BF16)) + _dot(p_new.astype(BF16), vn_ref[p])
        total = jnp.sum(p_old, axis=1, keepdims=True) + jnp.sum(p_new, axis=1, keepdims=True)
        o_ref[p] = _merge_pair(o / total, n_q)


def _band_sample(q, kn, vn, cache_kt, cache_vt, bias, n_b, n_s):
    n_keep = cache_kt.shape[2]
    new_spec = pl.BlockSpec((N_PAIRS, n_s, LANES), lambda b: (0, b, 0))
    cache_spec = pl.BlockSpec((None, WIDTH, n_keep), lambda b: (b, 0, 0))
    return pl.pallas_call(
        _band_sample_body,
        grid=(n_b,),
        in_specs=[new_spec, new_spec, new_spec, cache_spec, cache_spec,
                  _const_spec((N_PAIRS, 2 * n_s, n_keep + n_s))],
        out_specs=new_spec,
        out_shape=jax.ShapeDtypeStruct((N_PAIRS, n_b * n_s, LANES), F32),
        compiler_params=_params("parallel"),
        name="band_sample",
    )(q, kn, vn, cache_kt, cache_vt, bias)


def _band_bias(rel_bias, tq):
    win = LEFT_CTX + tq
    i = np.arange(tq)[:, None]
    w = np.arange(win)[None, :]
    qc, kc = i // CHUNK, w // CHUNK - LEFT_CTX // CHUNK
    ok = (kc <= qc) & (kc >= qc - LEFT_CTX // CHUNK)
    u = np.arange(win + tq - 1)
    rel = np.clip(LEFT_CTX + (tq - 1) - u, -REL_CLIP, REL_CLIP) + REL_CLIP
    diag = rel_bias.astype(F32)[:, rel]
    n = win + tq - 1
    flat = jnp.tile(diag, (1, tq))[:, tq - 1:tq - 1 + tq * (n - 1)]
    tab = flat.reshape(HEADS, tq, n - 1)[:, :, :win]
    tab = jnp.where(jnp.asarray(ok)[None], tab * LOG2E, NEG)
    return tab.reshape(N_PAIRS, 2 * tq, win)


def _fox_prompt_body(q_ref, k_ref, v_ref, cq_ref, ck_ref, o_ref, m_ref, accl_ref):
    n_s = q_ref.shape[1]

    def qblock(i, carry):
        q0 = pl.multiple_of(i * TQ, TQ)
        qs = [_stack_pair(q_ref[g, pl.ds(q0, TQ), :]) for g in range(PAIRS_PER_STEP)]
        cq = [cq_ref[g, pl.ds(q0, TQ), :] for g in range(PAIRS_PER_STEP)]
        _softmax_init(m_ref, accl_ref)

        def tile(jb, causal):
            k0 = pl.multiple_of(jb * TK, TK)
            for g in range(PAIRS_PER_STEP):
                _decay_tile(qs[g], k_ref[g, pl.ds(k0, TK), :], v_ref[g, pl.ds(k0, TK), :], cq[g],
                            ck_ref[g, jb], causal, m_ref.at[g], accl_ref.at[g])

        tile(i, True)

        def kv(jb, c2):
            tile(jb, False)
            return c2

        lax.fori_loop(0, i, kv, 0)
        for g in range(PAIRS_PER_STEP):
            o_ref[g, pl.ds(q0, TQ), :] = _merge_pair(_softmax_result(accl_ref.at[g]), TQ)
        return carry

    lax.fori_loop(0, n_s // TQ, qblock, 0)


def _fox_prompt(q, k, v, cum, n_b, n_s):
    assert TQ == TK
    spec = _pair_seq_spec(n_s)
    cq = cum.reshape(n_b, N_PAIRS, 2, n_s).transpose(0, 1, 3, 2)
    ck = cum.reshape(n_b, N_PAIRS, 2, n_s // TK, TK).transpose(0, 1, 3, 2, 4)
    g_ = PAIRS_PER_STEP
    return pl.pallas_call(
        _fox_prompt_body,
        grid=(n_b, N_PAIRS // g_),
        in_specs=[spec, spec, spec,
                  pl.BlockSpec((None, g_, n_s, 2), lambda b, g: (b, g, 0, 0)),
                  pl.BlockSpec((None, g_, n_s // TK, 2, TK), lambda b, g: (b, g, 0, 0, 0))],
        out_specs=spec,
        out_shape=jax.ShapeDtypeStruct((N_PAIRS, n_b * n_s, LANES), F32),
        scratch_shapes=[pltpu.VMEM((g_, 2 * TQ, LANES), F32), pltpu.VMEM((g_, 2 * TQ, 2 * LANES), F32)],
        compiler_params=_params("parallel", "parallel"),
        name="fox_prompt",
    )(q, k, v, cq, ck)


def _fox_sample_body(q_ref, kn_ref, vn_ref, kc_ref, vc_ref, cq_ref, ckn_ref, ckc_ref, o_ref,
                     m_ref, accl_ref):
    j = pl.program_id(1)
    n_q = q_ref.shape[1]
    cq = cq_ref[...]

    def pair_update(p, k2, v2, ck, new_rows):
        rows = slice(2 * p * n_q, (2 * p + 2) * n_q)
        _decay_tile(_stack_pair(q_ref[p]), k2, v2, cq[:, 2 * p:2 * p + 2], ck[2 * p:2 * p + 2], new_rows,
                    m_ref.at[rows], accl_ref.at[rows], kv_t=not new_rows)

    @pl.when(j == 0)
    def _():
        _softmax_init(m_ref, accl_ref)
        ckn = ckn_ref[...]
        for p in range(N_PAIRS):
            pair_update(p, kn_ref[p], vn_ref[p], ckn, True)

    @pl.when(j > 0)
    def _():
        ckc = ckc_ref[...]
        for sub in range(TKC // FOX_TK):
            keys = slice(sub * FOX_TK, (sub + 1) * FOX_TK)
            for p in range(N_PAIRS):
                sl = slice(p * LANES, (p + 1) * LANES)
                pair_update(p, kc_ref[sl, keys].astype(BF16), vc_ref[sl, keys].astype(BF16), ckc[:, keys],
                            False)

    @pl.when(j == pl.num_programs(1) - 1)
    def _():
        for p in range(N_PAIRS):
            rows = slice(2 * p * n_q, (2 * p + 2) * n_q)
            o_ref[p] = _merge_pair(_softmax_result(accl_ref.at[rows]), n_q)


def _fox_sample(q, kn, vn, cache_k, cache_v, cum, n_b, n_s):
    n_past = cache_k.shape[2]
    n_blk = n_past // TKC
    cq = cum[..., n_past:].transpose(0, 2, 1)
    ckn = cum[..., n_past:]
    ckc = cum[..., :n_past]
    new_spec = pl.BlockSpec((N_PAIRS, n_s, LANES), lambda b, j: (0, b, 0))
    cache_blk = lambda j: jnp.maximum(j - 1, 0)
    cache_spec = pl.BlockSpec((None, WIDTH, TKC), lambda b, j: (b, 0, cache_blk(j)))
    return pl.pallas_call(
        _fox_sample_body,
        grid=(n_b, n_blk + 1),
        in_specs=[new_spec, new_spec, new_spec, cache_spec, cache_spec,
                  pl.BlockSpec((None, n_s, HEADS), lambda b, j: (b, 0, 0)),
                  pl.BlockSpec((None, HEADS, n_s), lambda b, j: (b, 0, 0)),
                  pl.BlockSpec((None, HEADS, TKC), lambda b, j: (b, 0, cache_blk(j)))],
        out_specs=new_spec,
        out_shape=jax.ShapeDtypeStruct((N_PAIRS, n_b * n_s, LANES), F32),
        scratch_shapes=[pltpu.VMEM((HEADS * n_s, LANES), F32), pltpu.VMEM((HEADS * n_s, 2 * LANES), F32)],
        compiler_params=_params("parallel", "arbitrary"),
        name="fox_sample",
    )(q, kn, vn, cache_k, cache_v, cq, ckn, ckc)


def _cumsum_body(x_ref, o_ref):
    n_rows, n_cols = x_ref.shape
    r = lax.broadcasted_iota(jnp.int32, (LANES, LANES), 0)
    c = lax.broadcasted_iota(jnp.int32, (LANES, LANES), 1)
    ones = jnp.where(r <= c, 1.0, 0.0).astype(BF16)
    total = jnp.zeros((n_rows, 1), F32)
    for g in range(n_cols // LANES):
        x = x_ref[:, g * LANES:(g + 1) * LANES]
        h1 = x.astype(BF16)
        r1 = x - h1.astype(F32)
        h2 = r1.astype(BF16)
        h3 = (r1 - h2.astype(F32)).astype(BF16)
        y = _dot(h1, ones) + _dot(h2, ones) + _dot(h3, ones) + total
        o_ref[:, g * LANES:(g + 1) * LANES] = y * LOG2E
        total = y[:, LANES - 1:LANES]


def _cumsum_rows(x):
    rows, n = x.shape
    n_pad = -(-n // LANES) * LANES
    xp = jnp.pad(x, ((0, 0), (0, n_pad - n)))
    out = pl.pallas_call(
        _cumsum_body,
        out_shape=jax.ShapeDtypeStruct((rows, n_pad), F32),
        compiler_params=pltpu.CompilerParams(vmem_limit_bytes=VMEM_LIMIT),
        name="cumsum_rows",
    )(xp)
    return out[:, :n]


def _split_cols(w, sizes):
    out, off = [], 0
    for n in sizes:
        out.append(w[:, off:off + n])
        off += n
    return out


def _rope_tables(pos, n_rows):
    half = A_ROPE // 2
    inv_freq = ROPE_THETA ** (-jnp.arange(half, dtype=F32) / half)
    ang = pos.astype(F32)[:, None] * inv_freq[None, :]
    cos, sin = jnp.cos(ang), jnp.sin(ang)
    zeros = jnp.zeros((pos.shape[0], LANES - A_ROPE), F32)
    cos_t = jnp.concatenate([cos, cos, zeros], axis=1)
    sin_t = jnp.concatenate([-sin, sin, zeros], axis=1)
    reps = max(1, n_rows // pos.shape[0])
    return jnp.tile(cos_t, (reps, 1)), jnp.tile(sin_t, (reps, 1))


def _swap_halves(w):
    half = w.shape[-1] // 2
    return jnp.concatenate([w[..., half:], w[..., :half]], axis=-1)


def _pad_lanes(w):
    return jnp.pad(w, [(0, 0)] * (w.ndim - 1) + [(0, LANES - w.shape[-1])])


def _prep_even(w_in, q_norm, w_uq, kv_norm, w_uk, w_uv):
    wqa, wkv, wkr, wga, wqb, wkb, wvb, wgb = _split_cols(
        w_in, (A_Q_LORA, A_KV_LORA, A_ROPE, WIDTH, WIDTH, WIDTH, WIDTH, WIDTH))
    b = lambda a: a.astype(BF16)
    uq_rope = w_uq[:, :, A_NOPE:]
    uk_t = jnp.transpose(w_uk, (1, 2, 0))
    z = jnp.zeros((A_NOPE, A_KV_LORA), w_uk.dtype)
    wuk = jnp.stack([jnp.block([[uk_t[2 * p], z], [z, uk_t[2 * p + 1]]]) for p in range(N_PAIRS)])
    uv_t = jnp.transpose(w_uv, (1, 0, 2))
    zv = jnp.zeros((A_KV_LORA, A_V), w_uv.dtype)
    wuv = jnp.stack([jnp.block([[uv_t[2 * p], zv], [zv, uv_t[2 * p + 1]]]) for p in range(N_PAIRS)])
    return dict(
        wqa=b(wqa), wkv=b(wkv), wkr=b(_pad_lanes(wkr)), wkrs=b(_pad_lanes(_swap_halves(wkr))),
        wga=b(wga), wqb=b(wqb), wkb=b(wkb), wvb=b(wvb), wgb=b(wgb),
        qn=q_norm.reshape(1, -1), kvn=kv_norm.reshape(1, -1),
        wuqn=b(w_uq[:, :, :A_NOPE].reshape(A_Q_LORA, A_HEADS * A_NOPE)),
        wuqr=b(_pad_lanes(uq_rope).reshape(A_Q_LORA, A_HEADS * LANES)),
        wuqrs=b(_pad_lanes(_swap_halves(uq_rope)).reshape(A_Q_LORA, A_HEADS * LANES)),
        wuk=b(wuk), wuv=b(wuv),
        wkr_t=b(wkr.T), wkrs_t=b(_swap_halves(wkr).T), wkb_t=b(wkb.T), wvb_t=b(wvb.T))


def _prep_odd(w_in, forget_bias):
    wqc, wkc, wvc, wgc, wqd, wkd, wvd, wf, wgd = _split_cols(
        w_in, (WIDTH, WIDTH, WIDTH, WIDTH, WIDTH, WIDTH, WIDTH, HEADS, WIDTH))
    b = lambda a: a.astype(BF16)
    return dict(wqc=b(wqc), wkc=b(wkc), wvc=b(wvc), wgc=b(wgc), wqd=b(wqd), wkd=b(wkd), wvd=b(wvd),
                wf=b(_pad_lanes(wf)), wgd=b(wgd), fb=forget_bias.astype(F32).reshape(1, HEADS),
                wkc_t=b(wkc.T), wvc_t=b(wvc.T), wkd_t=b(wkd.T), wvd_t=b(wvd.T), wf_t=b(wf.T),
                fb_t=forget_bias.astype(F32).reshape(HEADS, 1))


def kernel(x_prompt, x_sample, cache_mla_ckv, cache_mla_krope, cache_sb_k, cache_sb_v, cache_band_k,
           cache_band_v, cache_fox_k, cache_fox_v, cache_fox_logf, norm_pre, norm_post, w_in_even,
           a_q_norm, a_w_uq, a_kv_norm, a_w_uk, a_w_uv, w_out_even, w_in_odd, c_rel_bias,
           d_forget_bias, w_out_odd):
    n_b, n_s, _ = x_prompt.shape
    d_b, d_s, _ = x_sample.shape
    n_past = cache_sb_k.shape[2]
    n_keep = cache_band_k.shape[2]
    assert n_s % (2 * TQ) == 0 and n_past % TKC == 0 and d_s == CHUNK and n_past % CHUNK == 0
    assert n_keep == LEFT_CTX and (d_b * d_s) % TM == 0 and TM % d_s == 0

    xp = x_prompt.reshape(n_b * n_s, D_MODEL)
    xs = x_sample.reshape(d_b * d_s, D_MODEL)
    row = lambda a: a.reshape(1, -1)
    heads = lambda a, b, s: a.reshape(b, s, HEADS, HEAD_DIM)
    rows_minor = lambda a: a.transpose(0, 2, 3, 1).reshape(a.shape[0], WIDTH, a.shape[1])

    we = _prep_even(w_in_even[0], a_q_norm[0], a_w_uq[0], a_kv_norm[0], a_w_uk[0], a_w_uv[0])
    wout_e = w_out_even[0].astype(BF16)
    cos_p, sin_p = _rope_tables(jnp.arange(n_s), TM)
    cos_s, sin_s = _rope_tables(n_past + jnp.arange(d_s), TM)

    tabs_t = (cos_p[:n_s, :A_ROPE].T, sin_p[:n_s, :A_ROPE].T)
    (ckv_p, kr_p, k2_p, q2_p, ga_p, gb_p, qb_p, kb_p, kb16_p, vb_p, vb16_p) = _in_even(
        xp, row(norm_pre[0]), cos_p, sin_p, we, seq_len=n_s, tabs_t=tabs_t)
    (ckv_s, kr_s, k2_s, q2_s, ga_s, gb_s, qb_s, kb_s, kb16_s, vb_s, vb16_s) = _in_even(
        xs, row(norm_pre[0]), cos_s, sin_s, we)

    lat_p = _mla_prompt(q2_p, k2_p, n_b, n_s).reshape(n_b * n_s, A_HEADS * A_KV_LORA)
    lat_s = _mla_sample(q2_s, k2_s, cache_mla_ckv[0], cache_mla_krope[0].transpose(0, 2, 1), d_b, d_s
                        ).reshape(d_b * d_s, A_HEADS * A_KV_LORA)
    sb_p = _sb_prompt(qb_p, kb16_p, vb16_p, n_b, n_s)
    sb_s = _sb_sample(qb_s, kb16_s, vb16_s, rows_minor(cache_sb_k[0]), rows_minor(cache_sb_v[0]), d_b, d_s)

    xp1 = _out_proj(xp, row(norm_post[0]), ga_p, gb_p, lat_p, sb_p, wout_e, we['wuv'])
    xs1 = _out_proj(xs, row(norm_post[0]), ga_s, gb_s, lat_s, sb_s, wout_e, we['wuv'])

    wo = _prep_odd(w_in_odd[0], d_forget_bias[0])
    wout_o = w_out_odd[0].astype(BF16)
    (qc_p, kc_p, kc16_p, vc_p, vc16_p, gc_p, qd_p, kd_p, kd16_p, vd_p, vd16_p, lf_p, gd_p) = _in_odd(
        xp1, row(norm_pre[1]), wo, seq_len=n_s)
    (qc_s, kc_s, kc16_s, vc_s, vc16_s, gc_s, qd_s, kd_s, kd16_s, vd_s, vd16_s, lf_s, gd_s) = _in_odd(
        xs1, row(norm_pre[1]), wo)

    band_p = _band_prompt(qc_p, kc16_p, vc16_p, _band_bias(c_rel_bias[0], BAND_TQ), n_b, n_s)
    band_s = _band_sample(qc_s, kc16_s, vc16_s, rows_minor(cache_band_k[0]), rows_minor(cache_band_v[0]),
                          _band_bias(c_rel_bias[0], d_s), d_b, d_s)

    lf_s3 = lf_s.reshape(d_b, d_s, HEADS)
    cum_p = _cumsum_rows(lf_p.reshape(n_b * HEADS, n_s)).reshape(n_b, HEADS, n_s)
    lf_all = jnp.concatenate([cache_fox_logf[0].astype(F32), lf_s3], axis=1)
    cum_s = _cumsum_rows(lf_all.transpose(0, 2, 1).reshape(d_b * HEADS, n_past + d_s)
                         ).reshape(d_b, HEADS, n_past + d_s)
    fox_p = _fox_prompt(qd_p, kd16_p, vd16_p, cum_p, n_b, n_s)
    fox_s = _fox_sample(qd_s, kd16_s, vd16_s, rows_minor(cache_fox_k[0]), rows_minor(cache_fox_v[0]), cum_s,
                        d_b, d_s)

    xp2 = _out_proj(xp1, row(norm_post[1]), gc_p, gd_p, band_p, fox_p, wout_o)
    xs2 = _out_proj(xs1, row(norm_post[1]), gc_s, gd_s, band_s, fox_s, wout_o)

    keep = min(LEFT_CTX, n_s)
    band_k_s = jnp.concatenate([cache_band_k[0], heads(kc_s, d_b, d_s)], axis=1)[:, d_s:]
    band_v_s = jnp.concatenate([cache_band_v[0], heads(vc_s, d_b, d_s)], axis=1)[:, d_s:]
    one = lambda a: a[None]
    heads_t = lambda a: a.reshape(n_b, HEADS, HEAD_DIM, a.shape[-1]).transpose(0, 3, 1, 2)
    return (xp2.reshape(n_b, n_s, D_MODEL), xs2.reshape(d_b, d_s, D_MODEL),
            one(ckv_p.reshape(n_b, n_s, A_KV_LORA)), one(kr_p.transpose(0, 2, 1)),
            one(heads_t(kb_p)), one(heads_t(vb_p)),
            one(heads_t(kc_p[:, :, n_s - keep:])), one(heads_t(vc_p[:, :, n_s - keep:])),
            one(heads_t(kd_p)), one(heads_t(vd_p)), one(lf_p.transpose(0, 2, 1)),
            one(ckv_s.reshape(d_b, d_s, A_KV_LORA)), one(kr_s.reshape(d_b, d_s, A_ROPE)),
            one(heads(kb_s, d_b, d_s)), one(heads(vb_s, d_b, d_s)),
            one(band_k_s), one(band_v_s),
            one(heads(kd_s, d_b, d_s)), one(heads(vd_s, d_b, d_s)), one(lf_s3))
```

```python
import functools

import numpy as np
import jax
import jax.numpy as jnp
from jax import lax
from jax.experimental import pallas as pl
from jax.experimental.pallas import tpu as pltpu

F32 = jnp.float32
BF16 = jnp.bfloat16

D_MODEL = 1024
PAST_LEN = 4096
CHUNK = 64
LEFT_CTX = 512
REL_CLIP = 128
EPS = 1e-6
NEG = -1e30
ROPE_THETA = 10000.0
A_HEADS = 8
A_Q_LORA = 256
A_KV_LORA = 128
A_NOPE = 64
A_ROPE = 32
A_V = 64
A_SCALE = (A_NOPE + A_ROPE) ** -0.5
HEADS = 8
HEAD_DIM = 64
WIDTH = HEADS * HEAD_DIM
QK_SCALE = HEAD_DIM ** -0.5
LOG2E = 1.4426950408889634
MLA_LOGIT_SCALE = A_SCALE * LOG2E
N_PAIRS = HEADS // 2

LANES = 128
VMEM_LIMIT = 52 * 1024 * 1024
TM = 256
TQ = 256
TK = 256
TKC = 1024
FOX_TQ = 512
FOX_TK = 1024
BAND_TQ = 128
MLA_RT = 256
MLA_QB = 512
PAIRS_PER_STEP = 4
STICK_DEAD = -104.0


def _dot(a, b):
    return jnp.dot(a, b, preferred_element_type=F32)


def _dot_nt(a, b):
    return lax.dot_general(a, b, (((1,), (1,)), ((), ())), preferred_element_type=F32)


def _rms(x, g):
    y = x * lax.rsqrt(jnp.mean(x * x, axis=-1, keepdims=True) + EPS)
    return y * g


def _log_sigmoid(z):
    return jnp.minimum(z, 0.0) - jnp.log1p(jnp.exp(-jnp.abs(z)))


def _silu(g):
    return g / (1.0 + jnp.exp(-g))


def _stack_pair(q2):
    qf = q2.astype(F32)
    lane = lax.broadcasted_iota(jnp.int32, qf.shape, 1)
    even = jnp.where(lane < HEAD_DIM, qf, 0.0)
    odd = jnp.where(lane >= HEAD_DIM, qf, 0.0)
    return jnp.concatenate([even, odd], axis=0).astype(BF16)


def _merge_pair(o, tq):
    top, bot = o[:tq], o[tq:]
    lane = lax.broadcasted_iota(jnp.int32, top.shape, 1)
    return jnp.where(lane < HEAD_DIM, top, bot)


def _params(*sem):
    return pltpu.CompilerParams(dimension_semantics=sem, vmem_limit_bytes=VMEM_LIMIT)


def _const_spec(shape):
    nd = len(shape)
    return pl.BlockSpec(shape, lambda *_: (0,) * nd)


def _in_even_body(*refs, rows_minor):
    (x_ref, gpre_ref, cos_ref, sin_ref, wqa_ref, wkv_ref, wkr_ref, wkrs_ref, wga_ref, wqb_ref, wkb_ref,
     wvb_ref, wgb_ref, qn_ref, kvn_ref, wuqn_ref, wuqr_ref, wuqrs_ref, wuk_ref) = refs[:19]
    n_in = 25 if rows_minor else 19
    (ckv_ref, krope_ref, k2_ref, q2_ref, ga_ref, gb_ref, qb_ref, kb_ref, kb16_ref, vb_ref,
     vb16_ref) = refs[n_in:]
    h = _rms(x_ref[...], gpre_ref[...]).astype(BF16)
    cos = cos_ref[...]
    sin = sin_ref[...]
    ckv = _rms(_dot(h, wkv_ref[...]), kvn_ref[...])
    ckv_ref[...] = ckv
    kr = _dot(h, wkr_ref[...]) * cos + _dot(h, wkrs_ref[...]) * sin
    if rows_minor:
        wkr_t, wkrs_t, wkb_t, wvb_t, cos_t, sin_t = refs[19:25]
        krope_ref[...] = _dot_nt(wkr_t[...], h) * cos_t[...] + _dot_nt(wkrs_t[...], h) * sin_t[...]
        kb_ref[...] = _dot_nt(wkb_t[...], h)
        vb_ref[...] = _dot_nt(wvb_t[...], h)
    else:
        krope_ref[...] = kr[:, :A_ROPE]
    k2_ref[:, :LANES] = ckv.astype(BF16)
    k2_ref[:, LANES:] = kr.astype(BF16)
    cq = _rms(_dot(h, wqa_ref[...]), qn_ref[...]).astype(BF16)
    qn = _dot(cq, wuqn_ref[...]).astype(BF16)
    for p in range(N_PAIRS):
        ql = _dot(qn[:, p * LANES:(p + 1) * LANES], wuk_ref[p])
        q2_ref[:, (2 * p) * 256:(2 * p) * 256 + LANES] = ql[:, :LANES].astype(BF16)
        q2_ref[:, (2 * p + 1) * 256:(2 * p + 1) * 256 + LANES] = ql[:, LANES:].astype(BF16)
    qr = _dot(cq, wuqr_ref[...])
    qrs = _dot(cq, wuqrs_ref[...])
    for hd in range(A_HEADS):
        rot = qr[:, hd * LANES:(hd + 1) * LANES] * cos + qrs[:, hd * LANES:(hd + 1) * LANES] * sin
        q2_ref[:, hd * 256 + LANES:(hd + 1) * 256] = rot.astype(BF16)
    ga_ref[...] = _dot(h, wga_ref[...])
    gb_ref[...] = _dot(h, wgb_ref[...])
    qb = _dot(h, wqb_ref[...]) * QK_SCALE
    kb = _dot(h, wkb_ref[...])
    vb = _dot(h, wvb_ref[...])
    if not rows_minor:
        kb_ref[...] = kb
        vb_ref[...] = vb
    for p in range(N_PAIRS):
        sl = slice(p * LANES, (p + 1) * LANES)
        qb_ref[p] = qb[:, sl].astype(BF16)
        kb16_ref[p] = kb[:, sl].astype(BF16)
        vb16_ref[p] = vb[:, sl].astype(BF16)


def _state_specs(rows, seq_len):
    sds = jax.ShapeDtypeStruct
    if seq_len is None:
        return (lambda n: pl.BlockSpec((TM, n), lambda i: (i, 0))), (lambda n: sds((rows, n), F32))
    nt = seq_len // TM
    return ((lambda n: pl.BlockSpec((None, n, TM), lambda i: (i // nt, 0, i % nt))),
            (lambda n: sds((rows // seq_len, n, seq_len), F32)))


def _in_even(x, gpre, cos, sin, w, seq_len=None, tabs_t=None):
    rows = x.shape[0]
    n_tab = cos.shape[0] // TM
    row_spec = lambda n: pl.BlockSpec((TM, n), lambda i: (i, 0))
    pm_spec = pl.BlockSpec((N_PAIRS, TM, LANES), lambda i: (0, i, 0))
    tab_spec = pl.BlockSpec((TM, LANES), lambda i: (i % n_tab, 0))
    weights = [w['wqa'], w['wkv'], w['wkr'], w['wkrs'], w['wga'], w['wqb'], w['wkb'], w['wvb'], w['wgb'],
               w['qn'], w['kvn'], w['wuqn'], w['wuqr'], w['wuqrs'], w['wuk']]
    in_specs = ([row_spec(D_MODEL), _const_spec((1, D_MODEL)), tab_spec, tab_spec]
                + [_const_spec(a.shape) for a in weights])
    args = [x, gpre, cos, sin, *weights]
    rows_minor = seq_len is not None
    if rows_minor:
        extra = [w['wkr_t'], w['wkrs_t'], w['wkb_t'], w['wvb_t']]
        tab_t_spec = pl.BlockSpec((A_ROPE, TM), lambda i: (0, i % n_tab))
        in_specs += [_const_spec(a.shape) for a in extra] + [tab_t_spec, tab_t_spec]
        args += extra + list(tabs_t)
    st_spec, st_shape = _state_specs(rows, seq_len)
    sds = jax.ShapeDtypeStruct
    pm = sds((N_PAIRS, rows, LANES), BF16)
    return pl.pallas_call(
        functools.partial(_in_even_body, rows_minor=rows_minor),
        grid=(rows // TM,),
        in_specs=in_specs,
        out_specs=[row_spec(A_KV_LORA), st_spec(A_ROPE), row_spec(256), row_spec(A_HEADS * 256),
                   row_spec(WIDTH), row_spec(WIDTH), pm_spec, st_spec(WIDTH), pm_spec,
                   st_spec(WIDTH), pm_spec],
        out_shape=[sds((rows, A_KV_LORA), F32), st_shape(A_ROPE), sds((rows, 256), BF16),
                   sds((rows, A_HEADS * 256), BF16), sds((rows, WIDTH), F32), sds((rows, WIDTH), F32),
                   pm, st_shape(WIDTH), pm, st_shape(WIDTH), pm],
        compiler_params=_params("parallel"),
        name="in_proj_even",
    )(*args)


def _in_odd_body(*refs, rows_minor):
    (x_ref, gpre_ref, fb_ref, wqc_ref, wkc_ref, wvc_ref, wgc_ref, wqd_ref, wkd_ref, wvd_ref, wf_ref,
     wgd_ref) = refs[:12]
    n_in = 18 if rows_minor else 12
    (qc_ref, kc_ref, kc16_ref, vc_ref, vc16_ref, gc_ref, qd_ref, kd_ref, kd16_ref, vd_ref, vd16_ref,
     logf_ref, gd_ref) = refs[n_in:]
    h = _rms(x_ref[...], gpre_ref[...]).astype(BF16)
    gc_ref[...] = _dot(h, wgc_ref[...])
    gd_ref[...] = _dot(h, wgd_ref[...])
    if rows_minor:
        wkc_t, wvc_t, wkd_t, wvd_t, wf_t, fb_t = refs[12:18]
        logf_ref[...] = _log_sigmoid(_dot_nt(wf_t[...], h) + fb_t[...])
        for w_t, o_ref in ((wkc_t, kc_ref), (wvc_t, vc_ref), (wkd_t, kd_ref), (wvd_t, vd_ref)):
            o_ref[...] = _dot_nt(w_t[...], h)
    else:
        logf_ref[...] = _log_sigmoid(_dot(h, wf_ref[...])[:, :HEADS] + fb_ref[...])
    for q_w, k_w, v_w, q_o, k_o, k16_o, v_o, v16_o in (
            (wqc_ref, wkc_ref, wvc_ref, qc_ref, kc_ref, kc16_ref, vc_ref, vc16_ref),
            (wqd_ref, wkd_ref, wvd_ref, qd_ref, kd_ref, kd16_ref, vd_ref, vd16_ref)):
        q = _dot(h, q_w[...]) * (QK_SCALE * LOG2E)
        k = _dot(h, k_w[...])
        v = _dot(h, v_w[...])
        if not rows_minor:
            k_o[...] = k
            v_o[...] = v
        for p in range(N_PAIRS):
            sl = slice(p * LANES, (p + 1) * LANES)
            q_o[p] = q[:, sl].astype(BF16)
            k16_o[p] = k[:, sl].astype(BF16)
            v16_o[p] = v[:, sl].astype(BF16)


def _in_odd(x, gpre, w, seq_len=None):
    rows = x.shape[0]
    row_spec = lambda n: pl.BlockSpec((TM, n), lambda i: (i, 0))
    pm_spec = pl.BlockSpec((N_PAIRS, TM, LANES), lambda i: (0, i, 0))
    weights = [w['wqc'], w['wkc'], w['wvc'], w['wgc'], w['wqd'], w['wkd'], w['wvd'], w['wf'], w['wgd']]
    rows_minor = seq_len is not None
    if rows_minor:
        weights += [w['wkc_t'], w['wvc_t'], w['wkd_t'], w['wvd_t'], w['wf_t'], w['fb_t']]
    st_spec, st_shape = _state_specs(rows, seq_len)
    sds = jax.ShapeDtypeStruct
    pm = sds((N_PAIRS, rows, LANES), BF16)
    full = sds((rows, WIDTH), F32)
    return pl.pallas_call(
        functools.partial(_in_odd_body, rows_minor=rows_minor),
        grid=(rows // TM,),
        in_specs=[row_spec(D_MODEL), _const_spec((1, D_MODEL)), _const_spec((1, HEADS))]
                 + [_const_spec(a.shape) for a in weights],
        out_specs=[pm_spec, st_spec(WIDTH), pm_spec, st_spec(WIDTH), pm_spec, row_spec(WIDTH),
                   pm_spec, st_spec(WIDTH), pm_spec, st_spec(WIDTH), pm_spec, st_spec(HEADS),
                   row_spec(WIDTH)],
        out_shape=[pm, st_shape(WIDTH), pm, st_shape(WIDTH), pm, full, pm, st_shape(WIDTH), pm,
                   st_shape(WIDTH), pm, st_shape(HEADS), full],
        compiler_params=_params("parallel"),
        name="in_proj_odd",
    )(x, gpre, w['fb'], *weights)


def _out_body(*refs, mla):
    if mla:
        x_ref, gpost_ref, g1_ref, g2_ref, a_ref, b_ref, wuv_ref, wout_ref, o_ref, mix_ref = refs
    else:
        x_ref, gpost_ref, g1_ref, g2_ref, a_ref, b_ref, wout_ref, o_ref, mix_ref = refs
    s1 = _silu(g1_ref[...])
    s2 = _silu(g2_ref[...])
    for p in range(N_PAIRS):
        sl = slice(p * LANES, (p + 1) * LANES)
        if mla:
            a = _dot(a_ref[:, p * 256:(p + 1) * 256], wuv_ref[p])
        else:
            a = a_ref[p]
        mix_ref[:, sl] = (s1[:, sl] * a).astype(BF16)
        mix_ref[:, WIDTH + p * LANES:WIDTH + (p + 1) * LANES] = (s2[:, sl] * b_ref[p]).astype(BF16)
    y = _dot(mix_ref[...], wout_ref[...])
    o_ref[...] = x_ref[...] + _rms(y, gpost_ref[...])


def _out_proj(x, gpost, g1, g2, a, b, wout, wuv=None):
    rows = x.shape[0]
    mla = wuv is not None
    row_spec = lambda n: pl.BlockSpec((TM, n), lambda i: (i, 0))
    pm_spec = pl.BlockSpec((N_PAIRS, TM, LANES), lambda i: (0, i, 0))
    in_specs = [row_spec(D_MODEL), _const_spec((1, D_MODEL)), row_spec(WIDTH), row_spec(WIDTH),
                row_spec(A_HEADS * A_KV_LORA) if mla else pm_spec, pm_spec]
    args = [x, gpost, g1, g2, a, b]
    if mla:
        in_specs.append(_const_spec(wuv.shape))
        args.append(wuv)
    in_specs.append(_const_spec(wout.shape))
    args.append(wout)
    return pl.pallas_call(
        functools.partial(_out_body, mla=mla),
        grid=(rows // TM,),
        in_specs=in_specs,
        out_specs=row_spec(D_MODEL),
        out_shape=jax.ShapeDtypeStruct((rows, D_MODEL), F32),
        scratch_shapes=[pltpu.VMEM((TM, 2 * WIDTH), BF16)],
        compiler_params=_params("parallel"),
        name="out_proj_even" if mla else "out_proj_odd",
    )(*args)


def _softmax_init(m_ref, accl_ref):
    m_ref[...] = jnp.full(m_ref.shape, NEG, F32)
    accl_ref[...] = jnp.zeros(accl_ref.shape, F32)


def _lanes(x, n):
    parts = [x] * (n // LANES)
    if n % LANES:
        parts.append(x[:, :n % LANES])
    return parts[0] if len(parts) == 1 else jnp.concatenate(parts, axis=1)


def _scores(q, k, kv_t):
    return _dot(q, k) if kv_t else _dot_nt(q, k)


def _weighted(p, v, kv_t):
    return _dot_nt(p, v) if kv_t else _dot(p, v)


def _with_ones(v, kv_t=False):
    return jnp.concatenate([v, jnp.ones(v.shape, BF16)], axis=0 if kv_t else 1)


def _softmax_update(s, v1, m_ref, accl_ref, kv_t=False):
    keys = s.shape[1]
    n = v1.shape[0] if kv_t else v1.shape[1]
    m_prev = m_ref[...]
    m_new = jnp.maximum(m_prev, jnp.max(s, axis=1, keepdims=True))
    alpha = jnp.exp2(m_prev - m_new)
    p = jnp.exp2(s - _lanes(m_new, keys))
    accl_ref[...] = _lanes(alpha, n) * accl_ref[...] + _weighted(p.astype(BF16), v1, kv_t)
    m_ref[...] = m_new


def _softmax_result(accl_ref):
    accl = accl_ref[...]
    n = accl.shape[1] // 2
    return accl[:, :n] / accl[:, n:]


def _fill_suffix_ones(tri_ref):
    n = tri_ref.shape[0]
    r = lax.broadcasted_iota(jnp.int32, (n, n), 0)
    c = lax.broadcasted_iota(jnp.int32, (n, n), 1)
    tri_ref[...] = jnp.where(r > c, 1.0, 0.0).astype(BF16)


def _stick_tile(qs, k2, v2, tri, acc_ref, car_ref, mask, kv_t=False):
    z = _scores(qs, k2, kv_t)
    lb = _log_sigmoid(z)
    l1 = lb - z
    if mask is not None:
        l1 = jnp.where(mask, l1, 0.0)
    hi = l1.astype(BF16)
    lo = (l1 - hi.astype(F32)).astype(BF16)
    suf = _dot(hi, tri) + _dot(lo, tri) + car_ref[...]
    w = jnp.exp(lb + suf)
    if mask is not None:
        w = jnp.where(mask, w, 0.0)
    acc_ref[...] += _weighted(w.astype(BF16), v2, kv_t)
    car_ref[...] += jnp.sum(l1, axis=1, keepdims=True)


def _stick_alive(car_ref):
    return (jnp.max(car_ref[...]) >= STICK_DEAD).astype(jnp.int32)


def _local_causal(rows, keys, tq, strict, key_offset=0):
    r = lax.broadcasted_iota(jnp.int32, (rows, keys), 0) & (tq - 1)
    c = lax.broadcasted_iota(jnp.int32, (rows, keys), 1) + key_offset
    return (c < r) if strict else (c <= r)


def _decay_tile(qs, k2, v2, cq, ck, causal, m_ref, accl_ref, kv_t=False):
    tq, keys = cq.shape[0], ck.shape[1]
    s = _scores(qs, k2, kv_t)
    for hh in range(2):
        rows = slice(hh * tq, (hh + 1) * tq)
        sh = s[rows] + (cq[:, hh:hh + 1] - ck[hh:hh + 1])
        if causal:
            sh = jnp.where(_local_causal(tq, keys, tq, strict=False), sh, NEG)
        _softmax_update(sh, _with_ones(v2, kv_t), m_ref.at[rows], accl_ref.at[rows], kv_t)


def _split3(c):
    hi = c.astype(BF16).astype(F32)
    rest = c - hi
    mid = rest.astype(BF16).astype(F32)
    return hi, mid, rest - mid


def _decay_lanes(c, base, query):
    hi, mid, lo = _split3(c if query else -c)
    lane = lax.broadcasted_iota(jnp.int32, (c.shape[0], LANES), 1) - (base if query else base + 3)
    parts = jnp.where(lane == 0, hi, jnp.where(lane == 1, mid, jnp.where(lane == 2, lo, 0.0)))
    ones_at = lane + 3 if not query else lane - 3
    return jnp.where((ones_at >= 0) & (ones_at < 3), 1.0, parts)


def _decay_operand(x, cum, hh, query):
    lane = lax.broadcasted_iota(jnp.int32, x.shape, 1)
    own = (lane < HEAD_DIM) if hh == 0 else (lane >= HEAD_DIM)
    base = HEAD_DIM if hh == 0 else 0
    return jnp.where(own, x, _decay_lanes(cum[:, hh:hh + 1], base, query)).astype(BF16)


def _decay_tile_heads(q1, k1, v1, causal_offset, m_ref, accl_ref):
    tq, keys = q1[0].shape[0], k1.shape[1]
    for hh in range(2):
        rows = slice(hh * tq, (hh + 1) * tq)
        s = _dot_nt(q1[hh], k1[hh])
        if causal_offset is not None:
            s = jnp.where(_local_causal(tq, keys, tq, strict=False, key_offset=causal_offset), s, NEG)
        _softmax_update(s, v1[hh], m_ref.at[rows], accl_ref.at[rows])


def _pair_result(accl, tq):
    o = accl / pltpu.roll(accl, HEAD_DIM, axis=1)
    return _merge_pair(o, tq)


def _mla_prompt_body(q_ref, k_ref, o_ref, m_ref, accl_ref):
    g = pl.program_id(1)
    grp = 4 * CHUNK

    def update(h, r0, k, mask):
        sl = pl.ds(r0, grp)
        s = _dot_nt(q_ref[sl, h * 256:(h + 1) * 256], k) * MLA_LOGIT_SCALE
        if mask is not None:
            s = jnp.where(mask, s, NEG)
        _softmax_update(s, _with_ones(k[:, :A_KV_LORA]), m_ref.at[h, sl], accl_ref.at[h, sl])

    def group(gl, carry):
        gq = g * (MLA_QB // grp) + gl
        g0 = pl.multiple_of(gl * grp, grp)
        for h in range(A_HEADS):
            _softmax_init(m_ref.at[h, pl.ds(g0, grp)], accl_ref.at[h, pl.ds(g0, grp)])

        def earlier(k0, n_keys):
            k = k_ref[pl.ds(k0, n_keys), :]
            for h in range(A_HEADS):
                update(h, g0, k, None)

        def kv(j, c2):
            earlier(pl.multiple_of(j * 2 * TK, 2 * TK), 2 * TK)
            return c2

        lax.fori_loop(0, gq // 2, kv, 0)

        @pl.when(gq % 2 == 1)
        def _():
            earlier(pl.multiple_of((gq - 1) * TK, TK), TK)

        kd = k_ref[pl.ds(pl.multiple_of(gq * TK, TK), TK), :]
        row_chunk = lax.broadcasted_iota(jnp.int32, (grp, TK), 0) // CHUNK
        key_chunk = lax.broadcasted_iota(jnp.int32, (grp, TK), 1) // CHUNK
        own = key_chunk <= row_chunk
        for h in range(A_HEADS):
            update(h, g0, kd, own)
        for h in range(A_HEADS):
            o_ref[pl.ds(g0, grp), h * A_KV_LORA:(h + 1) * A_KV_LORA] = _softmax_result(
                accl_ref.at[h, pl.ds(g0, grp)]).astype(BF16)
        return carry

    lax.fori_loop(0, MLA_QB // grp, group, 0)


def _mla_prompt(q2, k2, n_b, n_s):
    assert n_s % MLA_QB == 0 and TK == 4 * CHUNK
    return pl.pallas_call(
        _mla_prompt_body,
        grid=(n_b, n_s // MLA_QB),
        in_specs=[pl.BlockSpec((None, MLA_QB, A_HEADS * 256), lambda b, g: (b, g, 0)),
                  pl.BlockSpec((None, n_s, 256), lambda b, g: (b, 0, 0))],
        out_specs=pl.BlockSpec((None, MLA_QB, A_HEADS * A_KV_LORA), lambda b, g: (b, g, 0)),
        out_shape=jax.ShapeDtypeStruct((n_b, n_s, A_HEADS * A_KV_LORA), BF16),
        scratch_shapes=[pltpu.VMEM((A_HEADS, MLA_QB, LANES), F32),
                        pltpu.VMEM((A_HEADS, MLA_QB, 2 * LANES), F32)],
        compiler_params=_params("parallel", "parallel"),
        name="mla_prompt",
    )(q2.reshape(n_b, n_s, A_HEADS * 256), k2.reshape(n_b, n_s, 256))


def _mla_sample_body(q_ref, kn_ref, ckv_ref, kr_ref, o_ref, m_ref, accl_ref):
    j = pl.program_id(1)

    @pl.when(j == 0)
    def _():
        _softmax_init(m_ref, accl_ref)

    n_tiles = q_ref.shape[0] // MLA_RT
    ck = ckv_ref[...].astype(BF16)
    kr = kr_ref[...].T.astype(BF16)
    k = jnp.concatenate([ck, kr, jnp.zeros((TKC, LANES - A_ROPE), BF16)], axis=1)
    for t in range(n_tiles):
        sl = slice(t * MLA_RT, (t + 1) * MLA_RT)
        s = _dot_nt(q_ref[sl, :], k) * MLA_LOGIT_SCALE
        _softmax_update(s, _with_ones(ck), m_ref.at[sl], accl_ref.at[sl])

    @pl.when(j == pl.num_programs(1) - 1)
    def _():
        kn = kn_ref[...]
        for t in range(n_tiles):
            sl = slice(t * MLA_RT, (t + 1) * MLA_RT)
            s = _dot_nt(q_ref[sl, :], kn) * MLA_LOGIT_SCALE
            _softmax_update(s, _with_ones(kn[:, :A_KV_LORA]), m_ref.at[sl], accl_ref.at[sl])
        o_ref[...] = _softmax_result(accl_ref).astype(BF16)


def _mla_sample(q2, k2, cache_ckv, cache_kr, n_b, n_s):
    rows = n_s * A_HEADS
    n_past = cache_ckv.shape[1]
    return pl.pallas_call(
        _mla_sample_body,
        grid=(n_b, n_past // TKC),
        in_specs=[pl.BlockSpec((None, rows, 256), lambda b, j: (b, 0, 0)),
                  pl.BlockSpec((None, n_s, 256), lambda b, j: (b, 0, 0)),
                  pl.BlockSpec((None, TKC, A_KV_LORA), lambda b, j: (b, j, 0)),
                  pl.BlockSpec((None, A_ROPE, TKC), lambda b, j: (b, 0, j))],
        out_specs=pl.BlockSpec((None, rows, A_KV_LORA), lambda b, j: (b, 0, 0)),
        out_shape=jax.ShapeDtypeStruct((n_b, rows, A_KV_LORA), BF16),
        scratch_shapes=[pltpu.VMEM((rows, LANES), F32), pltpu.VMEM((rows, 2 * LANES), F32)],
        compiler_params=_params("parallel", "arbitrary"),
        name="mla_sample",
    )(q2.reshape(n_b, rows, 256), k2.reshape(n_b, n_s, 256), cache_ckv, cache_kr)


def _sb_prompt_body(q_ref, k_ref, v_ref, o_ref, tri_ref, acc_ref, car_ref):
    n_s = q_ref.shape[1]
    _fill_suffix_ones(tri_ref)

    def qblock(i, carry):
        q0 = pl.multiple_of(i * TQ, TQ)
        qs = [_stack_pair(q_ref[g, pl.ds(q0, TQ), :]) for g in range(PAIRS_PER_STEP)]
        acc_ref[...] = jnp.zeros(acc_ref.shape, F32)
        car_ref[...] = jnp.zeros(car_ref.shape, F32)

        def tiles(k0, mask):
            for g in range(PAIRS_PER_STEP):
                _stick_tile(qs[g], k_ref[g, pl.ds(k0, TK), :], v_ref[g, pl.ds(k0, TK), :], tri_ref[...],
                            acc_ref.at[g], car_ref.at[g], mask)

        tiles(q0, _local_causal(2 * TQ, TK, TQ, strict=True))

        def kv(state):
            jj, _ = state
            tiles(pl.multiple_of((i - 1 - jj) * TK, TK), None)
            return jj + 1, _stick_alive(car_ref)

        lax.while_loop(lambda st: (st[0] < i) & (st[1] > 0), kv, (jnp.int32(0), _stick_alive(car_ref)))
        for g in range(PAIRS_PER_STEP):
            o_ref[g, pl.ds(q0, TQ), :] = _merge_pair(acc_ref[g], TQ)
        return carry

    lax.fori_loop(0, n_s // TQ, qblock, 0)


def _pair_seq_spec(n_s):
    return pl.BlockSpec((PAIRS_PER_STEP, n_s, LANES), lambda b, g: (g, b, 0))


def _sb_prompt(q, k, v, n_b, n_s):
    assert TQ == TK
    spec = _pair_seq_spec(n_s)
    return pl.pallas_call(
        _sb_prompt_body,
        grid=(n_b, N_PAIRS // PAIRS_PER_STEP),
        in_specs=[spec, spec, spec],
        out_specs=spec,
        out_shape=jax.ShapeDtypeStruct((N_PAIRS, n_b * n_s, LANES), F32),
        scratch_shapes=[pltpu.VMEM((TK, TK), BF16), pltpu.VMEM((PAIRS_PER_STEP, 2 * TQ, LANES), F32),
                        pltpu.VMEM((PAIRS_PER_STEP, 2 * TQ, 1), F32)],
        compiler_params=_params("parallel", "parallel"),
        name="sb_prompt",
    )(q, k, v)


def _sb_sample_body(q_ref, kn_ref, vn_ref, ck_hbm, cv_hbm, o_ref, kbuf, vbuf, sem, tri_ref, acc_ref,
                    car_ref):
    b = pl.program_id(0)
    n_q = q_ref.shape[1]
    n_blk = ck_hbm.shape[2] // TK

    def block_copies(blk):
        rows = pl.ds(pl.multiple_of(blk * TK, TK), TK)
        return (pltpu.make_async_copy(ck_hbm.at[b, :, rows], kbuf, sem.at[0]),
                pltpu.make_async_copy(cv_hbm.at[b, :, rows], vbuf, sem.at[1]))

    def start(blk):
        for c in block_copies(blk):
            c.start()

    def cache_block(blk):
        for c in block_copies(blk):
            c.wait()
        for p in range(N_PAIRS):
            sl = slice(p * LANES, (p + 1) * LANES)
            _stick_tile(qs[p], kbuf[sl, :].astype(BF16), vbuf[sl, :].astype(BF16), tri_ref[...],
                        acc_ref.at[p], car_ref.at[p], None, kv_t=True)

    start(n_blk - 1)
    _fill_suffix_ones(tri_ref)
    acc_ref[...] = jnp.zeros(acc_ref.shape, F32)
    car_ref[...] = jnp.zeros(car_ref.shape, F32)
    qs = [_stack_pair(q_ref[p]) for p in range(N_PAIRS)]
    mask = _local_causal(2 * n_q, n_q, n_q, strict=True)
    for p in range(N_PAIRS):
        _stick_tile(qs[p], kn_ref[p], vn_ref[p], tri_ref[:n_q, :n_q], acc_ref.at[p], car_ref.at[p], mask)
    cache_block(n_blk - 1)

    def older(state):
        blk, _ = state
        start(blk)
        cache_block(blk)
        return blk - 1, _stick_alive(car_ref)

    lax.while_loop(lambda st: (st[0] >= 0) & (st[1] > 0), older,
                   (jnp.int32(n_blk - 2), _stick_alive(car_ref)))
    for p in range(N_PAIRS):
        o_ref[p] = _merge_pair(acc_ref[p], n_q)


def _sb_sample(q, kn, vn, cache_kt, cache_vt, n_b, n_s):
    new_spec = pl.BlockSpec((N_PAIRS, n_s, LANES), lambda b: (0, b, 0))
    hbm = pl.BlockSpec(memory_space=pl.ANY)
    return pl.pallas_call(
        _sb_sample_body,
        grid=(n_b,),
        in_specs=[new_spec, new_spec, new_spec, hbm, hbm],
        out_specs=new_spec,
        out_shape=jax.ShapeDtypeStruct((N_PAIRS, n_b * n_s, LANES), F32),
        scratch_shapes=[pltpu.VMEM((WIDTH, TK), F32), pltpu.VMEM((WIDTH, TK), F32),
                        pltpu.SemaphoreType.DMA((2,)), pltpu.VMEM((TK, TK), BF16),
                        pltpu.VMEM((N_PAIRS, 2 * n_s, LANES), F32), pltpu.VMEM((N_PAIRS, 2 * n_s, 1), F32)],
        compiler_params=_params("arbitrary"),
        name="sb_sample",
    )(q, kn, vn, cache_kt, cache_vt)


def _band_block(qs, kwin, vwin, bias, valid_from, tq):
    s = _dot_nt(qs, kwin) + bias
    if valid_from is not None:
        col = lax.broadcasted_iota(jnp.int32, s.shape, 1)
        s = jnp.where(col >= valid_from, s, NEG)
    p = jnp.exp2(s - jnp.max(s, axis=1, keepdims=True))
    o = _dot(p.astype(BF16), vwin) / jnp.sum(p, axis=1, keepdims=True)
    return _merge_pair(o, tq)


def _band_prompt_body(q_ref, k_ref, v_ref, bias_ref, o_ref, kpad_ref, vpad_ref):
    n_s = q_ref.shape[1]
    win = LEFT_CTX + BAND_TQ
    zeros = jnp.zeros((PAIRS_PER_STEP, LEFT_CTX, LANES), BF16)
    kpad_ref[:, :LEFT_CTX, :] = zeros
    vpad_ref[:, :LEFT_CTX, :] = zeros
    kpad_ref[:, LEFT_CTX:, :] = k_ref[...]
    vpad_ref[:, LEFT_CTX:, :] = v_ref[...]

    def qblock(i, carry):
        q0 = pl.multiple_of(i * BAND_TQ, BAND_TQ)
        for g in range(PAIRS_PER_STEP):
            qs = _stack_pair(q_ref[g, pl.ds(q0, BAND_TQ), :])
            o_ref[g, pl.ds(q0, BAND_TQ), :] = _band_block(
                qs, kpad_ref[g, pl.ds(q0, win), :], vpad_ref[g, pl.ds(q0, win), :], bias_ref[g],
                LEFT_CTX - q0, BAND_TQ)
        return carry

    lax.fori_loop(0, n_s // BAND_TQ, qblock, 0)


def _band_prompt(q, k, v, bias, n_b, n_s):
    spec = _pair_seq_spec(n_s)
    win = LEFT_CTX + BAND_TQ
    return pl.pallas_call(
        _band_prompt_body,
        grid=(n_b, N_PAIRS // PAIRS_PER_STEP),
        in_specs=[spec, spec, spec,
                  pl.BlockSpec((PAIRS_PER_STEP, 2 * BAND_TQ, win), lambda b, g: (g, 0, 0))],
        out_specs=spec,
        out_shape=jax.ShapeDtypeStruct((N_PAIRS, n_b * n_s, LANES), F32),
        scratch_shapes=[pltpu.VMEM((PAIRS_PER_STEP, LEFT_CTX + n_s, LANES), BF16),
                        pltpu.VMEM((PAIRS_PER_STEP, LEFT_CTX + n_s, LANES), BF16)],
        compiler_params=_params("parallel", "parallel"),
        name="band_prompt",
    )(q, k, v, bias)


def _band_sample_body(q_ref, kn_ref, vn_ref, kc_ref, vc_ref, bias_ref, o_ref):
    n_q = q_ref.shape[1]
    n_keep = kc_ref.shape[1]
    for p in range(N_PAIRS):
        sl = slice(p * LANES, (p + 1) * LANES)
        qs = _stack_pair(q_ref[p])
        bias = bias_ref[p]
        s_old = _dot(qs, kc_ref[sl, :].astype(BF16)) + bias[:, :n_keep]
        s_new = _dot_nt(qs, kn_ref[p]) + bias[:, n_keep:]
        m = jnp.maximum(jnp.max(s_old, axis=1, keepdims=True), jnp.max(s_new, axis=1, keepdims=True))
        p_old = jnp.exp2(s_old - m)
        p_new = jnp.exp2(s_new - m)
        o = _dot_nt(p_old.astype(BF16), vc_ref[sl, :].astype(BF16)) + _dot(p_new.astype(BF16), vn_ref[p])
        total = jnp.sum(p_old, axis=1, keepdims=True) + jnp.sum(p_new, axis=1, keepdims=True)
        o_ref[p] = _merge_pair(o / total, n_q)


def _band_sample(q, kn, vn, cache_kt, cache_vt, bias, n_b, n_s):
    n_keep = cache_kt.shape[2]
    new_spec = pl.BlockSpec((N_PAIRS, n_s, LANES), lambda b: (0, b, 0))
    cache_spec = pl.BlockSpec((None, WIDTH, n_keep), lambda b: (b, 0, 0))
    return pl.pallas_call(
        _band_sample_body,
        grid=(n_b,),
        in_specs=[new_spec, new_spec, new_spec, cache_spec, cache_spec,
                  _const_spec((N_PAIRS, 2 * n_s, n_keep + n_s))],
        out_specs=new_spec,
        out_shape=jax.ShapeDtypeStruct((N_PAIRS, n_b * n_s, LANES), F32),
        compiler_params=_params("parallel"),
        name="band_sample",
    )(q, kn, vn, cache_kt, cache_vt, bias)


def _band_bias(rel_bias, tq):
    win = LEFT_CTX + tq
    i = np.arange(tq)[:, None]
    w = np.arange(win)[None, :]
    qc, kc = i // CHUNK, w // CHUNK - LEFT_CTX // CHUNK
    ok = (kc <= qc) & (kc >= qc - LEFT_CTX // CHUNK)
    u = np.arange(win + tq - 1)
    rel = np.clip(LEFT_CTX + (tq - 1) - u, -REL_CLIP, REL_CLIP) + REL_CLIP
    diag = rel_bias.astype(F32)[:, rel]
    n = win + tq - 1
    flat = jnp.tile(diag, (1, tq))[:, tq - 1:tq - 1 + tq * (n - 1)]
    tab = flat.reshape(HEADS, tq, n - 1)[:, :, :win]
    tab = jnp.where(jnp.asarray(ok)[None], tab * LOG2E, NEG)
    return tab.reshape(N_PAIRS, 2 * tq, win)


def _fox_prompt_body(q_ref, k_ref, v_ref, cum_ref, o_ref, k1_ref, v1_ref, m_ref, accl_ref):
    n_s = q_ref.shape[1]

    def prepare(jb, carry):
        rows = pl.ds(pl.multiple_of(jb * TK, TK), TK)
        lane = lax.broadcasted_iota(jnp.int32, (TK, LANES), 1)
        for g in range(PAIRS_PER_STEP):
            k = k_ref[g, rows, :].astype(F32)
            v = v_ref[g, rows, :].astype(F32)
            cum = cum_ref[g, rows, :]
            for hh in range(2):
                k1_ref[g, hh, rows, :] = _decay_operand(k, cum, hh, query=False)
                own = (lane < HEAD_DIM) if hh == 0 else (lane >= HEAD_DIM)
                v1_ref[g, hh, rows, :] = jnp.where(own, v, 1.0).astype(BF16)
        return carry

    lax.fori_loop(0, n_s // TK, prepare, 0)

    tiles_per_q = FOX_TQ // TK

    def qblock(i, carry):
        q0 = pl.multiple_of(i * FOX_TQ, FOX_TQ)
        q1 = []
        for g in range(PAIRS_PER_STEP):
            q = q_ref[g, pl.ds(q0, FOX_TQ), :].astype(F32)
            cum = cum_ref[g, pl.ds(q0, FOX_TQ), :]
            q1.append([_decay_operand(q, cum, hh, query=True) for hh in range(2)])
        _softmax_init(m_ref, accl_ref)

        def tile(jb, causal_offset):
            rows = pl.ds(pl.multiple_of(jb * TK, TK), TK)
            for g in range(PAIRS_PER_STEP):
                _decay_tile_heads(q1[g], k1_ref[g, :, rows, :], v1_ref[g, :, rows, :], causal_offset,
                                  m_ref.at[g], accl_ref.at[g])

        for t in range(tiles_per_q):
            tile(i * tiles_per_q + t, t * TK)

        def kv(jb, c2):
            tile(jb, None)
            return c2

        lax.fori_loop(0, i * tiles_per_q, kv, 0)
        for g in range(PAIRS_PER_STEP):
            o_ref[g, pl.ds(q0, FOX_TQ), :] = _pair_result(accl_ref[g], FOX_TQ)
        return carry

    lax.fori_loop(0, n_s // FOX_TQ, qblock, 0)


def _fox_prompt(q, k, v, cum, n_b, n_s):
    assert FOX_TQ % TK == 0 and n_s % FOX_TQ == 0
    spec = _pair_seq_spec(n_s)
    cum_rows = cum.reshape(n_b, N_PAIRS, 2, n_s).transpose(0, 1, 3, 2)
    g_ = PAIRS_PER_STEP
    return pl.pallas_call(
        _fox_prompt_body,
        grid=(n_b, N_PAIRS // g_),
        in_specs=[spec, spec, spec, pl.BlockSpec((None, g_, n_s, 2), lambda b, g: (b, g, 0, 0))],
        out_specs=spec,
        out_shape=jax.ShapeDtypeStruct((N_PAIRS, n_b * n_s, LANES), F32),
        scratch_shapes=[pltpu.VMEM((g_, 2, n_s, LANES), BF16), pltpu.VMEM((g_, 2, n_s, LANES), BF16),
                        pltpu.VMEM((g_, 2 * FOX_TQ, LANES), F32), pltpu.VMEM((g_, 2 * FOX_TQ, LANES), F32)],
        compiler_params=_params("parallel", "parallel"),
        name="fox_prompt",
    )(q, k, v, cum_rows)


def _fox_sample_body(q_ref, kn_ref, vn_ref, kc_ref, vc_ref, cq_ref, ckn_ref, ckc_ref, o_ref,
                     m_ref, accl_ref):
    j = pl.program_id(1)
    n_q = q_ref.shape[1]
    cq = cq_ref[...]

    def pair_update(p, k2, v2, ck, new_rows):
        rows = slice(2 * p * n_q, (2 * p + 2) * n_q)
        _decay_tile(_stack_pair(q_ref[p]), k2, v2, cq[:, 2 * p:2 * p + 2], ck[2 * p:2 * p + 2], new_rows,
                    m_ref.at[rows], accl_ref.at[rows], kv_t=not new_rows)

    @pl.when(j == 0)
    def _():
        _softmax_init(m_ref, accl_ref)
        ckn = ckn_ref[...]
        for p in range(N_PAIRS):
            pair_update(p, kn_ref[p], vn_ref[p], ckn, True)

    @pl.when(j > 0)
    def _():
        ckc = ckc_ref[...]
        for sub in range(TKC // FOX_TK):
            keys = slice(sub * FOX_TK, (sub + 1) * FOX_TK)
            for p in range(N_PAIRS):
                sl = slice(p * LANES, (p + 1) * LANES)
                pair_update(p, kc_ref[sl, keys].astype(BF16), vc_ref[sl, keys].astype(BF16), ckc[:, keys],
                            False)

    @pl.when(j == pl.num_programs(1) - 1)
    def _():
        for p in range(N_PAIRS):
            rows = slice(2 * p * n_q, (2 * p + 2) * n_q)
            o_ref[p] = _merge_pair(_softmax_result(accl_ref.at[rows]), n_q)


def _fox_sample(q, kn, vn, cache_k, cache_v, cum, n_b, n_s):
    n_past = cache_k.shape[2]
    n_blk = n_past // TKC
    cq = cum[..., n_past:].transpose(0, 2, 1)
    ckn = cum[..., n_past:]
    ckc = cum[..., :n_past]
    new_spec = pl.BlockSpec((N_PAIRS, n_s, LANES), lambda b, j: (0, b, 0))
    cache_blk = lambda j: jnp.maximum(j - 1, 0)
    cache_spec = pl.BlockSpec((None, WIDTH, TKC), lambda b, j: (b, 0, cache_blk(j)))
    return pl.pallas_call(
        _fox_sample_body,
        grid=(n_b, n_blk + 1),
        in_specs=[new_spec, new_spec, new_spec, cache_spec, cache_spec,
                  pl.BlockSpec((None, n_s, HEADS), lambda b, j: (b, 0, 0)),
                  pl.BlockSpec((None, HEADS, n_s), lambda b, j: (b, 0, 0)),
                  pl.BlockSpec((None, HEADS, TKC), lambda b, j: (b, 0, cache_blk(j)))],
        out_specs=new_spec,
        out_shape=jax.ShapeDtypeStruct((N_PAIRS, n_b * n_s, LANES), F32),
        scratch_shapes=[pltpu.VMEM((HEADS * n_s, LANES), F32), pltpu.VMEM((HEADS * n_s, 2 * LANES), F32)],
        compiler_params=_params("parallel", "arbitrary"),
        name="fox_sample",
    )(q, kn, vn, cache_k, cache_v, cq, ckn, ckc)


def _cumsum_body(x_ref, o_ref):
    n_rows, n_cols = x_ref.shape
    r = lax.broadcasted_iota(jnp.int32, (LANES, LANES), 0)
    c = lax.broadcasted_iota(jnp.int32, (LANES, LANES), 1)
    ones = jnp.where(r <= c, 1.0, 0.0).astype(BF16)
    total = jnp.zeros((n_rows, 1), F32)
    for g in range(n_cols // LANES):
        x = x_ref[:, g * LANES:(g + 1) * LANES]
        h1 = x.astype(BF16)
        r1 = x - h1.astype(F32)
        h2 = r1.astype(BF16)
        h3 = (r1 - h2.astype(F32)).astype(BF16)
        y = _dot(h1, ones) + _dot(h2, ones) + _dot(h3, ones) + total
        o_ref[:, g * LANES:(g + 1) * LANES] = y * LOG2E
        total = y[:, LANES - 1:LANES]


def _cumsum_rows(x):
    rows, n = x.shape
    n_pad = -(-n // LANES) * LANES
    xp = jnp.pad(x, ((0, 0), (0, n_pad - n)))
    out = pl.pallas_call(
        _cumsum_body,
        out_shape=jax.ShapeDtypeStruct((rows, n_pad), F32),
        compiler_params=pltpu.CompilerParams(vmem_limit_bytes=VMEM_LIMIT),
        name="cumsum_rows",
    )(xp)
    return out[:, :n]


def _split_cols(w, sizes):
    out, off = [], 0
    for n in sizes:
        out.append(w[:, off:off + n])
        off += n
    return out


def _rope_tables(pos, n_rows):
    half = A_ROPE // 2
    inv_freq = ROPE_THETA ** (-jnp.arange(half, dtype=F32) / half)
    ang = pos.astype(F32)[:, None] * inv_freq[None, :]
    cos, sin = jnp.cos(ang), jnp.sin(ang)
    zeros = jnp.zeros((pos.shape[0], LANES - A_ROPE), F32)
    cos_t = jnp.concatenate([cos, cos, zeros], axis=1)
    sin_t = jnp.concatenate([-sin, sin, zeros], axis=1)
    reps = max(1, n_rows // pos.shape[0])
    return jnp.tile(cos_t, (reps, 1)), jnp.tile(sin_t, (reps, 1))


def _swap_halves(w):
    half = w.shape[-1] // 2
    return jnp.concatenate([w[..., half:], w[..., :half]], axis=-1)


def _pad_lanes(w):
    return jnp.pad(w, [(0, 0)] * (w.ndim - 1) + [(0, LANES - w.shape[-1])])


def _prep_even(w_in, q_norm, w_uq, kv_norm, w_uk, w_uv):
    wqa, wkv, wkr, wga, wqb, wkb, wvb, wgb = _split_cols(
        w_in, (A_Q_LORA, A_KV_LORA, A_ROPE, WIDTH, WIDTH, WIDTH, WIDTH, WIDTH))
    b = lambda a: a.astype(BF16)
    uq_rope = w_uq[:, :, A_NOPE:]
    uk_t = jnp.transpose(w_uk, (1, 2, 0))
    z = jnp.zeros((A_NOPE, A_KV_LORA), w_uk.dtype)
    wuk = jnp.stack([jnp.block([[uk_t[2 * p], z], [z, uk_t[2 * p + 1]]]) for p in range(N_PAIRS)])
    uv_t = jnp.transpose(w_uv, (1, 0, 2))
    zv = jnp.zeros((A_KV_LORA, A_V), w_uv.dtype)
    wuv = jnp.stack([jnp.block([[uv_t[2 * p], zv], [zv, uv_t[2 * p + 1]]]) for p in range(N_PAIRS)])
    return dict(
        wqa=b(wqa), wkv=b(wkv), wkr=b(_pad_lanes(wkr)), wkrs=b(_pad_lanes(_swap_halves(wkr))),
        wga=b(wga), wqb=b(wqb), wkb=b(wkb), wvb=b(wvb), wgb=b(wgb),
        qn=q_norm.reshape(1, -1), kvn=kv_norm.reshape(1, -1),
        wuqn=b(w_uq[:, :, :A_NOPE].reshape(A_Q_LORA, A_HEADS * A_NOPE)),
        wuqr=b(_pad_lanes(uq_rope).reshape(A_Q_LORA, A_HEADS * LANES)),
        wuqrs=b(_pad_lanes(_swap_halves(uq_rope)).reshape(A_Q_LORA, A_HEADS * LANES)),
        wuk=b(wuk), wuv=b(wuv),
        wkr_t=b(wkr.T), wkrs_t=b(_swap_halves(wkr).T), wkb_t=b(wkb.T), wvb_t=b(wvb.T))


def _prep_odd(w_in, forget_bias):
    wqc, wkc, wvc, wgc, wqd, wkd, wvd, wf, wgd = _split_cols(
        w_in, (WIDTH, WIDTH, WIDTH, WIDTH, WIDTH, WIDTH, WIDTH, HEADS, WIDTH))
    b = lambda a: a.astype(BF16)
    return dict(wqc=b(wqc), wkc=b(wkc), wvc=b(wvc), wgc=b(wgc), wqd=b(wqd), wkd=b(wkd), wvd=b(wvd),
                wf=b(_pad_lanes(wf)), wgd=b(wgd), fb=forget_bias.astype(F32).reshape(1, HEADS),
                wkc_t=b(wkc.T), wvc_t=b(wvc.T), wkd_t=b(wkd.T), wvd_t=b(wvd.T), wf_t=b(wf.T),
                fb_t=forget_bias.astype(F32).reshape(HEADS, 1))


def kernel(x_prompt, x_sample, cache_mla_ckv, cache_mla_krope, cache_sb_k, cache_sb_v, cache_band_k,
           cache_band_v, cache_fox_k, cache_fox_v, cache_fox_logf, norm_pre, norm_post, w_in_even,
           a_q_norm, a_w_uq, a_kv_norm, a_w_uk, a_w_uv, w_out_even, w_in_odd, c_rel_bias,
           d_forget_bias, w_out_odd):
    n_b, n_s, _ = x_prompt.shape
    d_b, d_s, _ = x_sample.shape
    n_past = cache_sb_k.shape[2]
    n_keep = cache_band_k.shape[2]
    assert n_s % (2 * TQ) == 0 and n_past % TKC == 0 and d_s == CHUNK and n_past % CHUNK == 0
    assert n_keep == LEFT_CTX and (d_b * d_s) % TM == 0 and TM % d_s == 0

    xp = x_prompt.reshape(n_b * n_s, D_MODEL)
    xs = x_sample.reshape(d_b * d_s, D_MODEL)
    row = lambda a: a.reshape(1, -1)
    heads = lambda a, b, s: a.reshape(b, s, HEADS, HEAD_DIM)
    rows_minor = lambda a: a.transpose(0, 2, 3, 1).reshape(a.shape[0], WIDTH, a.shape[1])

    we = _prep_even(w_in_even[0], a_q_norm[0], a_w_uq[0], a_kv_norm[0], a_w_uk[0], a_w_uv[0])
    wout_e = w_out_even[0].astype(BF16)
    cos_p, sin_p = _rope_tables(jnp.arange(n_s), TM)
    cos_s, sin_s = _rope_tables(n_past + jnp.arange(d_s), TM)

    tabs_t = (cos_p[:n_s, :A_ROPE].T, sin_p[:n_s, :A_ROPE].T)
    (ckv_p, kr_p, k2_p, q2_p, ga_p, gb_p, qb_p, kb_p, kb16_p, vb_p, vb16_p) = _in_even(
        xp, row(norm_pre[0]), cos_p, sin_p, we, seq_len=n_s, tabs_t=tabs_t)
    (ckv_s, kr_s, k2_s, q2_s, ga_s, gb_s, qb_s, kb_s, kb16_s, vb_s, vb16_s) = _in_even(
        xs, row(norm_pre[0]), cos_s, sin_s, we)

    lat_p = _mla_prompt(q2_p, k2_p, n_b, n_s).reshape(n_b * n_s, A_HEADS * A_KV_LORA)
    lat_s = _mla_sample(q2_s, k2_s, cache_mla_ckv[0], cache_mla_krope[0].transpose(0, 2, 1), d_b, d_s
                        ).reshape(d_b * d_s, A_HEADS * A_KV_LORA)
    sb_p = _sb_prompt(qb_p, kb16_p, vb16_p, n_b, n_s)
    sb_s = _sb_sample(qb_s, kb16_s, vb16_s, rows_minor(cache_sb_k[0]), rows_minor(cache_sb_v[0]), d_b, d_s)

    xp1 = _out_proj(xp, row(norm_post[0]), ga_p, gb_p, lat_p, sb_p, wout_e, we['wuv'])
    xs1 = _out_proj(xs, row(norm_post[0]), ga_s, gb_s, lat_s, sb_s, wout_e, we['wuv'])

    wo = _prep_odd(w_in_odd[0], d_forget_bias[0])
    wout_o = w_out_odd[0].astype(BF16)
    (qc_p, kc_p, kc16_p, vc_p, vc16_p, gc_p, qd_p, kd_p, kd16_p, vd_p, vd16_p, lf_p, gd_p) = _in_odd(
        xp1, row(norm_pre[1]), wo, seq_len=n_s)
    (qc_s, kc_s, kc16_s, vc_s, vc16_s, gc_s, qd_s, kd_s, kd16_s, vd_s, vd16_s, lf_s, gd_s) = _in_odd(
        xs1, row(norm_pre[1]), wo)

    band_p = _band_prompt(qc_p, kc16_p, vc16_p, _band_bias(c_rel_bias[0], BAND_TQ), n_b, n_s)
    band_s = _band_sample(qc_s, kc16_s, vc16_s, rows_minor(cache_band_k[0]), rows_minor(cache_band_v[0]),
                          _band_bias(c_rel_bias[0], d_s), d_b, d_s)

    lf_s3 = lf_s.reshape(d_b, d_s, HEADS)
    cum_p = _cumsum_rows(lf_p.reshape(n_b * HEADS, n_s)).reshape(n_b, HEADS, n_s)
    lf_all = jnp.concatenate([cache_fox_logf[0].astype(F32), lf_s3], axis=1)
    cum_s = _cumsum_rows(lf_all.transpose(0, 2, 1).reshape(d_b * HEADS, n_past + d_s)
                         ).reshape(d_b, HEADS, n_past + d_s)
    fox_p = _fox_prompt(qd_p, kd16_p, vd16_p, cum_p, n_b, n_s)
    fox_s = _fox_sample(qd_s, kd16_s, vd16_s, rows_minor(cache_fox_k[0]), rows_minor(cache_fox_v[0]), cum_s,
                        d_b, d_s)

    xp2 = _out_proj(xp1, row(norm_post[1]), gc_p, gd_p, band_p, fox_p, wout_o)
    xs2 = _out_proj(xs1, row(norm_post[1]), gc_s, gd_s, band_s, fox_s, wout_o)

    keep = min(LEFT_CTX, n_s)
    band_k_s = jnp.concatenate([cache_band_k[0], heads(kc_s, d_b, d_s)], axis=1)[:, d_s:]
    band_v_s = jnp.concatenate([cache_band_v[0], heads(vc_s, d_b, d_s)], axis=1)[:, d_s:]
    one = lambda a: a[None]
    heads_t = lambda a: a.reshape(n_b, HEADS, HEAD_DIM, a.shape[-1]).transpose(0, 3, 1, 2)
    return (xp2.reshape(n_b, n_s, D_MODEL), xs2.reshape(d_b, d_s, D_MODEL),
            one(ckv_p.reshape(n_b, n_s, A_KV_LORA)), one(kr_p.transpose(0, 2, 1)),
            one(heads_t(kb_p)), one(heads_t(vb_p)),
            one(heads_t(kc_p[:, :, n_s - keep:])), one(heads_t(vc_p[:, :, n_s - keep:])),
            one(heads_t(kd_p)), one(heads_t(vd_p)), one(lf_p.transpose(0, 2, 1)),
            one(ckv_s.reshape(d_b, d_s, A_KV_LORA)), one(kr_s.reshape(d_b, d_s, A_ROPE)),
            one(heads(kb_s, d_b, d_s)), one(heads(vb_s, d_b, d_s)),
            one(band_k_s), one(band_v_s),
            one(heads(kd_s, d_b, d_s)), one(heads(vd_s, d_b, d_s)), one(lf_s3))
```

```python
import functools

import numpy as np
import jax
import jax.numpy as jnp
from jax import lax
from jax.experimental import pallas as pl
from jax.experimental.pallas import tpu as pltpu

F32 = jnp.float32
BF16 = jnp.bfloat16

D_MODEL = 1024
PAST_LEN = 4096
CHUNK = 64
LEFT_CTX = 512
REL_CLIP = 128
EPS = 1e-6
NEG = -1e30
ROPE_THETA = 10000.0
A_HEADS = 8
A_Q_LORA = 256
A_KV_LORA = 128
A_NOPE = 64
A_ROPE = 32
A_V = 64
A_SCALE = (A_NOPE + A_ROPE) ** -0.5
HEADS = 8
HEAD_DIM = 64
WIDTH = HEADS * HEAD_DIM
QK_SCALE = HEAD_DIM ** -0.5
LOG2E = 1.4426950408889634
MLA_LOGIT_SCALE = A_SCALE * LOG2E
N_PAIRS = HEADS // 2

LANES = 128
VMEM_LIMIT = 52 * 1024 * 1024
TM = 512
TQ = 256
TK = 256
TKC = 1024
FOX_TQ = 512
FOX_TK = 1024
BAND_TQ = 128
MLA_RT = 256
MLA_QB = 512
PAIRS_PER_STEP = 4
STICK_DEAD = -104.0


def _dot(a, b):
    return jnp.dot(a, b, preferred_element_type=F32)


def _dot_nt(a, b):
    return lax.dot_general(a, b, (((1,), (1,)), ((), ())), preferred_element_type=F32)


def _rms(x, g):
    y = x * lax.rsqrt(jnp.mean(x * x, axis=-1, keepdims=True) + EPS)
    return y * g


def _log_sigmoid(z):
    return jnp.minimum(z, 0.0) - jnp.log1p(jnp.exp(-jnp.abs(z)))


def _silu(g):
    return g / (1.0 + jnp.exp(-g))


def _stack_pair(q2):
    qf = q2.astype(F32)
    lane = lax.broadcasted_iota(jnp.int32, qf.shape, 1)
    even = jnp.where(lane < HEAD_DIM, qf, 0.0)
    odd = jnp.where(lane >= HEAD_DIM, qf, 0.0)
    return jnp.concatenate([even, odd], axis=0).astype(BF16)


def _merge_pair(o, tq):
    top, bot = o[:tq], o[tq:]
    lane = lax.broadcasted_iota(jnp.int32, top.shape, 1)
    return jnp.where(lane < HEAD_DIM, top, bot)


def _params(*sem):
    return pltpu.CompilerParams(dimension_semantics=sem, vmem_limit_bytes=VMEM_LIMIT)


def _const_spec(shape):
    nd = len(shape)
    return pl.BlockSpec(shape, lambda *_: (0,) * nd)


def _in_even_body(x_ref, gpre_ref, cos_ref, sin_ref, wqa_ref, wkv_ref, wkr_ref, wkrs_ref, wga_ref,
                  wqb_ref, wkb_ref, wvb_ref, wgb_ref, qn_ref, kvn_ref, wuqn_ref, wuqr_ref,
                  wuqrs_ref, wuk_ref,
                  ckv_ref, krope_ref, k2_ref, q2_ref, ga_ref, gb_ref, qb_ref, kb_ref, kb16_ref,
                  vb_ref, vb16_ref, *, rows_minor):
    h = _rms(x_ref[...], gpre_ref[...]).astype(BF16)
    cos = cos_ref[...]
    sin = sin_ref[...]
    ckv = _rms(_dot(h, wkv_ref[...]), kvn_ref[...])
    ckv_ref[...] = ckv
    kr = _dot(h, wkr_ref[...]) * cos + _dot(h, wkrs_ref[...]) * sin
    krope_ref[...] = kr.T[:A_ROPE] if rows_minor else kr[:, :A_ROPE]
    k2_ref[:, :LANES] = ckv.astype(BF16)
    k2_ref[:, LANES:] = kr.astype(BF16)
    cq = _rms(_dot(h, wqa_ref[...]), qn_ref[...]).astype(BF16)
    qn = _dot(cq, wuqn_ref[...]).astype(BF16)
    for p in range(N_PAIRS):
        ql = _dot(qn[:, p * LANES:(p + 1) * LANES], wuk_ref[p])
        q2_ref[:, (2 * p) * 256:(2 * p) * 256 + LANES] = ql[:, :LANES].astype(BF16)
        q2_ref[:, (2 * p + 1) * 256:(2 * p + 1) * 256 + LANES] = ql[:, LANES:].astype(BF16)
    qr = _dot(cq, wuqr_ref[...])
    qrs = _dot(cq, wuqrs_ref[...])
    for hd in range(A_HEADS):
        rot = qr[:, hd * LANES:(hd + 1) * LANES] * cos + qrs[:, hd * LANES:(hd + 1) * LANES] * sin
        q2_ref[:, hd * 256 + LANES:(hd + 1) * 256] = rot.astype(BF16)
    ga_ref[...] = _dot(h, wga_ref[...])
    gb_ref[...] = _dot(h, wgb_ref[...])
    qb = _dot(h, wqb_ref[...]) * QK_SCALE
    kb = _dot(h, wkb_ref[...])
    vb = _dot(h, wvb_ref[...])
    kb_ref[...] = kb.T if rows_minor else kb
    vb_ref[...] = vb.T if rows_minor else vb
    for p in range(N_PAIRS):
        sl = slice(p * LANES, (p + 1) * LANES)
        qb_ref[p] = qb[:, sl].astype(BF16)
        kb16_ref[p] = kb[:, sl].astype(BF16)
        vb16_ref[p] = vb[:, sl].astype(BF16)


def _state_specs(rows, seq_len):
    sds = jax.ShapeDtypeStruct
    if seq_len is None:
        return (lambda n: pl.BlockSpec((TM, n), lambda i: (i, 0))), (lambda n: sds((rows, n), F32))
    nt = seq_len // TM
    return ((lambda n: pl.BlockSpec((None, n, TM), lambda i: (i // nt, 0, i % nt))),
            (lambda n: sds((rows // seq_len, n, seq_len), F32)))


def _in_even(x, gpre, cos, sin, w, seq_len=None):
    rows = x.shape[0]
    n_tab = cos.shape[0] // TM
    row_spec = lambda n: pl.BlockSpec((TM, n), lambda i: (i, 0))
    pm_spec = pl.BlockSpec((N_PAIRS, TM, LANES), lambda i: (0, i, 0))
    tab_spec = pl.BlockSpec((TM, LANES), lambda i: (i % n_tab, 0))
    weights = [w['wqa'], w['wkv'], w['wkr'], w['wkrs'], w['wga'], w['wqb'], w['wkb'], w['wvb'], w['wgb'],
               w['qn'], w['kvn'], w['wuqn'], w['wuqr'], w['wuqrs'], w['wuk']]
    in_specs = ([row_spec(D_MODEL), _const_spec((1, D_MODEL)), tab_spec, tab_spec]
                + [_const_spec(a.shape) for a in weights])
    args = [x, gpre, cos, sin, *weights]
    st_spec, st_shape = _state_specs(rows, seq_len)
    sds = jax.ShapeDtypeStruct
    pm = sds((N_PAIRS, rows, LANES), BF16)
    return pl.pallas_call(
        functools.partial(_in_even_body, rows_minor=seq_len is not None),
        grid=(rows // TM,),
        in_specs=in_specs,
        out_specs=[row_spec(A_KV_LORA), st_spec(A_ROPE), row_spec(256), row_spec(A_HEADS * 256),
                   row_spec(WIDTH), row_spec(WIDTH), pm_spec, st_spec(WIDTH), pm_spec,
                   st_spec(WIDTH), pm_spec],
        out_shape=[sds((rows, A_KV_LORA), F32), st_shape(A_ROPE), sds((rows, 256), BF16),
                   sds((rows, A_HEADS * 256), BF16), sds((rows, WIDTH), F32), sds((rows, WIDTH), F32),
                   pm, st_shape(WIDTH), pm, st_shape(WIDTH), pm],
        compiler_params=_params("parallel"),
        name="in_proj_even",
    )(*args)


def _in_odd_body(*refs, rows_minor):
    (x_ref, gpre_ref, fb_ref, wqc_ref, wkc_ref, wvc_ref, wgc_ref, wqd_ref, wkd_ref, wvd_ref, wf_ref,
     wgd_ref) = refs[:12]
    n_in = 14 if rows_minor else 12
    (qc_ref, kc_ref, kc16_ref, vc_ref, vc16_ref, gc_ref, qd_ref, kd_ref, kd16_ref, vd_ref, vd16_ref,
     logf_ref, gd_ref) = refs[n_in:]
    h = _rms(x_ref[...], gpre_ref[...]).astype(BF16)
    gc_ref[...] = _dot(h, wgc_ref[...])
    gd_ref[...] = _dot(h, wgd_ref[...])
    if rows_minor:
        wf_t, fb_t = refs[12:14]
        logf_ref[...] = _log_sigmoid(_dot_nt(wf_t[...], h) + fb_t[...])
    else:
        logf_ref[...] = _log_sigmoid(_dot(h, wf_ref[...])[:, :HEADS] + fb_ref[...])
    for q_w, k_w, v_w, q_o, k_o, k16_o, v_o, v16_o in (
            (wqc_ref, wkc_ref, wvc_ref, qc_ref, kc_ref, kc16_ref, vc_ref, vc16_ref),
            (wqd_ref, wkd_ref, wvd_ref, qd_ref, kd_ref, kd16_ref, vd_ref, vd16_ref)):
        q = _dot(h, q_w[...]) * (QK_SCALE * LOG2E)
        k = _dot(h, k_w[...])
        v = _dot(h, v_w[...])
        k_o[...] = k.T if rows_minor else k
        v_o[...] = v.T if rows_minor else v
        for p in range(N_PAIRS):
            sl = slice(p * LANES, (p + 1) * LANES)
            q_o[p] = q[:, sl].astype(BF16)
            k16_o[p] = k[:, sl].astype(BF16)
            v16_o[p] = v[:, sl].astype(BF16)


def _in_odd(x, gpre, w, seq_len=None):
    rows = x.shape[0]
    row_spec = lambda n: pl.BlockSpec((TM, n), lambda i: (i, 0))
    pm_spec = pl.BlockSpec((N_PAIRS, TM, LANES), lambda i: (0, i, 0))
    weights = [w['wqc'], w['wkc'], w['wvc'], w['wgc'], w['wqd'], w['wkd'], w['wvd'], w['wf'], w['wgd']]
    rows_minor = seq_len is not None
    if rows_minor:
        weights += [w['wf_t'], w['fb_t']]
    st_spec, st_shape = _state_specs(rows, seq_len)
    sds = jax.ShapeDtypeStruct
    pm = sds((N_PAIRS, rows, LANES), BF16)
    full = sds((rows, WIDTH), F32)
    return pl.pallas_call(
        functools.partial(_in_odd_body, rows_minor=rows_minor),
        grid=(rows // TM,),
        in_specs=[row_spec(D_MODEL), _const_spec((1, D_MODEL)), _const_spec((1, HEADS))]
                 + [_const_spec(a.shape) for a in weights],
        out_specs=[pm_spec, st_spec(WIDTH), pm_spec, st_spec(WIDTH), pm_spec, row_spec(WIDTH),
                   pm_spec, st_spec(WIDTH), pm_spec, st_spec(WIDTH), pm_spec, st_spec(HEADS),
                   row_spec(WIDTH)],
        out_shape=[pm, st_shape(WIDTH), pm, st_shape(WIDTH), pm, full, pm, st_shape(WIDTH), pm,
                   st_shape(WIDTH), pm, st_shape(HEADS), full],
        compiler_params=_params("parallel"),
        name="in_proj_odd",
    )(x, gpre, w['fb'], *weights)


def _out_body(*refs, mla):
    if mla:
        x_ref, gpost_ref, g1_ref, g2_ref, a_ref, b_ref, wuv_ref, wout_ref, o_ref, mix_ref = refs
    else:
        x_ref, gpost_ref, g1_ref, g2_ref, a_ref, b_ref, wout_ref, o_ref, mix_ref = refs
    s1 = _silu(g1_ref[...])
    s2 = _silu(g2_ref[...])
    for p in range(N_PAIRS):
        sl = slice(p * LANES, (p + 1) * LANES)
        if mla:
            a = _dot(a_ref[:, p * 256:(p + 1) * 256], wuv_ref[p])
        else:
            a = a_ref[p]
        mix_ref[:, sl] = (s1[:, sl] * a).astype(BF16)
        mix_ref[:, WIDTH + p * LANES:WIDTH + (p + 1) * LANES] = (s2[:, sl] * b_ref[p]).astype(BF16)
    y = _dot(mix_ref[...], wout_ref[...])
    o_ref[...] = x_ref[...] + _rms(y, gpost_ref[...])


def _out_proj(x, gpost, g1, g2, a, b, wout, wuv=None):
    rows = x.shape[0]
    mla = wuv is not None
    row_spec = lambda n: pl.BlockSpec((TM, n), lambda i: (i, 0))
    pm_spec = pl.BlockSpec((N_PAIRS, TM, LANES), lambda i: (0, i, 0))
    in_specs = [row_spec(D_MODEL), _const_spec((1, D_MODEL)), row_spec(WIDTH), row_spec(WIDTH),
                row_spec(A_HEADS * A_KV_LORA) if mla else pm_spec, pm_spec]
    args = [x, gpost, g1, g2, a, b]
    if mla:
        in_specs.append(_const_spec(wuv.shape))
        args.append(wuv)
    in_specs.append(_const_spec(wout.shape))
    args.append(wout)
    return pl.pallas_call(
        functools.partial(_out_body, mla=mla),
        grid=(rows // TM,),
        in_specs=in_specs,
        out_specs=row_spec(D_MODEL),
        out_shape=jax.ShapeDtypeStruct((rows, D_MODEL), F32),
        scratch_shapes=[pltpu.VMEM((TM, 2 * WIDTH), BF16)],
        compiler_params=_params("parallel"),
        name="out_proj_even" if mla else "out_proj_odd",
    )(*args)


def _softmax_init(m_ref, accl_ref):
    m_ref[...] = jnp.full(m_ref.shape, NEG, F32)
    accl_ref[...] = jnp.zeros(accl_ref.shape, F32)


def _lanes(x, n):
    parts = [x] * (n // LANES)
    if n % LANES:
        parts.append(x[:, :n % LANES])
    return parts[0] if len(parts) == 1 else jnp.concatenate(parts, axis=1)


def _scores(q, k, kv_t):
    return _dot(q, k) if kv_t else _dot_nt(q, k)


def _weighted(p, v, kv_t):
    return _dot_nt(p, v) if kv_t else _dot(p, v)


def _with_ones(v, kv_t=False):
    return jnp.concatenate([v, jnp.ones(v.shape, BF16)], axis=0 if kv_t else 1)


def _softmax_update(s, v1, m_ref, accl_ref, kv_t=False):
    keys = s.shape[1]
    n = v1.shape[0] if kv_t else v1.shape[1]
    m_prev = m_ref[...]
    m_new = jnp.maximum(m_prev, jnp.max(s, axis=1, keepdims=True))
    alpha = jnp.exp2(m_prev - m_new)
    p = jnp.exp2(s - _lanes(m_new, keys))
    accl_ref[...] = _lanes(alpha, n) * accl_ref[...] + _weighted(p.astype(BF16), v1, kv_t)
    m_ref[...] = m_new


def _softmax_result(accl_ref):
    accl = accl_ref[...]
    n = accl.shape[1] // 2
    return accl[:, :n] / accl[:, n:]


def _fill_suffix_ones(tri_ref):
    n = tri_ref.shape[0]
    r = lax.broadcasted_iota(jnp.int32, (n, n), 0)
    c = lax.broadcasted_iota(jnp.int32, (n, n), 1)
    tri_ref[...] = jnp.where(r > c, 1.0, 0.0).astype(BF16)


def _stick_tile(qs, k2, v2, tri, acc_ref, car_ref, mask, kv_t=False):
    z = _scores(qs, k2, kv_t)
    lb = _log_sigmoid(z)
    l1 = lb - z
    if mask is not None:
        l1 = jnp.where(mask, l1, 0.0)
    hi = l1.astype(BF16)
    lo = (l1 - hi.astype(F32)).astype(BF16)
    suf = _dot(hi, tri) + _dot(lo, tri) + car_ref[...]
    w = jnp.exp(lb + suf)
    if mask is not None:
        w = jnp.where(mask, w, 0.0)
    acc_ref[...] += _weighted(w.astype(BF16), v2, kv_t)
    car_ref[...] += jnp.sum(l1, axis=1, keepdims=True)


def _stick_alive(car_ref):
    return (jnp.max(car_ref[...]) >= STICK_DEAD).astype(jnp.int32)


def _local_causal(rows, keys, tq, strict, key_offset=0):
    r = lax.broadcasted_iota(jnp.int32, (rows, keys), 0) & (tq - 1)
    c = lax.broadcasted_iota(jnp.int32, (rows, keys), 1) + key_offset
    return (c < r) if strict else (c <= r)


def _decay_tile(qs, k2, v2, cq, ck, causal, m_ref, accl_ref, kv_t=False):
    tq, keys = cq.shape[0], ck.shape[1]
    s = _scores(qs, k2, kv_t)
    for hh in range(2):
        rows = slice(hh * tq, (hh + 1) * tq)
        sh = s[rows] + (cq[:, hh:hh + 1] - ck[hh:hh + 1])
        if causal:
            sh = jnp.where(_local_causal(tq, keys, tq, strict=False), sh, NEG)
        _softmax_update(sh, _with_ones(v2, kv_t), m_ref.at[rows], accl_ref.at[rows], kv_t)


def _split3(c):
    hi = c.astype(BF16).astype(F32)
    rest = c - hi
    mid = rest.astype(BF16).astype(F32)
    return hi, mid, rest - mid


def _decay_lanes(c, base, query):
    hi, mid, lo = _split3(c if query else -c)
    lane = lax.broadcasted_iota(jnp.int32, (c.shape[0], LANES), 1) - (base if query else base + 3)
    parts = jnp.where(lane == 0, hi, jnp.where(lane == 1, mid, jnp.where(lane == 2, lo, 0.0)))
    ones_at = lane + 3 if not query else lane - 3
    return jnp.where((ones_at >= 0) & (ones_at < 3), 1.0, parts)


def _decay_operand(x, cum, hh, query):
    lane = lax.broadcasted_iota(jnp.int32, x.shape, 1)
    own = (lane < HEAD_DIM) if hh == 0 else (lane >= HEAD_DIM)
    base = HEAD_DIM if hh == 0 else 0
    return jnp.where(own, x, _decay_lanes(cum[:, hh:hh + 1], base, query)).astype(BF16)


def _decay_tile_heads(q1, k1, v1, causal_offset, m_ref, accl_ref):
    tq, keys = q1[0].shape[0], k1.shape[1]
    for hh in range(2):
        rows = slice(hh * tq, (hh + 1) * tq)
        s = _dot_nt(q1[hh], k1[hh])
        if causal_offset is not None:
            s = jnp.where(_local_causal(tq, keys, tq, strict=False, key_offset=causal_offset), s, NEG)
        _softmax_update(s, v1[hh], m_ref.at[rows], accl_ref.at[rows])


def _pair_result(accl, tq):
    o = accl / pltpu.roll(accl, HEAD_DIM, axis=1)
    return _merge_pair(o, tq)


def _mla_prompt_body(q_ref, k_ref, o_ref, m_ref, accl_ref):
    g = pl.program_id(1)
    grp = 4 * CHUNK

    def update(h, r0, k, mask):
        sl = pl.ds(r0, grp)
        s = _dot_nt(q_ref[sl, h * 256:(h + 1) * 256], k) * MLA_LOGIT_SCALE
        if mask is not None:
            s = jnp.where(mask, s, NEG)
        _softmax_update(s, _with_ones(k[:, :A_KV_LORA]), m_ref.at[h, sl], accl_ref.at[h, sl])

    def group(gl, carry):
        gq = g * (MLA_QB // grp) + gl
        g0 = pl.multiple_of(gl * grp, grp)
        for h in range(A_HEADS):
            _softmax_init(m_ref.at[h, pl.ds(g0, grp)], accl_ref.at[h, pl.ds(g0, grp)])

        def earlier(k0, n_keys):
            k = k_ref[pl.ds(k0, n_keys), :]
            for h in range(A_HEADS):
                update(h, g0, k, None)

        def kv(j, c2):
            earlier(pl.multiple_of(j * 2 * TK, 2 * TK), 2 * TK)
            return c2

        lax.fori_loop(0, gq // 2, kv, 0)

        @pl.when(gq % 2 == 1)
        def _():
            earlier(pl.multiple_of((gq - 1) * TK, TK), TK)

        kd = k_ref[pl.ds(pl.multiple_of(gq * TK, TK), TK), :]
        row_chunk = lax.broadcasted_iota(jnp.int32, (grp, TK), 0) // CHUNK
        key_chunk = lax.broadcasted_iota(jnp.int32, (grp, TK), 1) // CHUNK
        own = key_chunk <= row_chunk
        for h in range(A_HEADS):
            update(h, g0, kd, own)
        for h in range(A_HEADS):
            o_ref[pl.ds(g0, grp), h * A_KV_LORA:(h + 1) * A_KV_LORA] = _softmax_result(
                accl_ref.at[h, pl.ds(g0, grp)]).astype(BF16)
        return carry

    lax.fori_loop(0, MLA_QB // grp, group, 0)


def _mla_prompt(q2, k2, n_b, n_s):
    assert n_s % MLA_QB == 0 and TK == 4 * CHUNK
    return pl.pallas_call(
        _mla_prompt_body,
        grid=(n_b, n_s // MLA_QB),
        in_specs=[pl.BlockSpec((None, MLA_QB, A_HEADS * 256), lambda b, g: (b, g, 0)),
                  pl.BlockSpec((None, n_s, 256), lambda b, g: (b, 0, 0))],
        out_specs=pl.BlockSpec((None, MLA_QB, A_HEADS * A_KV_LORA), lambda b, g: (b, g, 0)),
        out_shape=jax.ShapeDtypeStruct((n_b, n_s, A_HEADS * A_KV_LORA), BF16),
        scratch_shapes=[pltpu.VMEM((A_HEADS, MLA_QB, LANES), F32),
                        pltpu.VMEM((A_HEADS, MLA_QB, 2 * LANES), F32)],
        compiler_params=_params("parallel", "parallel"),
        name="mla_prompt",
    )(q2.reshape(n_b, n_s, A_HEADS * 256), k2.reshape(n_b, n_s, 256))


def _mla_sample_body(q_ref, kn_ref, ckv_ref, kr_ref, o_ref, m_ref, accl_ref):
    j = pl.program_id(1)

    @pl.when(j == 0)
    def _():
        _softmax_init(m_ref, accl_ref)

    n_tiles = q_ref.shape[0] // MLA_RT
    ck = ckv_ref[...].astype(BF16)
    kr = kr_ref[...].T.astype(BF16)
    k = jnp.concatenate([ck, kr, jnp.zeros((TKC, LANES - A_ROPE), BF16)], axis=1)
    for t in range(n_tiles):
        sl = slice(t * MLA_RT, (t + 1) * MLA_RT)
        s = _dot_nt(q_ref[sl, :], k) * MLA_LOGIT_SCALE
        _softmax_update(s, _with_ones(ck), m_ref.at[sl], accl_ref.at[sl])

    @pl.when(j == pl.num_programs(1) - 1)
    def _():
        kn = kn_ref[...]
        for t in range(n_tiles):
            sl = slice(t * MLA_RT, (t + 1) * MLA_RT)
            s = _dot_nt(q_ref[sl, :], kn) * MLA_LOGIT_SCALE
            _softmax_update(s, _with_ones(kn[:, :A_KV_LORA]), m_ref.at[sl], accl_ref.at[sl])
        o_ref[...] = _softmax_result(accl_ref).astype(BF16)


def _mla_sample(q2, k2, cache_ckv, cache_kr, n_b, n_s):
    rows = n_s * A_HEADS
    n_past = cache_ckv.shape[1]
    return pl.pallas_call(
        _mla_sample_body,
        grid=(n_b, n_past // TKC),
        in_specs=[pl.BlockSpec((None, rows, 256), lambda b, j: (b, 0, 0)),
                  pl.BlockSpec((None, n_s, 256), lambda b, j: (b, 0, 0)),
                  pl.BlockSpec((None, TKC, A_KV_LORA), lambda b, j: (b, j, 0)),
                  pl.BlockSpec((None, A_ROPE, TKC), lambda b, j: (b, 0, j))],
        out_specs=pl.BlockSpec((None, rows, A_KV_LORA), lambda b, j: (b, 0, 0)),
        out_shape=jax.ShapeDtypeStruct((n_b, rows, A_KV_LORA), BF16),
        scratch_shapes=[pltpu.VMEM((rows, LANES), F32), pltpu.VMEM((rows, 2 * LANES), F32)],
        compiler_params=_params("parallel", "arbitrary"),
        name="mla_sample",
    )(q2.reshape(n_b, rows, 256), k2.reshape(n_b, n_s, 256), cache_ckv, cache_kr)


def _sb_prompt_body(q_ref, k_ref, v_ref, o_ref, tri_ref, acc_ref, car_ref):
    n_s = q_ref.shape[1]
    _fill_suffix_ones(tri_ref)

    def qblock(i, carry):
        q0 = pl.multiple_of(i * TQ, TQ)
        qs = [_stack_pair(q_ref[g, pl.ds(q0, TQ), :]) for g in range(PAIRS_PER_STEP)]
        acc_ref[...] = jnp.zeros(acc_ref.shape, F32)
        car_ref[...] = jnp.zeros(car_ref.shape, F32)

        def tiles(k0, mask):
            for g in range(PAIRS_PER_STEP):
                _stick_tile(qs[g], k_ref[g, pl.ds(k0, TK), :], v_ref[g, pl.ds(k0, TK), :], tri_ref[...],
                            acc_ref.at[g], car_ref.at[g], mask)

        tiles(q0, _local_causal(2 * TQ, TK, TQ, strict=True))

        def kv(state):
            jj, _ = state
            tiles(pl.multiple_of((i - 1 - jj) * TK, TK), None)
            return jj + 1, _stick_alive(car_ref)

        lax.while_loop(lambda st: (st[0] < i) & (st[1] > 0), kv, (jnp.int32(0), _stick_alive(car_ref)))
        for g in range(PAIRS_PER_STEP):
            o_ref[g, pl.ds(q0, TQ), :] = _merge_pair(acc_ref[g], TQ)
        return carry

    lax.fori_loop(0, n_s // TQ, qblock, 0)


def _pair_seq_spec(n_s):
    return pl.BlockSpec((PAIRS_PER_STEP, n_s, LANES), lambda b, g: (g, b, 0))


def _sb_prompt(q, k, v, n_b, n_s):
    assert TQ == TK
    spec = _pair_seq_spec(n_s)
    return pl.pallas_call(
        _sb_prompt_body,
        grid=(n_b, N_PAIRS // PAIRS_PER_STEP),
        in_specs=[spec, spec, spec],
        out_specs=spec,
        out_shape=jax.ShapeDtypeStruct((N_PAIRS, n_b * n_s, LANES), F32),
        scratch_shapes=[pltpu.VMEM((TK, TK), BF16), pltpu.VMEM((PAIRS_PER_STEP, 2 * TQ, LANES), F32),
                        pltpu.VMEM((PAIRS_PER_STEP, 2 * TQ, 1), F32)],
        compiler_params=_params("parallel", "parallel"),
        name="sb_prompt",
    )(q, k, v)


def _sb_sample_body(q_ref, kn_ref, vn_ref, ck_hbm, cv_hbm, o_ref, kbuf, vbuf, sem, tri_ref, acc_ref,
                    car_ref):
    b = pl.program_id(0)
    n_q = q_ref.shape[1]
    n_blk = ck_hbm.shape[2] // TK

    def block_copies(blk):
        rows = pl.ds(pl.multiple_of(blk * TK, TK), TK)
        return (pltpu.make_async_copy(ck_hbm.at[b, :, rows], kbuf, sem.at[0]),
                pltpu.make_async_copy(cv_hbm.at[b, :, rows], vbuf, sem.at[1]))

    def start(blk):
        for c in block_copies(blk):
            c.start()

    def cache_block(blk):
        for c in block_copies(blk):
            c.wait()
        for p in range(N_PAIRS):
            sl = slice(p * LANES, (p + 1) * LANES)
            _stick_tile(qs[p], kbuf[sl, :].astype(BF16), vbuf[sl, :].astype(BF16), tri_ref[...],
                        acc_ref.at[p], car_ref.at[p], None, kv_t=True)

    start(n_blk - 1)
    _fill_suffix_ones(tri_ref)
    acc_ref[...] = jnp.zeros(acc_ref.shape, F32)
    car_ref[...] = jnp.zeros(car_ref.shape, F32)
    qs = [_stack_pair(q_ref[p]) for p in range(N_PAIRS)]
    mask = _local_causal(2 * n_q, n_q, n_q, strict=True)
    for p in range(N_PAIRS):
        _stick_tile(qs[p], kn_ref[p], vn_ref[p], tri_ref[:n_q, :n_q], acc_ref.at[p], car_ref.at[p], mask)
    cache_block(n_blk - 1)

    def older(state):
        blk, _ = state
        start(blk)
        cache_block(blk)
        return blk - 1, _stick_alive(car_ref)

    lax.while_loop(lambda st: (st[0] >= 0) & (st[1] > 0), older,
                   (jnp.int32(n_blk - 2), _stick_alive(car_ref)))
    for p in range(N_PAIRS):
        o_ref[p] = _merge_pair(acc_ref[p], n_q)


def _sb_sample(q, kn, vn, cache_kt, cache_vt, n_b, n_s):
    new_spec = pl.BlockSpec((N_PAIRS, n_s, LANES), lambda b: (0, b, 0))
    hbm = pl.BlockSpec(memory_space=pl.ANY)
    return pl.pallas_call(
        _sb_sample_body,
        grid=(n_b,),
        in_specs=[new_spec, new_spec, new_spec, hbm, hbm],
        out_specs=new_spec,
        out_shape=jax.ShapeDtypeStruct((N_PAIRS, n_b * n_s, LANES), F32),
        scratch_shapes=[pltpu.VMEM((WIDTH, TK), F32), pltpu.VMEM((WIDTH, TK), F32),
                        pltpu.SemaphoreType.DMA((2,)), pltpu.VMEM((TK, TK), BF16),
                        pltpu.VMEM((N_PAIRS, 2 * n_s, LANES), F32), pltpu.VMEM((N_PAIRS, 2 * n_s, 1), F32)],
        compiler_params=_params("arbitrary"),
        name="sb_sample",
    )(q, kn, vn, cache_kt, cache_vt)


def _band_block(qs, kwin, vwin, bias, valid_from, tq):
    s = _dot_nt(qs, kwin) + bias
    if valid_from is not None:
        col = lax.broadcasted_iota(jnp.int32, s.shape, 1)
        s = jnp.where(col >= valid_from, s, NEG)
    p = jnp.exp2(s - jnp.max(s, axis=1, keepdims=True))
    o = _dot(p.astype(BF16), vwin) / jnp.sum(p, axis=1, keepdims=True)
    return _merge_pair(o, tq)


def _band_prompt_body(q_ref, k_ref, v_ref, bias_ref, o_ref, kpad_ref, vpad_ref):
    n_s = q_ref.shape[1]
    win = LEFT_CTX + BAND_TQ
    zeros = jnp.zeros((PAIRS_PER_STEP, LEFT_CTX, LANES), BF16)
    kpad_ref[:, :LEFT_CTX, :] = zeros
    vpad_ref[:, :LEFT_CTX, :] = zeros
    kpad_ref[:, LEFT_CTX:, :] = k_ref[...]
    vpad_ref[:, LEFT_CTX:, :] = v_ref[...]

    def qblock(i, carry):
        q0 = pl.multiple_of(i * BAND_TQ, BAND_TQ)
        for g in range(PAIRS_PER_STEP):
            qs = _stack_pair(q_ref[g, pl.ds(q0, BAND_TQ), :])
            o_ref[g, pl.ds(q0, BAND_TQ), :] = _band_block(
                qs, kpad_ref[g, pl.ds(q0, win), :], vpad_ref[g, pl.ds(q0, win), :], bias_ref[g],
                LEFT_CTX - q0, BAND_TQ)
        return carry

    lax.fori_loop(0, n_s // BAND_TQ, qblock, 0)


def _band_prompt(q, k, v, bias, n_b, n_s):
    spec = _pair_seq_spec(n_s)
    win = LEFT_CTX + BAND_TQ
    return pl.pallas_call(
        _band_prompt_body,
        grid=(n_b, N_PAIRS // PAIRS_PER_STEP),
        in_specs=[spec, spec, spec,
                  pl.BlockSpec((PAIRS_PER_STEP, 2 * BAND_TQ, win), lambda b, g: (g, 0, 0))],
        out_specs=spec,
        out_shape=jax.ShapeDtypeStruct((N_PAIRS, n_b * n_s, LANES), F32),
        scratch_shapes=[pltpu.VMEM((PAIRS_PER_STEP, LEFT_CTX + n_s, LANES), BF16),
                        pltpu.VMEM((PAIRS_PER_STEP, LEFT_CTX + n_s, LANES), BF16)],
        compiler_params=_params("parallel", "parallel"),
        name="band_prompt",
    )(q, k, v, bias)


def _band_sample_body(q_ref, kn_ref, vn_ref, kc_ref, vc_ref, bias_ref, o_ref):
    n_q = q_ref.shape[1]
    n_keep = kc_ref.shape[1]
    for p in range(N_PAIRS):
        sl = slice(p * LANES, (p + 1) * LANES)
        qs = _stack_pair(q_ref[p])
        bias = bias_ref[p]
        s_old = _dot(qs, kc_ref[sl, :].astype(BF16)) + bias[:, :n_keep]
        s_new = _dot_nt(qs, kn_ref[p]) + bias[:, n_keep:]
        m = jnp.maximum(jnp.max(s_old, axis=1, keepdims=True), jnp.max(s_new, axis=1, keepdims=True))
        p_old = jnp.exp2(s_old - m)
        p_new = jnp.exp2(s_new - m)
        o = _dot_nt(p_old.astype(BF16), vc_ref[sl, :].astype(BF16)) + _dot(p_new.astype(BF16), vn_ref[p])
        total = jnp.sum(p_old, axis=1, keepdims=True) + jnp.sum(p_new, axis=1, keepdims=True)
        o_ref[p] = _merge_pair(o / total, n_q)


def _band_sample(q, kn, vn, cache_kt, cache_vt, bias, n_b, n_s):
    n_keep = cache_kt.shape[2]
    new_spec = pl.BlockSpec((N_PAIRS, n_s, LANES), lambda b: (0, b, 0))
    cache_spec = pl.BlockSpec((None, WIDTH, n_keep), lambda b: (b, 0, 0))
    return pl.pallas_call(
        _band_sample_body,
        grid=(n_b,),
        in_specs=[new_spec, new_spec, new_spec, cache_spec, cache_spec,
                  _const_spec((N_PAIRS, 2 * n_s, n_keep + n_s))],
        out_specs=new_spec,
        out_shape=jax.ShapeDtypeStruct((N_PAIRS, n_b * n_s, LANES), F32),
        compiler_params=_params("parallel"),
        name="band_sample",
    )(q, kn, vn, cache_kt, cache_vt, bias)


def _band_bias(rel_bias, tq):
    win = LEFT_CTX + tq
    i = np.arange(tq)[:, None]
    w = np.arange(win)[None, :]
    qc, kc = i // CHUNK, w // CHUNK - LEFT_CTX // CHUNK
    ok = (kc <= qc) & (kc >= qc - LEFT_CTX // CHUNK)
    u = np.arange(win + tq - 1)
    rel = np.clip(LEFT_CTX + (tq - 1) - u, -REL_CLIP, REL_CLIP) + REL_CLIP
    diag = rel_bias.astype(F32)[:, rel]
    n = win + tq - 1
    flat = jnp.tile(diag, (1, tq))[:, tq - 1:tq - 1 + tq * (n - 1)]
    tab = flat.reshape(HEADS, tq, n - 1)[:, :, :win]
    tab = jnp.where(jnp.asarray(ok)[None], tab * LOG2E, NEG)
    return tab.reshape(N_PAIRS, 2 * tq, win)


def _fox_prompt_body(q_ref, k_ref, v_ref, cum_ref, o_ref, k1_ref, v1_ref, m_ref, accl_ref):
    n_s = q_ref.shape[1]

    def prepare(jb, carry):
        rows = pl.ds(pl.multiple_of(jb * TK, TK), TK)
        lane = lax.broadcasted_iota(jnp.int32, (TK, LANES), 1)
        for g in range(PAIRS_PER_STEP):
            k = k_ref[g, rows, :].astype(F32)
            v = v_ref[g, rows, :].astype(F32)
            cum = cum_ref[g, rows, :]
            for hh in range(2):
                k1_ref[g, hh, rows, :] = _decay_operand(k, cum, hh, query=False)
                own = (lane < HEAD_DIM) if hh == 0 else (lane >= HEAD_DIM)
                v1_ref[g, hh, rows, :] = jnp.where(own, v, 1.0).astype(BF16)
        return carry

    lax.fori_loop(0, n_s // TK, prepare, 0)

    tiles_per_q = FOX_TQ // TK

    def qblock(i, carry):
        q0 = pl.multiple_of(i * FOX_TQ, FOX_TQ)
        q1 = []
        for g in range(PAIRS_PER_STEP):
            q = q_ref[g, pl.ds(q0, FOX_TQ), :].astype(F32)
            cum = cum_ref[g, pl.ds(q0, FOX_TQ), :]
            q1.append([_decay_operand(q, cum, hh, query=True) for hh in range(2)])
        _softmax_init(m_ref, accl_ref)

        def tile(jb, causal_offset):
            rows = pl.ds(pl.multiple_of(jb * TK, TK), TK)
            for g in range(PAIRS_PER_STEP):
                _decay_tile_heads(q1[g], k1_ref[g, :, rows, :], v1_ref[g, :, rows, :], causal_offset,
                                  m_ref.at[g], accl_ref.at[g])

        for t in range(tiles_per_q):
            tile(i * tiles_per_q + t, t * TK)

        def kv(jb, c2):
            tile(jb, None)
            return c2

        lax.fori_loop(0, i * tiles_per_q, kv, 0)
        for g in range(PAIRS_PER_STEP):
            o_ref[g, pl.ds(q0, FOX_TQ), :] = _pair_result(accl_ref[g], FOX_TQ)
        return carry

    lax.fori_loop(0, n_s // FOX_TQ, qblock, 0)


def _fox_prompt(q, k, v, cum, n_b, n_s):
    assert FOX_TQ % TK == 0 and n_s % FOX_TQ == 0
    spec = _pair_seq_spec(n_s)
    cum_rows = cum.reshape(n_b, N_PAIRS, 2, n_s).transpose(0, 1, 3, 2)
    g_ = PAIRS_PER_STEP
    return pl.pallas_call(
        _fox_prompt_body,
        grid=(n_b, N_PAIRS // g_),
        in_specs=[spec, spec, spec, pl.BlockSpec((None, g_, n_s, 2), lambda b, g: (b, g, 0, 0))],
        out_specs=spec,
        out_shape=jax.ShapeDtypeStruct((N_PAIRS, n_b * n_s, LANES), F32),
        scratch_shapes=[pltpu.VMEM((g_, 2, n_s, LANES), BF16), pltpu.VMEM((g_, 2, n_s, LANES), BF16),
                        pltpu.VMEM((g_, 2 * FOX_TQ, LANES), F32), pltpu.VMEM((g_, 2 * FOX_TQ, LANES), F32)],
        compiler_params=_params("parallel", "parallel"),
        name="fox_prompt",
    )(q, k, v, cum_rows)


def _fox_sample_body(q_ref, kn_ref, vn_ref, kc_ref, vc_ref, cq_ref, ckn_ref, ckc_ref, o_ref,
                     m_ref, accl_ref):
    j = pl.program_id(1)
    n_q = q_ref.shape[1]
    cq = cq_ref[...]

    def pair_update(p, k2, v2, ck, new_rows):
        rows = slice(2 * p * n_q, (2 * p + 2) * n_q)
        _decay_tile(_stack_pair(q_ref[p]), k2, v2, cq[:, 2 * p:2 * p + 2], ck[2 * p:2 * p + 2], new_rows,
                    m_ref.at[rows], accl_ref.at[rows], kv_t=not new_rows)

    @pl.when(j == 0)
    def _():
        _softmax_init(m_ref, accl_ref)
        ckn = ckn_ref[...]
        for p in range(N_PAIRS):
            pair_update(p, kn_ref[p], vn_ref[p], ckn, True)

    @pl.when(j > 0)
    def _():
        ckc = ckc_ref[...]
        for p in range(N_PAIRS):
            sl = slice(p * LANES, (p + 1) * LANES)
            for sub in range(TKC // FOX_TK):
                keys = slice(sub * FOX_TK, (sub + 1) * FOX_TK)
                pair_update(p, kc_ref[sl, keys].astype(BF16), vc_ref[sl, keys].astype(BF16), ckc[:, keys],
                            False)

    @pl.when(j == pl.num_programs(1) - 1)
    def _():
        for p in range(N_PAIRS):
            rows = slice(2 * p * n_q, (2 * p + 2) * n_q)
            o_ref[p] = _merge_pair(_softmax_result(accl_ref.at[rows]), n_q)


def _fox_sample(q, kn, vn, cache_k, cache_v, cum, n_b, n_s):
    n_past = cache_k.shape[2]
    n_blk = n_past // TKC
    cq = cum[..., n_past:].transpose(0, 2, 1)
    ckn = cum[..., n_past:]
    ckc = cum[..., :n_past]
    new_spec = pl.BlockSpec((N_PAIRS, n_s, LANES), lambda b, j: (0, b, 0))
    cache_blk = lambda j: jnp.maximum(j - 1, 0)
    cache_spec = pl.BlockSpec((None, WIDTH, TKC), lambda b, j: (b, 0, cache_blk(j)))
    return pl.pallas_call(
        _fox_sample_body,
        grid=(n_b, n_blk + 1),
        in_specs=[new_spec, new_spec, new_spec, cache_spec, cache_spec,
                  pl.BlockSpec((None, n_s, HEADS), lambda b, j: (b, 0, 0)),
                  pl.BlockSpec((None, HEADS, n_s), lambda b, j: (b, 0, 0)),
                  pl.BlockSpec((None, HEADS, TKC), lambda b, j: (b, 0, cache_blk(j)))],
        out_specs=new_spec,
        out_shape=jax.ShapeDtypeStruct((N_PAIRS, n_b * n_s, LANES), F32),
        scratch_shapes=[pltpu.VMEM((HEADS * n_s, LANES), F32), pltpu.VMEM((HEADS * n_s, 2 * LANES), F32)],
        compiler_params=_params("parallel", "arbitrary"),
        name="fox_sample",
    )(q, kn, vn, cache_k, cache_v, cq, ckn, ckc)


def _cumsum_body(x_ref, o_ref):
    n_rows, n_cols = x_ref.shape
    r = lax.broadcasted_iota(jnp.int32, (LANES, LANES), 0)
    c = lax.broadcasted_iota(jnp.int32, (LANES, LANES), 1)
    ones = jnp.where(r <= c, 1.0, 0.0).astype(BF16)
    total = jnp.zeros((n_rows, 1), F32)
    for g in range(n_cols // LANES):
        x = x_ref[:, g * LANES:(g + 1) * LANES]
        h1 = x.astype(BF16)
        r1 = x - h1.astype(F32)
        h2 = r1.astype(BF16)
        h3 = (r1 - h2.astype(F32)).astype(BF16)
        y = _dot(h1, ones) + _dot(h2, ones) + _dot(h3, ones) + total
        o_ref[:, g * LANES:(g + 1) * LANES] = y * LOG2E
        total = y[:, LANES - 1:LANES]


def _cumsum_rows(x):
    rows, n = x.shape
    n_pad = -(-n // LANES) * LANES
    xp = jnp.pad(x, ((0, 0), (0, n_pad - n)))
    out = pl.pallas_call(
        _cumsum_body,
        out_shape=jax.ShapeDtypeStruct((rows, n_pad), F32),
        compiler_params=pltpu.CompilerParams(vmem_limit_bytes=VMEM_LIMIT),
        name="cumsum_rows",
    )(xp)
    return out[:, :n]


def _split_cols(w, sizes):
    out, off = [], 0
    for n in sizes:
        out.append(w[:, off:off + n])
        off += n
    return out


def _rope_tables(pos, n_rows):
    half = A_ROPE // 2
    inv_freq = ROPE_THETA ** (-jnp.arange(half, dtype=F32) / half)
    ang = pos.astype(F32)[:, None] * inv_freq[None, :]
    cos, sin = jnp.cos(ang), jnp.sin(ang)
    zeros = jnp.zeros((pos.shape[0], LANES - A_ROPE), F32)
    cos_t = jnp.concatenate([cos, cos, zeros], axis=1)
    sin_t = jnp.concatenate([-sin, sin, zeros], axis=1)
    reps = max(1, n_rows // pos.shape[0])
    return jnp.tile(cos_t, (reps, 1)), jnp.tile(sin_t, (reps, 1))


def _swap_halves(w):
    half = w.shape[-1] // 2
    return jnp.concatenate([w[..., half:], w[..., :half]], axis=-1)


def _pad_lanes(w):
    return jnp.pad(w, [(0, 0)] * (w.ndim - 1) + [(0, LANES - w.shape[-1])])


def _prep_even(w_in, q_norm, w_uq, kv_norm, w_uk, w_uv):
    wqa, wkv, wkr, wga, wqb, wkb, wvb, wgb = _split_cols(
        w_in, (A_Q_LORA, A_KV_LORA, A_ROPE, WIDTH, WIDTH, WIDTH, WIDTH, WIDTH))
    b = lambda a: a.astype(BF16)
    uq_rope = w_uq[:, :, A_NOPE:]
    uk_t = jnp.transpose(w_uk, (1, 2, 0))
    z = jnp.zeros((A_NOPE, A_KV_LORA), w_uk.dtype)
    wuk = jnp.stack([jnp.block([[uk_t[2 * p], z], [z, uk_t[2 * p + 1]]]) for p in range(N_PAIRS)])
    uv_t = jnp.transpose(w_uv, (1, 0, 2))
    zv = jnp.zeros((A_KV_LORA, A_V), w_uv.dtype)
    wuv = jnp.stack([jnp.block([[uv_t[2 * p], zv], [zv, uv_t[2 * p + 1]]]) for p in range(N_PAIRS)])
    return dict(
        wqa=b(wqa), wkv=b(wkv), wkr=b(_pad_lanes(wkr)), wkrs=b(_pad_lanes(_swap_halves(wkr))),
        wga=b(wga), wqb=b(wqb), wkb=b(wkb), wvb=b(wvb), wgb=b(wgb),
        qn=q_norm.reshape(1, -1), kvn=kv_norm.reshape(1, -1),
        wuqn=b(w_uq[:, :, :A_NOPE].reshape(A_Q_LORA, A_HEADS * A_NOPE)),
        wuqr=b(_pad_lanes(uq_rope).reshape(A_Q_LORA, A_HEADS * LANES)),
        wuqrs=b(_pad_lanes(_swap_halves(uq_rope)).reshape(A_Q_LORA, A_HEADS * LANES)),
        wuk=b(wuk), wuv=b(wuv))


def _prep_odd(w_in, forget_bias):
    wqc, wkc, wvc, wgc, wqd, wkd, wvd, wf, wgd = _split_cols(
        w_in, (WIDTH, WIDTH, WIDTH, WIDTH, WIDTH, WIDTH, WIDTH, HEADS, WIDTH))
    b = lambda a: a.astype(BF16)
    return dict(wqc=b(wqc), wkc=b(wkc), wvc=b(wvc), wgc=b(wgc), wqd=b(wqd), wkd=b(wkd), wvd=b(wvd),
                wf=b(_pad_lanes(wf)), wgd=b(wgd), fb=forget_bias.astype(F32).reshape(1, HEADS),
                wf_t=b(wf.T), fb_t=forget_bias.astype(F32).reshape(HEADS, 1))


def kernel(x_prompt, x_sample, cache_mla_ckv, cache_mla_krope, cache_sb_k, cache_sb_v, cache_band_k,
           cache_band_v, cache_fox_k, cache_fox_v, cache_fox_logf, norm_pre, norm_post, w_in_even,
           a_q_norm, a_w_uq, a_kv_norm, a_w_uk, a_w_uv, w_out_even, w_in_odd, c_rel_bias,
           d_forget_bias, w_out_odd):
    n_b, n_s, _ = x_prompt.shape
    d_b, d_s, _ = x_sample.shape
    n_past = cache_sb_k.shape[2]
    n_keep = cache_band_k.shape[2]
    assert n_s % (2 * TQ) == 0 and n_past % TKC == 0 and d_s == CHUNK and n_past % CHUNK == 0
    assert n_keep == LEFT_CTX and (d_b * d_s) % TM == 0 and TM % d_s == 0

    xp = x_prompt.reshape(n_b * n_s, D_MODEL)
    xs = x_sample.reshape(d_b * d_s, D_MODEL)
    row = lambda a: a.reshape(1, -1)
    heads = lambda a, b, s: a.reshape(b, s, HEADS, HEAD_DIM)
    rows_minor = lambda a: a.transpose(0, 2, 3, 1).reshape(a.shape[0], WIDTH, a.shape[1])

    we = _prep_even(w_in_even[0], a_q_norm[0], a_w_uq[0], a_kv_norm[0], a_w_uk[0], a_w_uv[0])
    wout_e = w_out_even[0].astype(BF16)
    cos_p, sin_p = _rope_tables(jnp.arange(n_s), TM)
    cos_s, sin_s = _rope_tables(n_past + jnp.arange(d_s), TM)

    (ckv_p, kr_p, k2_p, q2_p, ga_p, gb_p, qb_p, kb_p, kb16_p, vb_p, vb16_p) = _in_even(
        xp, row(norm_pre[0]), cos_p, sin_p, we, seq_len=n_s)
    (ckv_s, kr_s, k2_s, q2_s, ga_s, gb_s, qb_s, kb_s, kb16_s, vb_s, vb16_s) = _in_even(
        xs, row(norm_pre[0]), cos_s, sin_s, we)

    lat_p = _mla_prompt(q2_p, k2_p, n_b, n_s).reshape(n_b * n_s, A_HEADS * A_KV_LORA)
    lat_s = _mla_sample(q2_s, k2_s, cache_mla_ckv[0], cache_mla_krope[0].transpose(0, 2, 1), d_b, d_s
                        ).reshape(d_b * d_s, A_HEADS * A_KV_LORA)
    sb_p = _sb_prompt(qb_p, kb16_p, vb16_p, n_b, n_s)
    sb_s = _sb_sample(qb_s, kb16_s, vb16_s, rows_minor(cache_sb_k[0]), rows_minor(cache_sb_v[0]), d_b, d_s)

    xp1 = _out_proj(xp, row(norm_post[0]), ga_p, gb_p, lat_p, sb_p, wout_e, we['wuv'])
    xs1 = _out_proj(xs, row(norm_post[0]), ga_s, gb_s, lat_s, sb_s, wout_e, we['wuv'])

    wo = _prep_odd(w_in_odd[0], d_forget_bias[0])
    wout_o = w_out_odd[0].astype(BF16)
    (qc_p, kc_p, kc16_p, vc_p, vc16_p, gc_p, qd_p, kd_p, kd16_p, vd_p, vd16_p, lf_p, gd_p) = _in_odd(
        xp1, row(norm_pre[1]), wo, seq_len=n_s)
    (qc_s, kc_s, kc16_s, vc_s, vc16_s, gc_s, qd_s, kd_s, kd16_s, vd_s, vd16_s, lf_s, gd_s) = _in_odd(
        xs1, row(norm_pre[1]), wo)

    band_p = _band_prompt(qc_p, kc16_p, vc16_p, _band_bias(c_rel_bias[0], BAND_TQ), n_b, n_s)
    band_s = _band_sample(qc_s, kc16_s, vc16_s, rows_minor(cache_band_k[0]), rows_minor(cache_band_v[0]),
                          _band_bias(c_rel_bias[0], d_s), d_b, d_s)

    lf_s3 = lf_s.reshape(d_b, d_s, HEADS)
    cum_p = _cumsum_rows(lf_p.reshape(n_b * HEADS, n_s)).reshape(n_b, HEADS, n_s)
    lf_all = jnp.concatenate([cache_fox_logf[0].astype(F32), lf_s3], axis=1)
    cum_s = _cumsum_rows(lf_all.transpose(0, 2, 1).reshape(d_b * HEADS, n_past + d_s)
                         ).reshape(d_b, HEADS, n_past + d_s)
    fox_p = _fox_prompt(qd_p, kd16_p, vd16_p, cum_p, n_b, n_s)
    fox_s = _fox_sample(qd_s, kd16_s, vd16_s, rows_minor(cache_fox_k[0]), rows_minor(cache_fox_v[0]), cum_s,
                        d_b, d_s)

    xp2 = _out_proj(xp1, row(norm_post[1]), gc_p, gd_p, band_p, fox_p, wout_o)
    xs2 = _out_proj(xs1, row(norm_post[1]), gc_s, gd_s, band_s, fox_s, wout_o)

    keep = min(LEFT_CTX, n_s)
    band_k_s = jnp.concatenate([cache_band_k[0], heads(kc_s, d_b, d_s)], axis=1)[:, d_s:]
    band_v_s = jnp.concatenate([cache_band_v[0], heads(vc_s, d_b, d_s)], axis=1)[:, d_s:]
    one = lambda a: a[None]
    heads_t = lambda a: a.reshape(n_b, HEADS, HEAD_DIM, a.shape[-1]).transpose(0, 3, 1, 2)
    return (xp2.reshape(n_b, n_s, D_MODEL), xs2.reshape(d_b, d_s, D_MODEL),
            one(ckv_p.reshape(n_b, n_s, A_KV_LORA)), one(kr_p.transpose(0, 2, 1)),
            one(heads_t(kb_p)), one(heads_t(vb_p)),
            one(heads_t(kc_p[:, :, n_s - keep:])), one(heads_t(vc_p[:, :, n_s - keep:])),
            one(heads_t(kd_p)), one(heads_t(vd_p)), one(lf_p.transpose(0, 2, 1)),
            one(ckv_s.reshape(d_b, d_s, A_KV_LORA)), one(kr_s.reshape(d_b, d_s, A_ROPE)),
            one(heads(kb_s, d_b, d_s)), one(heads(vb_s, d_b, d_s)),
            one(band_k_s), one(band_v_s),
            one(heads(kd_s, d_b, d_s)), one(heads(vd_s, d_b, d_s)), one(lf_s3))
```

```python
import functools

import numpy as np
import jax
import jax.numpy as jnp
from jax import lax
from jax.experimental import pallas as pl
from jax.experimental.pallas import tpu as pltpu

F32 = jnp.float32
BF16 = jnp.bfloat16

D_MODEL = 1024
PAST_LEN = 4096
CHUNK = 64
LEFT_CTX = 512
REL_CLIP = 128
EPS = 1e-6
NEG = -1e30
ROPE_THETA = 10000.0
A_HEADS = 8
A_Q_LORA = 256
A_KV_LORA = 128
A_NOPE = 64
A_ROPE = 32
A_V = 64
A_SCALE = (A_NOPE + A_ROPE) ** -0.5
HEADS = 8
HEAD_DIM = 64
WIDTH = HEADS * HEAD_DIM
QK_SCALE = HEAD_DIM ** -0.5
LOG2E = 1.4426950408889634
MLA_LOGIT_SCALE = A_SCALE * LOG2E
N_PAIRS = HEADS // 2

LANES = 128
VMEM_LIMIT = 52 * 1024 * 1024
TM = 512
TQ = 256
TK = 256
TKC = 1024
SB_T = 256
SB_TK = 256
FOX_TQ = 512
FOX_TK = 1024
BAND_TQ = 128
MLA_RT = 256
MLA_QB = 512
PAIRS_PER_STEP = 4
STICK_DEAD = -104.0


def _dot(a, b):
    return jnp.dot(a, b, preferred_element_type=F32)


def _dot_nt(a, b):
    return lax.dot_general(a, b, (((1,), (1,)), ((), ())), preferred_element_type=F32)


def _rms(x, g):
    y = x * lax.rsqrt(jnp.mean(x * x, axis=-1, keepdims=True) + EPS)
    return y * g


def _log_sigmoid(z):
    return jnp.minimum(z, 0.0) - jnp.log(1.0 + jnp.exp(-jnp.abs(z)))


def _silu(g):
    return g / (1.0 + jnp.exp(-g))


def _stack_pair(q2):
    qf = q2.astype(F32)
    lane = lax.broadcasted_iota(jnp.int32, qf.shape, 1)
    even = jnp.where(lane < HEAD_DIM, qf, 0.0)
    odd = jnp.where(lane >= HEAD_DIM, qf, 0.0)
    return jnp.concatenate([even, odd], axis=0).astype(BF16)


def _merge_pair(o, tq):
    top, bot = o[:tq], o[tq:]
    lane = lax.broadcasted_iota(jnp.int32, top.shape, 1)
    return jnp.where(lane < HEAD_DIM, top, bot)


def _params(*sem):
    return pltpu.CompilerParams(dimension_semantics=sem, vmem_limit_bytes=VMEM_LIMIT)


def _const_spec(shape):
    nd = len(shape)
    return pl.BlockSpec(shape, lambda *_: (0,) * nd)


def _in_even_body(x_ref, gpre_ref, cos_ref, sin_ref, wqa_ref, wkv_ref, wkr_ref, wkrs_ref, wga_ref,
                  wqb_ref, wkb_ref, wvb_ref, wgb_ref, qn_ref, kvn_ref, wuqn_ref, wuqr_ref,
                  wuqrs_ref, wuk_ref,
                  ckv_ref, krope_ref, k2_ref, q2_ref, ga_ref, gb_ref, qb_ref, kb_ref, kb16_ref,
                  vb_ref, vb16_ref, *, rows_minor):
    h = _rms(x_ref[...], gpre_ref[...]).astype(BF16)
    cos = cos_ref[...]
    sin = sin_ref[...]
    ckv = _rms(_dot(h, wkv_ref[...]), kvn_ref[...])
    ckv_ref[...] = ckv
    kr = _dot(h, wkr_ref[...]) * cos + _dot(h, wkrs_ref[...]) * sin
    krope_ref[...] = kr.T[:A_ROPE] if rows_minor else kr[:, :A_ROPE]
    k2_ref[:, :LANES] = ckv.astype(BF16)
    k2_ref[:, LANES:] = kr.astype(BF16)
    cq = _rms(_dot(h, wqa_ref[...]), qn_ref[...]).astype(BF16)
    qn = _dot(cq, wuqn_ref[...]).astype(BF16)
    for p in range(N_PAIRS):
        ql = _dot(qn[:, p * LANES:(p + 1) * LANES], wuk_ref[p])
        q2_ref[:, (2 * p) * 256:(2 * p) * 256 + LANES] = ql[:, :LANES].astype(BF16)
        q2_ref[:, (2 * p + 1) * 256:(2 * p + 1) * 256 + LANES] = ql[:, LANES:].astype(BF16)
    qr = _dot(cq, wuqr_ref[...])
    qrs = _dot(cq, wuqrs_ref[...])
    for hd in range(A_HEADS):
        rot = qr[:, hd * LANES:(hd + 1) * LANES] * cos + qrs[:, hd * LANES:(hd + 1) * LANES] * sin
        q2_ref[:, hd * 256 + LANES:(hd + 1) * 256] = rot.astype(BF16)
    ga_ref[...] = _dot(h, wga_ref[...]).astype(BF16)
    gb_ref[...] = _dot(h, wgb_ref[...]).astype(BF16)
    qb = _dot(h, wqb_ref[...]) * QK_SCALE
    kb = _dot(h, wkb_ref[...])
    vb = _dot(h, wvb_ref[...])
    kb_ref[...] = kb.T if rows_minor else kb
    vb_ref[...] = vb.T if rows_minor else vb
    for p in range(N_PAIRS):
        sl = slice(p * LANES, (p + 1) * LANES)
        qb_ref[p] = qb[:, sl].astype(BF16)
        kb16_ref[p] = kb[:, sl].astype(BF16)
        vb16_ref[p] = vb[:, sl].astype(BF16)


def _state_specs(rows, seq_len):
    sds = jax.ShapeDtypeStruct
    if seq_len is None:
        return (lambda n: pl.BlockSpec((TM, n), lambda i: (i, 0))), (lambda n: sds((rows, n), F32))
    nt = seq_len // TM
    return ((lambda n: pl.BlockSpec((None, n, TM), lambda i: (i // nt, 0, i % nt))),
            (lambda n: sds((rows // seq_len, n, seq_len), F32)))


def _in_even(x, gpre, cos, sin, w, seq_len=None):
    rows = x.shape[0]
    n_tab = cos.shape[0] // TM
    row_spec = lambda n: pl.BlockSpec((TM, n), lambda i: (i, 0))
    pm_spec = pl.BlockSpec((N_PAIRS, TM, LANES), lambda i: (0, i, 0))
    tab_spec = pl.BlockSpec((TM, LANES), lambda i: (i % n_tab, 0))
    weights = [w['wqa'], w['wkv'], w['wkr'], w['wkrs'], w['wga'], w['wqb'], w['wkb'], w['wvb'], w['wgb'],
               w['qn'], w['kvn'], w['wuqn'], w['wuqr'], w['wuqrs'], w['wuk']]
    in_specs = ([row_spec(D_MODEL), _const_spec((1, D_MODEL)), tab_spec, tab_spec]
                + [_const_spec(a.shape) for a in weights])
    args = [x, gpre, cos, sin, *weights]
    st_spec, st_shape = _state_specs(rows, seq_len)
    sds = jax.ShapeDtypeStruct
    pm = sds((N_PAIRS, rows, LANES), BF16)
    return pl.pallas_call(
        functools.partial(_in_even_body, rows_minor=seq_len is not None),
        grid=(rows // TM,),
        in_specs=in_specs,
        out_specs=[row_spec(A_KV_LORA), st_spec(A_ROPE), row_spec(256), row_spec(A_HEADS * 256),
                   row_spec(WIDTH), row_spec(WIDTH), pm_spec, st_spec(WIDTH), pm_spec,
                   st_spec(WIDTH), pm_spec],
        out_shape=[sds((rows, A_KV_LORA), F32), st_shape(A_ROPE), sds((rows, 256), BF16),
                   sds((rows, A_HEADS * 256), BF16), sds((rows, WIDTH), BF16), sds((rows, WIDTH), BF16),
                   pm, st_shape(WIDTH), pm, st_shape(WIDTH), pm],
        compiler_params=_params("parallel"),
        name="in_proj_even",
    )(*args)


def _in_odd_body(*refs, rows_minor):
    (x_ref, gpre_ref, fb_ref, wqc_ref, wkc_ref, wvc_ref, wgc_ref, wqd_ref, wkd_ref, wvd_ref, wf_ref,
     wgd_ref) = refs[:12]
    n_in = 14 if rows_minor else 12
    (qc_ref, kc_ref, kc16_ref, vc_ref, vc16_ref, gc_ref, qd_ref, kd_ref, kd16_ref, vd_ref, vd16_ref,
     logf_ref, gd_ref) = refs[n_in:]
    h = _rms(x_ref[...], gpre_ref[...]).astype(BF16)
    gc_ref[...] = _dot(h, wgc_ref[...]).astype(BF16)
    gd_ref[...] = _dot(h, wgd_ref[...]).astype(BF16)
    if rows_minor:
        wf_t, fb_t = refs[12:14]
        logf_ref[...] = _log_sigmoid(_dot_nt(wf_t[...], h) + fb_t[...])
    else:
        logf_ref[...] = _log_sigmoid(_dot(h, wf_ref[...])[:, :HEADS] + fb_ref[...])
    for q_w, k_w, v_w, q_o, k_o, k16_o, v_o, v16_o in (
            (wqc_ref, wkc_ref, wvc_ref, qc_ref, kc_ref, kc16_ref, vc_ref, vc16_ref),
            (wqd_ref, wkd_ref, wvd_ref, qd_ref, kd_ref, kd16_ref, vd_ref, vd16_ref)):
        q = _dot(h, q_w[...]) * (QK_SCALE * LOG2E)
        k = _dot(h, k_w[...])
        v = _dot(h, v_w[...])
        k_o[...] = k.T if rows_minor else k
        v_o[...] = v.T if rows_minor else v
        for p in range(N_PAIRS):
            sl = slice(p * LANES, (p + 1) * LANES)
            q_o[p] = q[:, sl].astype(BF16)
            k16_o[p] = k[:, sl].astype(BF16)
            v16_o[p] = v[:, sl].astype(BF16)


def _in_odd(x, gpre, w, seq_len=None):
    rows = x.shape[0]
    row_spec = lambda n: pl.BlockSpec((TM, n), lambda i: (i, 0))
    pm_spec = pl.BlockSpec((N_PAIRS, TM, LANES), lambda i: (0, i, 0))
    weights = [w['wqc'], w['wkc'], w['wvc'], w['wgc'], w['wqd'], w['wkd'], w['wvd'], w['wf'], w['wgd']]
    rows_minor = seq_len is not None
    if rows_minor:
        weights += [w['wf_t'], w['fb_t']]
    st_spec, st_shape = _state_specs(rows, seq_len)
    sds = jax.ShapeDtypeStruct
    pm = sds((N_PAIRS, rows, LANES), BF16)
    full = sds((rows, WIDTH), BF16)
    return pl.pallas_call(
        functools.partial(_in_odd_body, rows_minor=rows_minor),
        grid=(rows // TM,),
        in_specs=[row_spec(D_MODEL), _const_spec((1, D_MODEL)), _const_spec((1, HEADS))]
                 + [_const_spec(a.shape) for a in weights],
        out_specs=[pm_spec, st_spec(WIDTH), pm_spec, st_spec(WIDTH), pm_spec, row_spec(WIDTH),
                   pm_spec, st_spec(WIDTH), pm_spec, st_spec(WIDTH), pm_spec, st_spec(HEADS),
                   row_spec(WIDTH)],
        out_shape=[pm, st_shape(WIDTH), pm, st_shape(WIDTH), pm, full, pm, st_shape(WIDTH), pm,
                   st_shape(WIDTH), pm, st_shape(HEADS), full],
        compiler_params=_params("parallel"),
        name="in_proj_odd",
    )(x, gpre, w['fb'], *weights)


def _out_body(*refs, mla):
    if mla:
        x_ref, gpost_ref, g1_ref, g2_ref, a_ref, b_ref, wuv_ref, wout_ref, o_ref, mix_ref = refs
    else:
        x_ref, gpost_ref, g1_ref, g2_ref, a_ref, b_ref, wout_ref, o_ref, mix_ref = refs
    s1 = _silu(g1_ref[...].astype(F32))
    s2 = _silu(g2_ref[...].astype(F32))
    for p in range(N_PAIRS):
        sl = slice(p * LANES, (p + 1) * LANES)
        if mla:
            a = _dot(a_ref[:, p * 256:(p + 1) * 256], wuv_ref[p])
        else:
            a = a_ref[p].astype(F32)
        mix_ref[:, sl] = (s1[:, sl] * a).astype(BF16)
        mix_ref[:, WIDTH + p * LANES:WIDTH + (p + 1) * LANES] = (
            s2[:, sl] * b_ref[p].astype(F32)).astype(BF16)
    y = _dot(mix_ref[...], wout_ref[...])
    o_ref[...] = x_ref[...] + _rms(y, gpost_ref[...])


def _out_proj(x, gpost, g1, g2, a, b, wout, wuv=None):
    rows = x.shape[0]
    mla = wuv is not None
    row_spec = lambda n: pl.BlockSpec((TM, n), lambda i: (i, 0))
    pm_spec = pl.BlockSpec((N_PAIRS, TM, LANES), lambda i: (0, i, 0))
    in_specs = [row_spec(D_MODEL), _const_spec((1, D_MODEL)), row_spec(WIDTH), row_spec(WIDTH),
                row_spec(A_HEADS * A_KV_LORA) if mla else pm_spec, pm_spec]
    args = [x, gpost, g1, g2, a, b]
    if mla:
        in_specs.append(_const_spec(wuv.shape))
        args.append(wuv)
    in_specs.append(_const_spec(wout.shape))
    args.append(wout)
    return pl.pallas_call(
        functools.partial(_out_body, mla=mla),
        grid=(rows // TM,),
        in_specs=in_specs,
        out_specs=row_spec(D_MODEL),
        out_shape=jax.ShapeDtypeStruct((rows, D_MODEL), F32),
        scratch_shapes=[pltpu.VMEM((TM, 2 * WIDTH), BF16)],
        compiler_params=_params("parallel"),
        name="out_proj_even" if mla else "out_proj_odd",
    )(*args)


def _softmax_init(m_ref, accl_ref):
    m_ref[...] = jnp.full(m_ref.shape, NEG, F32)
    accl_ref[...] = jnp.zeros(accl_ref.shape, F32)


def _lanes(x, n):
    parts = [x] * (n // LANES)
    if n % LANES:
        parts.append(x[:, :n % LANES])
    return parts[0] if len(parts) == 1 else jnp.concatenate(parts, axis=1)


def _scores(q, k, kv_t):
    return _dot(q, k) if kv_t else _dot_nt(q, k)


def _weighted(p, v, kv_t):
    return _dot_nt(p, v) if kv_t else _dot(p, v)


def _with_ones(v, kv_t=False):
    return jnp.concatenate([v, jnp.ones(v.shape, BF16)], axis=0 if kv_t else 1)


def _softmax_update(s, v1, m_ref, accl_ref, kv_t=False):
    keys = s.shape[1]
    n = v1.shape[0] if kv_t else v1.shape[1]
    m_prev = m_ref[...]
    m_new = jnp.maximum(m_prev, jnp.max(s, axis=1, keepdims=True))
    alpha = jnp.exp2(m_prev - m_new)
    p = jnp.exp2(s - _lanes(m_new, keys))
    accl_ref[...] = _lanes(alpha, n) * accl_ref[...] + _weighted(p.astype(BF16), v1, kv_t)
    m_ref[...] = m_new


def _softmax_result(accl_ref):
    accl = accl_ref[...]
    n = accl.shape[1] // 2
    return accl[:, :n] / accl[:, n:]


def _fill_suffix_ones(tri_ref):
    n = tri_ref.shape[0]
    r = lax.broadcasted_iota(jnp.int32, (n, n), 0)
    c = lax.broadcasted_iota(jnp.int32, (n, n), 1)
    tri_ref[...] = jnp.where(r > c, 1.0, 0.0).astype(BF16)


def _stick_tile(qs, k2, v2, tri, acc_ref, car_ref, mask, kv_t=False):
    z = _scores(qs, k2, kv_t)
    lb = _log_sigmoid(z)
    l1 = lb - z
    if mask is not None:
        l1 = jnp.where(mask, l1, 0.0)
    hi = l1.astype(BF16)
    lo = (l1 - hi.astype(F32)).astype(BF16)
    suf = _dot(hi, tri) + _dot(lo, tri) + car_ref[...]
    w = jnp.exp(lb + suf)
    if mask is not None:
        w = jnp.where(mask, w, 0.0)
    acc_ref[...] += _weighted(w.astype(BF16), v2, kv_t)
    car_ref[...] += jnp.sum(l1, axis=1, keepdims=True)


def _stick_alive(car_ref):
    return (jnp.max(car_ref[...]) >= STICK_DEAD).astype(jnp.int32)


def _local_causal(rows, keys, tq, strict, key_offset=0):
    r = lax.broadcasted_iota(jnp.int32, (rows, keys), 0) & (tq - 1)
    c = lax.broadcasted_iota(jnp.int32, (rows, keys), 1) + key_offset
    return (c < r) if strict else (c <= r)


def _decay_tile(qs, k2, v2, cq, ck, causal, m_ref, accl_ref, kv_t=False):
    tq, keys = cq.shape[0], ck.shape[1]
    s = _scores(qs, k2, kv_t)
    for hh in range(2):
        rows = slice(hh * tq, (hh + 1) * tq)
        sh = s[rows] + (cq[:, hh:hh + 1] - ck[hh:hh + 1])
        if causal:
            sh = jnp.where(_local_causal(tq, keys, tq, strict=False), sh, NEG)
        _softmax_update(sh, _with_ones(v2, kv_t), m_ref.at[rows], accl_ref.at[rows], kv_t)


def _split3(c):
    hi = c.astype(BF16).astype(F32)
    rest = c - hi
    mid = rest.astype(BF16).astype(F32)
    return hi, mid, rest - mid


def _decay_lanes(c, base, query):
    hi, mid, lo = _split3(c if query else -c)
    lane = lax.broadcasted_iota(jnp.int32, (c.shape[0], LANES), 1) - (base if query else base + 3)
    parts = jnp.where(lane == 0, hi, jnp.where(lane == 1, mid, jnp.where(lane == 2, lo, 0.0)))
    ones_at = lane + 3 if not query else lane - 3
    return jnp.where((ones_at >= 0) & (ones_at < 3), 1.0, parts)


def _decay_operand(x, cum, hh, query):
    lane = lax.broadcasted_iota(jnp.int32, x.shape, 1)
    own = (lane < HEAD_DIM) if hh == 0 else (lane >= HEAD_DIM)
    base = HEAD_DIM if hh == 0 else 0
    return jnp.where(own, x, _decay_lanes(cum[:, hh:hh + 1], base, query)).astype(BF16)


def _decay_tile_heads(q1, k1, v1, causal_offset, m_ref, accl_ref):
    tq, keys = q1[0].shape[0], k1.shape[1]
    for hh in range(2):
        rows = slice(hh * tq, (hh + 1) * tq)
        s = _dot_nt(q1[hh], k1[hh])
        if causal_offset is not None:
            s = jnp.where(_local_causal(tq, keys, tq, strict=False, key_offset=causal_offset), s, NEG)
        _softmax_update(s, v1[hh], m_ref.at[rows], accl_ref.at[rows])


def _pair_result(accl, tq):
    o = accl / pltpu.roll(accl, HEAD_DIM, axis=1)
    return _merge_pair(o, tq)


def _mla_prompt_body(q_ref, k_ref, o_ref, m_ref, accl_ref):
    g = pl.program_id(1)
    grp = 4 * CHUNK

    def update(h, r0, k, mask):
        sl = pl.ds(r0, grp)
        s = _dot_nt(q_ref[sl, h * 256:(h + 1) * 256], k) * MLA_LOGIT_SCALE
        if mask is not None:
            s = jnp.where(mask, s, NEG)
        _softmax_update(s, _with_ones(k[:, :A_KV_LORA]), m_ref.at[h, sl], accl_ref.at[h, sl])

    def group(gl, carry):
        gq = g * (MLA_QB // grp) + gl
        g0 = pl.multiple_of(gl * grp, grp)
        for h in range(A_HEADS):
            _softmax_init(m_ref.at[h, pl.ds(g0, grp)], accl_ref.at[h, pl.ds(g0, grp)])

        def earlier(k0, n_keys):
            k = k_ref[pl.ds(k0, n_keys), :]
            for h in range(A_HEADS):
                update(h, g0, k, None)

        def kv(j, c2):
            earlier(pl.multiple_of(j * 2 * TK, 2 * TK), 2 * TK)
            return c2

        lax.fori_loop(0, gq // 2, kv, 0)

        @pl.when(gq % 2 == 1)
        def _():
            earlier(pl.multiple_of((gq - 1) * TK, TK), TK)

        kd = k_ref[pl.ds(pl.multiple_of(gq * TK, TK), TK), :]
        row_chunk = lax.broadcasted_iota(jnp.int32, (grp, TK), 0) // CHUNK
        key_chunk = lax.broadcasted_iota(jnp.int32, (grp, TK), 1) // CHUNK
        own = key_chunk <= row_chunk
        for h in range(A_HEADS):
            update(h, g0, kd, own)
        for h in range(A_HEADS):
            o_ref[pl.ds(g0, grp), h * A_KV_LORA:(h + 1) * A_KV_LORA] = _softmax_result(
                accl_ref.at[h, pl.ds(g0, grp)]).astype(BF16)
        return carry

    lax.fori_loop(0, MLA_QB // grp, group, 0)


def _mla_prompt(q2, k2, n_b, n_s):
    assert n_s % MLA_QB == 0 and TK == 4 * CHUNK
    return pl.pallas_call(
        _mla_prompt_body,
        grid=(n_b, n_s // MLA_QB),
        in_specs=[pl.BlockSpec((None, MLA_QB, A_HEADS * 256), lambda b, g: (b, g, 0)),
                  pl.BlockSpec((None, n_s, 256), lambda b, g: (b, 0, 0))],
        out_specs=pl.BlockSpec((None, MLA_QB, A_HEADS * A_KV_LORA), lambda b, g: (b, g, 0)),
        out_shape=jax.ShapeDtypeStruct((n_b, n_s, A_HEADS * A_KV_LORA), BF16),
        scratch_shapes=[pltpu.VMEM((A_HEADS, MLA_QB, LANES), F32),
                        pltpu.VMEM((A_HEADS, MLA_QB, 2 * LANES), F32)],
        compiler_params=_params("parallel", "parallel"),
        name="mla_prompt",
    )(q2.reshape(n_b, n_s, A_HEADS * 256), k2.reshape(n_b, n_s, 256))


def _mla_sample_body(q_ref, kn_ref, ckv_ref, kr_ref, o_ref, m_ref, accl_ref):
    j = pl.program_id(1)

    @pl.when(j == 0)
    def _():
        _softmax_init(m_ref, accl_ref)

    n_tiles = q_ref.shape[0] // MLA_RT
    ck = ckv_ref[...].astype(BF16)
    kr = kr_ref[...].T.astype(BF16)
    k = jnp.concatenate([ck, kr, jnp.zeros((TKC, LANES - A_ROPE), BF16)], axis=1)
    for t in range(n_tiles):
        sl = slice(t * MLA_RT, (t + 1) * MLA_RT)
        s = _dot_nt(q_ref[sl, :], k) * MLA_LOGIT_SCALE
        _softmax_update(s, _with_ones(ck), m_ref.at[sl], accl_ref.at[sl])

    @pl.when(j == pl.num_programs(1) - 1)
    def _():
        kn = kn_ref[...]
        for t in range(n_tiles):
            sl = slice(t * MLA_RT, (t + 1) * MLA_RT)
            s = _dot_nt(q_ref[sl, :], kn) * MLA_LOGIT_SCALE
            _softmax_update(s, _with_ones(kn[:, :A_KV_LORA]), m_ref.at[sl], accl_ref.at[sl])
        o_ref[...] = _softmax_result(accl_ref).astype(BF16)


def _mla_sample(q2, k2, cache_ckv, cache_kr, n_b, n_s):
    rows = n_s * A_HEADS
    n_past = cache_ckv.shape[1]
    return pl.pallas_call(
        _mla_sample_body,
        grid=(n_b, n_past // TKC),
        in_specs=[pl.BlockSpec((None, rows, 256), lambda b, j: (b, 0, 0)),
                  pl.BlockSpec((None, n_s, 256), lambda b, j: (b, 0, 0)),
                  pl.BlockSpec((None, TKC, A_KV_LORA), lambda b, j: (b, j, 0)),
                  pl.BlockSpec((None, A_ROPE, TKC), lambda b, j: (b, 0, j))],
        out_specs=pl.BlockSpec((None, rows, A_KV_LORA), lambda b, j: (b, 0, 0)),
        out_shape=jax.ShapeDtypeStruct((n_b, rows, A_KV_LORA), BF16),
        scratch_shapes=[pltpu.VMEM((rows, LANES), F32), pltpu.VMEM((rows, 2 * LANES), F32)],
        compiler_params=_params("parallel", "arbitrary"),
        name="mla_sample",
    )(q2.reshape(n_b, rows, 256), k2.reshape(n_b, n_s, 256), cache_ckv, cache_kr)


def _sb_prompt_body(q_ref, k_ref, v_ref, o_ref, tri_ref, acc_ref, car_ref):
    n_s = q_ref.shape[1]
    _fill_suffix_ones(tri_ref)

    def qblock(i, carry):
        q0 = pl.multiple_of(i * SB_T, SB_T)
        qs = [_stack_pair(q_ref[g, pl.ds(q0, SB_T), :]) for g in range(PAIRS_PER_STEP)]
        acc_ref[...] = jnp.zeros(acc_ref.shape, F32)
        car_ref[...] = jnp.zeros(car_ref.shape, F32)

        def tiles(k0, n_keys, mask):
            for g in range(PAIRS_PER_STEP):
                _stick_tile(qs[g], k_ref[g, pl.ds(k0, n_keys), :], v_ref[g, pl.ds(k0, n_keys), :],
                            tri_ref[:n_keys, :n_keys], acc_ref.at[g], car_ref.at[g], mask)

        tiles(q0, SB_T, _local_causal(2 * SB_T, SB_T, SB_T, strict=True))

        def kv(state):
            jj, _ = state
            tiles(pl.multiple_of(q0 - (jj + 1) * SB_TK, SB_TK), SB_TK, None)
            return jj + 1, _stick_alive(car_ref)

        lax.while_loop(lambda st: (st[0] < i * (SB_T // SB_TK)) & (st[1] > 0), kv,
                       (jnp.int32(0), _stick_alive(car_ref)))
        for g in range(PAIRS_PER_STEP):
            o_ref[g, pl.ds(q0, SB_T), :] = _merge_pair(acc_ref[g], SB_T).astype(o_ref.dtype)
        return carry

    lax.fori_loop(0, n_s // SB_T, qblock, 0)


def _pair_seq_spec(n_s):
    return pl.BlockSpec((PAIRS_PER_STEP, n_s, LANES), lambda b, g: (g, b, 0))


def _sb_prompt(q, k, v, n_b, n_s):
    assert n_s % SB_T == 0
    spec = _pair_seq_spec(n_s)
    return pl.pallas_call(
        _sb_prompt_body,
        grid=(n_b, N_PAIRS // PAIRS_PER_STEP),
        in_specs=[spec, spec, spec],
        out_specs=spec,
        out_shape=jax.ShapeDtypeStruct((N_PAIRS, n_b * n_s, LANES), BF16),
        scratch_shapes=[pltpu.VMEM((SB_T, SB_T), BF16), pltpu.VMEM((PAIRS_PER_STEP, 2 * SB_T, LANES), F32),
                        pltpu.VMEM((PAIRS_PER_STEP, 2 * SB_T, 1), F32)],
        compiler_params=_params("parallel", "parallel"),
        name="sb_prompt",
    )(q, k, v)


def _sb_sample_body(q_ref, kn_ref, vn_ref, ck_hbm, cv_hbm, o_ref, kbuf, vbuf, sem, tri_ref, acc_ref,
                    car_ref):
    b = pl.program_id(0)
    n_q = q_ref.shape[1]
    n_blk = ck_hbm.shape[2] // TK

    def block_copies(blk):
        rows = pl.ds(pl.multiple_of(blk * TK, TK), TK)
        return (pltpu.make_async_copy(ck_hbm.at[b, :, rows], kbuf, sem.at[0]),
                pltpu.make_async_copy(cv_hbm.at[b, :, rows], vbuf, sem.at[1]))

    def start(blk):
        for c in block_copies(blk):
            c.start()

    def cache_block(blk):
        for c in block_copies(blk):
            c.wait()
        for p in range(N_PAIRS):
            sl = slice(p * LANES, (p + 1) * LANES)
            _stick_tile(qs[p], kbuf[sl, :].astype(BF16), vbuf[sl, :].astype(BF16), tri_ref[...],
                        acc_ref.at[p], car_ref.at[p], None, kv_t=True)

    start(n_blk - 1)
    _fill_suffix_ones(tri_ref)
    acc_ref[...] = jnp.zeros(acc_ref.shape, F32)
    car_ref[...] = jnp.zeros(car_ref.shape, F32)
    qs = [_stack_pair(q_ref[p]) for p in range(N_PAIRS)]
    mask = _local_causal(2 * n_q, n_q, n_q, strict=True)
    for p in range(N_PAIRS):
        _stick_tile(qs[p], kn_ref[p], vn_ref[p], tri_ref[:n_q, :n_q], acc_ref.at[p], car_ref.at[p], mask)
    cache_block(n_blk - 1)

    def older(state):
        blk, _ = state
        start(blk)
        cache_block(blk)
        return blk - 1, _stick_alive(car_ref)

    lax.while_loop(lambda st: (st[0] >= 0) & (st[1] > 0), older,
                   (jnp.int32(n_blk - 2), _stick_alive(car_ref)))
    for p in range(N_PAIRS):
        o_ref[p] = _merge_pair(acc_ref[p], n_q).astype(o_ref.dtype)


def _sb_sample(q, kn, vn, cache_kt, cache_vt, n_b, n_s):
    new_spec = pl.BlockSpec((N_PAIRS, n_s, LANES), lambda b: (0, b, 0))
    hbm = pl.BlockSpec(memory_space=pl.ANY)
    return pl.pallas_call(
        _sb_sample_body,
        grid=(n_b,),
        in_specs=[new_spec, new_spec, new_spec, hbm, hbm],
        out_specs=new_spec,
        out_shape=jax.ShapeDtypeStruct((N_PAIRS, n_b * n_s, LANES), BF16),
        scratch_shapes=[pltpu.VMEM((WIDTH, TK), F32), pltpu.VMEM((WIDTH, TK), F32),
                        pltpu.SemaphoreType.DMA((2,)), pltpu.VMEM((TK, TK), BF16),
                        pltpu.VMEM((N_PAIRS, 2 * n_s, LANES), F32), pltpu.VMEM((N_PAIRS, 2 * n_s, 1), F32)],
        compiler_params=_params("arbitrary"),
        name="sb_sample",
    )(q, kn, vn, cache_kt, cache_vt)


def _band_block(qs, kwin, vwin, bias, valid_from, tq):
    s = _dot_nt(qs, kwin) + bias
    if valid_from is not None:
        col = lax.broadcasted_iota(jnp.int32, s.shape, 1)
        s = jnp.where(col >= valid_from, s, NEG)
    p = jnp.exp2(s - jnp.max(s, axis=1, keepdims=True))
    o = _dot(p.astype(BF16), vwin) / jnp.sum(p, axis=1, keepdims=True)
    return _merge_pair(o, tq)


def _band_prompt_body(q_ref, k_ref, v_ref, bias_ref, o_ref, kpad_ref, vpad_ref):
    n_s = q_ref.shape[1]
    win = LEFT_CTX + BAND_TQ
    zeros = jnp.zeros((PAIRS_PER_STEP, LEFT_CTX, LANES), BF16)
    kpad_ref[:, :LEFT_CTX, :] = zeros
    vpad_ref[:, :LEFT_CTX, :] = zeros
    kpad_ref[:, LEFT_CTX:, :] = k_ref[...]
    vpad_ref[:, LEFT_CTX:, :] = v_ref[...]

    def qblock(i, carry):
        q0 = pl.multiple_of(i * BAND_TQ, BAND_TQ)
        for g in range(PAIRS_PER_STEP):
            qs = _stack_pair(q_ref[g, pl.ds(q0, BAND_TQ), :])
            o_ref[g, pl.ds(q0, BAND_TQ), :] = _band_block(
                qs, kpad_ref[g, pl.ds(q0, win), :], vpad_ref[g, pl.ds(q0, win), :], bias_ref[g],
                LEFT_CTX - q0, BAND_TQ).astype(o_ref.dtype)
        return carry

    lax.fori_loop(0, n_s // BAND_TQ, qblock, 0)


def _band_prompt(q, k, v, bias, n_b, n_s):
    spec = _pair_seq_spec(n_s)
    win = LEFT_CTX + BAND_TQ
    return pl.pallas_call(
        _band_prompt_body,
        grid=(n_b, N_PAIRS // PAIRS_PER_STEP),
        in_specs=[spec, spec, spec,
                  pl.BlockSpec((PAIRS_PER_STEP, 2 * BAND_TQ, win), lambda b, g: (g, 0, 0))],
        out_specs=spec,
        out_shape=jax.ShapeDtypeStruct((N_PAIRS, n_b * n_s, LANES), BF16),
        scratch_shapes=[pltpu.VMEM((PAIRS_PER_STEP, LEFT_CTX + n_s, LANES), BF16),
                        pltpu.VMEM((PAIRS_PER_STEP, LEFT_CTX + n_s, LANES), BF16)],
        compiler_params=_params("parallel", "parallel"),
        name="band_prompt",
    )(q, k, v, bias)


def _band_sample_body(q_ref, kn_ref, vn_ref, kc_ref, vc_ref, bias_ref, o_ref):
    n_q = q_ref.shape[1]
    n_keep = kc_ref.shape[1]
    for p in range(N_PAIRS):
        sl = slice(p * LANES, (p + 1) * LANES)
        qs = _stack_pair(q_ref[p])
        bias = bias_ref[p]
        s_old = _dot(qs, kc_ref[sl, :].astype(BF16)) + bias[:, :n_keep]
        s_new = _dot_nt(qs, kn_ref[p]) + bias[:, n_keep:]
        m = jnp.maximum(jnp.max(s_old, axis=1, keepdims=True), jnp.max(s_new, axis=1, keepdims=True))
        p_old = jnp.exp2(s_old - m)
        p_new = jnp.exp2(s_new - m)
        o = _dot_nt(p_old.astype(BF16), vc_ref[sl, :].astype(BF16)) + _dot(p_new.astype(BF16), vn_ref[p])
        total = jnp.sum(p_old, axis=1, keepdims=True) + jnp.sum(p_new, axis=1, keepdims=True)
        o_ref[p] = _merge_pair(o / total, n_q).astype(o_ref.dtype)


def _band_sample(q, kn, vn, cache_kt, cache_vt, bias, n_b, n_s):
    n_keep = cache_kt.shape[2]
    new_spec = pl.BlockSpec((N_PAIRS, n_s, LANES), lambda b: (0, b, 0))
    cache_spec = pl.BlockSpec((None, WIDTH, n_keep), lambda b: (b, 0, 0))
    return pl.pallas_call(
        _band_sample_body,
        grid=(n_b,),
        in_specs=[new_spec, new_spec, new_spec, cache_spec, cache_spec,
                  _const_spec((N_PAIRS, 2 * n_s, n_keep + n_s))],
        out_specs=new_spec,
        out_shape=jax.ShapeDtypeStruct((N_PAIRS, n_b * n_s, LANES), BF16),
        compiler_params=_params("parallel"),
        name="band_sample",
    )(q, kn, vn, cache_kt, cache_vt, bias)


def _band_bias(rel_bias, tq):
    win = LEFT_CTX + tq
    i = np.arange(tq)[:, None]
    w = np.arange(win)[None, :]
    qc, kc = i // CHUNK, w // CHUNK - LEFT_CTX // CHUNK
    ok = (kc <= qc) & (kc >= qc - LEFT_CTX // CHUNK)
    u = np.arange(win + tq - 1)
    rel = np.clip(LEFT_CTX + (tq - 1) - u, -REL_CLIP, REL_CLIP) + REL_CLIP
    diag = rel_bias.astype(F32)[:, rel]
    n = win + tq - 1
    flat = jnp.tile(diag, (1, tq))[:, tq - 1:tq - 1 + tq * (n - 1)]
    tab = flat.reshape(HEADS, tq, n - 1)[:, :, :win]
    tab = jnp.where(jnp.asarray(ok)[None], tab * LOG2E, NEG)
    return tab.reshape(N_PAIRS, 2 * tq, win)


def _fox_prompt_body(q_ref, k_ref, v_ref, cum_ref, o_ref, k1_ref, v1_ref, m_ref, accl_ref):
    n_s = q_ref.shape[1]

    def prepare(jb, carry):
        rows = pl.ds(pl.multiple_of(jb * TK, TK), TK)
        lane = lax.broadcasted_iota(jnp.int32, (TK, LANES), 1)
        for g in range(PAIRS_PER_STEP):
            k = k_ref[g, rows, :].astype(F32)
            v = v_ref[g, rows, :].astype(F32)
            cum = cum_ref[g, rows, :]
            for hh in range(2):
                k1_ref[g, hh, rows, :] = _decay_operand(k, cum, hh, query=False)
                own = (lane < HEAD_DIM) if hh == 0 else (lane >= HEAD_DIM)
                v1_ref[g, hh, rows, :] = jnp.where(own, v, 1.0).astype(BF16)
        return carry

    lax.fori_loop(0, n_s // TK, prepare, 0)

    tiles_per_q = FOX_TQ // TK

    def qblock(i, carry):
        q0 = pl.multiple_of(i * FOX_TQ, FOX_TQ)
        q1 = []
        for g in range(PAIRS_PER_STEP):
            q = q_ref[g, pl.ds(q0, FOX_TQ), :].astype(F32)
            cum = cum_ref[g, pl.ds(q0, FOX_TQ), :]
            q1.append([_decay_operand(q, cum, hh, query=True) for hh in range(2)])
        _softmax_init(m_ref, accl_ref)

        def tile(jb, causal_offset):
            rows = pl.ds(pl.multiple_of(jb * TK, TK), TK)
            for g in range(PAIRS_PER_STEP):
                _decay_tile_heads(q1[g], k1_ref[g, :, rows, :], v1_ref[g, :, rows, :], causal_offset,
                                  m_ref.at[g], accl_ref.at[g])

        for t in range(tiles_per_q):
            tile(i * tiles_per_q + t, t * TK)

        def kv(jb, c2):
            tile(jb, None)
            return c2

        lax.fori_loop(0, i * tiles_per_q, kv, 0)
        for g in range(PAIRS_PER_STEP):
            o_ref[g, pl.ds(q0, FOX_TQ), :] = _pair_result(accl_ref[g], FOX_TQ).astype(o_ref.dtype)
        return carry

    lax.fori_loop(0, n_s // FOX_TQ, qblock, 0)


def _fox_prompt(q, k, v, cum, n_b, n_s):
    assert FOX_TQ % TK == 0 and n_s % FOX_TQ == 0
    spec = _pair_seq_spec(n_s)
    cum_rows = cum.reshape(n_b, N_PAIRS, 2, n_s).transpose(0, 1, 3, 2)
    g_ = PAIRS_PER_STEP
    return pl.pallas_call(
        _fox_prompt_body,
        grid=(n_b, N_PAIRS // g_),
        in_specs=[spec, spec, spec, pl.BlockSpec((None, g_, n_s, 2), lambda b, g: (b, g, 0, 0))],
        out_specs=spec,
        out_shape=jax.ShapeDtypeStruct((N_PAIRS, n_b * n_s, LANES), BF16),
        scratch_shapes=[pltpu.VMEM((g_, 2, n_s, LANES), BF16), pltpu.VMEM((g_, 2, n_s, LANES), BF16),
                        pltpu.VMEM((g_, 2 * FOX_TQ, LANES), F32), pltpu.VMEM((g_, 2 * FOX_TQ, LANES), F32)],
        compiler_params=_params("parallel", "parallel"),
        name="fox_prompt",
    )(q, k, v, cum_rows)


def _fox_sample_body(q_ref, kn_ref, vn_ref, kc_ref, vc_ref, cq_ref, ckn_ref, ckc_ref, o_ref,
                     m_ref, accl_ref):
    j = pl.program_id(1)
    n_q = q_ref.shape[1]
    cq = cq_ref[...]

    def pair_update(p, k2, v2, ck, new_rows):
        rows = slice(2 * p * n_q, (2 * p + 2) * n_q)
        _decay_tile(_stack_pair(q_ref[p]), k2, v2, cq[:, 2 * p:2 * p + 2], ck[2 * p:2 * p + 2], new_rows,
                    m_ref.at[rows], accl_ref.at[rows], kv_t=not new_rows)

    @pl.when(j == 0)
    def _():
        _softmax_init(m_ref, accl_ref)
        ckn = ckn_ref[...]
        for p in range(N_PAIRS):
            pair_update(p, kn_ref[p], vn_ref[p], ckn, True)

    @pl.when(j > 0)
    def _():
        ckc = ckc_ref[...]
        for p in range(N_PAIRS):
            sl = slice(p * LANES, (p + 1) * LANES)
            for sub in range(TKC // FOX_TK):
                keys = slice(sub * FOX_TK, (sub + 1) * FOX_TK)
                pair_update(p, kc_ref[sl, keys].astype(BF16), vc_ref[sl, keys].astype(BF16), ckc[:, keys],
                            False)

    @pl.when(j == pl.num_programs(1) - 1)
    def _():
        for p in range(N_PAIRS):
            rows = slice(2 * p * n_q, (2 * p + 2) * n_q)
            o_ref[p] = _merge_pair(_softmax_result(accl_ref.at[rows]), n_q).astype(o_ref.dtype)


def _fox_sample(q, kn, vn, cache_k, cache_v, cum, n_b, n_s):
    n_past = cache_k.shape[2]
    n_blk = n_past // TKC
    cq = cum[..., n_past:].transpose(0, 2, 1)
    ckn = cum[..., n_past:]
    ckc = cum[..., :n_past]
    new_spec = pl.BlockSpec((N_PAIRS, n_s, LANES), lambda b, j: (0, b, 0))
    cache_blk = lambda j: jnp.maximum(j - 1, 0)
    cache_spec = pl.BlockSpec((None, WIDTH, TKC), lambda b, j: (b, 0, cache_blk(j)))
    return pl.pallas_call(
        _fox_sample_body,
        grid=(n_b, n_blk + 1),
        in_specs=[new_spec, new_spec, new_spec, cache_spec, cache_spec,
                  pl.BlockSpec((None, n_s, HEADS), lambda b, j: (b, 0, 0)),
                  pl.BlockSpec((None, HEADS, n_s), lambda b, j: (b, 0, 0)),
                  pl.BlockSpec((None, HEADS, TKC), lambda b, j: (b, 0, cache_blk(j)))],
        out_specs=new_spec,
        out_shape=jax.ShapeDtypeStruct((N_PAIRS, n_b * n_s, LANES), BF16),
        scratch_shapes=[pltpu.VMEM((HEADS * n_s, LANES), F32), pltpu.VMEM((HEADS * n_s, 2 * LANES), F32)],
        compiler_params=_params("parallel", "arbitrary"),
        name="fox_sample",
    )(q, kn, vn, cache_k, cache_v, cq, ckn, ckc)


def _cumsum_body(x_ref, o_ref):
    n_rows, n_cols = x_ref.shape
    r = lax.broadcasted_iota(jnp.int32, (LANES, LANES), 0)
    c = lax.broadcasted_iota(jnp.int32, (LANES, LANES), 1)
    ones = jnp.where(r <= c, 1.0, 0.0).astype(BF16)
    total = jnp.zeros((n_rows, 1), F32)
    for g in range(n_cols // LANES):
        x = x_ref[:, g * LANES:(g + 1) * LANES]
        h1 = x.astype(BF16)
        r1 = x - h1.astype(F32)
        h2 = r1.astype(BF16)
        h3 = (r1 - h2.astype(F32)).astype(BF16)
        y = _dot(h1, ones) + _dot(h2, ones) + _dot(h3, ones) + total
        o_ref[:, g * LANES:(g + 1) * LANES] = y * LOG2E
        total = y[:, LANES - 1:LANES]


def _cumsum_rows(x):
    rows, n = x.shape
    n_pad = -(-n // LANES) * LANES
    xp = jnp.pad(x, ((0, 0), (0, n_pad - n)))
    out = pl.pallas_call(
        _cumsum_body,
        out_shape=jax.ShapeDtypeStruct((rows, n_pad), F32),
        compiler_params=pltpu.CompilerParams(vmem_limit_bytes=VMEM_LIMIT),
        name="cumsum_rows",
    )(xp)
    return out[:, :n]


def _split_cols(w, sizes):
    out, off = [], 0
    for n in sizes:
        out.append(w[:, off:off + n])
        off += n
    return out


def _rope_tables(pos, n_rows):
    half = A_ROPE // 2
    inv_freq = ROPE_THETA ** (-jnp.arange(half, dtype=F32) / half)
    ang = pos.astype(F32)[:, None] * inv_freq[None, :]
    cos, sin = jnp.cos(ang), jnp.sin(ang)
    zeros = jnp.zeros((pos.shape[0], LANES - A_ROPE), F32)
    cos_t = jnp.concatenate([cos, cos, zeros], axis=1)
    sin_t = jnp.concatenate([-sin, sin, zeros], axis=1)
    reps = max(1, n_rows // pos.shape[0])
    return jnp.tile(cos_t, (reps, 1)), jnp.tile(sin_t, (reps, 1))


def _swap_halves(w):
    half = w.shape[-1] // 2
    return jnp.concatenate([w[..., half:], w[..., :half]], axis=-1)


def _pad_lanes(w):
    return jnp.pad(w, [(0, 0)] * (w.ndim - 1) + [(0, LANES - w.shape[-1])])


def _prep_even(w_in, q_norm, w_uq, kv_norm, w_uk, w_uv):
    wqa, wkv, wkr, wga, wqb, wkb, wvb, wgb = _split_cols(
        w_in, (A_Q_LORA, A_KV_LORA, A_ROPE, WIDTH, WIDTH, WIDTH, WIDTH, WIDTH))
    b = lambda a: a.astype(BF16)
    uq_rope = w_uq[:, :, A_NOPE:]
    uk_t = jnp.transpose(w_uk, (1, 2, 0))
    z = jnp.zeros((A_NOPE, A_KV_LORA), w_uk.dtype)
    wuk = jnp.stack([jnp.block([[uk_t[2 * p], z], [z, uk_t[2 * p + 1]]]) for p in range(N_PAIRS)])
    uv_t = jnp.transpose(w_uv, (1, 0, 2))
    zv = jnp.zeros((A_KV_LORA, A_V), w_uv.dtype)
    wuv = jnp.stack([jnp.block([[uv_t[2 * p], zv], [zv, uv_t[2 * p + 1]]]) for p in range(N_PAIRS)])
    return dict(
        wqa=b(wqa), wkv=b(wkv), wkr=b(_pad_lanes(wkr)), wkrs=b(_pad_lanes(_swap_halves(wkr))),
        wga=b(wga), wqb=b(wqb), wkb=b(wkb), wvb=b(wvb), wgb=b(wgb),
        qn=q_norm.reshape(1, -1), kvn=kv_norm.reshape(1, -1),
        wuqn=b(w_uq[:, :, :A_NOPE].reshape(A_Q_LORA, A_HEADS * A_NOPE)),
        wuqr=b(_pad_lanes(uq_rope).reshape(A_Q_LORA, A_HEADS * LANES)),
        wuqrs=b(_pad_lanes(_swap_halves(uq_rope)).reshape(A_Q_LORA, A_HEADS * LANES)),
        wuk=b(wuk), wuv=b(wuv))


def _prep_odd(w_in, forget_bias):
    wqc, wkc, wvc, wgc, wqd, wkd, wvd, wf, wgd = _split_cols(
        w_in, (WIDTH, WIDTH, WIDTH, WIDTH, WIDTH, WIDTH, WIDTH, HEADS, WIDTH))
    b = lambda a: a.astype(BF16)
    return dict(wqc=b(wqc), wkc=b(wkc), wvc=b(wvc), wgc=b(wgc), wqd=b(wqd), wkd=b(wkd), wvd=b(wvd),
                wf=b(_pad_lanes(wf)), wgd=b(wgd), fb=forget_bias.astype(F32).reshape(1, HEADS),
                wf_t=b(wf.T), fb_t=forget_bias.astype(F32).reshape(HEADS, 1))


def kernel(x_prompt, x_sample, cache_mla_ckv, cache_mla_krope, cache_sb_k, cache_sb_v, cache_band_k,
           cache_band_v, cache_fox_k, cache_fox_v, cache_fox_logf, norm_pre, norm_post, w_in_even,
           a_q_norm, a_w_uq, a_kv_norm, a_w_uk, a_w_uv, w_out_even, w_in_odd, c_rel_bias,
           d_forget_bias, w_out_odd):
    n_b, n_s, _ = x_prompt.shape
    d_b, d_s, _ = x_sample.shape
    n_past = cache_sb_k.shape[2]
    n_keep = cache_band_k.shape[2]
    assert n_s % (2 * TQ) == 0 and n_past % TKC == 0 and d_s == CHUNK and n_past % CHUNK == 0
    assert n_keep == LEFT_CTX and (d_b * d_s) % TM == 0 and TM % d_s == 0

    xp = x_prompt.reshape(n_b * n_s, D_MODEL)
    xs = x_sample.reshape(d_b * d_s, D_MODEL)
    row = lambda a: a.reshape(1, -1)
    heads = lambda a, b, s: a.reshape(b, s, HEADS, HEAD_DIM)
    rows_minor = lambda a: a.transpose(0, 2, 3, 1).reshape(a.shape[0], WIDTH, a.shape[1])

    we = _prep_even(w_in_even[0], a_q_norm[0], a_w_uq[0], a_kv_norm[0], a_w_uk[0], a_w_uv[0])
    wout_e = w_out_even[0].astype(BF16)
    cos_p, sin_p = _rope_tables(jnp.arange(n_s), TM)
    cos_s, sin_s = _rope_tables(n_past + jnp.arange(d_s), TM)

    (ckv_p, kr_p, k2_p, q2_p, ga_p, gb_p, qb_p, kb_p, kb16_p, vb_p, vb16_p) = _in_even(
        xp, row(norm_pre[0]), cos_p, sin_p, we, seq_len=n_s)
    (ckv_s, kr_s, k2_s, q2_s, ga_s, gb_s, qb_s, kb_s, kb16_s, vb_s, vb16_s) = _in_even(
        xs, row(norm_pre[0]), cos_s, sin_s, we)

    lat_p = _mla_prompt(q2_p, k2_p, n_b, n_s).reshape(n_b * n_s, A_HEADS * A_KV_LORA)
    lat_s = _mla_sample(q2_s, k2_s, cache_mla_ckv[0], cache_mla_krope[0].transpose(0, 2, 1), d_b, d_s
                        ).reshape(d_b * d_s, A_HEADS * A_KV_LORA)
    sb_p = _sb_prompt(qb_p, kb16_p, vb16_p, n_b, n_s)
    sb_s = _sb_sample(qb_s, kb16_s, vb16_s, rows_minor(cache_sb_k[0]), rows_minor(cache_sb_v[0]), d_b, d_s)

    xp1 = _out_proj(xp, row(norm_post[0]), ga_p, gb_p, lat_p, sb_p, wout_e, we['wuv'])
    xs1 = _out_proj(xs, row(norm_post[0]), ga_s, gb_s, lat_s, sb_s, wout_e, we['wuv'])

    wo = _prep_odd(w_in_odd[0], d_forget_bias[0])
    wout_o = w_out_odd[0].astype(BF16)
    (qc_p, kc_p, kc16_p, vc_p, vc16_p, gc_p, qd_p, kd_p, kd16_p, vd_p, vd16_p, lf_p, gd_p) = _in_odd(
        xp1, row(norm_pre[1]), wo, seq_len=n_s)
    (qc_s, kc_s, kc16_s, vc_s, vc16_s, gc_s, qd_s, kd_s, kd16_s, vd_s, vd16_s, lf_s, gd_s) = _in_odd(
        xs1, row(norm_pre[1]), wo)

    band_p = _band_prompt(qc_p, kc16_p, vc16_p, _band_bias(c_rel_bias[0], BAND_TQ), n_b, n_s)
    band_s = _band_sample(qc_s, kc16_s, vc16_s, rows_minor(cache_band_k[0]), rows_minor(cache_band_v[0]),
                          _band_bias(c_rel_bias[0], d_s), d_b, d_s)

    lf_s3 = lf_s.reshape(d_b, d_s, HEADS)
    cum_p = _cumsum_rows(lf_p.reshape(n_b * HEADS, n_s)).reshape(n_b, HEADS, n_s)
    lf_all = jnp.concatenate([cache_fox_logf[0].astype(F32), lf_s3], axis=1)
    cum_s = _cumsum_rows(lf_all.transpose(0, 2, 1).reshape(d_b * HEADS, n_past + d_s)
                         ).reshape(d_b, HEADS, n_past + d_s)
    fox_p = _fox_prompt(qd_p, kd16_p, vd16_p, cum_p, n_b, n_s)
    fox_s = _fox_sample(qd_s, kd16_s, vd16_s, rows_minor(cache_fox_k[0]), rows_minor(cache_fox_v[0]), cum_s,
                        d_b, d_s)

    xp2 = _out_proj(xp1, row(norm_post[1]), gc_p, gd_p, band_p, fox_p, wout_o)
    xs2 = _out_proj(xs1, row(norm_post[1]), gc_s, gd_s, band_s, fox_s, wout_o)

    keep = min(LEFT_CTX, n_s)
    band_k_s = jnp.concatenate([cache_band_k[0], heads(kc_s, d_b, d_s)], axis=1)[:, d_s:]
    band_v_s = jnp.concatenate([cache_band_v[0], heads(vc_s, d_b, d_s)], axis=1)[:, d_s:]
    one = lambda a: a[None]
    heads_t = lambda a: a.reshape(n_b, HEADS, HEAD_DIM, a.shape[-1]).transpose(0, 3, 1, 2)
    return (xp2.reshape(n_b, n_s, D_MODEL), xs2.reshape(d_b, d_s, D_MODEL),
            one(ckv_p.reshape(n_b, n_s, A_KV_LORA)), one(kr_p.transpose(0, 2, 1)),
            one(heads_t(kb_p)), one(heads_t(vb_p)),
            one(heads_t(kc_p[:, :, n_s - keep:])), one(heads_t(vc_p[:, :, n_s - keep:])),
            one(heads_t(kd_p)), one(heads_t(vd_p)), one(lf_p.transpose(0, 2, 1)),
            one(ckv_s.reshape(d_b, d_s, A_KV_LORA)), one(kr_s.reshape(d_b, d_s, A_ROPE)),
            one(heads(kb_s, d_b, d_s)), one(heads(vb_s, d_b, d_s)),
            one(band_k_s), one(band_v_s),
            one(heads(kd_s, d_b, d_s)), one(heads(vd_s, d_b, d_s)), one(lf_s3))
```

```python
import functools

import numpy as np
import jax
import jax.numpy as jnp
from jax import lax
from jax.experimental import pallas as pl
from jax.experimental.pallas import tpu as pltpu

F32 = jnp.float32
BF16 = jnp.bfloat16

D_MODEL = 1024
PAST_LEN = 4096
CHUNK = 64
LEFT_CTX = 512
REL_CLIP = 128
EPS = 1e-6
NEG = -1e30
ROPE_THETA = 10000.0
A_HEADS = 8
A_Q_LORA = 256
A_KV_LORA = 128
A_NOPE = 64
A_ROPE = 32
A_V = 64
A_SCALE = (A_NOPE + A_ROPE) ** -0.5
HEADS = 8
HEAD_DIM = 64
WIDTH = HEADS * HEAD_DIM
QK_SCALE = HEAD_DIM ** -0.5
LOG2E = 1.4426950408889634
MLA_LOGIT_SCALE = A_SCALE * LOG2E
N_PAIRS = HEADS // 2

LANES = 128
VMEM_LIMIT = 52 * 1024 * 1024
TM = 512
TQ = 256
TK = 256
TKC = 1024
SB_T = 256
SB_TK = 256
FOX_TQ = 512
FOX_TK = 1024
BAND_TQ = 128
MLA_RT = 256
MLA_QB = 512
PAIRS_PER_STEP = 4
STICK_DEAD = -104.0


def _dot(a, b):
    return jnp.dot(a, b, preferred_element_type=F32)


def _dot_nt(a, b):
    return lax.dot_general(a, b, (((1,), (1,)), ((), ())), preferred_element_type=F32)


def _rms(x, g):
    y = x * lax.rsqrt(jnp.mean(x * x, axis=-1, keepdims=True) + EPS)
    return y * g


def _log_sigmoid(z):
    return jnp.minimum(z, 0.0) - jnp.log(1.0 + jnp.exp(-jnp.abs(z)))


def _silu(g):
    return g / (1.0 + jnp.exp(-g))


def _stack_pair(q2):
    qf = q2.astype(F32)
    lane = lax.broadcasted_iota(jnp.int32, qf.shape, 1)
    even = jnp.where(lane < HEAD_DIM, qf, 0.0)
    odd = jnp.where(lane >= HEAD_DIM, qf, 0.0)
    return jnp.concatenate([even, odd], axis=0).astype(BF16)


def _merge_pair(o, tq):
    top, bot = o[:tq], o[tq:]
    lane = lax.broadcasted_iota(jnp.int32, top.shape, 1)
    return jnp.where(lane < HEAD_DIM, top, bot)


def _params(*sem):
    return pltpu.CompilerParams(dimension_semantics=sem, vmem_limit_bytes=VMEM_LIMIT)


def _const_spec(shape):
    nd = len(shape)
    return pl.BlockSpec(shape, lambda *_: (0,) * nd)


def _in_even_body(x_ref, gpre_ref, cos_ref, sin_ref, wqa_ref, wkv_ref, wkr_ref, wkrs_ref, wga_ref,
                  wqb_ref, wkb_ref, wvb_ref, wgb_ref, qn_ref, kvn_ref, wuqn_ref, wuqr_ref,
                  wuqrs_ref, wuk_ref,
                  ckv_ref, krope_ref, k2_ref, q2_ref, ga_ref, gb_ref, qb_ref, kb_ref, kb16_ref,
                  vb_ref, vb16_ref, *, rows_minor):
    h = _rms(x_ref[...], gpre_ref[...]).astype(BF16)
    cos = cos_ref[...]
    sin = sin_ref[...]
    ckv = _rms(_dot(h, wkv_ref[...]), kvn_ref[...])
    ckv_ref[...] = ckv
    kr = _dot(h, wkr_ref[...]) * cos + _dot(h, wkrs_ref[...]) * sin
    krope_ref[...] = kr.T[:A_ROPE] if rows_minor else kr[:, :A_ROPE]
    k2_ref[:, :LANES] = ckv.astype(BF16)
    k2_ref[:, LANES:] = kr.astype(BF16)
    cq = _rms(_dot(h, wqa_ref[...]), qn_ref[...]).astype(BF16)
    qn = _dot(cq, wuqn_ref[...]).astype(BF16)
    for p in range(N_PAIRS):
        ql = _dot(qn[:, p * LANES:(p + 1) * LANES], wuk_ref[p])
        q2_ref[:, (2 * p) * 256:(2 * p) * 256 + LANES] = ql[:, :LANES].astype(BF16)
        q2_ref[:, (2 * p + 1) * 256:(2 * p + 1) * 256 + LANES] = ql[:, LANES:].astype(BF16)
    qr = _dot(cq, wuqr_ref[...])
    qrs = _dot(cq, wuqrs_ref[...])
    for hd in range(A_HEADS):
        rot = qr[:, hd * LANES:(hd + 1) * LANES] * cos + qrs[:, hd * LANES:(hd + 1) * LANES] * sin
        q2_ref[:, hd * 256 + LANES:(hd + 1) * 256] = rot.astype(BF16)
    ga_ref[...] = _dot(h, wga_ref[...]).astype(BF16)
    gb_ref[...] = _dot(h, wgb_ref[...]).astype(BF16)
    qb = _dot(h, wqb_ref[...]) * QK_SCALE
    kb = _dot(h, wkb_ref[...])
    vb = _dot(h, wvb_ref[...])
    kb_ref[...] = kb.T if rows_minor else kb
    vb_ref[...] = vb.T if rows_minor else vb
    for p in range(N_PAIRS):
        sl = slice(p * LANES, (p + 1) * LANES)
        qb_ref[p] = qb[:, sl].astype(BF16)
        kb16_ref[p] = kb[:, sl].astype(BF16)
        vb16_ref[p] = vb[:, sl].astype(BF16)


def _state_specs(rows, seq_len):
    sds = jax.ShapeDtypeStruct
    if seq_len is None:
        return (lambda n: pl.BlockSpec((TM, n), lambda i: (i, 0))), (lambda n: sds((rows, n), F32))
    nt = seq_len // TM
    return ((lambda n: pl.BlockSpec((None, n, TM), lambda i: (i // nt, 0, i % nt))),
            (lambda n: sds((rows // seq_len, n, seq_len), F32)))


def _in_even(x, gpre, cos, sin, w, seq_len=None):
    rows = x.shape[0]
    n_tab = cos.shape[0] // TM
    row_spec = lambda n: pl.BlockSpec((TM, n), lambda i: (i, 0))
    pm_spec = pl.BlockSpec((N_PAIRS, TM, LANES), lambda i: (0, i, 0))
    tab_spec = pl.BlockSpec((TM, LANES), lambda i: (i % n_tab, 0))
    weights = [w['wqa'], w['wkv'], w['wkr'], w['wkrs'], w['wga'], w['wqb'], w['wkb'], w['wvb'], w['wgb'],
               w['qn'], w['kvn'], w['wuqn'], w['wuqr'], w['wuqrs'], w['wuk']]
    in_specs = ([row_spec(D_MODEL), _const_spec((1, D_MODEL)), tab_spec, tab_spec]
                + [_const_spec(a.shape) for a in weights])
    args = [x, gpre, cos, sin, *weights]
    st_spec, st_shape = _state_specs(rows, seq_len)
    sds = jax.ShapeDtypeStruct
    pm = sds((N_PAIRS, rows, LANES), BF16)
    return pl.pallas_call(
        functools.partial(_in_even_body, rows_minor=seq_len is not None),
        grid=(rows // TM,),
        in_specs=in_specs,
        out_specs=[row_spec(A_KV_LORA), st_spec(A_ROPE), row_spec(256), row_spec(A_HEADS * 256),
                   row_spec(WIDTH), row_spec(WIDTH), pm_spec, st_spec(WIDTH), pm_spec,
                   st_spec(WIDTH), pm_spec],
        out_shape=[sds((rows, A_KV_LORA), F32), st_shape(A_ROPE), sds((rows, 256), BF16),
                   sds((rows, A_HEADS * 256), BF16), sds((rows, WIDTH), BF16), sds((rows, WIDTH), BF16),
                   pm, st_shape(WIDTH), pm, st_shape(WIDTH), pm],
        compiler_params=_params("parallel"),
        name="in_proj_even",
    )(*args)


def _in_odd_body(*refs, rows_minor):
    (x_ref, gpre_ref, fb_ref, wqc_ref, wkc_ref, wvc_ref, wgc_ref, wqd_ref, wkd_ref, wvd_ref, wf_ref,
     wgd_ref) = refs[:12]
    n_in = 14 if rows_minor else 12
    (qc_ref, kc_ref, kc16_ref, vc_ref, vc16_ref, gc_ref, qd_ref, kd_ref, kd16_ref, vd_ref, vd16_ref,
     logf_ref, gd_ref) = refs[n_in:]
    h = _rms(x_ref[...], gpre_ref[...]).astype(BF16)
    gc_ref[...] = _dot(h, wgc_ref[...]).astype(BF16)
    gd_ref[...] = _dot(h, wgd_ref[...]).astype(BF16)
    if rows_minor:
        wf_t, fb_t = refs[12:14]
        logf_ref[...] = _log_sigmoid(_dot_nt(wf_t[...], h) + fb_t[...])
    else:
        logf_ref[...] = _log_sigmoid(_dot(h, wf_ref[...])[:, :HEADS] + fb_ref[...])
    for q_w, k_w, v_w, q_o, k_o, k16_o, v_o, v16_o in (
            (wqc_ref, wkc_ref, wvc_ref, qc_ref, kc_ref, kc16_ref, vc_ref, vc16_ref),
            (wqd_ref, wkd_ref, wvd_ref, qd_ref, kd_ref, kd16_ref, vd_ref, vd16_ref)):
        q = _dot(h, q_w[...]) * (QK_SCALE * LOG2E)
        k = _dot(h, k_w[...])
        v = _dot(h, v_w[...])
        k_o[...] = k.T if rows_minor else k
        v_o[...] = v.T if rows_minor else v
        for p in range(N_PAIRS):
            sl = slice(p * LANES, (p + 1) * LANES)
            q_o[p] = q[:, sl].astype(BF16)
            k16_o[p] = k[:, sl].astype(BF16)
            v16_o[p] = v[:, sl].astype(BF16)


def _in_odd(x, gpre, w, seq_len=None):
    rows = x.shape[0]
    row_spec = lambda n: pl.BlockSpec((TM, n), lambda i: (i, 0))
    pm_spec = pl.BlockSpec((N_PAIRS, TM, LANES), lambda i: (0, i, 0))
    weights = [w['wqc'], w['wkc'], w['wvc'], w['wgc'], w['wqd'], w['wkd'], w['wvd'], w['wf'], w['wgd']]
    rows_minor = seq_len is not None
    if rows_minor:
        weights += [w['wf_t'], w['fb_t']]
    st_spec, st_shape = _state_specs(rows, seq_len)
    sds = jax.ShapeDtypeStruct
    pm = sds((N_PAIRS, rows, LANES), BF16)
    full = sds((rows, WIDTH), BF16)
    return pl.pallas_call(
        functools.partial(_in_odd_body, rows_minor=rows_minor),
        grid=(rows // TM,),
        in_specs=[row_spec(D_MODEL), _const_spec((1, D_MODEL)), _const_spec((1, HEADS))]
                 + [_const_spec(a.shape) for a in weights],
        out_specs=[pm_spec, st_spec(WIDTH), pm_spec, st_spec(WIDTH), pm_spec, row_spec(WIDTH),
                   pm_spec, st_spec(WIDTH), pm_spec, st_spec(WIDTH), pm_spec, st_spec(HEADS),
                   row_spec(WIDTH)],
        out_shape=[pm, st_shape(WIDTH), pm, st_shape(WIDTH), pm, full, pm, st_shape(WIDTH), pm,
                   st_shape(WIDTH), pm, st_shape(HEADS), full],
        compiler_params=_params("parallel"),
        name="in_proj_odd",
    )(x, gpre, w['fb'], *weights)


def _out_body(*refs, mla):
    if mla:
        x_ref, gpost_ref, g1_ref, g2_ref, a_ref, b_ref, wuv_ref, wout_ref, o_ref, mix_ref = refs
    else:
        x_ref, gpost_ref, g1_ref, g2_ref, a_ref, b_ref, wout_ref, o_ref, mix_ref = refs
    s1 = _silu(g1_ref[...].astype(F32))
    s2 = _silu(g2_ref[...].astype(F32))
    for p in range(N_PAIRS):
        sl = slice(p * LANES, (p + 1) * LANES)
        if mla:
            a = _dot(a_ref[:, p * 256:(p + 1) * 256], wuv_ref[p])
        else:
            a = a_ref[p].astype(F32)
        mix_ref[:, sl] = (s1[:, sl] * a).astype(BF16)
        mix_ref[:, WIDTH + p * LANES:WIDTH + (p + 1) * LANES] = (
            s2[:, sl] * b_ref[p].astype(F32)).astype(BF16)
    y = _dot(mix_ref[...], wout_ref[...])
    o_ref[...] = x_ref[...] + _rms(y, gpost_ref[...])


def _out_proj(x, gpost, g1, g2, a, b, wout, wuv=None):
    rows = x.shape[0]
    mla = wuv is not None
    row_spec = lambda n: pl.BlockSpec((TM, n), lambda i: (i, 0))
    pm_spec = pl.BlockSpec((N_PAIRS, TM, LANES), lambda i: (0, i, 0))
    in_specs = [row_spec(D_MODEL), _const_spec((1, D_MODEL)), row_spec(WIDTH), row_spec(WIDTH),
                row_spec(A_HEADS * A_KV_LORA) if mla else pm_spec, pm_spec]
    args = [x, gpost, g1, g2, a, b]
    if mla:
        in_specs.append(_const_spec(wuv.shape))
        args.append(wuv)
    in_specs.append(_const_spec(wout.shape))
    args.append(wout)
    return pl.pallas_call(
        functools.partial(_out_body, mla=mla),
        grid=(rows // TM,),
        in_specs=in_specs,
        out_specs=row_spec(D_MODEL),
        out_shape=jax.ShapeDtypeStruct((rows, D_MODEL), F32),
        scratch_shapes=[pltpu.VMEM((TM, 2 * WIDTH), BF16)],
        compiler_params=_params("parallel"),
        name="out_proj_even" if mla else "out_proj_odd",
    )(*args)


def _softmax_init(m_ref, accl_ref):
    m_ref[...] = jnp.full(m_ref.shape, NEG, F32)
    accl_ref[...] = jnp.zeros(accl_ref.shape, F32)


def _lanes(x, n):
    parts = [x] * (n // LANES)
    if n % LANES:
        parts.append(x[:, :n % LANES])
    return parts[0] if len(parts) == 1 else jnp.concatenate(parts, axis=1)


def _scores(q, k, kv_t):
    return _dot(q, k) if kv_t else _dot_nt(q, k)


def _weighted(p, v, kv_t):
    return _dot_nt(p, v) if kv_t else _dot(p, v)


def _with_ones(v, kv_t=False):
    return jnp.concatenate([v, jnp.ones(v.shape, BF16)], axis=0 if kv_t else 1)


def _softmax_update(s, v1, m_ref, accl_ref, kv_t=False):
    keys = s.shape[1]
    n = v1.shape[0] if kv_t else v1.shape[1]
    m_prev = m_ref[...]
    m_new = jnp.maximum(m_prev, jnp.max(s, axis=1, keepdims=True))
    alpha = jnp.exp2(m_prev - m_new)
    p = jnp.exp2(s - _lanes(m_new, keys))
    accl_ref[...] = _lanes(alpha, n) * accl_ref[...] + _weighted(p.astype(BF16), v1, kv_t)
    m_ref[...] = m_new


def _softmax_result(accl_ref):
    accl = accl_ref[...]
    n = accl.shape[1] // 2
    return accl[:, :n] / accl[:, n:]


def _fill_suffix_ones(tri_ref):
    n = tri_ref.shape[0]
    r = lax.broadcasted_iota(jnp.int32, (n, n), 0)
    c = lax.broadcasted_iota(jnp.int32, (n, n), 1)
    tri_ref[...] = jnp.where(r > c, 1.0, 0.0).astype(BF16)


def _stick_tiles(qs, k2, v2, tri, acc_refs, car_refs, mask, kv_t=False):
    n = len(qs)
    z = [_scores(qs[i], k2[i], kv_t) for i in range(n)]
    lb, l1, suf = [], [], []
    for i in range(n):
        lb.append(_log_sigmoid(z[i]))
        l = lb[i] - z[i]
        l1.append(l if mask is None else jnp.where(mask, l, 0.0))
        hi = l1[i].astype(BF16)
        lo = (l1[i] - hi.astype(F32)).astype(BF16)
        suf.append(_dot(hi, tri) + _dot(lo, tri))
    for i in range(n):
        w = jnp.exp(lb[i] + (suf[i] + car_refs[i][...]))
        if mask is not None:
            w = jnp.where(mask, w, 0.0)
        acc_refs[i][...] += _weighted(w.astype(BF16), v2[i], kv_t)
        car_refs[i][...] += jnp.sum(l1[i], axis=1, keepdims=True)


def _stick_alive(car_ref):
    return (jnp.max(car_ref[...]) >= STICK_DEAD).astype(jnp.int32)


def _local_causal(rows, keys, tq, strict, key_offset=0):
    r = lax.broadcasted_iota(jnp.int32, (rows, keys), 0) & (tq - 1)
    c = lax.broadcasted_iota(jnp.int32, (rows, keys), 1) + key_offset
    return (c < r) if strict else (c <= r)


def _decay_tile(s, v2, cq, ck, causal, m_ref, accl_ref, kv_t=False):
    tq, keys = cq.shape[0], ck.shape[1]
    for hh in range(2):
        rows = slice(hh * tq, (hh + 1) * tq)
        sh = s[rows] + (cq[:, hh:hh + 1] - ck[hh:hh + 1])
        if causal:
            sh = jnp.where(_local_causal(tq, keys, tq, strict=False), sh, NEG)
        _softmax_update(sh, _with_ones(v2, kv_t), m_ref.at[rows], accl_ref.at[rows], kv_t)


def _split3(c):
    hi = c.astype(BF16).astype(F32)
    rest = c - hi
    mid = rest.astype(BF16).astype(F32)
    return hi, mid, rest - mid


def _decay_lanes(c, base, query):
    hi, mid, lo = _split3(c if query else -c)
    lane = lax.broadcasted_iota(jnp.int32, (c.shape[0], LANES), 1) - (base if query else base + 3)
    parts = jnp.where(lane == 0, hi, jnp.where(lane == 1, mid, jnp.where(lane == 2, lo, 0.0)))
    ones_at = lane + 3 if not query else lane - 3
    return jnp.where((ones_at >= 0) & (ones_at < 3), 1.0, parts)


def _decay_operand(x, cum, hh, query):
    lane = lax.broadcasted_iota(jnp.int32, x.shape, 1)
    own = (lane < HEAD_DIM) if hh == 0 else (lane >= HEAD_DIM)
    base = HEAD_DIM if hh == 0 else 0
    return jnp.where(own, x, _decay_lanes(cum[:, hh:hh + 1], base, query)).astype(BF16)


def _decay_tile_heads(q1, k1, v1, causal_offset, m_ref, accl_ref):
    tq, keys = q1[0].shape[0], k1.shape[1]
    for hh in range(2):
        rows = slice(hh * tq, (hh + 1) * tq)
        s = _dot_nt(q1[hh], k1[hh])
        if causal_offset is not None:
            s = jnp.where(_local_causal(tq, keys, tq, strict=False, key_offset=causal_offset), s, NEG)
        _softmax_update(s, v1[hh], m_ref.at[rows], accl_ref.at[rows])


def _pair_result(accl, tq):
    o = accl / pltpu.roll(accl, HEAD_DIM, axis=1)
    return _merge_pair(o, tq)


def _mla_prompt_body(q_ref, k_ref, o_ref, m_ref, accl_ref):
    g = pl.program_id(1)
    grp = 4 * CHUNK

    def update_heads(r0, k, mask):
        sl = pl.ds(r0, grp)
        scores = [_dot_nt(q_ref[sl, h * 256:(h + 1) * 256], k) for h in range(A_HEADS)]
        v1 = _with_ones(k[:, :A_KV_LORA])
        for h in range(A_HEADS):
            s = scores[h] * MLA_LOGIT_SCALE
            if mask is not None:
                s = jnp.where(mask, s, NEG)
            _softmax_update(s, v1, m_ref.at[h, sl], accl_ref.at[h, sl])

    def group(gl, carry):
        gq = g * (MLA_QB // grp) + gl
        g0 = pl.multiple_of(gl * grp, grp)
        for h in range(A_HEADS):
            _softmax_init(m_ref.at[h, pl.ds(g0, grp)], accl_ref.at[h, pl.ds(g0, grp)])

        def earlier(k0, n_keys):
            update_heads(g0, k_ref[pl.ds(k0, n_keys), :], None)

        def kv(j, c2):
            earlier(pl.multiple_of(j * 2 * TK, 2 * TK), 2 * TK)
            return c2

        lax.fori_loop(0, gq // 2, kv, 0)

        @pl.when(gq % 2 == 1)
        def _():
            earlier(pl.multiple_of((gq - 1) * TK, TK), TK)

        kd = k_ref[pl.ds(pl.multiple_of(gq * TK, TK), TK), :]
        row_chunk = lax.broadcasted_iota(jnp.int32, (grp, TK), 0) // CHUNK
        key_chunk = lax.broadcasted_iota(jnp.int32, (grp, TK), 1) // CHUNK
        update_heads(g0, kd, key_chunk <= row_chunk)
        for h in range(A_HEADS):
            o_ref[pl.ds(g0, grp), h * A_KV_LORA:(h + 1) * A_KV_LORA] = _softmax_result(
                accl_ref.at[h, pl.ds(g0, grp)]).astype(BF16)
        return carry

    lax.fori_loop(0, MLA_QB // grp, group, 0)


def _mla_prompt(q2, k2, n_b, n_s):
    assert n_s % MLA_QB == 0 and TK == 4 * CHUNK
    return pl.pallas_call(
        _mla_prompt_body,
        grid=(n_b, n_s // MLA_QB),
        in_specs=[pl.BlockSpec((None, MLA_QB, A_HEADS * 256), lambda b, g: (b, g, 0)),
                  pl.BlockSpec((None, n_s, 256), lambda b, g: (b, 0, 0))],
        out_specs=pl.BlockSpec((None, MLA_QB, A_HEADS * A_KV_LORA), lambda b, g: (b, g, 0)),
        out_shape=jax.ShapeDtypeStruct((n_b, n_s, A_HEADS * A_KV_LORA), BF16),
        scratch_shapes=[pltpu.VMEM((A_HEADS, MLA_QB, LANES), F32),
                        pltpu.VMEM((A_HEADS, MLA_QB, 2 * LANES), F32)],
        compiler_params=_params("parallel", "parallel"),
        name="mla_prompt",
    )(q2.reshape(n_b, n_s, A_HEADS * 256), k2.reshape(n_b, n_s, 256))


def _mla_sample_body(q_ref, kn_ref, ckv_ref, kr_ref, o_ref, m_ref, accl_ref):
    j = pl.program_id(1)

    @pl.when(j == 0)
    def _():
        _softmax_init(m_ref, accl_ref)

    n_tiles = q_ref.shape[0] // MLA_RT
    ck = ckv_ref[...].astype(BF16)
    kr = kr_ref[...].T.astype(BF16)
    k = jnp.concatenate([ck, kr, jnp.zeros((TKC, LANES - A_ROPE), BF16)], axis=1)
    row_tiles = [slice(t * MLA_RT, (t + 1) * MLA_RT) for t in range(n_tiles)]

    def update_tiles(keys, values):
        scores = [_dot_nt(q_ref[sl, :], keys) for sl in row_tiles]
        v1 = _with_ones(values)
        for sl, s in zip(row_tiles, scores):
            _softmax_update(s * MLA_LOGIT_SCALE, v1, m_ref.at[sl], accl_ref.at[sl])

    update_tiles(k, ck)

    @pl.when(j == pl.num_programs(1) - 1)
    def _():
        kn = kn_ref[...]
        update_tiles(kn, kn[:, :A_KV_LORA])
        o_ref[...] = _softmax_result(accl_ref).astype(BF16)


def _mla_sample(q2, k2, cache_ckv, cache_kr, n_b, n_s):
    rows = n_s * A_HEADS
    n_past = cache_ckv.shape[1]
    return pl.pallas_call(
        _mla_sample_body,
        grid=(n_b, n_past // TKC),
        in_specs=[pl.BlockSpec((None, rows, 256), lambda b, j: (b, 0, 0)),
                  pl.BlockSpec((None, n_s, 256), lambda b, j: (b, 0, 0)),
                  pl.BlockSpec((None, TKC, A_KV_LORA), lambda b, j: (b, j, 0)),
                  pl.BlockSpec((None, A_ROPE, TKC), lambda b, j: (b, 0, j))],
        out_specs=pl.BlockSpec((None, rows, A_KV_LORA), lambda b, j: (b, 0, 0)),
        out_shape=jax.ShapeDtypeStruct((n_b, rows, A_KV_LORA), BF16),
        scratch_shapes=[pltpu.VMEM((rows, LANES), F32), pltpu.VMEM((rows, 2 * LANES), F32)],
        compiler_params=_params("parallel", "arbitrary"),
        name="mla_sample",
    )(q2.reshape(n_b, rows, 256), k2.reshape(n_b, n_s, 256), cache_ckv, cache_kr)


def _sb_prompt_body(q_ref, k_ref, v_ref, o_ref, tri_ref, acc_ref, car_ref):
    n_s = q_ref.shape[1]
    _fill_suffix_ones(tri_ref)

    def qblock(i, carry):
        q0 = pl.multiple_of(i * SB_T, SB_T)
        qs = [_stack_pair(q_ref[g, pl.ds(q0, SB_T), :]) for g in range(PAIRS_PER_STEP)]
        acc_ref[...] = jnp.zeros(acc_ref.shape, F32)
        car_ref[...] = jnp.zeros(car_ref.shape, F32)

        def tiles(k0, n_keys, mask):
            pairs = range(PAIRS_PER_STEP)
            _stick_tiles(qs, [k_ref[g, pl.ds(k0, n_keys), :] for g in pairs],
                         [v_ref[g, pl.ds(k0, n_keys), :] for g in pairs], tri_ref[:n_keys, :n_keys],
                         [acc_ref.at[g] for g in pairs], [car_ref.at[g] for g in pairs], mask)

        tiles(q0, SB_T, _local_causal(2 * SB_T, SB_T, SB_T, strict=True))

        def kv(state):
            jj, _ = state
            tiles(pl.multiple_of(q0 - (jj + 1) * SB_TK, SB_TK), SB_TK, None)
            return jj + 1, _stick_alive(car_ref)

        lax.while_loop(lambda st: (st[0] < i * (SB_T // SB_TK)) & (st[1] > 0), kv,
                       (jnp.int32(0), _stick_alive(car_ref)))
        for g in range(PAIRS_PER_STEP):
            o_ref[g, pl.ds(q0, SB_T), :] = _merge_pair(acc_ref[g], SB_T).astype(o_ref.dtype)
        return carry

    lax.fori_loop(0, n_s // SB_T, qblock, 0)


def _pair_seq_spec(n_s):
    return pl.BlockSpec((PAIRS_PER_STEP, n_s, LANES), lambda b, g: (g, b, 0))


def _sb_prompt(q, k, v, n_b, n_s):
    assert n_s % SB_T == 0
    spec = _pair_seq_spec(n_s)
    return pl.pallas_call(
        _sb_prompt_body,
        grid=(n_b, N_PAIRS // PAIRS_PER_STEP),
        in_specs=[spec, spec, spec],
        out_specs=spec,
        out_shape=jax.ShapeDtypeStruct((N_PAIRS, n_b * n_s, LANES), BF16),
        scratch_shapes=[pltpu.VMEM((SB_T, SB_T), BF16), pltpu.VMEM((PAIRS_PER_STEP, 2 * SB_T, LANES), F32),
                        pltpu.VMEM((PAIRS_PER_STEP, 2 * SB_T, 1), F32)],
        compiler_params=_params("parallel", "parallel"),
        name="sb_prompt",
    )(q, k, v)


def _sb_sample_body(q_ref, kn_ref, vn_ref, ck_hbm, cv_hbm, o_ref, kbuf, vbuf, sem, tri_ref, acc_ref,
                    car_ref):
    b = pl.program_id(0)
    n_q = q_ref.shape[1]
    n_blk = ck_hbm.shape[2] // TK

    def block_copies(blk):
        rows = pl.ds(pl.multiple_of(blk * TK, TK), TK)
        return (pltpu.make_async_copy(ck_hbm.at[b, :, rows], kbuf, sem.at[0]),
                pltpu.make_async_copy(cv_hbm.at[b, :, rows], vbuf, sem.at[1]))

    def start(blk):
        for c in block_copies(blk):
            c.start()

    def cache_block(blk):
        for c in block_copies(blk):
            c.wait()
        tiles = lambda buf: [buf[p * LANES:(p + 1) * LANES, :].astype(BF16) for p in range(N_PAIRS)]
        _stick_tiles(qs, tiles(kbuf), tiles(vbuf), tri_ref[...], accs, cars, None, kv_t=True)

    start(n_blk - 1)
    _fill_suffix_ones(tri_ref)
    acc_ref[...] = jnp.zeros(acc_ref.shape, F32)
    car_ref[...] = jnp.zeros(car_ref.shape, F32)
    qs = [_stack_pair(q_ref[p]) for p in range(N_PAIRS)]
    accs = [acc_ref.at[p] for p in range(N_PAIRS)]
    cars = [car_ref.at[p] for p in range(N_PAIRS)]
    _stick_tiles(qs, [kn_ref[p] for p in range(N_PAIRS)], [vn_ref[p] for p in range(N_PAIRS)],
                 tri_ref[:n_q, :n_q], accs, cars, _local_causal(2 * n_q, n_q, n_q, strict=True))
    cache_block(n_blk - 1)

    def older(state):
        blk, _ = state
        start(blk)
        cache_block(blk)
        return blk - 1, _stick_alive(car_ref)

    lax.while_loop(lambda st: (st[0] >= 0) & (st[1] > 0), older,
                   (jnp.int32(n_blk - 2), _stick_alive(car_ref)))
    for p in range(N_PAIRS):
        o_ref[p] = _merge_pair(acc_ref[p], n_q).astype(o_ref.dtype)


def _sb_sample(q, kn, vn, cache_kt, cache_vt, n_b, n_s):
    new_spec = pl.BlockSpec((N_PAIRS, n_s, LANES), lambda b: (0, b, 0))
    hbm = pl.BlockSpec(memory_space=pl.ANY)
    return pl.pallas_call(
        _sb_sample_body,
        grid=(n_b,),
        in_specs=[new_spec, new_spec, new_spec, hbm, hbm],
        out_specs=new_spec,
        out_shape=jax.ShapeDtypeStruct((N_PAIRS, n_b * n_s, LANES), BF16),
        scratch_shapes=[pltpu.VMEM((WIDTH, TK), F32), pltpu.VMEM((WIDTH, TK), F32),
                        pltpu.SemaphoreType.DMA((2,)), pltpu.VMEM((TK, TK), BF16),
                        pltpu.VMEM((N_PAIRS, 2 * n_s, LANES), F32), pltpu.VMEM((N_PAIRS, 2 * n_s, 1), F32)],
        compiler_params=_params("arbitrary"),
        name="sb_sample",
    )(q, kn, vn, cache_kt, cache_vt)


def _band_block(scores, vwin, bias, valid_from, tq):
    s = scores + bias
    if valid_from is not None:
        col = lax.broadcasted_iota(jnp.int32, s.shape, 1)
        s = jnp.where(col >= valid_from, s, NEG)
    p = jnp.exp2(s - jnp.max(s, axis=1, keepdims=True))
    o = _dot(p.astype(BF16), vwin) / jnp.sum(p, axis=1, keepdims=True)
    return _merge_pair(o, tq)


def _band_prompt_body(q_ref, k_ref, v_ref, bias_ref, o_ref, kpad_ref, vpad_ref):
    n_s = q_ref.shape[1]
    win = LEFT_CTX + BAND_TQ
    zeros = jnp.zeros((PAIRS_PER_STEP, LEFT_CTX, LANES), BF16)
    kpad_ref[:, :LEFT_CTX, :] = zeros
    vpad_ref[:, :LEFT_CTX, :] = zeros
    kpad_ref[:, LEFT_CTX:, :] = k_ref[...]
    vpad_ref[:, LEFT_CTX:, :] = v_ref[...]

    def qblock(i, carry):
        q0 = pl.multiple_of(i * BAND_TQ, BAND_TQ)
        scores = [_dot_nt(_stack_pair(q_ref[g, pl.ds(q0, BAND_TQ), :]), kpad_ref[g, pl.ds(q0, win), :])
                  for g in range(PAIRS_PER_STEP)]
        for g in range(PAIRS_PER_STEP):
            o_ref[g, pl.ds(q0, BAND_TQ), :] = _band_block(
                scores[g], vpad_ref[g, pl.ds(q0, win), :], bias_ref[g], LEFT_CTX - q0, BAND_TQ
            ).astype(o_ref.dtype)
        return carry

    lax.fori_loop(0, n_s // BAND_TQ, qblock, 0)


def _band_prompt(q, k, v, bias, n_b, n_s):
    spec = _pair_seq_spec(n_s)
    win = LEFT_CTX + BAND_TQ
    return pl.pallas_call(
        _band_prompt_body,
        grid=(n_b, N_PAIRS // PAIRS_PER_STEP),
        in_specs=[spec, spec, spec,
                  pl.BlockSpec((PAIRS_PER_STEP, 2 * BAND_TQ, win), lambda b, g: (g, 0, 0))],
        out_specs=spec,
        out_shape=jax.ShapeDtypeStruct((N_PAIRS, n_b * n_s, LANES), BF16),
        scratch_shapes=[pltpu.VMEM((PAIRS_PER_STEP, LEFT_CTX + n_s, LANES), BF16),
                        pltpu.VMEM((PAIRS_PER_STEP, LEFT_CTX + n_s, LANES), BF16)],
        compiler_params=_params("parallel", "parallel"),
        name="band_prompt",
    )(q, k, v, bias)


def _band_sample_body(q_ref, kn_ref, vn_ref, kc_ref, vc_ref, bias_ref, o_ref):
    n_q = q_ref.shape[1]
    n_keep = kc_ref.shape[1]
    pair_lanes = [slice(p * LANES, (p + 1) * LANES) for p in range(N_PAIRS)]
    qs = [_stack_pair(q_ref[p]) for p in range(N_PAIRS)]
    scores_old = [_dot(qs[p], kc_ref[pair_lanes[p], :].astype(BF16)) for p in range(N_PAIRS)]
    scores_new = [_dot_nt(qs[p], kn_ref[p]) for p in range(N_PAIRS)]
    for p in range(N_PAIRS):
        sl = pair_lanes[p]
        bias = bias_ref[p]
        s_old = scores_old[p] + bias[:, :n_keep]
        s_new = scores_new[p] + bias[:, n_keep:]
        m = jnp.maximum(jnp.max(s_old, axis=1, keepdims=True), jnp.max(s_new, axis=1, keepdims=True))
        p_old = jnp.exp2(s_old - m)
        p_new = jnp.exp2(s_new - m)
        o = _dot_nt(p_old.astype(BF16), vc_ref[sl, :].astype(BF16)) + _dot(p_new.astype(BF16), vn_ref[p])
        total = jnp.sum(p_old, axis=1, keepdims=True) + jnp.sum(p_new, axis=1, keepdims=True)
        o_ref[p] = _merge_pair(o / total, n_q).astype(o_ref.dtype)


def _band_sample(q, kn, vn, cache_kt, cache_vt, bias, n_b, n_s):
    n_keep = cache_kt.shape[2]
    new_spec = pl.BlockSpec((N_PAIRS, n_s, LANES), lambda b: (0, b, 0))
    cache_spec = pl.BlockSpec((None, WIDTH, n_keep), lambda b: (b, 0, 0))
    return pl.pallas_call(
        _band_sample_body,
        grid=(n_b,),
        in_specs=[new_spec, new_spec, new_spec, cache_spec, cache_spec,
                  _const_spec((N_PAIRS, 2 * n_s, n_keep + n_s))],
        out_specs=new_spec,
        out_shape=jax.ShapeDtypeStruct((N_PAIRS, n_b * n_s, LANES), BF16),
        compiler_params=_params("parallel"),
        name="band_sample",
    )(q, kn, vn, cache_kt, cache_vt, bias)


def _band_bias(rel_bias, tq):
    win = LEFT_CTX + tq
    i = np.arange(tq)[:, None]
    w = np.arange(win)[None, :]
    qc, kc = i // CHUNK, w // CHUNK - LEFT_CTX // CHUNK
    ok = (kc <= qc) & (kc >= qc - LEFT_CTX // CHUNK)
    u = np.arange(win + tq - 1)
    rel = np.clip(LEFT_CTX + (tq - 1) - u, -REL_CLIP, REL_CLIP) + REL_CLIP
    diag = rel_bias.astype(F32)[:, rel]
    n = win + tq - 1
    flat = jnp.tile(diag, (1, tq))[:, tq - 1:tq - 1 + tq * (n - 1)]
    tab = flat.reshape(HEADS, tq, n - 1)[:, :, :win]
    tab = jnp.where(jnp.asarray(ok)[None], tab * LOG2E, NEG)
    return tab.reshape(N_PAIRS, 2 * tq, win)


def _fox_prompt_body(q_ref, k_ref, v_ref, cum_ref, o_ref, k1_ref, v1_ref, m_ref, accl_ref):
    n_s = q_ref.shape[1]

    def prepare(jb, carry):
        rows = pl.ds(pl.multiple_of(jb * TK, TK), TK)
        lane = lax.broadcasted_iota(jnp.int32, (TK, LANES), 1)
        for g in range(PAIRS_PER_STEP):
            k = k_ref[g, rows, :].astype(F32)
            v = v_ref[g, rows, :].astype(F32)
            cum = cum_ref[g, rows, :]
            for hh in range(2):
                k1_ref[g, hh, rows, :] = _decay_operand(k, cum, hh, query=False)
                own = (lane < HEAD_DIM) if hh == 0 else (lane >= HEAD_DIM)
                v1_ref[g, hh, rows, :] = jnp.where(own, v, 1.0).astype(BF16)
        return carry

    lax.fori_loop(0, n_s // TK, prepare, 0)

    tiles_per_q = FOX_TQ // TK

    def qblock(i, carry):
        q0 = pl.multiple_of(i * FOX_TQ, FOX_TQ)
        q1 = []
        for g in range(PAIRS_PER_STEP):
            q = q_ref[g, pl.ds(q0, FOX_TQ), :].astype(F32)
            cum = cum_ref[g, pl.ds(q0, FOX_TQ), :]
            q1.append([_decay_operand(q, cum, hh, query=True) for hh in range(2)])
        _softmax_init(m_ref, accl_ref)

        def tile(jb, causal_offset):
            rows = pl.ds(pl.multiple_of(jb * TK, TK), TK)
            for g in range(PAIRS_PER_STEP):
                _decay_tile_heads(q1[g], k1_ref[g, :, rows, :], v1_ref[g, :, rows, :], causal_offset,
                                  m_ref.at[g], accl_ref.at[g])

        for t in range(tiles_per_q):
            tile(i * tiles_per_q + t, t * TK)

        def kv(jb, c2):
            tile(jb, None)
            return c2

        lax.fori_loop(0, i * tiles_per_q, kv, 0)
        for g in range(PAIRS_PER_STEP):
            o_ref[g, pl.ds(q0, FOX_TQ), :] = _pair_result(accl_ref[g], FOX_TQ).astype(o_ref.dtype)
        return carry

    lax.fori_loop(0, n_s // FOX_TQ, qblock, 0)


def _fox_prompt(q, k, v, cum, n_b, n_s):
    assert FOX_TQ % TK == 0 and n_s % FOX_TQ == 0
    spec = _pair_seq_spec(n_s)
    cum_rows = cum.reshape(n_b, N_PAIRS, 2, n_s).transpose(0, 1, 3, 2)
    g_ = PAIRS_PER_STEP
    return pl.pallas_call(
        _fox_prompt_body,
        grid=(n_b, N_PAIRS // g_),
        in_specs=[spec, spec, spec, pl.BlockSpec((None, g_, n_s, 2), lambda b, g: (b, g, 0, 0))],
        out_specs=spec,
        out_shape=jax.ShapeDtypeStruct((N_PAIRS, n_b * n_s, LANES), BF16),
        scratch_shapes=[pltpu.VMEM((g_, 2, n_s, LANES), BF16), pltpu.VMEM((g_, 2, n_s, LANES), BF16),
                        pltpu.VMEM((g_, 2 * FOX_TQ, LANES), F32), pltpu.VMEM((g_, 2 * FOX_TQ, LANES), F32)],
        compiler_params=_params("parallel", "parallel"),
        name="fox_prompt",
    )(q, k, v, cum_rows)


def _fox_sample_body(q_ref, kn_ref, vn_ref, kc_ref, vc_ref, cq_ref, ckn_ref, ckc_ref, o_ref,
                     m_ref, accl_ref):
    j = pl.program_id(1)
    n_q = q_ref.shape[1]
    cq = cq_ref[...]

    def update_pairs(k_of, v_of, ck, new_rows):
        scores = [_scores(_stack_pair(q_ref[p]), k_of(p), not new_rows) for p in range(N_PAIRS)]
        for p in range(N_PAIRS):
            rows = slice(2 * p * n_q, (2 * p + 2) * n_q)
            _decay_tile(scores[p], v_of(p), cq[:, 2 * p:2 * p + 2], ck[2 * p:2 * p + 2], new_rows,
                        m_ref.at[rows], accl_ref.at[rows], kv_t=not new_rows)

    @pl.when(j == 0)
    def _():
        _softmax_init(m_ref, accl_ref)
        update_pairs(lambda p: kn_ref[p], lambda p: vn_ref[p], ckn_ref[...], True)

    @pl.when(j > 0)
    def _():
        tile = lambda ref: (lambda p: ref[p * LANES:(p + 1) * LANES, :].astype(BF16))
        update_pairs(tile(kc_ref), tile(vc_ref), ckc_ref[...], False)

    @pl.when(j == pl.num_programs(1) - 1)
    def _():
        for p in range(N_PAIRS):
            rows = slice(2 * p * n_q, (2 * p + 2) * n_q)
            o_ref[p] = _merge_pair(_softmax_result(accl_ref.at[rows]), n_q).astype(o_ref.dtype)


def _fox_sample(q, kn, vn, cache_k, cache_v, cum, n_b, n_s):
    n_past = cache_k.shape[2]
    n_blk = n_past // TKC
    cq = cum[..., n_past:].transpose(0, 2, 1)
    ckn = cum[..., n_past:]
    ckc = cum[..., :n_past]
    new_spec = pl.BlockSpec((N_PAIRS, n_s, LANES), lambda b, j: (0, b, 0))
    cache_blk = lambda j: jnp.maximum(j - 1, 0)
    cache_spec = pl.BlockSpec((None, WIDTH, TKC), lambda b, j: (b, 0, cache_blk(j)))
    return pl.pallas_call(
        _fox_sample_body,
        grid=(n_b, n_blk + 1),
        in_specs=[new_spec, new_spec, new_spec, cache_spec, cache_spec,
                  pl.BlockSpec((None, n_s, HEADS), lambda b, j: (b, 0, 0)),
                  pl.BlockSpec((None, HEADS, n_s), lambda b, j: (b, 0, 0)),
                  pl.BlockSpec((None, HEADS, TKC), lambda b, j: (b, 0, cache_blk(j)))],
        out_specs=new_spec,
        out_shape=jax.ShapeDtypeStruct((N_PAIRS, n_b * n_s, LANES), BF16),
        scratch_shapes=[pltpu.VMEM((HEADS * n_s, LANES), F32), pltpu.VMEM((HEADS * n_s, 2 * LANES), F32)],
        compiler_params=_params("parallel", "arbitrary"),
        name="fox_sample",
    )(q, kn, vn, cache_k, cache_v, cq, ckn, ckc)


def _cumsum_body(x_ref, o_ref):
    n_rows, n_cols = x_ref.shape
    r = lax.broadcasted_iota(jnp.int32, (LANES, LANES), 0)
    c = lax.broadcasted_iota(jnp.int32, (LANES, LANES), 1)
    ones = jnp.where(r <= c, 1.0, 0.0).astype(BF16)
    total = jnp.zeros((n_rows, 1), F32)
    for g in range(n_cols // LANES):
        x = x_ref[:, g * LANES:(g + 1) * LANES]
        h1 = x.astype(BF16)
        r1 = x - h1.astype(F32)
        h2 = r1.astype(BF16)
        h3 = (r1 - h2.astype(F32)).astype(BF16)
        y = _dot(h1, ones) + _dot(h2, ones) + _dot(h3, ones) + total
        o_ref[:, g * LANES:(g + 1) * LANES] = y * LOG2E
        total = y[:, LANES - 1:LANES]


def _cumsum_rows(x):
    rows, n = x.shape
    n_pad = -(-n // LANES) * LANES
    xp = jnp.pad(x, ((0, 0), (0, n_pad - n)))
    out = pl.pallas_call(
        _cumsum_body,
        out_shape=jax.ShapeDtypeStruct((rows, n_pad), F32),
        compiler_params=pltpu.CompilerParams(vmem_limit_bytes=VMEM_LIMIT),
        name="cumsum_rows",
    )(xp)
    return out[:, :n]


def _split_cols(w, sizes):
    out, off = [], 0
    for n in sizes:
        out.append(w[:, off:off + n])
        off += n
    return out


def _rope_tables(pos, n_rows):
    half = A_ROPE // 2
    inv_freq = ROPE_THETA ** (-jnp.arange(half, dtype=F32) / half)
    ang = pos.astype(F32)[:, None] * inv_freq[None, :]
    cos, sin = jnp.cos(ang), jnp.sin(ang)
    zeros = jnp.zeros((pos.shape[0], LANES - A_ROPE), F32)
    cos_t = jnp.concatenate([cos, cos, zeros], axis=1)
    sin_t = jnp.concatenate([-sin, sin, zeros], axis=1)
    reps = max(1, n_rows // pos.shape[0])
    return jnp.tile(cos_t, (reps, 1)), jnp.tile(sin_t, (reps, 1))


def _swap_halves(w):
    half = w.shape[-1] // 2
    return jnp.concatenate([w[..., half:], w[..., :half]], axis=-1)


def _pad_lanes(w):
    return jnp.pad(w, [(0, 0)] * (w.ndim - 1) + [(0, LANES - w.shape[-1])])


def _prep_even(w_in, q_norm, w_uq, kv_norm, w_uk, w_uv):
    wqa, wkv, wkr, wga, wqb, wkb, wvb, wgb = _split_cols(
        w_in, (A_Q_LORA, A_KV_LORA, A_ROPE, WIDTH, WIDTH, WIDTH, WIDTH, WIDTH))
    b = lambda a: a.astype(BF16)
    uq_rope = w_uq[:, :, A_NOPE:]
    uk_t = jnp.transpose(w_uk, (1, 2, 0))
    z = jnp.zeros((A_NOPE, A_KV_LORA), w_uk.dtype)
    wuk = jnp.stack([jnp.block([[uk_t[2 * p], z], [z, uk_t[2 * p + 1]]]) for p in range(N_PAIRS)])
    uv_t = jnp.transpose(w_uv, (1, 0, 2))
    zv = jnp.zeros((A_KV_LORA, A_V), w_uv.dtype)
    wuv = jnp.stack([jnp.block([[uv_t[2 * p], zv], [zv, uv_t[2 * p + 1]]]) for p in range(N_PAIRS)])
    return dict(
        wqa=b(wqa), wkv=b(wkv), wkr=b(_pad_lanes(wkr)), wkrs=b(_pad_lanes(_swap_halves(wkr))),
        wga=b(wga), wqb=b(wqb), wkb=b(wkb), wvb=b(wvb), wgb=b(wgb),
        qn=q_norm.reshape(1, -1), kvn=kv_norm.reshape(1, -1),
        wuqn=b(w_uq[:, :, :A_NOPE].reshape(A_Q_LORA, A_HEADS * A_NOPE)),
        wuqr=b(_pad_lanes(uq_rope).reshape(A_Q_LORA, A_HEADS * LANES)),
        wuqrs=b(_pad_lanes(_swap_halves(uq_rope)).reshape(A_Q_LORA, A_HEADS * LANES)),
        wuk=b(wuk), wuv=b(wuv))


def _prep_odd(w_in, forget_bias):
    wqc, wkc, wvc, wgc, wqd, wkd, wvd, wf, wgd = _split_cols(
        w_in, (WIDTH, WIDTH, WIDTH, WIDTH, WIDTH, WIDTH, WIDTH, HEADS, WIDTH))
    b = lambda a: a.astype(BF16)
    return dict(wqc=b(wqc), wkc=b(wkc), wvc=b(wvc), wgc=b(wgc), wqd=b(wqd), wkd=b(wkd), wvd=b(wvd),
                wf=b(_pad_lanes(wf)), wgd=b(wgd), fb=forget_bias.astype(F32).reshape(1, HEADS),
                wf_t=b(wf.T), fb_t=forget_bias.astype(F32).reshape(HEADS, 1))


def kernel(x_prompt, x_sample, cache_mla_ckv, cache_mla_krope, cache_sb_k, cache_sb_v, cache_band_k,
           cache_band_v, cache_fox_k, cache_fox_v, cache_fox_logf, norm_pre, norm_post, w_in_even,
           a_q_norm, a_w_uq, a_kv_norm, a_w_uk, a_w_uv, w_out_even, w_in_odd, c_rel_bias,
           d_forget_bias, w_out_odd):
    n_b, n_s, _ = x_prompt.shape
    d_b, d_s, _ = x_sample.shape
    n_past = cache_sb_k.shape[2]
    n_keep = cache_band_k.shape[2]
    assert n_s % (2 * TQ) == 0 and n_past % TKC == 0 and d_s == CHUNK and n_past % CHUNK == 0
    assert n_keep == LEFT_CTX and (d_b * d_s) % TM == 0 and TM % d_s == 0

    xp = x_prompt.reshape(n_b * n_s, D_MODEL)
    xs = x_sample.reshape(d_b * d_s, D_MODEL)
    row = lambda a: a.reshape(1, -1)
    heads = lambda a, b, s: a.reshape(b, s, HEADS, HEAD_DIM)
    rows_minor = lambda a: a.transpose(0, 2, 3, 1).reshape(a.shape[0], WIDTH, a.shape[1])

    we = _prep_even(w_in_even[0], a_q_norm[0], a_w_uq[0], a_kv_norm[0], a_w_uk[0], a_w_uv[0])
    wout_e = w_out_even[0].astype(BF16)
    cos_p, sin_p = _rope_tables(jnp.arange(n_s), TM)
    cos_s, sin_s = _rope_tables(n_past + jnp.arange(d_s), TM)

    (ckv_p, kr_p, k2_p, q2_p, ga_p, gb_p, qb_p, kb_p, kb16_p, vb_p, vb16_p) = _in_even(
        xp, row(norm_pre[0]), cos_p, sin_p, we, seq_len=n_s)
    (ckv_s, kr_s, k2_s, q2_s, ga_s, gb_s, qb_s, kb_s, kb16_s, vb_s, vb16_s) = _in_even(
        xs, row(norm_pre[0]), cos_s, sin_s, we)

    lat_p = _mla_prompt(q2_p, k2_p, n_b, n_s).reshape(n_b * n_s, A_HEADS * A_KV_LORA)
    lat_s = _mla_sample(q2_s, k2_s, cache_mla_ckv[0], cache_mla_krope[0].transpose(0, 2, 1), d_b, d_s
                        ).reshape(d_b * d_s, A_HEADS * A_KV_LORA)
    sb_p = _sb_prompt(qb_p, kb16_p, vb16_p, n_b, n_s)
    sb_s = _sb_sample(qb_s, kb16_s, vb16_s, rows_minor(cache_sb_k[0]), rows_minor(cache_sb_v[0]), d_b, d_s)

    xp1 = _out_proj(xp, row(norm_post[0]), ga_p, gb_p, lat_p, sb_p, wout_e, we['wuv'])
    xs1 = _out_proj(xs, row(norm_post[0]), ga_s, gb_s, lat_s, sb_s, wout_e, we['wuv'])

    wo = _prep_odd(w_in_odd[0], d_forget_bias[0])
    wout_o = w_out_odd[0].astype(BF16)
    (qc_p, kc_p, kc16_p, vc_p, vc16_p, gc_p, qd_p, kd_p, kd16_p, vd_p, vd16_p, lf_p, gd_p) = _in_odd(
        xp1, row(norm_pre[1]), wo, seq_len=n_s)
    (qc_s, kc_s, kc16_s, vc_s, vc16_s, gc_s, qd_s, kd_s, kd16_s, vd_s, vd16_s, lf_s, gd_s) = _in_odd(
        xs1, row(norm_pre[1]), wo)

    band_p = _band_prompt(qc_p, kc16_p, vc16_p, _band_bias(c_rel_bias[0], BAND_TQ), n_b, n_s)
    band_s = _band_sample(qc_s, kc16_s, vc16_s, rows_minor(cache_band_k[0]), rows_minor(cache_band_v[0]),
                          _band_bias(c_rel_bias[0], d_s), d_b, d_s)

    lf_s3 = lf_s.reshape(d_b, d_s, HEADS)
    cum_p = _cumsum_rows(lf_p.reshape(n_b * HEADS, n_s)).reshape(n_b, HEADS, n_s)
    lf_all = jnp.concatenate([cache_fox_logf[0].astype(F32), lf_s3], axis=1)
    cum_s = _cumsum_rows(lf_all.transpose(0, 2, 1).reshape(d_b * HEADS, n_past + d_s)
                         ).reshape(d_b, HEADS, n_past + d_s)
    fox_p = _fox_prompt(qd_p, kd16_p, vd16_p, cum_p, n_b, n_s)
    fox_s = _fox_sample(qd_s, kd16_s, vd16_s, rows_minor(cache_fox_k[0]), rows_minor(cache_fox_v[0]), cum_s,
                        d_b, d_s)

    xp2 = _out_proj(xp1, row(norm_post[1]), gc_p, gd_p, band_p, fox_p, wout_o)
    xs2 = _out_proj(xs1, row(norm_post[1]), gc_s, gd_s, band_s, fox_s, wout_o)

    keep = min(LEFT_CTX, n_s)
    band_k_s = jnp.concatenate([cache_band_k[0], heads(kc_s, d_b, d_s)], axis=1)[:, d_s:]
    band_v_s = jnp.concatenate([cache_band_v[0], heads(vc_s, d_b, d_s)], axis=1)[:, d_s:]
    one = lambda a: a[None]
    heads_t = lambda a: a.reshape(n_b, HEADS, HEAD_DIM, a.shape[-1]).transpose(0, 3, 1, 2)
    return (xp2.reshape(n_b, n_s, D_MODEL), xs2.reshape(d_b, d_s, D_MODEL),
            one(ckv_p.reshape(n_b, n_s, A_KV_LORA)), one(kr_p.transpose(0, 2, 1)),
            one(heads_t(kb_p)), one(heads_t(vb_p)),
            one(heads_t(kc_p[:, :, n_s - keep:])), one(heads_t(vc_p[:, :, n_s - keep:])),
            one(heads_t(kd_p)), one(heads_t(vd_p)), one(lf_p.transpose(0, 2, 1)),
            one(ckv_s.reshape(d_b, d_s, A_KV_LORA)), one(kr_s.reshape(d_b, d_s, A_ROPE)),
            one(heads(kb_s, d_b, d_s)), one(heads(vb_s, d_b, d_s)),
            one(band_k_s), one(band_v_s),
            one(heads(kd_s, d_b, d_s)), one(heads(vd_s, d_b, d_s)), one(lf_s3))
```

```python
import functools

import numpy as np
import jax
import jax.numpy as jnp
from jax import lax
from jax.experimental import pallas as pl
from jax.experimental.pallas import tpu as pltpu

F32 = jnp.float32
BF16 = jnp.bfloat16

D_MODEL = 1024
PAST_LEN = 4096
CHUNK = 64
LEFT_CTX = 512
REL_CLIP = 128
EPS = 1e-6
NEG = -1e30
ROPE_THETA = 10000.0
A_HEADS = 8
A_Q_LORA = 256
A_KV_LORA = 128
A_NOPE = 64
A_ROPE = 32
A_V = 64
A_SCALE = (A_NOPE + A_ROPE) ** -0.5
HEADS = 8
HEAD_DIM = 64
WIDTH = HEADS * HEAD_DIM
QK_SCALE = HEAD_DIM ** -0.5
LOG2E = 1.4426950408889634
MLA_LOGIT_SCALE = A_SCALE * LOG2E
N_PAIRS = HEADS // 2

LANES = 128
VMEM_LIMIT = 52 * 1024 * 1024
TM = 512
TQ = 256
TK = 256
TKC = 1024
SB_T = 256
SB_TK = 256
FOX_TQ = 512
FOX_TK = 1024
BAND_TQ = 256
MLA_RT = 256
MLA_QB = 512
PAIRS_PER_STEP = 4
STICK_DEAD = -104.0


def _dot(a, b):
    return jnp.dot(a, b, preferred_element_type=F32)


def _dot_nt(a, b):
    return lax.dot_general(a, b, (((1,), (1,)), ((), ())), preferred_element_type=F32)


def _rms(x, g):
    y = x * lax.rsqrt(jnp.mean(x * x, axis=-1, keepdims=True) + EPS)
    return y * g


def _log_sigmoid(z):
    return jnp.minimum(z, 0.0) - jnp.log(1.0 + jnp.exp(-jnp.abs(z)))


def _silu(g):
    return g / (1.0 + jnp.exp(-g))


def _stack_pair(q2):
    qf = q2.astype(F32)
    lane = lax.broadcasted_iota(jnp.int32, qf.shape, 1)
    even = jnp.where(lane < HEAD_DIM, qf, 0.0)
    odd = jnp.where(lane >= HEAD_DIM, qf, 0.0)
    return jnp.concatenate([even, odd], axis=0).astype(BF16)


def _merge_pair(o, tq):
    top, bot = o[:tq], o[tq:]
    lane = lax.broadcasted_iota(jnp.int32, top.shape, 1)
    return jnp.where(lane < HEAD_DIM, top, bot)


def _params(*sem):
    return pltpu.CompilerParams(dimension_semantics=sem, vmem_limit_bytes=VMEM_LIMIT)


def _const_spec(shape):
    nd = len(shape)
    return pl.BlockSpec(shape, lambda *_: (0,) * nd)


def _in_even_body(x_ref, gpre_ref, cos_ref, sin_ref, wqa_ref, wkv_ref, wkr_ref, wkrs_ref, wga_ref,
                  wqb_ref, wkb_ref, wvb_ref, wgb_ref, qn_ref, kvn_ref, wuqn_ref, wuqr_ref,
                  wuqrs_ref, wuk_ref,
                  ckv_ref, krope_ref, k2_ref, q2_ref, ga_ref, gb_ref, qb_ref, kb_ref, kb16_ref,
                  vb_ref, vb16_ref, *, rows_minor):
    h = _rms(x_ref[...], gpre_ref[...]).astype(BF16)
    cos = cos_ref[...]
    sin = sin_ref[...]
    ckv = _rms(_dot(h, wkv_ref[...]), kvn_ref[...])
    ckv_ref[...] = ckv
    kr = _dot(h, wkr_ref[...]) * cos + _dot(h, wkrs_ref[...]) * sin
    krope_ref[...] = kr.T[:A_ROPE] if rows_minor else kr[:, :A_ROPE]
    k2_ref[:, :LANES] = ckv.astype(BF16)
    k2_ref[:, LANES:] = kr.astype(BF16)
    cq = _rms(_dot(h, wqa_ref[...]), qn_ref[...]).astype(BF16)
    qn = _dot(cq, wuqn_ref[...]).astype(BF16)
    for p in range(N_PAIRS):
        ql = _dot(qn[:, p * LANES:(p + 1) * LANES], wuk_ref[p])
        q2_ref[:, (2 * p) * 256:(2 * p) * 256 + LANES] = ql[:, :LANES].astype(BF16)
        q2_ref[:, (2 * p + 1) * 256:(2 * p + 1) * 256 + LANES] = ql[:, LANES:].astype(BF16)
    qr = _dot(cq, wuqr_ref[...])
    qrs = _dot(cq, wuqrs_ref[...])
    for hd in range(A_HEADS):
        rot = qr[:, hd * LANES:(hd + 1) * LANES] * cos + qrs[:, hd * LANES:(hd + 1) * LANES] * sin
        q2_ref[:, hd * 256 + LANES:(hd + 1) * 256] = rot.astype(BF16)
    ga_ref[...] = _dot(h, wga_ref[...]).astype(BF16)
    gb_ref[...] = _dot(h, wgb_ref[...]).astype(BF16)
    qb = _dot(h, wqb_ref[...]) * QK_SCALE
    kb = _dot(h, wkb_ref[...])
    vb = _dot(h, wvb_ref[...])
    kb_ref[...] = kb.T if rows_minor else kb
    vb_ref[...] = vb.T if rows_minor else vb
    for p in range(N_PAIRS):
        sl = slice(p * LANES, (p + 1) * LANES)
        qb_ref[p] = qb[:, sl].astype(BF16)
        kb16_ref[p] = kb[:, sl].astype(BF16)
        vb16_ref[p] = vb[:, sl].astype(BF16)


def _state_specs(rows, seq_len):
    sds = jax.ShapeDtypeStruct
    if seq_len is None:
        return (lambda n: pl.BlockSpec((TM, n), lambda i: (i, 0))), (lambda n: sds((rows, n), F32))
    nt = seq_len // TM
    return ((lambda n: pl.BlockSpec((None, n, TM), lambda i: (i // nt, 0, i % nt))),
            (lambda n: sds((rows // seq_len, n, seq_len), F32)))


def _in_even(x, gpre, cos, sin, w, seq_len=None):
    rows = x.shape[0]
    n_tab = cos.shape[0] // TM
    row_spec = lambda n: pl.BlockSpec((TM, n), lambda i: (i, 0))
    pm_spec = pl.BlockSpec((N_PAIRS, TM, LANES), lambda i: (0, i, 0))
    tab_spec = pl.BlockSpec((TM, LANES), lambda i: (i % n_tab, 0))
    weights = [w['wqa'], w['wkv'], w['wkr'], w['wkrs'], w['wga'], w['wqb'], w['wkb'], w['wvb'], w['wgb'],
               w['qn'], w['kvn'], w['wuqn'], w['wuqr'], w['wuqrs'], w['wuk']]
    in_specs = ([row_spec(D_MODEL), _const_spec((1, D_MODEL)), tab_spec, tab_spec]
                + [_const_spec(a.shape) for a in weights])
    args = [x, gpre, cos, sin, *weights]
    st_spec, st_shape = _state_specs(rows, seq_len)
    sds = jax.ShapeDtypeStruct
    pm = sds((N_PAIRS, rows, LANES), BF16)
    return pl.pallas_call(
        functools.partial(_in_even_body, rows_minor=seq_len is not None),
        grid=(rows // TM,),
        in_specs=in_specs,
        out_specs=[row_spec(A_KV_LORA), st_spec(A_ROPE), row_spec(256), row_spec(A_HEADS * 256),
                   row_spec(WIDTH), row_spec(WIDTH), pm_spec, st_spec(WIDTH), pm_spec,
                   st_spec(WIDTH), pm_spec],
        out_shape=[sds((rows, A_KV_LORA), F32), st_shape(A_ROPE), sds((rows, 256), BF16),
                   sds((rows, A_HEADS * 256), BF16), sds((rows, WIDTH), BF16), sds((rows, WIDTH), BF16),
                   pm, st_shape(WIDTH), pm, st_shape(WIDTH), pm],
        compiler_params=_params("parallel"),
        name="in_proj_even",
    )(*args)


def _in_odd_body(*refs, rows_minor):
    (x_ref, gpre_ref, fb_ref, wqc_ref, wkc_ref, wvc_ref, wgc_ref, wqd_ref, wkd_ref, wvd_ref, wf_ref,
     wgd_ref) = refs[:12]
    n_in = 14 if rows_minor else 12
    (qc_ref, kc_ref, kc16_ref, vc_ref, vc16_ref, gc_ref, qd_ref, kd_ref, kd16_ref, vd_ref, vd16_ref,
     logf_ref, gd_ref) = refs[n_in:]
    h = _rms(x_ref[...], gpre_ref[...]).astype(BF16)
    gc_ref[...] = _dot(h, wgc_ref[...]).astype(BF16)
    gd_ref[...] = _dot(h, wgd_ref[...]).astype(BF16)
    if rows_minor:
        wf_t, fb_t = refs[12:14]
        logf_ref[...] = _log_sigmoid(_dot_nt(wf_t[...], h) + fb_t[...])
    else:
        logf_ref[...] = _log_sigmoid(_dot(h, wf_ref[...])[:, :HEADS] + fb_ref[...])
    for q_w, k_w, v_w, q_o, k_o, k16_o, v_o, v16_o in (
            (wqc_ref, wkc_ref, wvc_ref, qc_ref, kc_ref, kc16_ref, vc_ref, vc16_ref),
            (wqd_ref, wkd_ref, wvd_ref, qd_ref, kd_ref, kd16_ref, vd_ref, vd16_ref)):
        q = _dot(h, q_w[...]) * (QK_SCALE * LOG2E)
        k = _dot(h, k_w[...])
        v = _dot(h, v_w[...])
        k_o[...] = k.T if rows_minor else k
        v_o[...] = v.T if rows_minor else v
        for p in range(N_PAIRS):
            sl = slice(p * LANES, (p + 1) * LANES)
            q_o[p] = q[:, sl].astype(BF16)
            k16_o[p] = k[:, sl].astype(BF16)
            v16_o[p] = v[:, sl].astype(BF16)


def _in_odd(x, gpre, w, seq_len=None):
    rows = x.shape[0]
    row_spec = lambda n: pl.BlockSpec((TM, n), lambda i: (i, 0))
    pm_spec = pl.BlockSpec((N_PAIRS, TM, LANES), lambda i: (0, i, 0))
    weights = [w['wqc'], w['wkc'], w['wvc'], w['wgc'], w['wqd'], w['wkd'], w['wvd'], w['wf'], w['wgd']]
    rows_minor = seq_len is not None
    if rows_minor:
        weights += [w['wf_t'], w['fb_t']]
    st_spec, st_shape = _state_specs(rows, seq_len)
    sds = jax.ShapeDtypeStruct
    pm = sds((N_PAIRS, rows, LANES), BF16)
    full = sds((rows, WIDTH), BF16)
    return pl.pallas_call(
        functools.partial(_in_odd_body, rows_minor=rows_minor),
        grid=(rows // TM,),
        in_specs=[row_spec(D_MODEL), _const_spec((1, D_MODEL)), _const_spec((1, HEADS))]
                 + [_const_spec(a.shape) for a in weights],
        out_specs=[pm_spec, st_spec(WIDTH), pm_spec, st_spec(WIDTH), pm_spec, row_spec(WIDTH),
                   pm_spec, st_spec(WIDTH), pm_spec, st_spec(WIDTH), pm_spec, st_spec(HEADS),
                   row_spec(WIDTH)],
        out_shape=[pm, st_shape(WIDTH), pm, st_shape(WIDTH), pm, full, pm, st_shape(WIDTH), pm,
                   st_shape(WIDTH), pm, st_shape(HEADS), full],
        compiler_params=_params("parallel"),
        name="in_proj_odd",
    )(x, gpre, w['fb'], *weights)


def _out_body(*refs, mla):
    if mla:
        x_ref, gpost_ref, g1_ref, g2_ref, a_ref, b_ref, wuv_ref, wout_ref, o_ref, mix_ref = refs
    else:
        x_ref, gpost_ref, g1_ref, g2_ref, a_ref, b_ref, wout_ref, o_ref, mix_ref = refs
    s1 = _silu(g1_ref[...].astype(F32))
    s2 = _silu(g2_ref[...].astype(F32))
    for p in range(N_PAIRS):
        sl = slice(p * LANES, (p + 1) * LANES)
        if mla:
            a = _dot(a_ref[:, p * 256:(p + 1) * 256], wuv_ref[p])
        else:
            a = a_ref[p].astype(F32)
        mix_ref[:, sl] = (s1[:, sl] * a).astype(BF16)
        mix_ref[:, WIDTH + p * LANES:WIDTH + (p + 1) * LANES] = (
            s2[:, sl] * b_ref[p].astype(F32)).astype(BF16)
    y = _dot(mix_ref[...], wout_ref[...])
    o_ref[...] = x_ref[...] + _rms(y, gpost_ref[...])


def _out_proj(x, gpost, g1, g2, a, b, wout, wuv=None):
    rows = x.shape[0]
    mla = wuv is not None
    row_spec = lambda n: pl.BlockSpec((TM, n), lambda i: (i, 0))
    pm_spec = pl.BlockSpec((N_PAIRS, TM, LANES), lambda i: (0, i, 0))
    in_specs = [row_spec(D_MODEL), _const_spec((1, D_MODEL)), row_spec(WIDTH), row_spec(WIDTH),
                row_spec(A_HEADS * A_KV_LORA) if mla else pm_spec, pm_spec]
    args = [x, gpost, g1, g2, a, b]
    if mla:
        in_specs.append(_const_spec(wuv.shape))
        args.append(wuv)
    in_specs.append(_const_spec(wout.shape))
    args.append(wout)
    return pl.pallas_call(
        functools.partial(_out_body, mla=mla),
        grid=(rows // TM,),
        in_specs=in_specs,
        out_specs=row_spec(D_MODEL),
        out_shape=jax.ShapeDtypeStruct((rows, D_MODEL), F32),
        scratch_shapes=[pltpu.VMEM((TM, 2 * WIDTH), BF16)],
        compiler_params=_params("parallel"),
        name="out_proj_even" if mla else "out_proj_odd",
    )(*args)


def _softmax_init(m_ref, accl_ref):
    m_ref[...] = jnp.full(m_ref.shape, NEG, F32)
    accl_ref[...] = jnp.zeros(accl_ref.shape, F32)


def _lanes(x, n):
    parts = [x] * (n // LANES)
    if n % LANES:
        parts.append(x[:, :n % LANES])
    return parts[0] if len(parts) == 1 else jnp.concatenate(parts, axis=1)


def _scores(q, k, kv_t):
    return _dot(q, k) if kv_t else _dot_nt(q, k)


def _weighted(p, v, kv_t):
    return _dot_nt(p, v) if kv_t else _dot(p, v)


def _with_ones(v, kv_t=False):
    return jnp.concatenate([v, jnp.ones(v.shape, BF16)], axis=0 if kv_t else 1)


def _softmax_update(s, v1, m_ref, accl_ref, kv_t=False):
    keys = s.shape[1]
    n = v1.shape[0] if kv_t else v1.shape[1]
    m_prev = m_ref[...]
    m_new = jnp.maximum(m_prev, jnp.max(s, axis=1, keepdims=True))
    alpha = jnp.exp2(m_prev - m_new)
    p = jnp.exp2(s - _lanes(m_new, keys))
    accl_ref[...] = _lanes(alpha, n) * accl_ref[...] + _weighted(p.astype(BF16), v1, kv_t)
    m_ref[...] = m_new


def _softmax_result(accl_ref):
    accl = accl_ref[...]
    n = accl.shape[1] // 2
    return accl[:, :n] / accl[:, n:]


def _fill_suffix_ones(tri_ref):
    n = tri_ref.shape[0]
    r = lax.broadcasted_iota(jnp.int32, (n, n), 0)
    c = lax.broadcasted_iota(jnp.int32, (n, n), 1)
    tri_ref[...] = jnp.where(r > c, 1.0, 0.0).astype(BF16)


def _stick_tiles(qs, k2, v2, tri, acc_refs, car_refs, mask, kv_t=False):
    n = len(qs)
    z = [_scores(qs[i], k2[i], kv_t) for i in range(n)]
    lb, l1, suf = [], [], []
    for i in range(n):
        lb.append(_log_sigmoid(z[i]))
        l = lb[i] - z[i]
        l1.append(l if mask is None else jnp.where(mask, l, 0.0))
        hi = l1[i].astype(BF16)
        lo = (l1[i] - hi.astype(F32)).astype(BF16)
        suf.append(_dot(hi, tri) + _dot(lo, tri))
    for i in range(n):
        w = jnp.exp(lb[i] + (suf[i] + car_refs[i][...]))
        if mask is not None:
            w = jnp.where(mask, w, 0.0)
        acc_refs[i][...] += _weighted(w.astype(BF16), v2[i], kv_t)
        car_refs[i][...] += jnp.sum(l1[i], axis=1, keepdims=True)


def _stick_alive(car_ref):
    return (jnp.max(car_ref[...]) >= STICK_DEAD).astype(jnp.int32)


def _local_causal(rows, keys, tq, strict, key_offset=0):
    r = lax.broadcasted_iota(jnp.int32, (rows, keys), 0) & (tq - 1)
    c = lax.broadcasted_iota(jnp.int32, (rows, keys), 1) + key_offset
    return (c < r) if strict else (c <= r)


def _decay_tile(s, v2, cq, ck, causal, m_ref, accl_ref, kv_t=False):
    tq, keys = cq.shape[0], ck.shape[1]
    for hh in range(2):
        rows = slice(hh * tq, (hh + 1) * tq)
        sh = s[rows] + (cq[:, hh:hh + 1] - ck[hh:hh + 1])
        if causal:
            sh = jnp.where(_local_causal(tq, keys, tq, strict=False), sh, NEG)
        _softmax_update(sh, _with_ones(v2, kv_t), m_ref.at[rows], accl_ref.at[rows], kv_t)


def _split3(c):
    hi = c.astype(BF16).astype(F32)
    rest = c - hi
    mid = rest.astype(BF16).astype(F32)
    return hi, mid, rest - mid


def _decay_lanes(c, base, query):
    hi, mid, lo = _split3(c if query else -c)
    lane = lax.broadcasted_iota(jnp.int32, (c.shape[0], LANES), 1) - (base if query else base + 3)
    parts = jnp.where(lane == 0, hi, jnp.where(lane == 1, mid, jnp.where(lane == 2, lo, 0.0)))
    ones_at = lane + 3 if not query else lane - 3
    return jnp.where((ones_at >= 0) & (ones_at < 3), 1.0, parts)


def _decay_operand(x, cum, hh, query):
    lane = lax.broadcasted_iota(jnp.int32, x.shape, 1)
    own = (lane < HEAD_DIM) if hh == 0 else (lane >= HEAD_DIM)
    base = HEAD_DIM if hh == 0 else 0
    return jnp.where(own, x, _decay_lanes(cum[:, hh:hh + 1], base, query)).astype(BF16)


def _one_ahead(score_fns, consume_fns):
    ahead = score_fns[0]()
    for i, consume in enumerate(consume_fns):
        s = ahead
        if i + 1 < len(score_fns):
            ahead = score_fns[i + 1]()
        consume(s)


def _decay_head_update(s, v1, causal_offset, m_ref, accl_ref):
    tq, keys = s.shape
    if causal_offset is not None:
        s = jnp.where(_local_causal(tq, keys, tq, strict=False, key_offset=causal_offset), s, NEG)
    _softmax_update(s, v1, m_ref, accl_ref)


def _pair_result(accl, tq):
    o = accl / pltpu.roll(accl, HEAD_DIM, axis=1)
    return _merge_pair(o, tq)


def _mla_prompt_body(q_ref, k_ref, o_ref, m_ref, accl_ref):
    g = pl.program_id(1)
    grp = 4 * CHUNK

    def update_heads(r0, k, mask):
        sl = pl.ds(r0, grp)
        v1 = _with_ones(k[:, :A_KV_LORA])

        def consume(s, h):
            s = s * MLA_LOGIT_SCALE
            if mask is not None:
                s = jnp.where(mask, s, NEG)
            _softmax_update(s, v1, m_ref.at[h, sl], accl_ref.at[h, sl])

        _one_ahead([lambda h=h: _dot_nt(q_ref[sl, h * 256:(h + 1) * 256], k) for h in range(A_HEADS)],
                   [lambda s, h=h: consume(s, h) for h in range(A_HEADS)])

    def group(gl, carry):
        gq = g * (MLA_QB // grp) + gl
        g0 = pl.multiple_of(gl * grp, grp)
        for h in range(A_HEADS):
            _softmax_init(m_ref.at[h, pl.ds(g0, grp)], accl_ref.at[h, pl.ds(g0, grp)])

        def earlier(k0, n_keys):
            update_heads(g0, k_ref[pl.ds(k0, n_keys), :], None)

        def kv(j, c2):
            earlier(pl.multiple_of(j * 2 * TK, 2 * TK), 2 * TK)
            return c2

        lax.fori_loop(0, gq // 2, kv, 0)

        @pl.when(gq % 2 == 1)
        def _():
            earlier(pl.multiple_of((gq - 1) * TK, TK), TK)

        kd = k_ref[pl.ds(pl.multiple_of(gq * TK, TK), TK), :]
        row_chunk = lax.broadcasted_iota(jnp.int32, (grp, TK), 0) // CHUNK
        key_chunk = lax.broadcasted_iota(jnp.int32, (grp, TK), 1) // CHUNK
        update_heads(g0, kd, key_chunk <= row_chunk)
        for h in range(A_HEADS):
            o_ref[pl.ds(g0, grp), h * A_KV_LORA:(h + 1) * A_KV_LORA] = _softmax_result(
                accl_ref.at[h, pl.ds(g0, grp)]).astype(BF16)
        return carry

    lax.fori_loop(0, MLA_QB // grp, group, 0)


def _mla_prompt(q2, k2, n_b, n_s):
    assert n_s % MLA_QB == 0 and TK == 4 * CHUNK
    return pl.pallas_call(
        _mla_prompt_body,
        grid=(n_b, n_s // MLA_QB),
        in_specs=[pl.BlockSpec((None, MLA_QB, A_HEADS * 256), lambda b, g: (b, g, 0)),
                  pl.BlockSpec((None, n_s, 256), lambda b, g: (b, 0, 0))],
        out_specs=pl.BlockSpec((None, MLA_QB, A_HEADS * A_KV_LORA), lambda b, g: (b, g, 0)),
        out_shape=jax.ShapeDtypeStruct((n_b, n_s, A_HEADS * A_KV_LORA), BF16),
        scratch_shapes=[pltpu.VMEM((A_HEADS, MLA_QB, LANES), F32),
                        pltpu.VMEM((A_HEADS, MLA_QB, 2 * LANES), F32)],
        compiler_params=_params("parallel", "parallel"),
        name="mla_prompt",
    )(q2.reshape(n_b, n_s, A_HEADS * 256), k2.reshape(n_b, n_s, 256))


def _mla_sample_body(q_ref, kn_ref, ckv_ref, kr_ref, o_ref, m_ref, accl_ref):
    j = pl.program_id(1)

    @pl.when(j == 0)
    def _():
        _softmax_init(m_ref, accl_ref)

    n_tiles = q_ref.shape[0] // MLA_RT
    ck = ckv_ref[...].astype(BF16)
    kr = kr_ref[...].T.astype(BF16)
    k = jnp.concatenate([ck, kr, jnp.zeros((TKC, LANES - A_ROPE), BF16)], axis=1)
    row_tiles = [slice(t * MLA_RT, (t + 1) * MLA_RT) for t in range(n_tiles)]

    def update_tiles(keys, values):
        scores = [_dot_nt(q_ref[sl, :], keys) for sl in row_tiles]
        v1 = _with_ones(values)
        for sl, s in zip(row_tiles, scores):
            _softmax_update(s * MLA_LOGIT_SCALE, v1, m_ref.at[sl], accl_ref.at[sl])

    update_tiles(k, ck)

    @pl.when(j == pl.num_programs(1) - 1)
    def _():
        kn = kn_ref[...]
        update_tiles(kn, kn[:, :A_KV_LORA])
        o_ref[...] = _softmax_result(accl_ref).astype(BF16)


def _mla_sample(q2, k2, cache_ckv, cache_kr, n_b, n_s):
    rows = n_s * A_HEADS
    n_past = cache_ckv.shape[1]
    return pl.pallas_call(
        _mla_sample_body,
        grid=(n_b, n_past // TKC),
        in_specs=[pl.BlockSpec((None, rows, 256), lambda b, j: (b, 0, 0)),
                  pl.BlockSpec((None, n_s, 256), lambda b, j: (b, 0, 0)),
                  pl.BlockSpec((None, TKC, A_KV_LORA), lambda b, j: (b, j, 0)),
                  pl.BlockSpec((None, A_ROPE, TKC), lambda b, j: (b, 0, j))],
        out_specs=pl.BlockSpec((None, rows, A_KV_LORA), lambda b, j: (b, 0, 0)),
        out_shape=jax.ShapeDtypeStruct((n_b, rows, A_KV_LORA), BF16),
        scratch_shapes=[pltpu.VMEM((rows, LANES), F32), pltpu.VMEM((rows, 2 * LANES), F32)],
        compiler_params=_params("parallel", "arbitrary"),
        name="mla_sample",
    )(q2.reshape(n_b, rows, 256), k2.reshape(n_b, n_s, 256), cache_ckv, cache_kr)


def _sb_prompt_body(q_ref, k_ref, v_ref, o_ref, tri_ref, acc_ref, car_ref):
    n_s = q_ref.shape[1]
    _fill_suffix_ones(tri_ref)

    def qblock(i, carry):
        q0 = pl.multiple_of(i * SB_T, SB_T)
        qs = [_stack_pair(q_ref[g, pl.ds(q0, SB_T), :]) for g in range(PAIRS_PER_STEP)]
        acc_ref[...] = jnp.zeros(acc_ref.shape, F32)
        car_ref[...] = jnp.zeros(car_ref.shape, F32)

        def tiles(k0, n_keys, mask):
            pairs = range(PAIRS_PER_STEP)
            _stick_tiles(qs, [k_ref[g, pl.ds(k0, n_keys), :] for g in pairs],
                         [v_ref[g, pl.ds(k0, n_keys), :] for g in pairs], tri_ref[:n_keys, :n_keys],
                         [acc_ref.at[g] for g in pairs], [car_ref.at[g] for g in pairs], mask)

        tiles(q0, SB_T, _local_causal(2 * SB_T, SB_T, SB_T, strict=True))

        def kv(state):
            jj, _ = state
            tiles(pl.multiple_of(q0 - (jj + 1) * SB_TK, SB_TK), SB_TK, None)
            return jj + 1, _stick_alive(car_ref)

        lax.while_loop(lambda st: (st[0] < i * (SB_T // SB_TK)) & (st[1] > 0), kv,
                       (jnp.int32(0), _stick_alive(car_ref)))
        for g in range(PAIRS_PER_STEP):
            o_ref[g, pl.ds(q0, SB_T), :] = _merge_pair(acc_ref[g], SB_T).astype(o_ref.dtype)
        return carry

    lax.fori_loop(0, n_s // SB_T, qblock, 0)


def _pair_seq_spec(n_s):
    return pl.BlockSpec((PAIRS_PER_STEP, n_s, LANES), lambda b, g: (g, b, 0))


def _sb_prompt(q, k, v, n_b, n_s):
    assert n_s % SB_T == 0
    spec = _pair_seq_spec(n_s)
    return pl.pallas_call(
        _sb_prompt_body,
        grid=(n_b, N_PAIRS // PAIRS_PER_STEP),
        in_specs=[spec, spec, spec],
        out_specs=spec,
        out_shape=jax.ShapeDtypeStruct((N_PAIRS, n_b * n_s, LANES), BF16),
        scratch_shapes=[pltpu.VMEM((SB_T, SB_T), BF16), pltpu.VMEM((PAIRS_PER_STEP, 2 * SB_T, LANES), F32),
                        pltpu.VMEM((PAIRS_PER_STEP, 2 * SB_T, 1), F32)],
        compiler_params=_params("parallel", "parallel"),
        name="sb_prompt",
    )(q, k, v)


def _sb_sample_body(q_ref, kn_ref, vn_ref, ck_hbm, cv_hbm, o_ref, kbuf, vbuf, sem, tri_ref, acc_ref,
                    car_ref):
    b = pl.program_id(0)
    n_q = q_ref.shape[1]
    n_blk = ck_hbm.shape[2] // TK

    def block_copies(blk):
        rows = pl.ds(pl.multiple_of(blk * TK, TK), TK)
        return (pltpu.make_async_copy(ck_hbm.at[b, :, rows], kbuf, sem.at[0]),
                pltpu.make_async_copy(cv_hbm.at[b, :, rows], vbuf, sem.at[1]))

    def start(blk):
        for c in block_copies(blk):
            c.start()

    def cache_block(blk):
        for c in block_copies(blk):
            c.wait()
        tiles = lambda buf: [buf[p * LANES:(p + 1) * LANES, :].astype(BF16) for p in range(N_PAIRS)]
        _stick_tiles(qs, tiles(kbuf), tiles(vbuf), tri_ref[...], accs, cars, None, kv_t=True)

    start(n_blk - 1)
    _fill_suffix_ones(tri_ref)
    acc_ref[...] = jnp.zeros(acc_ref.shape, F32)
    car_ref[...] = jnp.zeros(car_ref.shape, F32)
    qs = [_stack_pair(q_ref[p]) for p in range(N_PAIRS)]
    accs = [acc_ref.at[p] for p in range(N_PAIRS)]
    cars = [car_ref.at[p] for p in range(N_PAIRS)]
    _stick_tiles(qs, [kn_ref[p] for p in range(N_PAIRS)], [vn_ref[p] for p in range(N_PAIRS)],
                 tri_ref[:n_q, :n_q], accs, cars, _local_causal(2 * n_q, n_q, n_q, strict=True))
    cache_block(n_blk - 1)

    def older(state):
        blk, _ = state
        start(blk)
        cache_block(blk)
        return blk - 1, _stick_alive(car_ref)

    lax.while_loop(lambda st: (st[0] >= 0) & (st[1] > 0), older,
                   (jnp.int32(n_blk - 2), _stick_alive(car_ref)))
    for p in range(N_PAIRS):
        o_ref[p] = _merge_pair(acc_ref[p], n_q).astype(o_ref.dtype)


def _sb_sample(q, kn, vn, cache_kt, cache_vt, n_b, n_s):
    new_spec = pl.BlockSpec((N_PAIRS, n_s, LANES), lambda b: (0, b, 0))
    hbm = pl.BlockSpec(memory_space=pl.ANY)
    return pl.pallas_call(
        _sb_sample_body,
        grid=(n_b,),
        in_specs=[new_spec, new_spec, new_spec, hbm, hbm],
        out_specs=new_spec,
        out_shape=jax.ShapeDtypeStruct((N_PAIRS, n_b * n_s, LANES), BF16),
        scratch_shapes=[pltpu.VMEM((WIDTH, TK), F32), pltpu.VMEM((WIDTH, TK), F32),
                        pltpu.SemaphoreType.DMA((2,)), pltpu.VMEM((TK, TK), BF16),
                        pltpu.VMEM((N_PAIRS, 2 * n_s, LANES), F32), pltpu.VMEM((N_PAIRS, 2 * n_s, 1), F32)],
        compiler_params=_params("arbitrary"),
        name="sb_sample",
    )(q, kn, vn, cache_kt, cache_vt)


def _band_block(scores, vwin, bias, valid_from, tq):
    s = scores + bias
    if valid_from is not None:
        col = lax.broadcasted_iota(jnp.int32, s.shape, 1)
        s = jnp.where(col >= valid_from, s, NEG)
    p = jnp.exp2(s - jnp.max(s, axis=1, keepdims=True))
    o = _dot(p.astype(BF16), vwin) / jnp.sum(p, axis=1, keepdims=True)
    return _merge_pair(o, tq)


def _band_prompt_body(q_ref, k_ref, v_ref, bias_ref, o_ref, kpad_ref, vpad_ref):
    n_s = q_ref.shape[1]
    win = LEFT_CTX + BAND_TQ
    zeros = jnp.zeros((PAIRS_PER_STEP, LEFT_CTX, LANES), BF16)
    kpad_ref[:, :LEFT_CTX, :] = zeros
    vpad_ref[:, :LEFT_CTX, :] = zeros
    kpad_ref[:, LEFT_CTX:, :] = k_ref[...]
    vpad_ref[:, LEFT_CTX:, :] = v_ref[...]

    def qblock(i, carry):
        q0 = pl.multiple_of(i * BAND_TQ, BAND_TQ)
        scores = [_dot_nt(_stack_pair(q_ref[g, pl.ds(q0, BAND_TQ), :]), kpad_ref[g, pl.ds(q0, win), :])
                  for g in range(PAIRS_PER_STEP)]
        for g in range(PAIRS_PER_STEP):
            o_ref[g, pl.ds(q0, BAND_TQ), :] = _band_block(
                scores[g], vpad_ref[g, pl.ds(q0, win), :], bias_ref[g], LEFT_CTX - q0, BAND_TQ
            ).astype(o_ref.dtype)
        return carry

    lax.fori_loop(0, n_s // BAND_TQ, qblock, 0)


def _band_prompt(q, k, v, bias, n_b, n_s):
    spec = _pair_seq_spec(n_s)
    win = LEFT_CTX + BAND_TQ
    return pl.pallas_call(
        _band_prompt_body,
        grid=(n_b, N_PAIRS // PAIRS_PER_STEP),
        in_specs=[spec, spec, spec,
                  pl.BlockSpec((PAIRS_PER_STEP, 2 * BAND_TQ, win), lambda b, g: (g, 0, 0))],
        out_specs=spec,
        out_shape=jax.ShapeDtypeStruct((N_PAIRS, n_b * n_s, LANES), BF16),
        scratch_shapes=[pltpu.VMEM((PAIRS_PER_STEP, LEFT_CTX + n_s, LANES), BF16),
                        pltpu.VMEM((PAIRS_PER_STEP, LEFT_CTX + n_s, LANES), BF16)],
        compiler_params=_params("parallel", "parallel"),
        name="band_prompt",
    )(q, k, v, bias)


def _band_sample_body(q_ref, kn_ref, vn_ref, kc_ref, vc_ref, bias_ref, o_ref):
    n_q = q_ref.shape[1]
    n_keep = kc_ref.shape[1]
    pair_lanes = [slice(p * LANES, (p + 1) * LANES) for p in range(N_PAIRS)]
    qs = [_stack_pair(q_ref[p]) for p in range(N_PAIRS)]
    scores_old = [_dot(qs[p], kc_ref[pair_lanes[p], :].astype(BF16)) for p in range(N_PAIRS)]
    scores_new = [_dot_nt(qs[p], kn_ref[p]) for p in range(N_PAIRS)]
    for p in range(N_PAIRS):
        sl = pair_lanes[p]
        bias = bias_ref[p]
        s_old = scores_old[p] + bias[:, :n_keep]
        s_new = scores_new[p] + bias[:, n_keep:]
        m = jnp.maximum(jnp.max(s_old, axis=1, keepdims=True), jnp.max(s_new, axis=1, keepdims=True))
        p_old = jnp.exp2(s_old - m)
        p_new = jnp.exp2(s_new - m)
        o = _dot_nt(p_old.astype(BF16), vc_ref[sl, :].astype(BF16)) + _dot(p_new.astype(BF16), vn_ref[p])
        total = jnp.sum(p_old, axis=1, keepdims=True) + jnp.sum(p_new, axis=1, keepdims=True)
        o_ref[p] = _merge_pair(o / total, n_q).astype(o_ref.dtype)


def _band_sample(q, kn, vn, cache_kt, cache_vt, bias, n_b, n_s):
    n_keep = cache_kt.shape[2]
    new_spec = pl.BlockSpec((N_PAIRS, n_s, LANES), lambda b: (0, b, 0))
    cache_spec = pl.BlockSpec((None, WIDTH, n_keep), lambda b: (b, 0, 0))
    return pl.pallas_call(
        _band_sample_body,
        grid=(n_b,),
        in_specs=[new_spec, new_spec, new_spec, cache_spec, cache_spec,
                  _const_spec((N_PAIRS, 2 * n_s, n_keep + n_s))],
        out_specs=new_spec,
        out_shape=jax.ShapeDtypeStruct((N_PAIRS, n_b * n_s, LANES), BF16),
        compiler_params=_params("parallel"),
        name="band_sample",
    )(q, kn, vn, cache_kt, cache_vt, bias)


def _band_bias(rel_bias, tq):
    win = LEFT_CTX + tq
    i = np.arange(tq)[:, None]
    w = np.arange(win)[None, :]
    qc, kc = i // CHUNK, w // CHUNK - LEFT_CTX // CHUNK
    ok = (kc <= qc) & (kc >= qc - LEFT_CTX // CHUNK)
    u = np.arange(win + tq - 1)
    rel = np.clip(LEFT_CTX + (tq - 1) - u, -REL_CLIP, REL_CLIP) + REL_CLIP
    diag = rel_bias.astype(F32)[:, rel]
    n = win + tq - 1
    flat = jnp.tile(diag, (1, tq))[:, tq - 1:tq - 1 + tq * (n - 1)]
    tab = flat.reshape(HEADS, tq, n - 1)[:, :, :win]
    tab = jnp.where(jnp.asarray(ok)[None], tab * LOG2E, NEG)
    return tab.reshape(N_PAIRS, 2 * tq, win)


def _fox_prompt_body(q_ref, k_ref, v_ref, cum_ref, o_ref, k1_ref, v1_ref, m_ref, accl_ref):
    n_s = q_ref.shape[1]

    def prepare(jb, carry):
        rows = pl.ds(pl.multiple_of(jb * TK, TK), TK)
        lane = lax.broadcasted_iota(jnp.int32, (TK, LANES), 1)
        for g in range(PAIRS_PER_STEP):
            k = k_ref[g, rows, :].astype(F32)
            v = v_ref[g, rows, :].astype(F32)
            cum = cum_ref[g, rows, :]
            for hh in range(2):
                k1_ref[g, hh, rows, :] = _decay_operand(k, cum, hh, query=False)
                own = (lane < HEAD_DIM) if hh == 0 else (lane >= HEAD_DIM)
                v1_ref[g, hh, rows, :] = jnp.where(own, v, 1.0).astype(BF16)
        return carry

    lax.fori_loop(0, n_s // TK, prepare, 0)

    tiles_per_q = FOX_TQ // TK

    def qblock(i, carry):
        q0 = pl.multiple_of(i * FOX_TQ, FOX_TQ)
        q1 = []
        for g in range(PAIRS_PER_STEP):
            q = q_ref[g, pl.ds(q0, FOX_TQ), :].astype(F32)
            cum = cum_ref[g, pl.ds(q0, FOX_TQ), :]
            q1.append([_decay_operand(q, cum, hh, query=True) for hh in range(2)])
        _softmax_init(m_ref, accl_ref)

        def tile(jb, causal_offset):
            rows = pl.ds(pl.multiple_of(jb * TK, TK), TK)
            chains = [(g, hh) for g in range(PAIRS_PER_STEP) for hh in range(2)]
            head_rows = lambda hh: slice(hh * FOX_TQ, (hh + 1) * FOX_TQ)
            _one_ahead(
                [lambda g=g, hh=hh: _dot_nt(q1[g][hh], k1_ref[g, hh, rows, :]) for g, hh in chains],
                [lambda s, g=g, hh=hh: _decay_head_update(
                    s, v1_ref[g, hh, rows, :], causal_offset, m_ref.at[g, head_rows(hh)],
                    accl_ref.at[g, head_rows(hh)]) for g, hh in chains])

        for t in range(tiles_per_q):
            tile(i * tiles_per_q + t, t * TK)

        def kv(jb, c2):
            tile(jb, None)
            return c2

        lax.fori_loop(0, i * tiles_per_q, kv, 0)
        for g in range(PAIRS_PER_STEP):
            o_ref[g, pl.ds(q0, FOX_TQ), :] = _pair_result(accl_ref[g], FOX_TQ).astype(o_ref.dtype)
        return carry

    lax.fori_loop(0, n_s // FOX_TQ, qblock, 0)


def _fox_prompt(q, k, v, cum, n_b, n_s):
    assert FOX_TQ % TK == 0 and n_s % FOX_TQ == 0
    spec = _pair_seq_spec(n_s)
    cum_rows = cum.reshape(n_b, N_PAIRS, 2, n_s).transpose(0, 1, 3, 2)
    g_ = PAIRS_PER_STEP
    return pl.pallas_call(
        _fox_prompt_body,
        grid=(n_b, N_PAIRS // g_),
        in_specs=[spec, spec, spec, pl.BlockSpec((None, g_, n_s, 2), lambda b, g: (b, g, 0, 0))],
        out_specs=spec,
        out_shape=jax.ShapeDtypeStruct((N_PAIRS, n_b * n_s, LANES), BF16),
        scratch_shapes=[pltpu.VMEM((g_, 2, n_s, LANES), BF16), pltpu.VMEM((g_, 2, n_s, LANES), BF16),
                        pltpu.VMEM((g_, 2 * FOX_TQ, LANES), F32), pltpu.VMEM((g_, 2 * FOX_TQ, LANES), F32)],
        compiler_params=_params("parallel", "parallel"),
        name="fox_prompt",
    )(q, k, v, cum_rows)


def _fox_sample_body(q_ref, kn_ref, vn_ref, kc_ref, vc_ref, cq_ref, ckn_ref, ckc_ref, o_ref,
                     m_ref, accl_ref):
    j = pl.program_id(1)
    n_q = q_ref.shape[1]
    cq = cq_ref[...]

    def update_pairs(k_of, v_of, ck, new_rows):
        scores = [_scores(_stack_pair(q_ref[p]), k_of(p), not new_rows) for p in range(N_PAIRS)]
        for p in range(N_PAIRS):
            rows = slice(2 * p * n_q, (2 * p + 2) * n_q)
            _decay_tile(scores[p], v_of(p), cq[:, 2 * p:2 * p + 2], ck[2 * p:2 * p + 2], new_rows,
                        m_ref.at[rows], accl_ref.at[rows], kv_t=not new_rows)

    @pl.when(j == 0)
    def _():
        _softmax_init(m_ref, accl_ref)
        update_pairs(lambda p: kn_ref[p], lambda p: vn_ref[p], ckn_ref[...], True)

    @pl.when(j > 0)
    def _():
        tile = lambda ref: (lambda p: ref[p * LANES:(p + 1) * LANES, :].astype(BF16))
        update_pairs(tile(kc_ref), tile(vc_ref), ckc_ref[...], False)

    @pl.when(j == pl.num_programs(1) - 1)
    def _():
        for p in range(N_PAIRS):
            rows = slice(2 * p * n_q, (2 * p + 2) * n_q)
            o_ref[p] = _merge_pair(_softmax_result(accl_ref.at[rows]), n_q).astype(o_ref.dtype)


def _fox_sample(q, kn, vn, cache_k, cache_v, cum, n_b, n_s):
    n_past = cache_k.shape[2]
    n_blk = n_past // TKC
    cq = cum[..., n_past:].transpose(0, 2, 1)
    ckn = cum[..., n_past:]
    ckc = cum[..., :n_past]
    new_spec = pl.BlockSpec((N_PAIRS, n_s, LANES), lambda b, j: (0, b, 0))
    cache_blk = lambda j: jnp.maximum(j - 1, 0)
    cache_spec = pl.BlockSpec((None, WIDTH, TKC), lambda b, j: (b, 0, cache_blk(j)))
    return pl.pallas_call(
        _fox_sample_body,
        grid=(n_b, n_blk + 1),
        in_specs=[new_spec, new_spec, new_spec, cache_spec, cache_spec,
                  pl.BlockSpec((None, n_s, HEADS), lambda b, j: (b, 0, 0)),
                  pl.BlockSpec((None, HEADS, n_s), lambda b, j: (b, 0, 0)),
                  pl.BlockSpec((None, HEADS, TKC), lambda b, j: (b, 0, cache_blk(j)))],
        out_specs=new_spec,
        out_shape=jax.ShapeDtypeStruct((N_PAIRS, n_b * n_s, LANES), BF16),
        scratch_shapes=[pltpu.VMEM((HEADS * n_s, LANES), F32), pltpu.VMEM((HEADS * n_s, 2 * LANES), F32)],
        compiler_params=_params("parallel", "arbitrary"),
        name="fox_sample",
    )(q, kn, vn, cache_k, cache_v, cq, ckn, ckc)


def _cumsum_body(x_ref, o_ref):
    n_rows, n_cols = x_ref.shape
    r = lax.broadcasted_iota(jnp.int32, (LANES, LANES), 0)
    c = lax.broadcasted_iota(jnp.int32, (LANES, LANES), 1)
    ones = jnp.where(r <= c, 1.0, 0.0).astype(BF16)
    total = jnp.zeros((n_rows, 1), F32)
    for g in range(n_cols // LANES):
        x = x_ref[:, g * LANES:(g + 1) * LANES]
        h1 = x.astype(BF16)
        r1 = x - h1.astype(F32)
        h2 = r1.astype(BF16)
        h3 = (r1 - h2.astype(F32)).astype(BF16)
        y = _dot(h1, ones) + _dot(h2, ones) + _dot(h3, ones) + total
        o_ref[:, g * LANES:(g + 1) * LANES] = y * LOG2E
        total = y[:, LANES - 1:LANES]


def _cumsum_rows(x):
    rows, n = x.shape
    n_pad = -(-n // LANES) * LANES
    xp = jnp.pad(x, ((0, 0), (0, n_pad - n)))
    out = pl.pallas_call(
        _cumsum_body,
        out_shape=jax.ShapeDtypeStruct((rows, n_pad), F32),
        compiler_params=pltpu.CompilerParams(vmem_limit_bytes=VMEM_LIMIT),
        name="cumsum_rows",
    )(xp)
    return out[:, :n]


def _split_cols(w, sizes):
    out, off = [], 0
    for n in sizes:
        out.append(w[:, off:off + n])
        off += n
    return out


def _rope_tables(pos, n_rows):
    half = A_ROPE // 2
    inv_freq = ROPE_THETA ** (-jnp.arange(half, dtype=F32) / half)
    ang = pos.astype(F32)[:, None] * inv_freq[None, :]
    cos, sin = jnp.cos(ang), jnp.sin(ang)
    zeros = jnp.zeros((pos.shape[0], LANES - A_ROPE), F32)
    cos_t = jnp.concatenate([cos, cos, zeros], axis=1)
    sin_t = jnp.concatenate([-sin, sin, zeros], axis=1)
    reps = max(1, n_rows // pos.shape[0])
    return jnp.tile(cos_t, (reps, 1)), jnp.tile(sin_t, (reps, 1))


def _swap_halves(w):
    half = w.shape[-1] // 2
    return jnp.concatenate([w[..., half:], w[..., :half]], axis=-1)


def _pad_lanes(w):
    return jnp.pad(w, [(0, 0)] * (w.ndim - 1) + [(0, LANES - w.shape[-1])])


def _prep_even(w_in, q_norm, w_uq, kv_norm, w_uk, w_uv):
    wqa, wkv, wkr, wga, wqb, wkb, wvb, wgb = _split_cols(
        w_in, (A_Q_LORA, A_KV_LORA, A_ROPE, WIDTH, WIDTH, WIDTH, WIDTH, WIDTH))
    b = lambda a: a.astype(BF16)
    uq_rope = w_uq[:, :, A_NOPE:]
    uk_t = jnp.transpose(w_uk, (1, 2, 0))
    z = jnp.zeros((A_NOPE, A_KV_LORA), w_uk.dtype)
    wuk = jnp.stack([jnp.block([[uk_t[2 * p], z], [z, uk_t[2 * p + 1]]]) for p in range(N_PAIRS)])
    uv_t = jnp.transpose(w_uv, (1, 0, 2))
    zv = jnp.zeros((A_KV_LORA, A_V), w_uv.dtype)
    wuv = jnp.stack([jnp.block([[uv_t[2 * p], zv], [zv, uv_t[2 * p + 1]]]) for p in range(N_PAIRS)])
    return dict(
        wqa=b(wqa), wkv=b(wkv), wkr=b(_pad_lanes(wkr)), wkrs=b(_pad_lanes(_swap_halves(wkr))),
        wga=b(wga), wqb=b(wqb), wkb=b(wkb), wvb=b(wvb), wgb=b(wgb),
        qn=q_norm.reshape(1, -1), kvn=kv_norm.reshape(1, -1),
        wuqn=b(w_uq[:, :, :A_NOPE].reshape(A_Q_LORA, A_HEADS * A_NOPE)),
        wuqr=b(_pad_lanes(uq_rope).reshape(A_Q_LORA, A_HEADS * LANES)),
        wuqrs=b(_pad_lanes(_swap_halves(uq_rope)).reshape(A_Q_LORA, A_HEADS * LANES)),
        wuk=b(wuk), wuv=b(wuv))


def _prep_odd(w_in, forget_bias):
    wqc, wkc, wvc, wgc, wqd, wkd, wvd, wf, wgd = _split_cols(
        w_in, (WIDTH, WIDTH, WIDTH, WIDTH, WIDTH, WIDTH, WIDTH, HEADS, WIDTH))
    b = lambda a: a.astype(BF16)
    return dict(wqc=b(wqc), wkc=b(wkc), wvc=b(wvc), wgc=b(wgc), wqd=b(wqd), wkd=b(wkd), wvd=b(wvd),
                wf=b(_pad_lanes(wf)), wgd=b(wgd), fb=forget_bias.astype(F32).reshape(1, HEADS),
                wf_t=b(wf.T), fb_t=forget_bias.astype(F32).reshape(HEADS, 1))


def kernel(x_prompt, x_sample, cache_mla_ckv, cache_mla_krope, cache_sb_k, cache_sb_v, cache_band_k,
           cache_band_v, cache_fox_k, cache_fox_v, cache_fox_logf, norm_pre, norm_post, w_in_even,
           a_q_norm, a_w_uq, a_kv_norm, a_w_uk, a_w_uv, w_out_even, w_in_odd, c_rel_bias,
           d_forget_bias, w_out_odd):
    n_b, n_s, _ = x_prompt.shape
    d_b, d_s, _ = x_sample.shape
    n_past = cache_sb_k.shape[2]
    n_keep = cache_band_k.shape[2]
    assert n_s % (2 * TQ) == 0 and n_past % TKC == 0 and d_s == CHUNK and n_past % CHUNK == 0
    assert n_keep == LEFT_CTX and (d_b * d_s) % TM == 0 and TM % d_s == 0

    xp = x_prompt.reshape(n_b * n_s, D_MODEL)
    xs = x_sample.reshape(d_b * d_s, D_MODEL)
    row = lambda a: a.reshape(1, -1)
    heads = lambda a, b, s: a.reshape(b, s, HEADS, HEAD_DIM)
    rows_minor = lambda a: a.transpose(0, 2, 3, 1).reshape(a.shape[0], WIDTH, a.shape[1])

    we = _prep_even(w_in_even[0], a_q_norm[0], a_w_uq[0], a_kv_norm[0], a_w_uk[0], a_w_uv[0])
    wout_e = w_out_even[0].astype(BF16)
    cos_p, sin_p = _rope_tables(jnp.arange(n_s), TM)
    cos_s, sin_s = _rope_tables(n_past + jnp.arange(d_s), TM)

    (ckv_p, kr_p, k2_p, q2_p, ga_p, gb_p, qb_p, kb_p, kb16_p, vb_p, vb16_p) = _in_even(
        xp, row(norm_pre[0]), cos_p, sin_p, we, seq_len=n_s)
    (ckv_s, kr_s, k2_s, q2_s, ga_s, gb_s, qb_s, kb_s, kb16_s, vb_s, vb16_s) = _in_even(
        xs, row(norm_pre[0]), cos_s, sin_s, we)

    lat_p = _mla_prompt(q2_p, k2_p, n_b, n_s).reshape(n_b * n_s, A_HEADS * A_KV_LORA)
    lat_s = _mla_sample(q2_s, k2_s, cache_mla_ckv[0], cache_mla_krope[0].transpose(0, 2, 1), d_b, d_s
                        ).reshape(d_b * d_s, A_HEADS * A_KV_LORA)
    sb_p = _sb_prompt(qb_p, kb16_p, vb16_p, n_b, n_s)
    sb_s = _sb_sample(qb_s, kb16_s, vb16_s, rows_minor(cache_sb_k[0]), rows_minor(cache_sb_v[0]), d_b, d_s)

    xp1 = _out_proj(xp, row(norm_post[0]), ga_p, gb_p, lat_p, sb_p, wout_e, we['wuv'])
    xs1 = _out_proj(xs, row(norm_post[0]), ga_s, gb_s, lat_s, sb_s, wout_e, we['wuv'])

    wo = _prep_odd(w_in_odd[0], d_forget_bias[0])
    wout_o = w_out_odd[0].astype(BF16)
    (qc_p, kc_p, kc16_p, vc_p, vc16_p, gc_p, qd_p, kd_p, kd16_p, vd_p, vd16_p, lf_p, gd_p) = _in_odd(
        xp1, row(norm_pre[1]), wo, seq_len=n_s)
    (qc_s, kc_s, kc16_s, vc_s, vc16_s, gc_s, qd_s, kd_s, kd16_s, vd_s, vd16_s, lf_s, gd_s) = _in_odd(
        xs1, row(norm_pre[1]), wo)

    band_p = _band_prompt(qc_p, kc16_p, vc16_p, _band_bias(c_rel_bias[0], BAND_TQ), n_b, n_s)
    band_s = _band_sample(qc_s, kc16_s, vc16_s, rows_minor(cache_band_k[0]), rows_minor(cache_band_v[0]),
                          _band_bias(c_rel_bias[0], d_s), d_b, d_s)

    lf_s3 = lf_s.reshape(d_b, d_s, HEADS)
    cum_p = _cumsum_rows(lf_p.reshape(n_b * HEADS, n_s)).reshape(n_b, HEADS, n_s)
    lf_all = jnp.concatenate([cache_fox_logf[0].astype(F32), lf_s3], axis=1)
    cum_s = _cumsum_rows(lf_all.transpose(0, 2, 1).reshape(d_b * HEADS, n_past + d_s)
                         ).reshape(d_b, HEADS, n_past + d_s)
    fox_p = _fox_prompt(qd_p, kd16_p, vd16_p, cum_p, n_b, n_s)
    fox_s = _fox_sample(qd_s, kd16_s, vd16_s, rows_minor(cache_fox_k[0]), rows_minor(cache_fox_v[0]), cum_s,
                        d_b, d_s)

    xp2 = _out_proj(xp1, row(norm_post[1]), gc_p, gd_p, band_p, fox_p, wout_o)
    xs2 = _out_proj(xs1, row(norm_post[1]), gc_s, gd_s, band_s, fox_s, wout_o)

    keep = min(LEFT_CTX, n_s)
    band_k_s = jnp.concatenate([cache_band_k[0], heads(kc_s, d_b, d_s)], axis=1)[:, d_s:]
    band_v_s = jnp.concatenate([cache_band_v[0], heads(vc_s, d_b, d_s)], axis=1)[:, d_s:]
    one = lambda a: a[None]
    heads_t = lambda a: a.reshape(n_b, HEADS, HEAD_DIM, a.shape[-1]).transpose(0, 3, 1, 2)
    return (xp2.reshape(n_b, n_s, D_MODEL), xs2.reshape(d_b, d_s, D_MODEL),
            one(ckv_p.reshape(n_b, n_s, A_KV_LORA)), one(kr_p.transpose(0, 2, 1)),
            one(heads_t(kb_p)), one(heads_t(vb_p)),
            one(heads_t(kc_p[:, :, n_s - keep:])), one(heads_t(vc_p[:, :, n_s - keep:])),
            one(heads_t(kd_p)), one(heads_t(vd_p)), one(lf_p.transpose(0, 2, 1)),
            one(ckv_s.reshape(d_b, d_s, A_KV_LORA)), one(kr_s.reshape(d_b, d_s, A_ROPE)),
            one(heads(kb_s, d_b, d_s)), one(heads(vb_s, d_b, d_s)),
            one(band_k_s), one(band_v_s),
            one(heads(kd_s, d_b, d_s)), one(heads(vd_s, d_b, d_s)), one(lf_s3))
```

```python
import functools

import numpy as np
import jax
import jax.numpy as jnp
from jax import lax
from jax.experimental import pallas as pl
from jax.experimental.pallas import tpu as pltpu

F32 = jnp.float32
BF16 = jnp.bfloat16

D_MODEL = 1024
PAST_LEN = 4096
CHUNK = 64
LEFT_CTX = 512
REL_CLIP = 128
EPS = 1e-6
NEG = -1e30
ROPE_THETA = 10000.0
A_HEADS = 8
A_Q_LORA = 256
A_KV_LORA = 128
A_NOPE = 64
A_ROPE = 32
A_V = 64
A_SCALE = (A_NOPE + A_ROPE) ** -0.5
HEADS = 8
HEAD_DIM = 64
WIDTH = HEADS * HEAD_DIM
QK_SCALE = HEAD_DIM ** -0.5
LOG2E = 1.4426950408889634
MLA_LOGIT_SCALE = A_SCALE * LOG2E
N_PAIRS = HEADS // 2

LANES = 128
VMEM_LIMIT = 52 * 1024 * 1024
TM = 512
TK = 256
TKC = 1024
SB_T = 256
SB_TK = 256
FOX_TQ = 512
BAND_TQ = 256
MLA_RT = 256
MLA_QB = 512
PAIRS_PER_STEP = 4
STICK_DEAD = -104.0


def _dot(a, b):
    return jnp.dot(a, b, preferred_element_type=F32)


def _dot_nt(a, b):
    return lax.dot_general(a, b, (((1,), (1,)), ((), ())), preferred_element_type=F32)


def _rms(x, g):
    y = x * lax.rsqrt(jnp.mean(x * x, axis=-1, keepdims=True) + EPS)
    return y * g


def _log_sigmoid(z):
    return jnp.minimum(z, 0.0) - jnp.log(1.0 + jnp.exp(-jnp.abs(z)))


def _silu(g):
    return g / (1.0 + jnp.exp(-g))


def _stack_pair(q2):
    qf = q2.astype(F32)
    lane = lax.broadcasted_iota(jnp.int32, qf.shape, 1)
    even = jnp.where(lane < HEAD_DIM, qf, 0.0)
    odd = jnp.where(lane >= HEAD_DIM, qf, 0.0)
    return jnp.concatenate([even, odd], axis=0).astype(BF16)


def _merge_pair(o, tq):
    top, bot = o[:tq], o[tq:]
    lane = lax.broadcasted_iota(jnp.int32, top.shape, 1)
    return jnp.where(lane < HEAD_DIM, top, bot)


def _params(*sem):
    return pltpu.CompilerParams(dimension_semantics=sem, vmem_limit_bytes=VMEM_LIMIT)


def _const_spec(shape):
    nd = len(shape)
    return pl.BlockSpec(shape, lambda *_: (0,) * nd)


def _in_even_body(x_ref, gpre_ref, cos_ref, sin_ref, wqa_ref, wkv_ref, wkr_ref, wkrs_ref, wga_ref,
                  wqb_ref, wkb_ref, wvb_ref, wgb_ref, qn_ref, kvn_ref, wuqn_ref, wuqr_ref,
                  wuqrs_ref, wuk_ref,
                  ckv_ref, krope_ref, k2_ref, q2_ref, ga_ref, gb_ref, qb_ref, kb_ref, kb16_ref,
                  vb_ref, vb16_ref, *, rows_minor):
    h = _rms(x_ref[...], gpre_ref[...]).astype(BF16)
    cos = cos_ref[...]
    sin = sin_ref[...]
    ckv = _rms(_dot(h, wkv_ref[...]), kvn_ref[...])
    ckv_ref[...] = ckv
    kr = _dot(h, wkr_ref[...]) * cos + _dot(h, wkrs_ref[...]) * sin
    krope_ref[...] = kr.T[:A_ROPE] if rows_minor else kr[:, :A_ROPE]
    k2_ref[:, :LANES] = ckv.astype(BF16)
    k2_ref[:, LANES:] = kr.astype(BF16)
    cq = _rms(_dot(h, wqa_ref[...]), qn_ref[...]).astype(BF16)
    qn = _dot(cq, wuqn_ref[...]).astype(BF16)
    for p in range(N_PAIRS):
        ql = _dot(qn[:, p * LANES:(p + 1) * LANES], wuk_ref[p])
        q2_ref[:, (2 * p) * 256:(2 * p) * 256 + LANES] = ql[:, :LANES].astype(BF16)
        q2_ref[:, (2 * p + 1) * 256:(2 * p + 1) * 256 + LANES] = ql[:, LANES:].astype(BF16)
    qr = _dot(cq, wuqr_ref[...])
    qrs = _dot(cq, wuqrs_ref[...])
    for hd in range(A_HEADS):
        rot = qr[:, hd * LANES:(hd + 1) * LANES] * cos + qrs[:, hd * LANES:(hd + 1) * LANES] * sin
        q2_ref[:, hd * 256 + LANES:(hd + 1) * 256] = rot.astype(BF16)
    ga_ref[...] = _dot(h, wga_ref[...]).astype(BF16)
    gb_ref[...] = _dot(h, wgb_ref[...]).astype(BF16)
    qb = _dot(h, wqb_ref[...]) * QK_SCALE
    kb = _dot(h, wkb_ref[...])
    vb = _dot(h, wvb_ref[...])
    kb_ref[...] = kb.T if rows_minor else kb
    vb_ref[...] = vb.T if rows_minor else vb
    for p in range(N_PAIRS):
        sl = slice(p * LANES, (p + 1) * LANES)
        qb_ref[p] = qb[:, sl].astype(BF16)
        kb16_ref[p] = kb[:, sl].astype(BF16)
        vb16_ref[p] = vb[:, sl].astype(BF16)


def _state_specs(rows, seq_len):
    sds = jax.ShapeDtypeStruct
    if seq_len is None:
        return (lambda n: pl.BlockSpec((TM, n), lambda i: (i, 0))), (lambda n: sds((rows, n), F32))
    nt = seq_len // TM
    return ((lambda n: pl.BlockSpec((None, n, TM), lambda i: (i // nt, 0, i % nt))),
            (lambda n: sds((rows // seq_len, n, seq_len), F32)))


def _in_even(x, gpre, cos, sin, w, seq_len=None):
    rows = x.shape[0]
    n_tab = cos.shape[0] // TM
    row_spec = lambda n: pl.BlockSpec((TM, n), lambda i: (i, 0))
    pm_spec = pl.BlockSpec((N_PAIRS, TM, LANES), lambda i: (0, i, 0))
    tab_spec = pl.BlockSpec((TM, LANES), lambda i: (i % n_tab, 0))
    weights = [w['wqa'], w['wkv'], w['wkr'], w['wkrs'], w['wga'], w['wqb'], w['wkb'], w['wvb'], w['wgb'],
               w['qn'], w['kvn'], w['wuqn'], w['wuqr'], w['wuqrs'], w['wuk']]
    in_specs = ([row_spec(D_MODEL), _const_spec((1, D_MODEL)), tab_spec, tab_spec]
                + [_const_spec(a.shape) for a in weights])
    args = [x, gpre, cos, sin, *weights]
    st_spec, st_shape = _state_specs(rows, seq_len)
    sds = jax.ShapeDtypeStruct
    pm = sds((N_PAIRS, rows, LANES), BF16)
    return pl.pallas_call(
        functools.partial(_in_even_body, rows_minor=seq_len is not None),
        grid=(rows // TM,),
        in_specs=in_specs,
        out_specs=[row_spec(A_KV_LORA), st_spec(A_ROPE), row_spec(256), row_spec(A_HEADS * 256),
                   row_spec(WIDTH), row_spec(WIDTH), pm_spec, st_spec(WIDTH), pm_spec,
                   st_spec(WIDTH), pm_spec],
        out_shape=[sds((rows, A_KV_LORA), F32), st_shape(A_ROPE), sds((rows, 256), BF16),
                   sds((rows, A_HEADS * 256), BF16), sds((rows, WIDTH), BF16), sds((rows, WIDTH), BF16),
                   pm, st_shape(WIDTH), pm, st_shape(WIDTH), pm],
        compiler_params=_params("parallel"),
        name="in_proj_even",
    )(*args)


def _in_odd_body(*refs, rows_minor):
    (x_ref, gpre_ref, fb_ref, wqc_ref, wkc_ref, wvc_ref, wgc_ref, wqd_ref, wkd_ref, wvd_ref, wf_ref,
     wgd_ref) = refs[:12]
    n_in = 14 if rows_minor else 12
    (qc_ref, kc_ref, kc16_ref, vc_ref, vc16_ref, gc_ref, qd_ref, kd_ref, kd16_ref, vd_ref, vd16_ref,
     logf_ref, gd_ref) = refs[n_in:]
    h = _rms(x_ref[...], gpre_ref[...]).astype(BF16)
    gc_ref[...] = _dot(h, wgc_ref[...]).astype(BF16)
    gd_ref[...] = _dot(h, wgd_ref[...]).astype(BF16)
    if rows_minor:
        wf_t, fb_t = refs[12:14]
        logf_ref[...] = _log_sigmoid(_dot_nt(wf_t[...], h) + fb_t[...])
    else:
        logf_ref[...] = _log_sigmoid(_dot(h, wf_ref[...])[:, :HEADS] + fb_ref[...])
    for q_w, k_w, v_w, q_o, k_o, k16_o, v_o, v16_o in (
            (wqc_ref, wkc_ref, wvc_ref, qc_ref, kc_ref, kc16_ref, vc_ref, vc16_ref),
            (wqd_ref, wkd_ref, wvd_ref, qd_ref, kd_ref, kd16_ref, vd_ref, vd16_ref)):
        q = _dot(h, q_w[...]) * (QK_SCALE * LOG2E)
        k = _dot(h, k_w[...])
        v = _dot(h, v_w[...])
        k_o[...] = k.T if rows_minor else k
        v_o[...] = v.T if rows_minor else v
        for p in range(N_PAIRS):
            sl = slice(p * LANES, (p + 1) * LANES)
            q_o[p] = q[:, sl].astype(BF16)
            k16_o[p] = k[:, sl].astype(BF16)
            v16_o[p] = v[:, sl].astype(BF16)


def _in_odd(x, gpre, w, seq_len=None):
    rows = x.shape[0]
    row_spec = lambda n: pl.BlockSpec((TM, n), lambda i: (i, 0))
    pm_spec = pl.BlockSpec((N_PAIRS, TM, LANES), lambda i: (0, i, 0))
    weights = [w['wqc'], w['wkc'], w['wvc'], w['wgc'], w['wqd'], w['wkd'], w['wvd'], w['wf'], w['wgd']]
    rows_minor = seq_len is not None
    if rows_minor:
        weights += [w['wf_t'], w['fb_t']]
    st_spec, st_shape = _state_specs(rows, seq_len)
    sds = jax.ShapeDtypeStruct
    pm = sds((N_PAIRS, rows, LANES), BF16)
    full = sds((rows, WIDTH), BF16)
    return pl.pallas_call(
        functools.partial(_in_odd_body, rows_minor=rows_minor),
        grid=(rows // TM,),
        in_specs=[row_spec(D_MODEL), _const_spec((1, D_MODEL)), _const_spec((1, HEADS))]
                 + [_const_spec(a.shape) for a in weights],
        out_specs=[pm_spec, st_spec(WIDTH), pm_spec, st_spec(WIDTH), pm_spec, row_spec(WIDTH),
                   pm_spec, st_spec(WIDTH), pm_spec, st_spec(WIDTH), pm_spec, st_spec(HEADS),
                   row_spec(WIDTH)],
        out_shape=[pm, st_shape(WIDTH), pm, st_shape(WIDTH), pm, full, pm, st_shape(WIDTH), pm,
                   st_shape(WIDTH), pm, st_shape(HEADS), full],
        compiler_params=_params("parallel"),
        name="in_proj_odd",
    )(x, gpre, w['fb'], *weights)


def _out_body(*refs, mla):
    if mla:
        x_ref, gpost_ref, g1_ref, g2_ref, a_ref, b_ref, wuv_ref, wout_ref, o_ref, mix_ref = refs
    else:
        x_ref, gpost_ref, g1_ref, g2_ref, a_ref, b_ref, wout_ref, o_ref, mix_ref = refs
    s1 = _silu(g1_ref[...].astype(F32))
    s2 = _silu(g2_ref[...].astype(F32))
    for p in range(N_PAIRS):
        sl = slice(p * LANES, (p + 1) * LANES)
        if mla:
            a = _dot(a_ref[:, p * 256:(p + 1) * 256], wuv_ref[p])
        else:
            a = a_ref[p].astype(F32)
        mix_ref[:, sl] = (s1[:, sl] * a).astype(BF16)
        mix_ref[:, WIDTH + p * LANES:WIDTH + (p + 1) * LANES] = (
            s2[:, sl] * b_ref[p].astype(F32)).astype(BF16)
    y = _dot(mix_ref[...], wout_ref[...])
    o_ref[...] = x_ref[...] + _rms(y, gpost_ref[...])


def _out_proj(x, gpost, g1, g2, a, b, wout, wuv=None):
    rows = x.shape[0]
    mla = wuv is not None
    row_spec = lambda n: pl.BlockSpec((TM, n), lambda i: (i, 0))
    pm_spec = pl.BlockSpec((N_PAIRS, TM, LANES), lambda i: (0, i, 0))
    in_specs = [row_spec(D_MODEL), _const_spec((1, D_MODEL)), row_spec(WIDTH), row_spec(WIDTH),
                row_spec(A_HEADS * A_KV_LORA) if mla else pm_spec, pm_spec]
    args = [x, gpost, g1, g2, a, b]
    if mla:
        in_specs.append(_const_spec(wuv.shape))
        args.append(wuv)
    in_specs.append(_const_spec(wout.shape))
    args.append(wout)
    return pl.pallas_call(
        functools.partial(_out_body, mla=mla),
        grid=(rows // TM,),
        in_specs=in_specs,
        out_specs=row_spec(D_MODEL),
        out_shape=jax.ShapeDtypeStruct((rows, D_MODEL), F32),
        scratch_shapes=[pltpu.VMEM((TM, 2 * WIDTH), BF16)],
        compiler_params=_params("parallel"),
        name="out_proj_even" if mla else "out_proj_odd",
    )(*args)


def _softmax_init(m_ref, accl_ref):
    m_ref[...] = jnp.full(m_ref.shape, NEG, F32)
    accl_ref[...] = jnp.zeros(accl_ref.shape, F32)


def _lanes(x, n):
    parts = [x] * (n // LANES)
    if n % LANES:
        parts.append(x[:, :n % LANES])
    return parts[0] if len(parts) == 1 else jnp.concatenate(parts, axis=1)


def _scores(q, k, kv_t):
    return _dot(q, k) if kv_t else _dot_nt(q, k)


def _weighted(p, v, kv_t):
    return _dot_nt(p, v) if kv_t else _dot(p, v)


def _with_ones(v, kv_t=False):
    return jnp.concatenate([v, jnp.ones(v.shape, BF16)], axis=0 if kv_t else 1)


def _softmax_update(s, v1, m_ref, accl_ref, kv_t=False):
    keys = s.shape[1]
    n = v1.shape[0] if kv_t else v1.shape[1]
    m_prev = m_ref[...]
    m_new = jnp.maximum(m_prev, jnp.max(s, axis=1, keepdims=True))
    alpha = jnp.exp2(m_prev - m_new)
    p = jnp.exp2(s - _lanes(m_new, keys))
    accl_ref[...] = _lanes(alpha, n) * accl_ref[...] + _weighted(p.astype(BF16), v1, kv_t)
    m_ref[...] = m_new


def _softmax_result(accl_ref):
    accl = accl_ref[...]
    n = accl.shape[1] // 2
    return accl[:, :n] / accl[:, n:]


def _fill_suffix_ones(tri_ref):
    n = tri_ref.shape[0]
    r = lax.broadcasted_iota(jnp.int32, (n, n), 0)
    c = lax.broadcasted_iota(jnp.int32, (n, n), 1)
    tri_ref[...] = jnp.where(r > c, 1.0, 0.0).astype(BF16)


def _stick_tiles(qs, k2, v2, tri, acc_refs, car_refs, mask, kv_t=False):
    n = len(qs)
    z = [_scores(qs[i], k2[i], kv_t) for i in range(n)]
    lb, l1, suf = [], [], []
    for i in range(n):
        lb.append(_log_sigmoid(z[i]))
        l = lb[i] - z[i]
        l1.append(l if mask is None else jnp.where(mask, l, 0.0))
        hi = l1[i].astype(BF16)
        lo = (l1[i] - hi.astype(F32)).astype(BF16)
        suf.append(_dot(hi, tri) + _dot(lo, tri))
    for i in range(n):
        w = jnp.exp(lb[i] + (suf[i] + car_refs[i][...]))
        if mask is not None:
            w = jnp.where(mask, w, 0.0)
        acc_refs[i][...] += _weighted(w.astype(BF16), v2[i], kv_t)
        car_refs[i][...] += jnp.sum(l1[i], axis=1, keepdims=True)


def _stick_alive(car_ref):
    return (jnp.max(car_ref[...]) >= STICK_DEAD).astype(jnp.int32)


def _local_causal(rows, keys, tq, strict, key_offset=0):
    r = lax.broadcasted_iota(jnp.int32, (rows, keys), 0) & (tq - 1)
    c = lax.broadcasted_iota(jnp.int32, (rows, keys), 1) + key_offset
    return (c < r) if strict else (c <= r)


def _decay_tile(s, v2, cq, ck, causal, m_ref, accl_ref, kv_t=False):
    tq, keys = cq.shape[0], ck.shape[1]
    for hh in range(2):
        rows = slice(hh * tq, (hh + 1) * tq)
        sh = s[rows] + (cq[:, hh:hh + 1] - ck[hh:hh + 1])
        if causal:
            sh = jnp.where(_local_causal(tq, keys, tq, strict=False), sh, NEG)
        _softmax_update(sh, _with_ones(v2, kv_t), m_ref.at[rows], accl_ref.at[rows], kv_t)


def _split3(c):
    hi = c.astype(BF16).astype(F32)
    rest = c - hi
    mid = rest.astype(BF16).astype(F32)
    return hi, mid, rest - mid


def _decay_lanes(c, base, query):
    hi, mid, lo = _split3(c if query else -c)
    lane = lax.broadcasted_iota(jnp.int32, (c.shape[0], LANES), 1) - (base if query else base + 3)
    parts = jnp.where(lane == 0, hi, jnp.where(lane == 1, mid, jnp.where(lane == 2, lo, 0.0)))
    ones_at = lane + 3 if not query else lane - 3
    return jnp.where((ones_at >= 0) & (ones_at < 3), 1.0, parts)


def _decay_operand(x, cum, hh, query):
    lane = lax.broadcasted_iota(jnp.int32, x.shape, 1)
    own = (lane < HEAD_DIM) if hh == 0 else (lane >= HEAD_DIM)
    base = HEAD_DIM if hh == 0 else 0
    return jnp.where(own, x, _decay_lanes(cum[:, hh:hh + 1], base, query)).astype(BF16)


def _one_ahead(score_fns, consume_fns):
    ahead = score_fns[0]()
    for i, consume in enumerate(consume_fns):
        s = ahead
        if i + 1 < len(score_fns):
            ahead = score_fns[i + 1]()
        consume(s)


def _decay_head_update(s, v1, causal_offset, m_ref, accl_ref):
    tq, keys = s.shape
    if causal_offset is not None:
        s = jnp.where(_local_causal(tq, keys, tq, strict=False, key_offset=causal_offset), s, NEG)
    _softmax_update(s, v1, m_ref, accl_ref)


def _pair_result(accl, tq):
    o = accl / pltpu.roll(accl, HEAD_DIM, axis=1)
    return _merge_pair(o, tq)


def _mla_prompt_body(q_ref, k_ref, o_ref, m_ref, accl_ref):
    g = pl.program_id(1)
    grp = 4 * CHUNK

    def update_heads(r0, k, mask):
        sl = pl.ds(r0, grp)
        v1 = _with_ones(k[:, :A_KV_LORA])

        def consume(s, h):
            s = s * MLA_LOGIT_SCALE
            if mask is not None:
                s = jnp.where(mask, s, NEG)
            _softmax_update(s, v1, m_ref.at[h, sl], accl_ref.at[h, sl])

        _one_ahead([lambda h=h: _dot_nt(q_ref[sl, h * 256:(h + 1) * 256], k) for h in range(A_HEADS)],
                   [lambda s, h=h: consume(s, h) for h in range(A_HEADS)])

    def group(gl, carry):
        gq = g * (MLA_QB // grp) + gl
        g0 = pl.multiple_of(gl * grp, grp)
        for h in range(A_HEADS):
            _softmax_init(m_ref.at[h, pl.ds(g0, grp)], accl_ref.at[h, pl.ds(g0, grp)])

        def earlier(k0, n_keys):
            update_heads(g0, k_ref[pl.ds(k0, n_keys), :], None)

        def kv(j, c2):
            earlier(pl.multiple_of(j * 2 * TK, 2 * TK), 2 * TK)
            return c2

        lax.fori_loop(0, gq // 2, kv, 0)

        @pl.when(gq % 2 == 1)
        def _():
            earlier(pl.multiple_of((gq - 1) * TK, TK), TK)

        kd = k_ref[pl.ds(pl.multiple_of(gq * TK, TK), TK), :]
        row_chunk = lax.broadcasted_iota(jnp.int32, (grp, TK), 0) // CHUNK
        key_chunk = lax.broadcasted_iota(jnp.int32, (grp, TK), 1) // CHUNK
        update_heads(g0, kd, key_chunk <= row_chunk)
        for h in range(A_HEADS):
            o_ref[pl.ds(g0, grp), h * A_KV_LORA:(h + 1) * A_KV_LORA] = _softmax_result(
                accl_ref.at[h, pl.ds(g0, grp)]).astype(BF16)
        return carry

    lax.fori_loop(0, MLA_QB // grp, group, 0)


def _mla_prompt(q2, k2, n_b, n_s):
    assert n_s % MLA_QB == 0 and TK == 4 * CHUNK
    return pl.pallas_call(
        _mla_prompt_body,
        grid=(n_b, n_s // MLA_QB),
        in_specs=[pl.BlockSpec((None, MLA_QB, A_HEADS * 256), lambda b, g: (b, g, 0)),
                  pl.BlockSpec((None, n_s, 256), lambda b, g: (b, 0, 0))],
        out_specs=pl.BlockSpec((None, MLA_QB, A_HEADS * A_KV_LORA), lambda b, g: (b, g, 0)),
        out_shape=jax.ShapeDtypeStruct((n_b, n_s, A_HEADS * A_KV_LORA), BF16),
        scratch_shapes=[pltpu.VMEM((A_HEADS, MLA_QB, LANES), F32),
                        pltpu.VMEM((A_HEADS, MLA_QB, 2 * LANES), F32)],
        compiler_params=_params("parallel", "parallel"),
        name="mla_prompt",
    )(q2.reshape(n_b, n_s, A_HEADS * 256), k2.reshape(n_b, n_s, 256))


def _mla_sample_body(q_ref, kn_ref, ckv_ref, kr_ref, o_ref, m_ref, accl_ref):
    j = pl.program_id(1)

    @pl.when(j == 0)
    def _():
        _softmax_init(m_ref, accl_ref)

    n_tiles = q_ref.shape[0] // MLA_RT
    ck = ckv_ref[...].astype(BF16)
    kr = kr_ref[...].T.astype(BF16)
    k = jnp.concatenate([ck, kr, jnp.zeros((TKC, LANES - A_ROPE), BF16)], axis=1)
    row_tiles = [slice(t * MLA_RT, (t + 1) * MLA_RT) for t in range(n_tiles)]

    def update_tiles(keys, values):
        scores = [_dot_nt(q_ref[sl, :], keys) for sl in row_tiles]
        v1 = _with_ones(values)
        for sl, s in zip(row_tiles, scores):
            _softmax_update(s * MLA_LOGIT_SCALE, v1, m_ref.at[sl], accl_ref.at[sl])

    update_tiles(k, ck)

    @pl.when(j == pl.num_programs(1) - 1)
    def _():
        kn = kn_ref[...]
        update_tiles(kn, kn[:, :A_KV_LORA])
        o_ref[...] = _softmax_result(accl_ref).astype(BF16)


def _mla_sample(q2, k2, cache_ckv, cache_kr, n_b, n_s):
    rows = n_s * A_HEADS
    n_past = cache_ckv.shape[1]
    return pl.pallas_call(
        _mla_sample_body,
        grid=(n_b, n_past // TKC),
        in_specs=[pl.BlockSpec((None, rows, 256), lambda b, j: (b, 0, 0)),
                  pl.BlockSpec((None, n_s, 256), lambda b, j: (b, 0, 0)),
                  pl.BlockSpec((None, TKC, A_KV_LORA), lambda b, j: (b, j, 0)),
                  pl.BlockSpec((None, A_ROPE, TKC), lambda b, j: (b, 0, j))],
        out_specs=pl.BlockSpec((None, rows, A_KV_LORA), lambda b, j: (b, 0, 0)),
        out_shape=jax.ShapeDtypeStruct((n_b, rows, A_KV_LORA), BF16),
        scratch_shapes=[pltpu.VMEM((rows, LANES), F32), pltpu.VMEM((rows, 2 * LANES), F32)],
        compiler_params=_params("parallel", "arbitrary"),
        name="mla_sample",
    )(q2.reshape(n_b, rows, 256), k2.reshape(n_b, n_s, 256), cache_ckv, cache_kr)


def _sb_prompt_body(q_ref, k_ref, v_ref, o_ref, tri_ref, acc_ref, car_ref):
    n_s = q_ref.shape[1]
    _fill_suffix_ones(tri_ref)

    def qblock(i, carry):
        q0 = pl.multiple_of(i * SB_T, SB_T)
        qs = [_stack_pair(q_ref[g, pl.ds(q0, SB_T), :]) for g in range(PAIRS_PER_STEP)]
        acc_ref[...] = jnp.zeros(acc_ref.shape, F32)
        car_ref[...] = jnp.zeros(car_ref.shape, F32)

        def tiles(k0, n_keys, mask):
            pairs = range(PAIRS_PER_STEP)
            _stick_tiles(qs, [k_ref[g, pl.ds(k0, n_keys), :] for g in pairs],
                         [v_ref[g, pl.ds(k0, n_keys), :] for g in pairs], tri_ref[:n_keys, :n_keys],
                         [acc_ref.at[g] for g in pairs], [car_ref.at[g] for g in pairs], mask)

        tiles(q0, SB_T, _local_causal(2 * SB_T, SB_T, SB_T, strict=True))

        def kv(state):
            jj, _ = state
            tiles(pl.multiple_of(q0 - (jj + 1) * SB_TK, SB_TK), SB_TK, None)
            return jj + 1, _stick_alive(car_ref)

        lax.while_loop(lambda st: (st[0] < i * (SB_T // SB_TK)) & (st[1] > 0), kv,
                       (jnp.int32(0), _stick_alive(car_ref)))
        for g in range(PAIRS_PER_STEP):
            o_ref[g, pl.ds(q0, SB_T), :] = _merge_pair(acc_ref[g], SB_T).astype(o_ref.dtype)
        return carry

    lax.fori_loop(0, n_s // SB_T, qblock, 0)


def _pair_seq_spec(n_s):
    return pl.BlockSpec((PAIRS_PER_STEP, n_s, LANES), lambda b, g: (g, b, 0))


def _sb_prompt(q, k, v, n_b, n_s):
    assert n_s % SB_T == 0
    spec = _pair_seq_spec(n_s)
    return pl.pallas_call(
        _sb_prompt_body,
        grid=(n_b, N_PAIRS // PAIRS_PER_STEP),
        in_specs=[spec, spec, spec],
        out_specs=spec,
        out_shape=jax.ShapeDtypeStruct((N_PAIRS, n_b * n_s, LANES), BF16),
        scratch_shapes=[pltpu.VMEM((SB_T, SB_T), BF16), pltpu.VMEM((PAIRS_PER_STEP, 2 * SB_T, LANES), F32),
                        pltpu.VMEM((PAIRS_PER_STEP, 2 * SB_T, 1), F32)],
        compiler_params=_params("parallel", "parallel"),
        name="sb_prompt",
    )(q, k, v)


def _sb_sample_body(q_ref, kn_ref, vn_ref, ck_hbm, cv_hbm, o_ref, kbuf, vbuf, sem, tri_ref, acc_ref,
                    car_ref):
    b = pl.program_id(0)
    n_q = q_ref.shape[1]
    n_blk = ck_hbm.shape[2] // TK

    def block_copies(blk):
        rows = pl.ds(pl.multiple_of(blk * TK, TK), TK)
        return (pltpu.make_async_copy(ck_hbm.at[b, :, rows], kbuf, sem.at[0]),
                pltpu.make_async_copy(cv_hbm.at[b, :, rows], vbuf, sem.at[1]))

    def start(blk):
        for c in block_copies(blk):
            c.start()

    def cache_block(blk):
        for c in block_copies(blk):
            c.wait()
        tiles = lambda buf: [buf[p * LANES:(p + 1) * LANES, :].astype(BF16) for p in range(N_PAIRS)]
        _stick_tiles(qs, tiles(kbuf), tiles(vbuf), tri_ref[...], accs, cars, None, kv_t=True)

    start(n_blk - 1)
    _fill_suffix_ones(tri_ref)
    acc_ref[...] = jnp.zeros(acc_ref.shape, F32)
    car_ref[...] = jnp.zeros(car_ref.shape, F32)
    qs = [_stack_pair(q_ref[p]) for p in range(N_PAIRS)]
    accs = [acc_ref.at[p] for p in range(N_PAIRS)]
    cars = [car_ref.at[p] for p in range(N_PAIRS)]
    _stick_tiles(qs, [kn_ref[p] for p in range(N_PAIRS)], [vn_ref[p] for p in range(N_PAIRS)],
                 tri_ref[:n_q, :n_q], accs, cars, _local_causal(2 * n_q, n_q, n_q, strict=True))
    cache_block(n_blk - 1)

    def older(state):
        blk, _ = state
        start(blk)
        cache_block(blk)
        return blk - 1, _stick_alive(car_ref)

    lax.while_loop(lambda st: (st[0] >= 0) & (st[1] > 0), older,
                   (jnp.int32(n_blk - 2), _stick_alive(car_ref)))
    for p in range(N_PAIRS):
        o_ref[p] = _merge_pair(acc_ref[p], n_q).astype(o_ref.dtype)


def _sb_sample(q, kn, vn, cache_kt, cache_vt, n_b, n_s):
    new_spec = pl.BlockSpec((N_PAIRS, n_s, LANES), lambda b: (0, b, 0))
    hbm = pl.BlockSpec(memory_space=pl.ANY)
    return pl.pallas_call(
        _sb_sample_body,
        grid=(n_b,),
        in_specs=[new_spec, new_spec, new_spec, hbm, hbm],
        out_specs=new_spec,
        out_shape=jax.ShapeDtypeStruct((N_PAIRS, n_b * n_s, LANES), BF16),
        scratch_shapes=[pltpu.VMEM((WIDTH, TK), F32), pltpu.VMEM((WIDTH, TK), F32),
                        pltpu.SemaphoreType.DMA((2,)), pltpu.VMEM((TK, TK), BF16),
                        pltpu.VMEM((N_PAIRS, 2 * n_s, LANES), F32), pltpu.VMEM((N_PAIRS, 2 * n_s, 1), F32)],
        compiler_params=_params("arbitrary"),
        name="sb_sample",
    )(q, kn, vn, cache_kt, cache_vt)


def _band_block(scores, vwin, bias, valid_from, tq):
    s = scores + bias
    if valid_from is not None:
        col = lax.broadcasted_iota(jnp.int32, s.shape, 1)
        s = jnp.where(col >= valid_from, s, NEG)
    p = jnp.exp2(s - jnp.max(s, axis=1, keepdims=True))
    o = _dot(p.astype(BF16), vwin) / jnp.sum(p, axis=1, keepdims=True)
    return _merge_pair(o, tq)


def _band_prompt_body(q_ref, k_ref, v_ref, bias_ref, o_ref, kpad_ref, vpad_ref):
    n_s = q_ref.shape[1]
    win = LEFT_CTX + BAND_TQ
    zeros = jnp.zeros((PAIRS_PER_STEP, LEFT_CTX, LANES), BF16)
    kpad_ref[:, :LEFT_CTX, :] = zeros
    vpad_ref[:, :LEFT_CTX, :] = zeros
    kpad_ref[:, LEFT_CTX:, :] = k_ref[...]
    vpad_ref[:, LEFT_CTX:, :] = v_ref[...]

    def qblock(i, carry):
        q0 = pl.multiple_of(i * BAND_TQ, BAND_TQ)
        scores = [_dot_nt(_stack_pair(q_ref[g, pl.ds(q0, BAND_TQ), :]), kpad_ref[g, pl.ds(q0, win), :])
                  for g in range(PAIRS_PER_STEP)]
        for g in range(PAIRS_PER_STEP):
            o_ref[g, pl.ds(q0, BAND_TQ), :] = _band_block(
                scores[g], vpad_ref[g, pl.ds(q0, win), :], bias_ref[g], LEFT_CTX - q0, BAND_TQ
            ).astype(o_ref.dtype)
        return carry

    lax.fori_loop(0, n_s // BAND_TQ, qblock, 0)


def _band_prompt(q, k, v, bias, n_b, n_s):
    spec = _pair_seq_spec(n_s)
    win = LEFT_CTX + BAND_TQ
    return pl.pallas_call(
        _band_prompt_body,
        grid=(n_b, N_PAIRS // PAIRS_PER_STEP),
        in_specs=[spec, spec, spec,
                  pl.BlockSpec((PAIRS_PER_STEP, 2 * BAND_TQ, win), lambda b, g: (g, 0, 0))],
        out_specs=spec,
        out_shape=jax.ShapeDtypeStruct((N_PAIRS, n_b * n_s, LANES), BF16),
        scratch_shapes=[pltpu.VMEM((PAIRS_PER_STEP, LEFT_CTX + n_s, LANES), BF16),
                        pltpu.VMEM((PAIRS_PER_STEP, LEFT_CTX + n_s, LANES), BF16)],
        compiler_params=_params("parallel", "parallel"),
        name="band_prompt",
    )(q, k, v, bias)


def _band_sample_body(q_ref, kn_ref, vn_ref, kc_ref, vc_ref, bias_ref, o_ref):
    n_q = q_ref.shape[1]
    n_keep = kc_ref.shape[1]
    pair_lanes = [slice(p * LANES, (p + 1) * LANES) for p in range(N_PAIRS)]
    qs = [_stack_pair(q_ref[p]) for p in range(N_PAIRS)]
    scores_old = [_dot(qs[p], kc_ref[pair_lanes[p], :].astype(BF16)) for p in range(N_PAIRS)]
    scores_new = [_dot_nt(qs[p], kn_ref[p]) for p in range(N_PAIRS)]
    for p in range(N_PAIRS):
        sl = pair_lanes[p]
        bias = bias_ref[p]
        s_old = scores_old[p] + bias[:, :n_keep]
        s_new = scores_new[p] + bias[:, n_keep:]
        m = jnp.maximum(jnp.max(s_old, axis=1, keepdims=True), jnp.max(s_new, axis=1, keepdims=True))
        p_old = jnp.exp2(s_old - m)
        p_new = jnp.exp2(s_new - m)
        o = _dot_nt(p_old.astype(BF16), vc_ref[sl, :].astype(BF16)) + _dot(p_new.astype(BF16), vn_ref[p])
        total = jnp.sum(p_old, axis=1, keepdims=True) + jnp.sum(p_new, axis=1, keepdims=True)
        o_ref[p] = _merge_pair(o / total, n_q).astype(o_ref.dtype)


def _band_sample(q, kn, vn, cache_kt, cache_vt, bias, n_b, n_s):
    n_keep = cache_kt.shape[2]
    new_spec = pl.BlockSpec((N_PAIRS, n_s, LANES), lambda b: (0, b, 0))
    cache_spec = pl.BlockSpec((None, WIDTH, n_keep), lambda b: (b, 0, 0))
    return pl.pallas_call(
        _band_sample_body,
        grid=(n_b,),
        in_specs=[new_spec, new_spec, new_spec, cache_spec, cache_spec,
                  _const_spec((N_PAIRS, 2 * n_s, n_keep + n_s))],
        out_specs=new_spec,
        out_shape=jax.ShapeDtypeStruct((N_PAIRS, n_b * n_s, LANES), BF16),
        compiler_params=_params("parallel"),
        name="band_sample",
    )(q, kn, vn, cache_kt, cache_vt, bias)


def _band_bias(rel_bias, tq):
    win = LEFT_CTX + tq
    i = np.arange(tq)[:, None]
    w = np.arange(win)[None, :]
    qc, kc = i // CHUNK, w // CHUNK - LEFT_CTX // CHUNK
    ok = (kc <= qc) & (kc >= qc - LEFT_CTX // CHUNK)
    u = np.arange(win + tq - 1)
    rel = np.clip(LEFT_CTX + (tq - 1) - u, -REL_CLIP, REL_CLIP) + REL_CLIP
    diag = rel_bias.astype(F32)[:, rel]
    n = win + tq - 1
    flat = jnp.tile(diag, (1, tq))[:, tq - 1:tq - 1 + tq * (n - 1)]
    tab = flat.reshape(HEADS, tq, n - 1)[:, :, :win]
    tab = jnp.where(jnp.asarray(ok)[None], tab * LOG2E, NEG)
    return tab.reshape(N_PAIRS, 2 * tq, win)


def _fox_prompt_body(q_ref, k_ref, v_ref, cum_ref, o_ref, k1_ref, v1_ref, m_ref, accl_ref):
    n_s = q_ref.shape[1]

    def prepare(jb, carry):
        rows = pl.ds(pl.multiple_of(jb * TK, TK), TK)
        lane = lax.broadcasted_iota(jnp.int32, (TK, LANES), 1)
        for g in range(PAIRS_PER_STEP):
            k = k_ref[g, rows, :].astype(F32)
            v = v_ref[g, rows, :].astype(F32)
            cum = cum_ref[g, rows, :]
            for hh in range(2):
                k1_ref[g, hh, rows, :] = _decay_operand(k, cum, hh, query=False)
                own = (lane < HEAD_DIM) if hh == 0 else (lane >= HEAD_DIM)
                v1_ref[g, hh, rows, :] = jnp.where(own, v, 1.0).astype(BF16)
        return carry

    lax.fori_loop(0, n_s // TK, prepare, 0)

    tiles_per_q = FOX_TQ // TK

    def qblock(i, carry):
        q0 = pl.multiple_of(i * FOX_TQ, FOX_TQ)
        q1 = []
        for g in range(PAIRS_PER_STEP):
            q = q_ref[g, pl.ds(q0, FOX_TQ), :].astype(F32)
            cum = cum_ref[g, pl.ds(q0, FOX_TQ), :]
            q1.append([_decay_operand(q, cum, hh, query=True) for hh in range(2)])
        _softmax_init(m_ref, accl_ref)

        def tile(jb, causal_offset):
            rows = pl.ds(pl.multiple_of(jb * TK, TK), TK)
            chains = [(g, hh) for g in range(PAIRS_PER_STEP) for hh in range(2)]
            head_rows = lambda hh: slice(hh * FOX_TQ, (hh + 1) * FOX_TQ)
            _one_ahead(
                [lambda g=g, hh=hh: _dot_nt(q1[g][hh], k1_ref[g, hh, rows, :]) for g, hh in chains],
                [lambda s, g=g, hh=hh: _decay_head_update(
                    s, v1_ref[g, hh, rows, :], causal_offset, m_ref.at[g, head_rows(hh)],
                    accl_ref.at[g, head_rows(hh)]) for g, hh in chains])

        for t in range(tiles_per_q):
            tile(i * tiles_per_q + t, t * TK)

        def kv(jb, c2):
            tile(jb, None)
            return c2

        lax.fori_loop(0, i * tiles_per_q, kv, 0)
        for g in range(PAIRS_PER_STEP):
            o_ref[g, pl.ds(q0, FOX_TQ), :] = _pair_result(accl_ref[g], FOX_TQ).astype(o_ref.dtype)
        return carry

    lax.fori_loop(0, n_s // FOX_TQ, qblock, 0)


def _fox_prompt(q, k, v, cum, n_b, n_s):
    assert FOX_TQ % TK == 0 and n_s % FOX_TQ == 0
    spec = _pair_seq_spec(n_s)
    cum_rows = cum.reshape(n_b, N_PAIRS, 2, n_s).transpose(0, 1, 3, 2)
    g_ = PAIRS_PER_STEP
    return pl.pallas_call(
        _fox_prompt_body,
        grid=(n_b, N_PAIRS // g_),
        in_specs=[spec, spec, spec, pl.BlockSpec((None, g_, n_s, 2), lambda b, g: (b, g, 0, 0))],
        out_specs=spec,
        out_shape=jax.ShapeDtypeStruct((N_PAIRS, n_b * n_s, LANES), BF16),
        scratch_shapes=[pltpu.VMEM((g_, 2, n_s, LANES), BF16), pltpu.VMEM((g_, 2, n_s, LANES), BF16),
                        pltpu.VMEM((g_, 2 * FOX_TQ, LANES), F32), pltpu.VMEM((g_, 2 * FOX_TQ, LANES), F32)],
        compiler_params=_params("parallel", "parallel"),
        name="fox_prompt",
    )(q, k, v, cum_rows)


def _fox_sample_body(q_ref, kn_ref, vn_ref, kc_ref, vc_ref, cq_ref, ckn_ref, ckc_ref, o_ref,
                     m_ref, accl_ref):
    j = pl.program_id(1)
    n_q = q_ref.shape[1]
    cq = cq_ref[...]

    def update_pairs(k_of, v_of, ck, new_rows):
        scores = [_scores(_stack_pair(q_ref[p]), k_of(p), not new_rows) for p in range(N_PAIRS)]
        for p in range(N_PAIRS):
            rows = slice(2 * p * n_q, (2 * p + 2) * n_q)
            _decay_tile(scores[p], v_of(p), cq[:, 2 * p:2 * p + 2], ck[2 * p:2 * p + 2], new_rows,
                        m_ref.at[rows], accl_ref.at[rows], kv_t=not new_rows)

    @pl.when(j == 0)
    def _():
        _softmax_init(m_ref, accl_ref)
        update_pairs(lambda p: kn_ref[p], lambda p: vn_ref[p], ckn_ref[...], True)

    @pl.when(j > 0)
    def _():
        tile = lambda ref: (lambda p: ref[p * LANES:(p + 1) * LANES, :].astype(BF16))
        update_pairs(tile(kc_ref), tile(vc_ref), ckc_ref[...], False)

    @pl.when(j == pl.num_programs(1) - 1)
    def _():
        for p in range(N_PAIRS):
            rows = slice(2 * p * n_q, (2 * p + 2) * n_q)
            o_ref[p] = _merge_pair(_softmax_result(accl_ref.at[rows]), n_q).astype(o_ref.dtype)


def _fox_sample(q, kn, vn, cache_k, cache_v, cum, n_b, n_s):
    n_past = cache_k.shape[2]
    n_blk = n_past // TKC
    cq = cum[..., n_past:].transpose(0, 2, 1)
    ckn = cum[..., n_past:]
    ckc = cum[..., :n_past]
    new_spec = pl.BlockSpec((N_PAIRS, n_s, LANES), lambda b, j: (0, b, 0))
    cache_blk = lambda j: jnp.maximum(j - 1, 0)
    cache_spec = pl.BlockSpec((None, WIDTH, TKC), lambda b, j: (b, 0, cache_blk(j)))
    return pl.pallas_call(
        _fox_sample_body,
        grid=(n_b, n_blk + 1),
        in_specs=[new_spec, new_spec, new_spec, cache_spec, cache_spec,
                  pl.BlockSpec((None, n_s, HEADS), lambda b, j: (b, 0, 0)),
                  pl.BlockSpec((None, HEADS, n_s), lambda b, j: (b, 0, 0)),
                  pl.BlockSpec((None, HEADS, TKC), lambda b, j: (b, 0, cache_blk(j)))],
        out_specs=new_spec,
        out_shape=jax.ShapeDtypeStruct((N_PAIRS, n_b * n_s, LANES), BF16),
        scratch_shapes=[pltpu.VMEM((HEADS * n_s, LANES), F32), pltpu.VMEM((HEADS * n_s, 2 * LANES), F32)],
        compiler_params=_params("parallel", "arbitrary"),
        name="fox_sample",
    )(q, kn, vn, cache_k, cache_v, cq, ckn, ckc)


def _cumsum_body(x_ref, o_ref):
    n_rows, n_cols = x_ref.shape
    r = lax.broadcasted_iota(jnp.int32, (LANES, LANES), 0)
    c = lax.broadcasted_iota(jnp.int32, (LANES, LANES), 1)
    ones = jnp.where(r <= c, 1.0, 0.0).astype(BF16)
    total = jnp.zeros((n_rows, 1), F32)
    for g in range(n_cols // LANES):
        x = x_ref[:, g * LANES:(g + 1) * LANES]
        h1 = x.astype(BF16)
        r1 = x - h1.astype(F32)
        h2 = r1.astype(BF16)
        h3 = (r1 - h2.astype(F32)).astype(BF16)
        y = _dot(h1, ones) + _dot(h2, ones) + _dot(h3, ones) + total
        o_ref[:, g * LANES:(g + 1) * LANES] = y * LOG2E
        total = y[:, LANES - 1:LANES]


def _cumsum_rows(x):
    rows, n = x.shape
    n_pad = -(-n // LANES) * LANES
    xp = jnp.pad(x, ((0, 0), (0, n_pad - n)))
    out = pl.pallas_call(
        _cumsum_body,
        out_shape=jax.ShapeDtypeStruct((rows, n_pad), F32),
        compiler_params=pltpu.CompilerParams(vmem_limit_bytes=VMEM_LIMIT),
        name="cumsum_rows",
    )(xp)
    return out[:, :n]


def _split_cols(w, sizes):
    out, off = [], 0
    for n in sizes:
        out.append(w[:, off:off + n])
        off += n
    return out


def _rope_tables(pos, n_rows):
    half = A_ROPE // 2
    inv_freq = ROPE_THETA ** (-jnp.arange(half, dtype=F32) / half)
    ang = pos.astype(F32)[:, None] * inv_freq[None, :]
    cos, sin = jnp.cos(ang), jnp.sin(ang)
    zeros = jnp.zeros((pos.shape[0], LANES - A_ROPE), F32)
    cos_t = jnp.concatenate([cos, cos, zeros], axis=1)
    sin_t = jnp.concatenate([-sin, sin, zeros], axis=1)
    reps = max(1, n_rows // pos.shape[0])
    return jnp.tile(cos_t, (reps, 1)), jnp.tile(sin_t, (reps, 1))


def _swap_halves(w):
    half = w.shape[-1] // 2
    return jnp.concatenate([w[..., half:], w[..., :half]], axis=-1)


def _pad_lanes(w):
    return jnp.pad(w, [(0, 0)] * (w.ndim - 1) + [(0, LANES - w.shape[-1])])


def _prep_even(w_in, q_norm, w_uq, kv_norm, w_uk, w_uv):
    wqa, wkv, wkr, wga, wqb, wkb, wvb, wgb = _split_cols(
        w_in, (A_Q_LORA, A_KV_LORA, A_ROPE, WIDTH, WIDTH, WIDTH, WIDTH, WIDTH))
    b = lambda a: a.astype(BF16)
    uq_rope = w_uq[:, :, A_NOPE:]
    uk_t = jnp.transpose(w_uk, (1, 2, 0))
    z = jnp.zeros((A_NOPE, A_KV_LORA), w_uk.dtype)
    wuk = jnp.stack([jnp.block([[uk_t[2 * p], z], [z, uk_t[2 * p + 1]]]) for p in range(N_PAIRS)])
    uv_t = jnp.transpose(w_uv, (1, 0, 2))
    zv = jnp.zeros((A_KV_LORA, A_V), w_uv.dtype)
    wuv = jnp.stack([jnp.block([[uv_t[2 * p], zv], [zv, uv_t[2 * p + 1]]]) for p in range(N_PAIRS)])
    return dict(
        wqa=b(wqa), wkv=b(wkv), wkr=b(_pad_lanes(wkr)), wkrs=b(_pad_lanes(_swap_halves(wkr))),
        wga=b(wga), wqb=b(wqb), wkb=b(wkb), wvb=b(wvb), wgb=b(wgb),
        qn=q_norm.reshape(1, -1), kvn=kv_norm.reshape(1, -1),
        wuqn=b(w_uq[:, :, :A_NOPE].reshape(A_Q_LORA, A_HEADS * A_NOPE)),
        wuqr=b(_pad_lanes(uq_rope).reshape(A_Q_LORA, A_HEADS * LANES)),
        wuqrs=b(_pad_lanes(_swap_halves(uq_rope)).reshape(A_Q_LORA, A_HEADS * LANES)),
        wuk=b(wuk), wuv=b(wuv))


def _prep_odd(w_in, forget_bias):
    wqc, wkc, wvc, wgc, wqd, wkd, wvd, wf, wgd = _split_cols(
        w_in, (WIDTH, WIDTH, WIDTH, WIDTH, WIDTH, WIDTH, WIDTH, HEADS, WIDTH))
    b = lambda a: a.astype(BF16)
    return dict(wqc=b(wqc), wkc=b(wkc), wvc=b(wvc), wgc=b(wgc), wqd=b(wqd), wkd=b(wkd), wvd=b(wvd),
                wf=b(_pad_lanes(wf)), wgd=b(wgd), fb=forget_bias.astype(F32).reshape(1, HEADS),
                wf_t=b(wf.T), fb_t=forget_bias.astype(F32).reshape(HEADS, 1))


def kernel(x_prompt, x_sample, cache_mla_ckv, cache_mla_krope, cache_sb_k, cache_sb_v, cache_band_k,
           cache_band_v, cache_fox_k, cache_fox_v, cache_fox_logf, norm_pre, norm_post, w_in_even,
           a_q_norm, a_w_uq, a_kv_norm, a_w_uk, a_w_uv, w_out_even, w_in_odd, c_rel_bias,
           d_forget_bias, w_out_odd):
    n_b, n_s, _ = x_prompt.shape
    d_b, d_s, _ = x_sample.shape
    n_past = cache_sb_k.shape[2]
    n_keep = cache_band_k.shape[2]
    assert n_s % MLA_QB == 0 and n_past % TKC == 0 and d_s == CHUNK and n_past % CHUNK == 0
    assert n_keep == LEFT_CTX and (d_b * d_s) % TM == 0 and TM % d_s == 0

    xp = x_prompt.reshape(n_b * n_s, D_MODEL)
    xs = x_sample.reshape(d_b * d_s, D_MODEL)
    row = lambda a: a.reshape(1, -1)
    heads = lambda a, b, s: a.reshape(b, s, HEADS, HEAD_DIM)
    rows_minor = lambda a: a.transpose(0, 2, 3, 1).reshape(a.shape[0], WIDTH, a.shape[1])

    we = _prep_even(w_in_even[0], a_q_norm[0], a_w_uq[0], a_kv_norm[0], a_w_uk[0], a_w_uv[0])
    wout_e = w_out_even[0].astype(BF16)
    cos_p, sin_p = _rope_tables(jnp.arange(n_s), TM)
    cos_s, sin_s = _rope_tables(n_past + jnp.arange(d_s), TM)

    (ckv_p, kr_p, k2_p, q2_p, ga_p, gb_p, qb_p, kb_p, kb16_p, vb_p, vb16_p) = _in_even(
        xp, row(norm_pre[0]), cos_p, sin_p, we, seq_len=n_s)
    (ckv_s, kr_s, k2_s, q2_s, ga_s, gb_s, qb_s, kb_s, kb16_s, vb_s, vb16_s) = _in_even(
        xs, row(norm_pre[0]), cos_s, sin_s, we)

    lat_p = _mla_prompt(q2_p, k2_p, n_b, n_s).reshape(n_b * n_s, A_HEADS * A_KV_LORA)
    lat_s = _mla_sample(q2_s, k2_s, cache_mla_ckv[0], cache_mla_krope[0].transpose(0, 2, 1), d_b, d_s
                        ).reshape(d_b * d_s, A_HEADS * A_KV_LORA)
    sb_p = _sb_prompt(qb_p, kb16_p, vb16_p, n_b, n_s)
    sb_s = _sb_sample(qb_s, kb16_s, vb16_s, rows_minor(cache_sb_k[0]), rows_minor(cache_sb_v[0]), d_b, d_s)

    xp1 = _out_proj(xp, row(norm_post[0]), ga_p, gb_p, lat_p, sb_p, wout_e, we['wuv'])
    xs1 = _out_proj(xs, row(norm_post[0]), ga_s, gb_s, lat_s, sb_s, wout_e, we['wuv'])

    wo = _prep_odd(w_in_odd[0], d_forget_bias[0])
    wout_o = w_out_odd[0].astype(BF16)
    (qc_p, kc_p, kc16_p, vc_p, vc16_p, gc_p, qd_p, kd_p, kd16_p, vd_p, vd16_p, lf_p, gd_p) = _in_odd(
        xp1, row(norm_pre[1]), wo, seq_len=n_s)
    (qc_s, kc_s, kc16_s, vc_s, vc16_s, gc_s, qd_s, kd_s, kd16_s, vd_s, vd16_s, lf_s, gd_s) = _in_odd(
        xs1, row(norm_pre[1]), wo)

    band_p = _band_prompt(qc_p, kc16_p, vc16_p, _band_bias(c_rel_bias[0], BAND_TQ), n_b, n_s)
    band_s = _band_sample(qc_s, kc16_s, vc16_s, rows_minor(cache_band_k[0]), rows_minor(cache_band_v[0]),
                          _band_bias(c_rel_bias[0], d_s), d_b, d_s)

    lf_s3 = lf_s.reshape(d_b, d_s, HEADS)
    cum_p = _cumsum_rows(lf_p.reshape(n_b * HEADS, n_s)).reshape(n_b, HEADS, n_s)
    lf_all = jnp.concatenate([cache_fox_logf[0].astype(F32), lf_s3], axis=1)
    cum_s = _cumsum_rows(lf_all.transpose(0, 2, 1).reshape(d_b * HEADS, n_past + d_s)
                         ).reshape(d_b, HEADS, n_past + d_s)
    fox_p = _fox_prompt(qd_p, kd16_p, vd16_p, cum_p, n_b, n_s)
    fox_s = _fox_sample(qd_s, kd16_s, vd16_s, rows_minor(cache_fox_k[0]), rows_minor(cache_fox_v[0]), cum_s,
                        d_b, d_s)

    xp2 = _out_proj(xp1, row(norm_post[1]), gc_p, gd_p, band_p, fox_p, wout_o)
    xs2 = _out_proj(xs1, row(norm_post[1]), gc_s, gd_s, band_s, fox_s, wout_o)

    keep = min(LEFT_CTX, n_s)
    band_k_s = jnp.concatenate([cache_band_k[0], heads(kc_s, d_b, d_s)], axis=1)[:, d_s:]
    band_v_s = jnp.concatenate([cache_band_v[0], heads(vc_s, d_b, d_s)], axis=1)[:, d_s:]
    one = lambda a: a[None]
    heads_t = lambda a: a.reshape(n_b, HEADS, HEAD_DIM, a.shape[-1]).transpose(0, 3, 1, 2)
    return (xp2.reshape(n_b, n_s, D_MODEL), xs2.reshape(d_b, d_s, D_MODEL),
            one(ckv_p.reshape(n_b, n_s, A_KV_LORA)), one(kr_p.transpose(0, 2, 1)),
            one(heads_t(kb_p)), one(heads_t(vb_p)),
            one(heads_t(kc_p[:, :, n_s - keep:])), one(heads_t(vc_p[:, :, n_s - keep:])),
            one(heads_t(kd_p)), one(heads_t(vd_p)), one(lf_p.transpose(0, 2, 1)),
            one(ckv_s.reshape(d_b, d_s, A_KV_LORA)), one(kr_s.reshape(d_b, d_s, A_ROPE)),
            one(heads(kb_s, d_b, d_s)), one(heads(vb_s, d_b, d_s)),
            one(band_k_s), one(band_v_s),
            one(heads(kd_s, d_b, d_s)), one(heads(vd_s, d_b, d_s)), one(lf_s3))
```

```python
import functools

import numpy as np
import jax
import jax.numpy as jnp
from jax import lax
from jax.experimental import pallas as pl
from jax.experimental.pallas import tpu as pltpu

F32 = jnp.float32
BF16 = jnp.bfloat16

D_MODEL = 1024
PAST_LEN = 4096
CHUNK = 64
LEFT_CTX = 512
REL_CLIP = 128
EPS = 1e-6
NEG = -1e30
ROPE_THETA = 10000.0
A_HEADS = 8
A_Q_LORA = 256
A_KV_LORA = 128
A_NOPE = 64
A_ROPE = 32
A_V = 64
A_SCALE = (A_NOPE + A_ROPE) ** -0.5
HEADS = 8
HEAD_DIM = 64
WIDTH = HEADS * HEAD_DIM
QK_SCALE = HEAD_DIM ** -0.5
LOG2E = 1.4426950408889634
MLA_LOGIT_SCALE = A_SCALE * LOG2E
N_PAIRS = HEADS // 2

LANES = 128
VMEM_LIMIT = 52 * 1024 * 1024
TM = 512
TK = 256
TKC = 2048
SB_T = 256
SB_TK = 256
FOX_TQ = 512
BAND_TQ = 256
MLA_RT = 256
MLA_QB = 512
PAIRS_PER_STEP = 4
STICK_DEAD = -104.0


def _dot(a, b):
    return jnp.dot(a, b, preferred_element_type=F32)


def _dot_nt(a, b):
    return lax.dot_general(a, b, (((1,), (1,)), ((), ())), preferred_element_type=F32)


def _rms(x, g):
    y = x * lax.rsqrt(jnp.mean(x * x, axis=-1, keepdims=True) + EPS)
    return y * g


def _log_sigmoid(z):
    return jnp.minimum(z, 0.0) - jnp.log(1.0 + jnp.exp(-jnp.abs(z)))


def _silu(g):
    return g / (1.0 + jnp.exp(-g))


def _stack_pair(q2):
    qf = q2.astype(F32)
    lane = lax.broadcasted_iota(jnp.int32, qf.shape, 1)
    even = jnp.where(lane < HEAD_DIM, qf, 0.0)
    odd = jnp.where(lane >= HEAD_DIM, qf, 0.0)
    return jnp.concatenate([even, odd], axis=0).astype(BF16)


def _merge_pair(o, tq):
    top, bot = o[:tq], o[tq:]
    lane = lax.broadcasted_iota(jnp.int32, top.shape, 1)
    return jnp.where(lane < HEAD_DIM, top, bot)


def _params(*sem):
    return pltpu.CompilerParams(dimension_semantics=sem, vmem_limit_bytes=VMEM_LIMIT)


def _const_spec(shape):
    nd = len(shape)
    return pl.BlockSpec(shape, lambda *_: (0,) * nd)


def _in_even_body(x_ref, gpre_ref, cos_ref, sin_ref, wqa_ref, wkv_ref, wkr_ref, wkrs_ref, wga_ref,
                  wqb_ref, wkb_ref, wvb_ref, wgb_ref, qn_ref, kvn_ref, wuqn_ref, wuqr_ref,
                  wuqrs_ref, wuk_ref,
                  ckv_ref, krope_ref, k2_ref, q2_ref, ga_ref, gb_ref, qb_ref, kb_ref, kb16_ref,
                  vb_ref, vb16_ref, *, rows_minor):
    h = _rms(x_ref[...], gpre_ref[...]).astype(BF16)
    cos = cos_ref[...]
    sin = sin_ref[...]
    ckv = _rms(_dot(h, wkv_ref[...]), kvn_ref[...])
    ckv_ref[...] = ckv
    kr = _dot(h, wkr_ref[...]) * cos + _dot(h, wkrs_ref[...]) * sin
    krope_ref[...] = kr.T[:A_ROPE] if rows_minor else kr[:, :A_ROPE]
    k2_ref[:, :LANES] = ckv.astype(BF16)
    k2_ref[:, LANES:] = kr.astype(BF16)
    cq = _rms(_dot(h, wqa_ref[...]), qn_ref[...]).astype(BF16)
    qn = _dot(cq, wuqn_ref[...]).astype(BF16)
    for p in range(N_PAIRS):
        ql = _dot(qn[:, p * LANES:(p + 1) * LANES], wuk_ref[p])
        q2_ref[:, (2 * p) * 256:(2 * p) * 256 + LANES] = ql[:, :LANES].astype(BF16)
        q2_ref[:, (2 * p + 1) * 256:(2 * p + 1) * 256 + LANES] = ql[:, LANES:].astype(BF16)
    qr = _dot(cq, wuqr_ref[...])
    qrs = _dot(cq, wuqrs_ref[...])
    for hd in range(A_HEADS):
        rot = qr[:, hd * LANES:(hd + 1) * LANES] * cos + qrs[:, hd * LANES:(hd + 1) * LANES] * sin
        q2_ref[:, hd * 256 + LANES:(hd + 1) * 256] = rot.astype(BF16)
    ga_ref[...] = _dot(h, wga_ref[...]).astype(BF16)
    gb_ref[...] = _dot(h, wgb_ref[...]).astype(BF16)
    qb = _dot(h, wqb_ref[...]) * QK_SCALE
    kb = _dot(h, wkb_ref[...])
    vb = _dot(h, wvb_ref[...])
    kb_ref[...] = kb.T if rows_minor else kb
    vb_ref[...] = vb.T if rows_minor else vb
    for p in range(N_PAIRS):
        sl = slice(p * LANES, (p + 1) * LANES)
        qb_ref[p] = qb[:, sl].astype(BF16)
        kb16_ref[p] = kb[:, sl].astype(BF16)
        vb16_ref[p] = vb[:, sl].astype(BF16)


def _state_specs(rows, seq_len):
    sds = jax.ShapeDtypeStruct
    if seq_len is None:
        return (lambda n: pl.BlockSpec((TM, n), lambda i: (i, 0))), (lambda n: sds((rows, n), F32))
    nt = seq_len // TM
    return ((lambda n: pl.BlockSpec((None, n, TM), lambda i: (i // nt, 0, i % nt))),
            (lambda n: sds((rows // seq_len, n, seq_len), F32)))


def _in_even(x, gpre, cos, sin, w, seq_len=None):
    rows = x.shape[0]
    n_tab = cos.shape[0] // TM
    row_spec = lambda n: pl.BlockSpec((TM, n), lambda i: (i, 0))
    pm_spec = pl.BlockSpec((N_PAIRS, TM, LANES), lambda i: (0, i, 0))
    tab_spec = pl.BlockSpec((TM, LANES), lambda i: (i % n_tab, 0))
    weights = [w['wqa'], w['wkv'], w['wkr'], w['wkrs'], w['wga'], w['wqb'], w['wkb'], w['wvb'], w['wgb'],
               w['qn'], w['kvn'], w['wuqn'], w['wuqr'], w['wuqrs'], w['wuk']]
    in_specs = ([row_spec(D_MODEL), _const_spec((1, D_MODEL)), tab_spec, tab_spec]
                + [_const_spec(a.shape) for a in weights])
    args = [x, gpre, cos, sin, *weights]
    st_spec, st_shape = _state_specs(rows, seq_len)
    sds = jax.ShapeDtypeStruct
    pm = sds((N_PAIRS, rows, LANES), BF16)
    return pl.pallas_call(
        functools.partial(_in_even_body, rows_minor=seq_len is not None),
        grid=(rows // TM,),
        in_specs=in_specs,
        out_specs=[row_spec(A_KV_LORA), st_spec(A_ROPE), row_spec(256), row_spec(A_HEADS * 256),
                   row_spec(WIDTH), row_spec(WIDTH), pm_spec, st_spec(WIDTH), pm_spec,
                   st_spec(WIDTH), pm_spec],
        out_shape=[sds((rows, A_KV_LORA), F32), st_shape(A_ROPE), sds((rows, 256), BF16),
                   sds((rows, A_HEADS * 256), BF16), sds((rows, WIDTH), BF16), sds((rows, WIDTH), BF16),
                   pm, st_shape(WIDTH), pm, st_shape(WIDTH), pm],
        compiler_params=_params("parallel"),
        name="in_proj_even",
    )(*args)


def _in_odd_body(*refs, rows_minor):
    (x_ref, gpre_ref, fb_ref, wqc_ref, wkc_ref, wvc_ref, wgc_ref, wqd_ref, wkd_ref, wvd_ref, wf_ref,
     wgd_ref) = refs[:12]
    n_in = 14 if rows_minor else 12
    (qc_ref, kc_ref, kc16_ref, vc_ref, vc16_ref, gc_ref, qd_ref, kd_ref, kd16_ref, vd_ref, vd16_ref,
     logf_ref, gd_ref) = refs[n_in:]
    h = _rms(x_ref[...], gpre_ref[...]).astype(BF16)
    gc_ref[...] = _dot(h, wgc_ref[...]).astype(BF16)
    gd_ref[...] = _dot(h, wgd_ref[...]).astype(BF16)
    if rows_minor:
        wf_t, fb_t = refs[12:14]
        logf_ref[...] = _log_sigmoid(_dot_nt(wf_t[...], h) + fb_t[...])
    else:
        logf_ref[...] = _log_sigmoid(_dot(h, wf_ref[...])[:, :HEADS] + fb_ref[...])
    for q_w, k_w, v_w, q_o, k_o, k16_o, v_o, v16_o in (
            (wqc_ref, wkc_ref, wvc_ref, qc_ref, kc_ref, kc16_ref, vc_ref, vc16_ref),
            (wqd_ref, wkd_ref, wvd_ref, qd_ref, kd_ref, kd16_ref, vd_ref, vd16_ref)):
        q = _dot(h, q_w[...]) * (QK_SCALE * LOG2E)
        k = _dot(h, k_w[...])
        v = _dot(h, v_w[...])
        k_o[...] = k.T if rows_minor else k
        v_o[...] = v.T if rows_minor else v
        for p in range(N_PAIRS):
            sl = slice(p * LANES, (p + 1) * LANES)
            q_o[p] = q[:, sl].astype(BF16)
            k16_o[p] = k[:, sl].astype(BF16)
            v16_o[p] = v[:, sl].astype(BF16)


def _in_odd(x, gpre, w, seq_len=None):
    rows = x.shape[0]
    row_spec = lambda n: pl.BlockSpec((TM, n), lambda i: (i, 0))
    pm_spec = pl.BlockSpec((N_PAIRS, TM, LANES), lambda i: (0, i, 0))
    weights = [w['wqc'], w['wkc'], w['wvc'], w['wgc'], w['wqd'], w['wkd'], w['wvd'], w['wf'], w['wgd']]
    rows_minor = seq_len is not None
    if rows_minor:
        weights += [w['wf_t'], w['fb_t']]
    st_spec, st_shape = _state_specs(rows, seq_len)
    sds = jax.ShapeDtypeStruct
    pm = sds((N_PAIRS, rows, LANES), BF16)
    full = sds((rows, WIDTH), BF16)
    return pl.pallas_call(
        functools.partial(_in_odd_body, rows_minor=rows_minor),
        grid=(rows // TM,),
        in_specs=[row_spec(D_MODEL), _const_spec((1, D_MODEL)), _const_spec((1, HEADS))]
                 + [_const_spec(a.shape) for a in weights],
        out_specs=[pm_spec, st_spec(WIDTH), pm_spec, st_spec(WIDTH), pm_spec, row_spec(WIDTH),
                   pm_spec, st_spec(WIDTH), pm_spec, st_spec(WIDTH), pm_spec, st_spec(HEADS),
                   row_spec(WIDTH)],
        out_shape=[pm, st_shape(WIDTH), pm, st_shape(WIDTH), pm, full, pm, st_shape(WIDTH), pm,
                   st_shape(WIDTH), pm, st_shape(HEADS), full],
        compiler_params=_params("parallel"),
        name="in_proj_odd",
    )(x, gpre, w['fb'], *weights)


def _out_body(*refs, mla):
    if mla:
        x_ref, gpost_ref, g1_ref, g2_ref, a_ref, b_ref, wuv_ref, wout_ref, o_ref, mix_ref = refs
    else:
        x_ref, gpost_ref, g1_ref, g2_ref, a_ref, b_ref, wout_ref, o_ref, mix_ref = refs
    s1 = _silu(g1_ref[...].astype(F32))
    s2 = _silu(g2_ref[...].astype(F32))
    for p in range(N_PAIRS):
        sl = slice(p * LANES, (p + 1) * LANES)
        if mla:
            a = _dot(a_ref[:, p * 256:(p + 1) * 256], wuv_ref[p])
        else:
            a = a_ref[p].astype(F32)
        mix_ref[:, sl] = (s1[:, sl] * a).astype(BF16)
        mix_ref[:, WIDTH + p * LANES:WIDTH + (p + 1) * LANES] = (
            s2[:, sl] * b_ref[p].astype(F32)).astype(BF16)
    y = _dot(mix_ref[...], wout_ref[...])
    o_ref[...] = x_ref[...] + _rms(y, gpost_ref[...])


def _out_proj(x, gpost, g1, g2, a, b, wout, wuv=None):
    rows = x.shape[0]
    mla = wuv is not None
    row_spec = lambda n: pl.BlockSpec((TM, n), lambda i: (i, 0))
    pm_spec = pl.BlockSpec((N_PAIRS, TM, LANES), lambda i: (0, i, 0))
    in_specs = [row_spec(D_MODEL), _const_spec((1, D_MODEL)), row_spec(WIDTH), row_spec(WIDTH),
                row_spec(A_HEADS * A_KV_LORA) if mla else pm_spec, pm_spec]
    args = [x, gpost, g1, g2, a, b]
    if mla:
        in_specs.append(_const_spec(wuv.shape))
        args.append(wuv)
    in_specs.append(_const_spec(wout.shape))
    args.append(wout)
    return pl.pallas_call(
        functools.partial(_out_body, mla=mla),
        grid=(rows // TM,),
        in_specs=in_specs,
        out_specs=row_spec(D_MODEL),
        out_shape=jax.ShapeDtypeStruct((rows, D_MODEL), F32),
        scratch_shapes=[pltpu.VMEM((TM, 2 * WIDTH), BF16)],
        compiler_params=_params("parallel"),
        name="out_proj_even" if mla else "out_proj_odd",
    )(*args)


def _softmax_init(m_ref, accl_ref):
    m_ref[...] = jnp.full(m_ref.shape, NEG, F32)
    accl_ref[...] = jnp.zeros(accl_ref.shape, F32)


def _lanes(x, n):
    parts = [x] * (n // LANES)
    if n % LANES:
        parts.append(x[:, :n % LANES])
    return parts[0] if len(parts) == 1 else jnp.concatenate(parts, axis=1)


def _scores(q, k, kv_t):
    return _dot(q, k) if kv_t else _dot_nt(q, k)


def _weighted(p, v, kv_t):
    return _dot_nt(p, v) if kv_t else _dot(p, v)


def _with_ones(v, kv_t=False):
    return jnp.concatenate([v, jnp.ones(v.shape, BF16)], axis=0 if kv_t else 1)


def _softmax_update(s, v1, m_ref, accl_ref, kv_t=False):
    keys = s.shape[1]
    n = v1.shape[0] if kv_t else v1.shape[1]
    m_prev = m_ref[...]
    m_new = jnp.maximum(m_prev, jnp.max(s, axis=1, keepdims=True))
    alpha = jnp.exp2(m_prev - m_new)
    p = jnp.exp2(s - _lanes(m_new, keys))
    accl_ref[...] = _lanes(alpha, n) * accl_ref[...] + _weighted(p.astype(BF16), v1, kv_t)
    m_ref[...] = m_new


def _softmax_result(accl_ref):
    accl = accl_ref[...]
    n = accl.shape[1] // 2
    return accl[:, :n] / accl[:, n:]


def _fill_suffix_ones(tri_ref):
    n = tri_ref.shape[0]
    r = lax.broadcasted_iota(jnp.int32, (n, n), 0)
    c = lax.broadcasted_iota(jnp.int32, (n, n), 1)
    tri_ref[...] = jnp.where(r > c, 1.0, 0.0).astype(BF16)


def _stick_tiles(qs, k2, v2, tri, acc_refs, car_refs, mask, kv_t=False):
    n = len(qs)
    z = [_scores(qs[i], k2[i], kv_t) for i in range(n)]
    lb, l1, suf = [], [], []
    for i in range(n):
        lb.append(_log_sigmoid(z[i]))
        l = lb[i] - z[i]
        l1.append(l if mask is None else jnp.where(mask, l, 0.0))
        hi = l1[i].astype(BF16)
        lo = (l1[i] - hi.astype(F32)).astype(BF16)
        suf.append(_dot(hi, tri) + _dot(lo, tri))
    for i in range(n):
        w = jnp.exp(lb[i] + (suf[i] + car_refs[i][...]))
        if mask is not None:
            w = jnp.where(mask, w, 0.0)
        acc_refs[i][...] += _weighted(w.astype(BF16), v2[i], kv_t)
        car_refs[i][...] += jnp.sum(l1[i], axis=1, keepdims=True)


def _stick_alive(car_ref):
    return (jnp.max(car_ref[...]) >= STICK_DEAD).astype(jnp.int32)


def _local_causal(rows, keys, tq, strict, key_offset=0):
    r = lax.broadcasted_iota(jnp.int32, (rows, keys), 0) & (tq - 1)
    c = lax.broadcasted_iota(jnp.int32, (rows, keys), 1) + key_offset
    return (c < r) if strict else (c <= r)


def _decay_tile(s, v2, cq, ck, causal, m_ref, accl_ref, kv_t=False):
    tq, keys = cq.shape[0], ck.shape[1]
    for hh in range(2):
        rows = slice(hh * tq, (hh + 1) * tq)
        sh = s[rows] + (cq[:, hh:hh + 1] - ck[hh:hh + 1])
        if causal:
            sh = jnp.where(_local_causal(tq, keys, tq, strict=False), sh, NEG)
        _softmax_update(sh, _with_ones(v2, kv_t), m_ref.at[rows], accl_ref.at[rows], kv_t)


def _split3(c):
    hi = c.astype(BF16).astype(F32)
    rest = c - hi
    mid = rest.astype(BF16).astype(F32)
    return hi, mid, rest - mid


def _decay_lanes(c, base, query):
    hi, mid, lo = _split3(c if query else -c)
    lane = lax.broadcasted_iota(jnp.int32, (c.shape[0], LANES), 1) - (base if query else base + 3)
    parts = jnp.where(lane == 0, hi, jnp.where(lane == 1, mid, jnp.where(lane == 2, lo, 0.0)))
    ones_at = lane + 3 if not query else lane - 3
    return jnp.where((ones_at >= 0) & (ones_at < 3), 1.0, parts)


def _decay_operand(x, cum, hh, query):
    lane = lax.broadcasted_iota(jnp.int32, x.shape, 1)
    own = (lane < HEAD_DIM) if hh == 0 else (lane >= HEAD_DIM)
    base = HEAD_DIM if hh == 0 else 0
    return jnp.where(own, x, _decay_lanes(cum[:, hh:hh + 1], base, query)).astype(BF16)


def _one_ahead(score_fns, consume_fns):
    ahead = score_fns[0]()
    for i, consume in enumerate(consume_fns):
        s = ahead
        if i + 1 < len(score_fns):
            ahead = score_fns[i + 1]()
        consume(s)


def _decay_head_update(s, v1, causal_offset, m_ref, accl_ref):
    tq, keys = s.shape
    if causal_offset is not None:
        s = jnp.where(_local_causal(tq, keys, tq, strict=False, key_offset=causal_offset), s, NEG)
    _softmax_update(s, v1, m_ref, accl_ref)


def _pair_result(accl, tq):
    o = accl / pltpu.roll(accl, HEAD_DIM, axis=1)
    return _merge_pair(o, tq)


def _mla_prompt_body(q_ref, k_ref, o_ref, m_ref, accl_ref):
    g = pl.program_id(1)
    grp = 4 * CHUNK

    def update_heads(r0, k, mask):
        sl = pl.ds(r0, grp)
        v1 = _with_ones(k[:, :A_KV_LORA])

        def consume(s, h):
            s = s * MLA_LOGIT_SCALE
            if mask is not None:
                s = jnp.where(mask, s, NEG)
            _softmax_update(s, v1, m_ref.at[h, sl], accl_ref.at[h, sl])

        _one_ahead([lambda h=h: _dot_nt(q_ref[sl, h * 256:(h + 1) * 256], k) for h in range(A_HEADS)],
                   [lambda s, h=h: consume(s, h) for h in range(A_HEADS)])

    def group(gl, carry):
        gq = g * (MLA_QB // grp) + gl
        g0 = pl.multiple_of(gl * grp, grp)
        for h in range(A_HEADS):
            _softmax_init(m_ref.at[h, pl.ds(g0, grp)], accl_ref.at[h, pl.ds(g0, grp)])

        def earlier(k0, n_keys):
            update_heads(g0, k_ref[pl.ds(k0, n_keys), :], None)

        def kv(j, c2):
            earlier(pl.multiple_of(j * 2 * TK, 2 * TK), 2 * TK)
            return c2

        lax.fori_loop(0, gq // 2, kv, 0)

        @pl.when(gq % 2 == 1)
        def _():
            earlier(pl.multiple_of((gq - 1) * TK, TK), TK)

        kd = k_ref[pl.ds(pl.multiple_of(gq * TK, TK), TK), :]
        row_chunk = lax.broadcasted_iota(jnp.int32, (grp, TK), 0) // CHUNK
        key_chunk = lax.broadcasted_iota(jnp.int32, (grp, TK), 1) // CHUNK
        update_heads(g0, kd, key_chunk <= row_chunk)
        for h in range(A_HEADS):
            o_ref[pl.ds(g0, grp), h * A_KV_LORA:(h + 1) * A_KV_LORA] = _softmax_result(
                accl_ref.at[h, pl.ds(g0, grp)]).astype(BF16)
        return carry

    lax.fori_loop(0, MLA_QB // grp, group, 0)


def _mla_prompt(q2, k2, n_b, n_s):
    assert n_s % MLA_QB == 0 and TK == 4 * CHUNK
    return pl.pallas_call(
        _mla_prompt_body,
        grid=(n_b, n_s // MLA_QB),
        in_specs=[pl.BlockSpec((None, MLA_QB, A_HEADS * 256), lambda b, g: (b, g, 0)),
                  pl.BlockSpec((None, n_s, 256), lambda b, g: (b, 0, 0))],
        out_specs=pl.BlockSpec((None, MLA_QB, A_HEADS * A_KV_LORA), lambda b, g: (b, g, 0)),
        out_shape=jax.ShapeDtypeStruct((n_b, n_s, A_HEADS * A_KV_LORA), BF16),
        scratch_shapes=[pltpu.VMEM((A_HEADS, MLA_QB, LANES), F32),
                        pltpu.VMEM((A_HEADS, MLA_QB, 2 * LANES), F32)],
        compiler_params=_params("parallel", "parallel"),
        name="mla_prompt",
    )(q2.reshape(n_b, n_s, A_HEADS * 256), k2.reshape(n_b, n_s, 256))


def _mla_sample_body(q_ref, kn_ref, ckv_ref, kr_ref, o_ref, m_ref, accl_ref):
    j = pl.program_id(1)

    @pl.when(j == 0)
    def _():
        _softmax_init(m_ref, accl_ref)

    n_tiles = q_ref.shape[0] // MLA_RT
    ck = ckv_ref[...].astype(BF16)
    kr = kr_ref[...].T.astype(BF16)
    k = jnp.concatenate([ck, kr, jnp.zeros((TKC, LANES - A_ROPE), BF16)], axis=1)
    row_tiles = [slice(t * MLA_RT, (t + 1) * MLA_RT) for t in range(n_tiles)]

    def update_tiles(keys, values):
        scores = [_dot_nt(q_ref[sl, :], keys) for sl in row_tiles]
        v1 = _with_ones(values)
        for sl, s in zip(row_tiles, scores):
            _softmax_update(s * MLA_LOGIT_SCALE, v1, m_ref.at[sl], accl_ref.at[sl])

    update_tiles(k, ck)

    @pl.when(j == pl.num_programs(1) - 1)
    def _():
        kn = kn_ref[...]
        update_tiles(kn, kn[:, :A_KV_LORA])
        o_ref[...] = _softmax_result(accl_ref).astype(BF16)


def _mla_sample(q2, k2, cache_ckv, cache_kr, n_b, n_s):
    rows = n_s * A_HEADS
    n_past = cache_ckv.shape[1]
    return pl.pallas_call(
        _mla_sample_body,
        grid=(n_b, n_past // TKC),
        in_specs=[pl.BlockSpec((None, rows, 256), lambda b, j: (b, 0, 0)),
                  pl.BlockSpec((None, n_s, 256), lambda b, j: (b, 0, 0)),
                  pl.BlockSpec((None, TKC, A_KV_LORA), lambda b, j: (b, j, 0)),
                  pl.BlockSpec((None, A_ROPE, TKC), lambda b, j: (b, 0, j))],
        out_specs=pl.BlockSpec((None, rows, A_KV_LORA), lambda b, j: (b, 0, 0)),
        out_shape=jax.ShapeDtypeStruct((n_b, rows, A_KV_LORA), BF16),
        scratch_shapes=[pltpu.VMEM((rows, LANES), F32), pltpu.VMEM((rows, 2 * LANES), F32)],
        compiler_params=_params("parallel", "arbitrary"),
        name="mla_sample",
    )(q2.reshape(n_b, rows, 256), k2.reshape(n_b, n_s, 256), cache_ckv, cache_kr)


def _sb_prompt_body(q_ref, k_ref, v_ref, o_ref, tri_ref, acc_ref, car_ref):
    n_s = q_ref.shape[1]
    _fill_suffix_ones(tri_ref)

    def qblock(i, carry):
        q0 = pl.multiple_of(i * SB_T, SB_T)
        qs = [_stack_pair(q_ref[g, pl.ds(q0, SB_T), :]) for g in range(PAIRS_PER_STEP)]
        acc_ref[...] = jnp.zeros(acc_ref.shape, F32)
        car_ref[...] = jnp.zeros(car_ref.shape, F32)

        def tiles(k0, n_keys, mask):
            pairs = range(PAIRS_PER_STEP)
            _stick_tiles(qs, [k_ref[g, pl.ds(k0, n_keys), :] for g in pairs],
                         [v_ref[g, pl.ds(k0, n_keys), :] for g in pairs], tri_ref[:n_keys, :n_keys],
                         [acc_ref.at[g] for g in pairs], [car_ref.at[g] for g in pairs], mask)

        tiles(q0, SB_T, _local_causal(2 * SB_T, SB_T, SB_T, strict=True))

        def kv(state):
            jj, _ = state
            tiles(pl.multiple_of(q0 - (jj + 1) * SB_TK, SB_TK), SB_TK, None)
            return jj + 1, _stick_alive(car_ref)

        lax.while_loop(lambda st: (st[0] < i * (SB_T // SB_TK)) & (st[1] > 0), kv,
                       (jnp.int32(0), _stick_alive(car_ref)))
        for g in range(PAIRS_PER_STEP):
            o_ref[g, pl.ds(q0, SB_T), :] = _merge_pair(acc_ref[g], SB_T).astype(o_ref.dtype)
        return carry

    lax.fori_loop(0, n_s // SB_T, qblock, 0)


def _pair_seq_spec(n_s):
    return pl.BlockSpec((PAIRS_PER_STEP, n_s, LANES), lambda b, g: (g, b, 0))


def _sb_prompt(q, k, v, n_b, n_s):
    assert n_s % SB_T == 0
    spec = _pair_seq_spec(n_s)
    return pl.pallas_call(
        _sb_prompt_body,
        grid=(n_b, N_PAIRS // PAIRS_PER_STEP),
        in_specs=[spec, spec, spec],
        out_specs=spec,
        out_shape=jax.ShapeDtypeStruct((N_PAIRS, n_b * n_s, LANES), BF16),
        scratch_shapes=[pltpu.VMEM((SB_T, SB_T), BF16), pltpu.VMEM((PAIRS_PER_STEP, 2 * SB_T, LANES), F32),
                        pltpu.VMEM((PAIRS_PER_STEP, 2 * SB_T, 1), F32)],
        compiler_params=_params("parallel", "parallel"),
        name="sb_prompt",
    )(q, k, v)


def _sb_sample_body(q_ref, kn_ref, vn_ref, ck_hbm, cv_hbm, o_ref, kbuf, vbuf, sem, tri_ref, acc_ref,
                    car_ref):
    b = pl.program_id(0)
    n_q = q_ref.shape[1]
    n_blk = ck_hbm.shape[2] // TK

    def block_copies(blk):
        rows = pl.ds(pl.multiple_of(blk * TK, TK), TK)
        return (pltpu.make_async_copy(ck_hbm.at[b, :, rows], kbuf, sem.at[0]),
                pltpu.make_async_copy(cv_hbm.at[b, :, rows], vbuf, sem.at[1]))

    def start(blk):
        for c in block_copies(blk):
            c.start()

    def cache_block(blk):
        for c in block_copies(blk):
            c.wait()
        tiles = lambda buf: [buf[p * LANES:(p + 1) * LANES, :].astype(BF16) for p in range(N_PAIRS)]
        _stick_tiles(qs, tiles(kbuf), tiles(vbuf), tri_ref[...], accs, cars, None, kv_t=True)

    start(n_blk - 1)
    _fill_suffix_ones(tri_ref)
    acc_ref[...] = jnp.zeros(acc_ref.shape, F32)
    car_ref[...] = jnp.zeros(car_ref.shape, F32)
    qs = [_stack_pair(q_ref[p]) for p in range(N_PAIRS)]
    accs = [acc_ref.at[p] for p in range(N_PAIRS)]
    cars = [car_ref.at[p] for p in range(N_PAIRS)]
    _stick_tiles(qs, [kn_ref[p] for p in range(N_PAIRS)], [vn_ref[p] for p in range(N_PAIRS)],
                 tri_ref[:n_q, :n_q], accs, cars, _local_causal(2 * n_q, n_q, n_q, strict=True))
    cache_block(n_blk - 1)

    def older(state):
        blk, _ = state
        start(blk)
        cache_block(blk)
        return blk - 1, _stick_alive(car_ref)

    lax.while_loop(lambda st: (st[0] >= 0) & (st[1] > 0), older,
                   (jnp.int32(n_blk - 2), _stick_alive(car_ref)))
    for p in range(N_PAIRS):
        o_ref[p] = _merge_pair(acc_ref[p], n_q).astype(o_ref.dtype)


def _sb_sample(q, kn, vn, cache_kt, cache_vt, n_b, n_s):
    new_spec = pl.BlockSpec((N_PAIRS, n_s, LANES), lambda b: (0, b, 0))
    hbm = pl.BlockSpec(memory_space=pl.ANY)
    return pl.pallas_call(
        _sb_sample_body,
        grid=(n_b,),
        in_specs=[new_spec, new_spec, new_spec, hbm, hbm],
        out_specs=new_spec,
        out_shape=jax.ShapeDtypeStruct((N_PAIRS, n_b * n_s, LANES), BF16),
        scratch_shapes=[pltpu.VMEM((WIDTH, TK), F32), pltpu.VMEM((WIDTH, TK), F32),
                        pltpu.SemaphoreType.DMA((2,)), pltpu.VMEM((TK, TK), BF16),
                        pltpu.VMEM((N_PAIRS, 2 * n_s, LANES), F32), pltpu.VMEM((N_PAIRS, 2 * n_s, 1), F32)],
        compiler_params=_params("arbitrary"),
        name="sb_sample",
    )(q, kn, vn, cache_kt, cache_vt)


def _band_block(scores, vwin, bias, valid_from, tq):
    s = scores + bias
    if valid_from is not None:
        col = lax.broadcasted_iota(jnp.int32, s.shape, 1)
        s = jnp.where(col >= valid_from, s, NEG)
    p = jnp.exp2(s - jnp.max(s, axis=1, keepdims=True))
    o = _dot(p.astype(BF16), vwin) / jnp.sum(p, axis=1, keepdims=True)
    return _merge_pair(o, tq)


def _band_prompt_body(q_ref, k_ref, v_ref, bias_ref, o_ref, kpad_ref, vpad_ref):
    n_s = q_ref.shape[1]
    win = LEFT_CTX + BAND_TQ
    zeros = jnp.zeros((PAIRS_PER_STEP, LEFT_CTX, LANES), BF16)
    kpad_ref[:, :LEFT_CTX, :] = zeros
    vpad_ref[:, :LEFT_CTX, :] = zeros
    kpad_ref[:, LEFT_CTX:, :] = k_ref[...]
    vpad_ref[:, LEFT_CTX:, :] = v_ref[...]

    def qblock(i, carry):
        q0 = pl.multiple_of(i * BAND_TQ, BAND_TQ)
        scores = [_dot_nt(_stack_pair(q_ref[g, pl.ds(q0, BAND_TQ), :]), kpad_ref[g, pl.ds(q0, win), :])
                  for g in range(PAIRS_PER_STEP)]
        for g in range(PAIRS_PER_STEP):
            o_ref[g, pl.ds(q0, BAND_TQ), :] = _band_block(
                scores[g], vpad_ref[g, pl.ds(q0, win), :], bias_ref[g], LEFT_CTX - q0, BAND_TQ
            ).astype(o_ref.dtype)
        return carry

    lax.fori_loop(0, n_s // BAND_TQ, qblock, 0)


def _band_prompt(q, k, v, bias, n_b, n_s):
    spec = _pair_seq_spec(n_s)
    win = LEFT_CTX + BAND_TQ
    return pl.pallas_call(
        _band_prompt_body,
        grid=(n_b, N_PAIRS // PAIRS_PER_STEP),
        in_specs=[spec, spec, spec,
                  pl.BlockSpec((PAIRS_PER_STEP, 2 * BAND_TQ, win), lambda b, g: (g, 0, 0))],
        out_specs=spec,
        out_shape=jax.ShapeDtypeStruct((N_PAIRS, n_b * n_s, LANES), BF16),
        scratch_shapes=[pltpu.VMEM((PAIRS_PER_STEP, LEFT_CTX + n_s, LANES), BF16),
                        pltpu.VMEM((PAIRS_PER_STEP, LEFT_CTX + n_s, LANES), BF16)],
        compiler_params=_params("parallel", "parallel"),
        name="band_prompt",
    )(q, k, v, bias)


def _band_sample_body(q_ref, kn_ref, vn_ref, kc_ref, vc_ref, bias_ref, o_ref):
    n_q = q_ref.shape[1]
    n_keep = kc_ref.shape[1]
    pair_lanes = [slice(p * LANES, (p + 1) * LANES) for p in range(N_PAIRS)]
    qs = [_stack_pair(q_ref[p]) for p in range(N_PAIRS)]
    scores_old = [_dot(qs[p], kc_ref[pair_lanes[p], :].astype(BF16)) for p in range(N_PAIRS)]
    scores_new = [_dot_nt(qs[p], kn_ref[p]) for p in range(N_PAIRS)]
    for p in range(N_PAIRS):
        sl = pair_lanes[p]
        bias = bias_ref[p]
        s_old = scores_old[p] + bias[:, :n_keep]
        s_new = scores_new[p] + bias[:, n_keep:]
        m = jnp.maximum(jnp.max(s_old, axis=1, keepdims=True), jnp.max(s_new, axis=1, keepdims=True))
        p_old = jnp.exp2(s_old - m)
        p_new = jnp.exp2(s_new - m)
        o = _dot_nt(p_old.astype(BF16), vc_ref[sl, :].astype(BF16)) + _dot(p_new.astype(BF16), vn_ref[p])
        total = jnp.sum(p_old, axis=1, keepdims=True) + jnp.sum(p_new, axis=1, keepdims=True)
        o_ref[p] = _merge_pair(o / total, n_q).astype(o_ref.dtype)


def _band_sample(q, kn, vn, cache_kt, cache_vt, bias, n_b, n_s):
    n_keep = cache_kt.shape[2]
    new_spec = pl.BlockSpec((N_PAIRS, n_s, LANES), lambda b: (0, b, 0))
    cache_spec = pl.BlockSpec((None, WIDTH, n_keep), lambda b: (b, 0, 0))
    return pl.pallas_call(
        _band_sample_body,
        grid=(n_b,),
        in_specs=[new_spec, new_spec, new_spec, cache_spec, cache_spec,
                  _const_spec((N_PAIRS, 2 * n_s, n_keep + n_s))],
        out_specs=new_spec,
        out_shape=jax.ShapeDtypeStruct((N_PAIRS, n_b * n_s, LANES), BF16),
        compiler_params=_params("parallel"),
        name="band_sample",
    )(q, kn, vn, cache_kt, cache_vt, bias)


def _band_bias(rel_bias, tq):
    win = LEFT_CTX + tq
    i = np.arange(tq)[:, None]
    w = np.arange(win)[None, :]
    qc, kc = i // CHUNK, w // CHUNK - LEFT_CTX // CHUNK
    ok = (kc <= qc) & (kc >= qc - LEFT_CTX // CHUNK)
    u = np.arange(win + tq - 1)
    rel = np.clip(LEFT_CTX + (tq - 1) - u, -REL_CLIP, REL_CLIP) + REL_CLIP
    diag = rel_bias.astype(F32)[:, rel]
    n = win + tq - 1
    flat = jnp.tile(diag, (1, tq))[:, tq - 1:tq - 1 + tq * (n - 1)]
    tab = flat.reshape(HEADS, tq, n - 1)[:, :, :win]
    tab = jnp.where(jnp.asarray(ok)[None], tab * LOG2E, NEG)
    return tab.reshape(N_PAIRS, 2 * tq, win)


def _fox_prompt_body(q_ref, k_ref, v_ref, cum_ref, o_ref, k1_ref, v1_ref, m_ref, accl_ref):
    n_s = q_ref.shape[1]

    def prepare(jb, carry):
        rows = pl.ds(pl.multiple_of(jb * TK, TK), TK)
        lane = lax.broadcasted_iota(jnp.int32, (TK, LANES), 1)
        for g in range(PAIRS_PER_STEP):
            k = k_ref[g, rows, :].astype(F32)
            v = v_ref[g, rows, :].astype(F32)
            cum = cum_ref[g, rows, :]
            for hh in range(2):
                k1_ref[g, hh, rows, :] = _decay_operand(k, cum, hh, query=False)
                own = (lane < HEAD_DIM) if hh == 0 else (lane >= HEAD_DIM)
                v1_ref[g, hh, rows, :] = jnp.where(own, v, 1.0).astype(BF16)
        return carry

    lax.fori_loop(0, n_s // TK, prepare, 0)

    tiles_per_q = FOX_TQ // TK

    def qblock(i, carry):
        q0 = pl.multiple_of(i * FOX_TQ, FOX_TQ)
        q1 = []
        for g in range(PAIRS_PER_STEP):
            q = q_ref[g, pl.ds(q0, FOX_TQ), :].astype(F32)
            cum = cum_ref[g, pl.ds(q0, FOX_TQ), :]
            q1.append([_decay_operand(q, cum, hh, query=True) for hh in range(2)])
        _softmax_init(m_ref, accl_ref)

        def tile(jb, causal_offset):
            rows = pl.ds(pl.multiple_of(jb * TK, TK), TK)
            chains = [(g, hh) for g in range(PAIRS_PER_STEP) for hh in range(2)]
            head_rows = lambda hh: slice(hh * FOX_TQ, (hh + 1) * FOX_TQ)
            _one_ahead(
                [lambda g=g, hh=hh: _dot_nt(q1[g][hh], k1_ref[g, hh, rows, :]) for g, hh in chains],
                [lambda s, g=g, hh=hh: _decay_head_update(
                    s, v1_ref[g, hh, rows, :], causal_offset, m_ref.at[g, head_rows(hh)],
                    accl_ref.at[g, head_rows(hh)]) for g, hh in chains])

        for t in range(tiles_per_q):
            tile(i * tiles_per_q + t, t * TK)

        def kv(jb, c2):
            tile(jb, None)
            return c2

        lax.fori_loop(0, i * tiles_per_q, kv, 0)
        for g in range(PAIRS_PER_STEP):
            o_ref[g, pl.ds(q0, FOX_TQ), :] = _pair_result(accl_ref[g], FOX_TQ).astype(o_ref.dtype)
        return carry

    lax.fori_loop(0, n_s // FOX_TQ, qblock, 0)


def _fox_prompt(q, k, v, cum, n_b, n_s):
    assert FOX_TQ % TK == 0 and n_s % FOX_TQ == 0
    spec = _pair_seq_spec(n_s)
    cum_rows = cum.reshape(n_b, N_PAIRS, 2, n_s).transpose(0, 1, 3, 2)
    g_ = PAIRS_PER_STEP
    return pl.pallas_call(
        _fox_prompt_body,
        grid=(n_b, N_PAIRS // g_),
        in_specs=[spec, spec, spec, pl.BlockSpec((None, g_, n_s, 2), lambda b, g: (b, g, 0, 0))],
        out_specs=spec,
        out_shape=jax.ShapeDtypeStruct((N_PAIRS, n_b * n_s, LANES), BF16),
        scratch_shapes=[pltpu.VMEM((g_, 2, n_s, LANES), BF16), pltpu.VMEM((g_, 2, n_s, LANES), BF16),
                        pltpu.VMEM((g_, 2 * FOX_TQ, LANES), F32), pltpu.VMEM((g_, 2 * FOX_TQ, LANES), F32)],
        compiler_params=_params("parallel", "parallel"),
        name="fox_prompt",
    )(q, k, v, cum_rows)


def _fox_sample_body(q_ref, kn_ref, vn_ref, kc_ref, vc_ref, cq_ref, ckn_ref, ckc_ref, o_ref,
                     m_ref, accl_ref):
    j = pl.program_id(1)
    n_q = q_ref.shape[1]
    cq = cq_ref[...]

    def update_pairs(k_of, v_of, ck, new_rows):
        scores = [_scores(_stack_pair(q_ref[p]), k_of(p), not new_rows) for p in range(N_PAIRS)]
        for p in range(N_PAIRS):
            rows = slice(2 * p * n_q, (2 * p + 2) * n_q)
            _decay_tile(scores[p], v_of(p), cq[:, 2 * p:2 * p + 2], ck[2 * p:2 * p + 2], new_rows,
                        m_ref.at[rows], accl_ref.at[rows], kv_t=not new_rows)

    @pl.when(j == 0)
    def _():
        _softmax_init(m_ref, accl_ref)
        update_pairs(lambda p: kn_ref[p], lambda p: vn_ref[p], ckn_ref[...], True)

    @pl.when(j > 0)
    def _():
        tile = lambda ref: (lambda p: ref[p * LANES:(p + 1) * LANES, :].astype(BF16))
        update_pairs(tile(kc_ref), tile(vc_ref), ckc_ref[...], False)

    @pl.when(j == pl.num_programs(1) - 1)
    def _():
        for p in range(N_PAIRS):
            rows = slice(2 * p * n_q, (2 * p + 2) * n_q)
            o_ref[p] = _merge_pair(_softmax_result(accl_ref.at[rows]), n_q).astype(o_ref.dtype)


def _fox_sample(q, kn, vn, cache_k, cache_v, cum, n_b, n_s):
    n_past = cache_k.shape[2]
    n_blk = n_past // TKC
    cq = cum[..., n_past:].transpose(0, 2, 1)
    ckn = cum[..., n_past:]
    ckc = cum[..., :n_past]
    new_spec = pl.BlockSpec((N_PAIRS, n_s, LANES), lambda b, j: (0, b, 0))
    cache_blk = lambda j: jnp.maximum(j - 1, 0)
    cache_spec = pl.BlockSpec((None, WIDTH, TKC), lambda b, j: (b, 0, cache_blk(j)))
    return pl.pallas_call(
        _fox_sample_body,
        grid=(n_b, n_blk + 1),
        in_specs=[new_spec, new_spec, new_spec, cache_spec, cache_spec,
                  pl.BlockSpec((None, n_s, HEADS), lambda b, j: (b, 0, 0)),
                  pl.BlockSpec((None, HEADS, n_s), lambda b, j: (b, 0, 0)),
                  pl.BlockSpec((None, HEADS, TKC), lambda b, j: (b, 0, cache_blk(j)))],
        out_specs=new_spec,
        out_shape=jax.ShapeDtypeStruct((N_PAIRS, n_b * n_s, LANES), BF16),
        scratch_shapes=[pltpu.VMEM((HEADS * n_s, LANES), F32), pltpu.VMEM((HEADS * n_s, 2 * LANES), F32)],
        compiler_params=_params("parallel", "arbitrary"),
        name="fox_sample",
    )(q, kn, vn, cache_k, cache_v, cq, ckn, ckc)


def _cumsum_body(x_ref, o_ref):
    n_rows, n_cols = x_ref.shape
    r = lax.broadcasted_iota(jnp.int32, (LANES, LANES), 0)
    c = lax.broadcasted_iota(jnp.int32, (LANES, LANES), 1)
    ones = jnp.where(r <= c, 1.0, 0.0).astype(BF16)
    total = jnp.zeros((n_rows, 1), F32)
    for g in range(n_cols // LANES):
        x = x_ref[:, g * LANES:(g + 1) * LANES]
        h1 = x.astype(BF16)
        r1 = x - h1.astype(F32)
        h2 = r1.astype(BF16)
        h3 = (r1 - h2.astype(F32)).astype(BF16)
        y = _dot(h1, ones) + _dot(h2, ones) + _dot(h3, ones) + total
        o_ref[:, g * LANES:(g + 1) * LANES] = y * LOG2E
        total = y[:, LANES - 1:LANES]


def _cumsum_rows(x):
    rows, n = x.shape
    n_pad = -(-n // LANES) * LANES
    xp = jnp.pad(x, ((0, 0), (0, n_pad - n)))
    out = pl.pallas_call(
        _cumsum_body,
        out_shape=jax.ShapeDtypeStruct((rows, n_pad), F32),
        compiler_params=pltpu.CompilerParams(vmem_limit_bytes=VMEM_LIMIT),
        name="cumsum_rows",
    )(xp)
    return out[:, :n]


def _split_cols(w, sizes):
    out, off = [], 0
    for n in sizes:
        out.append(w[:, off:off + n])
        off += n
    return out


def _rope_tables(pos, n_rows):
    half = A_ROPE // 2
    inv_freq = ROPE_THETA ** (-jnp.arange(half, dtype=F32) / half)
    ang = pos.astype(F32)[:, None] * inv_freq[None, :]
    cos, sin = jnp.cos(ang), jnp.sin(ang)
    zeros = jnp.zeros((pos.shape[0], LANES - A_ROPE), F32)
    cos_t = jnp.concatenate([cos, cos, zeros], axis=1)
    sin_t = jnp.concatenate([-sin, sin, zeros], axis=1)
    reps = max(1, n_rows // pos.shape[0])
    return jnp.tile(cos_t, (reps, 1)), jnp.tile(sin_t, (reps, 1))


def _swap_halves(w):
    half = w.shape[-1] // 2
    return jnp.concatenate([w[..., half:], w[..., :half]], axis=-1)


def _pad_lanes(w):
    return jnp.pad(w, [(0, 0)] * (w.ndim - 1) + [(0, LANES - w.shape[-1])])


def _prep_even(w_in, q_norm, w_uq, kv_norm, w_uk, w_uv):
    wqa, wkv, wkr, wga, wqb, wkb, wvb, wgb = _split_cols(
        w_in, (A_Q_LORA, A_KV_LORA, A_ROPE, WIDTH, WIDTH, WIDTH, WIDTH, WIDTH))
    b = lambda a: a.astype(BF16)
    uq_rope = w_uq[:, :, A_NOPE:]
    uk_t = jnp.transpose(w_uk, (1, 2, 0))
    z = jnp.zeros((A_NOPE, A_KV_LORA), w_uk.dtype)
    wuk = jnp.stack([jnp.block([[uk_t[2 * p], z], [z, uk_t[2 * p + 1]]]) for p in range(N_PAIRS)])
    uv_t = jnp.transpose(w_uv, (1, 0, 2))
    zv = jnp.zeros((A_KV_LORA, A_V), w_uv.dtype)
    wuv = jnp.stack([jnp.block([[uv_t[2 * p], zv], [zv, uv_t[2 * p + 1]]]) for p in range(N_PAIRS)])
    return dict(
        wqa=b(wqa), wkv=b(wkv), wkr=b(_pad_lanes(wkr)), wkrs=b(_pad_lanes(_swap_halves(wkr))),
        wga=b(wga), wqb=b(wqb), wkb=b(wkb), wvb=b(wvb), wgb=b(wgb),
        qn=q_norm.reshape(1, -1), kvn=kv_norm.reshape(1, -1),
        wuqn=b(w_uq[:, :, :A_NOPE].reshape(A_Q_LORA, A_HEADS * A_NOPE)),
        wuqr=b(_pad_lanes(uq_rope).reshape(A_Q_LORA, A_HEADS * LANES)),
        wuqrs=b(_pad_lanes(_swap_halves(uq_rope)).reshape(A_Q_LORA, A_HEADS * LANES)),
        wuk=b(wuk), wuv=b(wuv))


def _prep_odd(w_in, forget_bias):
    wqc, wkc, wvc, wgc, wqd, wkd, wvd, wf, wgd = _split_cols(
        w_in, (WIDTH, WIDTH, WIDTH, WIDTH, WIDTH, WIDTH, WIDTH, HEADS, WIDTH))
    b = lambda a: a.astype(BF16)
    return dict(wqc=b(wqc), wkc=b(wkc), wvc=b(wvc), wgc=b(wgc), wqd=b(wqd), wkd=b(wkd), wvd=b(wvd),
                wf=b(_pad_lanes(wf)), wgd=b(wgd), fb=forget_bias.astype(F32).reshape(1, HEADS),
                wf_t=b(wf.T), fb_t=forget_bias.astype(F32).reshape(HEADS, 1))


def kernel(x_prompt, x_sample, cache_mla_ckv, cache_mla_krope, cache_sb_k, cache_sb_v, cache_band_k,
           cache_band_v, cache_fox_k, cache_fox_v, cache_fox_logf, norm_pre, norm_post, w_in_even,
           a_q_norm, a_w_uq, a_kv_norm, a_w_uk, a_w_uv, w_out_even, w_in_odd, c_rel_bias,
           d_forget_bias, w_out_odd):
    n_b, n_s, _ = x_prompt.shape
    d_b, d_s, _ = x_sample.shape
    n_past = cache_sb_k.shape[2]
    n_keep = cache_band_k.shape[2]
    assert n_s % MLA_QB == 0 and n_past % TKC == 0 and d_s == CHUNK and n_past % CHUNK == 0
    assert n_keep == LEFT_CTX and (d_b * d_s) % TM == 0 and TM % d_s == 0

    xp = x_prompt.reshape(n_b * n_s, D_MODEL)
    xs = x_sample.reshape(d_b * d_s, D_MODEL)
    row = lambda a: a.reshape(1, -1)
    heads = lambda a, b, s: a.reshape(b, s, HEADS, HEAD_DIM)
    rows_minor = lambda a: a.transpose(0, 2, 3, 1).reshape(a.shape[0], WIDTH, a.shape[1])

    we = _prep_even(w_in_even[0], a_q_norm[0], a_w_uq[0], a_kv_norm[0], a_w_uk[0], a_w_uv[0])
    wout_e = w_out_even[0].astype(BF16)
    cos_p, sin_p = _rope_tables(jnp.arange(n_s), TM)
    cos_s, sin_s = _rope_tables(n_past + jnp.arange(d_s), TM)

    (ckv_p, kr_p, k2_p, q2_p, ga_p, gb_p, qb_p, kb_p, kb16_p, vb_p, vb16_p) = _in_even(
        xp, row(norm_pre[0]), cos_p, sin_p, we, seq_len=n_s)
    (ckv_s, kr_s, k2_s, q2_s, ga_s, gb_s, qb_s, kb_s, kb16_s, vb_s, vb16_s) = _in_even(
        xs, row(norm_pre[0]), cos_s, sin_s, we)

    lat_p = _mla_prompt(q2_p, k2_p, n_b, n_s).reshape(n_b * n_s, A_HEADS * A_KV_LORA)
    lat_s = _mla_sample(q2_s, k2_s, cache_mla_ckv[0], cache_mla_krope[0].transpose(0, 2, 1), d_b, d_s
                        ).reshape(d_b * d_s, A_HEADS * A_KV_LORA)
    sb_p = _sb_prompt(qb_p, kb16_p, vb16_p, n_b, n_s)
    sb_s = _sb_sample(qb_s, kb16_s, vb16_s, rows_minor(cache_sb_k[0]), rows_minor(cache_sb_v[0]), d_b, d_s)

    xp1 = _out_proj(xp, row(norm_post[0]), ga_p, gb_p, lat_p, sb_p, wout_e, we['wuv'])
    xs1 = _out_proj(xs, row(norm_post[0]), ga_s, gb_s, lat_s, sb_s, wout_e, we['wuv'])

    wo = _prep_odd(w_in_odd[0], d_forget_bias[0])
    wout_o = w_out_odd[0].astype(BF16)
    (qc_p, kc_p, kc16_p, vc_p, vc16_p, gc_p, qd_p, kd_p, kd16_p, vd_p, vd16_p, lf_p, gd_p) = _in_odd(
        xp1, row(norm_pre[1]), wo, seq_len=n_s)
    (qc_s, kc_s, kc16_s, vc_s, vc16_s, gc_s, qd_s, kd_s, kd16_s, vd_s, vd16_s, lf_s, gd_s) = _in_odd(
        xs1, row(norm_pre[1]), wo)

    band_p = _band_prompt(qc_p, kc16_p, vc16_p, _band_bias(c_rel_bias[0], BAND_TQ), n_b, n_s)
    band_s = _band_sample(qc_s, kc16_s, vc16_s, rows_minor(cache_band_k[0]), rows_minor(cache_band_v[0]),
                          _band_bias(c_rel_bias[0], d_s), d_b, d_s)

    lf_s3 = lf_s.reshape(d_b, d_s, HEADS)
    cum_p = _cumsum_rows(lf_p.reshape(n_b * HEADS, n_s)).reshape(n_b, HEADS, n_s)
    lf_all = jnp.concatenate([cache_fox_logf[0].astype(F32), lf_s3], axis=1)
    cum_s = _cumsum_rows(lf_all.transpose(0, 2, 1).reshape(d_b * HEADS, n_past + d_s)
                         ).reshape(d_b, HEADS, n_past + d_s)
    fox_p = _fox_prompt(qd_p, kd16_p, vd16_p, cum_p, n_b, n_s)
    fox_s = _fox_sample(qd_s, kd16_s, vd16_s, rows_minor(cache_fox_k[0]), rows_minor(cache_fox_v[0]), cum_s,
                        d_b, d_s)

    xp2 = _out_proj(xp1, row(norm_post[1]), gc_p, gd_p, band_p, fox_p, wout_o)
    xs2 = _out_proj(xs1, row(norm_post[1]), gc_s, gd_s, band_s, fox_s, wout_o)

    keep = min(LEFT_CTX, n_s)
    band_k_s = jnp.concatenate([cache_band_k[0], heads(kc_s, d_b, d_s)], axis=1)[:, d_s:]
    band_v_s = jnp.concatenate([cache_band_v[0], heads(vc_s, d_b, d_s)], axis=1)[:, d_s:]
    one = lambda a: a[None]
    heads_t = lambda a: a.reshape(n_b, HEADS, HEAD_DIM, a.shape[-1]).transpose(0, 3, 1, 2)
    return (xp2.reshape(n_b, n_s, D_MODEL), xs2.reshape(d_b, d_s, D_MODEL),
            one(ckv_p.reshape(n_b, n_s, A_KV_LORA)), one(kr_p.transpose(0, 2, 1)),
            one(heads_t(kb_p)), one(heads_t(vb_p)),
            one(heads_t(kc_p[:, :, n_s - keep:])), one(heads_t(vc_p[:, :, n_s - keep:])),
            one(heads_t(kd_p)), one(heads_t(vd_p)), one(lf_p.transpose(0, 2, 1)),
            one(ckv_s.reshape(d_b, d_s, A_KV_LORA)), one(kr_s.reshape(d_b, d_s, A_ROPE)),
            one(heads(kb_s, d_b, d_s)), one(heads(vb_s, d_b, d_s)),
            one(band_k_s), one(band_v_s),
            one(heads(kd_s, d_b, d_s)), one(heads(vd_s, d_b, d_s)), one(lf_s3))
```
